```python
import jax, jax.numpy as jnp
from jax import lax
import numpy as np

D_MODEL = 1024
BATCH = 16
SEQ = 256
DEPTH = 4
DEC_BATCH = 4
DEC_SEQ = 4096
PAST_LEN = 256

GRID_W = 64
N_EVEN = (DEPTH + 1) // 2
N_ODD = DEPTH // 2
EPS = 1e-6

HEAD_DIM = 64
A_Q_HEADS = 8
A_KV_HEADS = 2
A_GROUPS = A_Q_HEADS // A_KV_HEADS
WINDOW = 128
BLOCK = 128
ROPE_BASE = 10000.0
NEG_INF = -1e30

B_HEADS = 4
B_DK = 64
B_DV = 128
GLA_RANK = 16
GLA_NORMALIZER = 16.0
GLA_CHUNK = 64

A_WIDTH = A_Q_HEADS * HEAD_DIM
B_WIDTH = B_HEADS * B_DV
MIX_WIDTH = A_WIDTH + B_WIDTH
SIZES_EVEN = (A_Q_HEADS * HEAD_DIM, A_KV_HEADS * HEAD_DIM, A_KV_HEADS * HEAD_DIM,
              B_HEADS * B_DK, B_HEADS * B_DK, B_HEADS * B_DV, B_HEADS * B_DV, 2 * GLA_RANK)
P_EVEN = sum(SIZES_EVEN)

D_RNN = D_MODEL
LRU_BLOCK_W = 256
LRU_BLOCKS = D_RNN // LRU_BLOCK_W
CONV_W = 4
LRU_C = 8.0

D_FF = ((8 * D_MODEL + 3 * 256 - 1) // (3 * 256)) * 256

kernel_name = 'hybrid_diffusion_prefix_step'


def rmsnorm(x, g):
    xf = x.astype(jnp.float32)
    y = xf * lax.rsqrt(jnp.mean(xf * xf, axis=-1, keepdims=True) + EPS)
    return (y * g.astype(jnp.float32)).astype(x.dtype)


def modulate(x, g, shift, scale):
    return rmsnorm(x, g) * (1 + scale) + shift


def ada(cvec, w, b):
    return jnp.split(jax.nn.silu(cvec) @ w + b, 6, axis=-1)


def split_cols(z, sizes):
    return jnp.split(z, list(np.cumsum(sizes)[:-1]), axis=-1)


def rope_2d(x):
    T = x.shape[1]
    n_rows = T // GRID_W
    rows = jnp.repeat(jnp.arange(n_rows, dtype=jnp.float32), GRID_W)
    cols = jnp.tile(jnp.arange(GRID_W, dtype=jnp.float32), n_rows)
    nf = HEAD_DIM // 4
    inv = ROPE_BASE ** (-jnp.arange(nf, dtype=jnp.float32) / nf)
    ar = rows[:, None] * inv
    ac = cols[:, None] * inv
    ang = jnp.concatenate([ar, ar, ac, ac], axis=-1)
    shp = (T,) + (1,) * (x.ndim - 3) + (HEAD_DIM,)
    cos = jnp.cos(ang).reshape(shp)
    sin = jnp.sin(ang).reshape(shp)

    def rot_half(z):
        z1, z2 = jnp.split(z, 2, axis=-1)
        return jnp.concatenate([-z2, z1], axis=-1)

    xr, xc = jnp.split(x, 2, axis=-1)
    xrot = jnp.concatenate([rot_half(xr), rot_half(xc)], axis=-1)
    return (x * cos + xrot * sin).astype(x.dtype)


def sink_softmax(scores, sink):
    s = sink.astype(jnp.float32).reshape(A_KV_HEADS, A_GROUPS, 1, 1)
    s = jnp.broadcast_to(s, scores.shape[:-1] + (1,))
    p = jax.nn.softmax(jnp.concatenate([s, scores], axis=-1), axis=-1)
    return p[..., 1:]


def ctx_attention(q, k, v, sink):
    B, Tc = q.shape[:2]
    nb = Tc // BLOCK
    scale = HEAD_DIM ** -0.5
    qb = q.reshape(B, nb, BLOCK, A_KV_HEADS, A_GROUPS, HEAD_DIM).transpose(1, 0, 2, 3, 4, 5)

    def one_block(qblk):
        s = jnp.einsum('bqkgd,bskd->bkgqs', qblk, k).astype(jnp.float32) * scale
        p = sink_softmax(s, sink)
        return jnp.einsum('bkgqs,bskd->bqkgd', p.astype(v.dtype), v)

    o = lax.map(one_block, qb)
    return o.transpose(1, 0, 2, 3, 4, 5).reshape(B, Tc, A_WIDTH)


def latent_attention(q, k, v, kc, vc, sink):
    B, T = q.shape[:2]
    Tc = kc.shape[1]
    nb = T // BLOCK
    scale = HEAD_DIM ** -0.5
    qb = q.reshape(B, nb, BLOCK, A_KV_HEADS, A_GROUPS, HEAD_DIM)
    pad = ((0, 0), (BLOCK, BLOCK), (0, 0), (0, 0))
    kp = jnp.pad(k, pad).reshape(B, nb + 2, BLOCK, A_KV_HEADS, HEAD_DIM)
    vp = jnp.pad(v, pad).reshape(B, nb + 2, BLOCK, A_KV_HEADS, HEAD_DIM)
    kw = jnp.concatenate([kp[:, :-2], kp[:, 1:-1], kp[:, 2:]], axis=2)
    vw = jnp.concatenate([vp[:, :-2], vp[:, 1:-1], vp[:, 2:]], axis=2)
    qi = jnp.arange(nb)[:, None] * BLOCK + jnp.arange(BLOCK)[None, :]
    kj = (jnp.arange(nb)[:, None] - 1) * BLOCK + jnp.arange(3 * BLOCK)[None, :]
    valid = ((kj[:, None, :] >= 0) & (kj[:, None, :] < T)
             & (jnp.abs(qi[:, :, None] - kj[:, None, :]) <= WINDOW))
    s_w = jnp.einsum('bnqkgd,bnskd->bnkgqs', qb, kw).astype(jnp.float32) * scale
    s_w = jnp.where(valid[None, :, None, None], s_w, NEG_INF)
    s_c = jnp.einsum('bnqkgd,bskd->bnkgqs', qb, kc).astype(jnp.float32) * scale
    p = sink_softmax(jnp.concatenate([s_c, s_w], axis=-1), sink)
    p_c = p[..., :Tc].astype(v.dtype)
    p_w = p[..., Tc:].astype(v.dtype)
    o = (jnp.einsum('bnkgqs,bskd->bnqkgd', p_c, vc)
         + jnp.einsum('bnkgqs,bnskd->bnqkgd', p_w, vw))
    return o.reshape(B, T, A_WIDTH)


def gla_chunked(q, k, v, log_a, s0):
    B, T, H, dk = q.shape
    n = T // GLA_CHUNK

    def chunks(z):
        return z.reshape(B, n, GLA_CHUNK, H, z.shape[-1]).transpose(1, 0, 3, 2, 4)

    qc, kc, vc, ac = chunks(q), chunks(k), chunks(v), chunks(log_a)
    bcum = jnp.cumsum(ac, axis=3)
    b_last = bcum[:, :, :, -1:, :]
    q_e = qc * jnp.exp(bcum)
    k_e = kc * jnp.exp(-bcum)
    k_s = kc * jnp.exp(b_last - bcum)
    mask = jnp.tril(jnp.ones((GLA_CHUNK, GLA_CHUNK), dtype=bool))
    attn = jnp.where(mask, jnp.einsum('nbhid,nbhjd->nbhij', q_e, k_e), 0.0)
    o_intra = jnp.einsum('nbhij,nbhjv->nbhiv', attn, vc)
    decay_last = jnp.exp(b_last[:, :, :, 0, :])
    kv = jnp.einsum('nbhjd,nbhjv->nbhdv', k_s, vc)

    def step(s, xs):
        q_ec, dl, kv_c = xs
        o = jnp.einsum('bhid,bhdv->bhiv', q_ec, s)
        return dl[..., None] * s + kv_c, o

    s_fin, o_inter = lax.scan(step, s0, (q_e, decay_last, kv))
    o = (o_intra + o_inter).transpose(1, 0, 3, 2, 4).reshape(B, T, H, v.shape[-1])
    return o, s_fin


def gla_bidir(q, k, v, lr, w_alpha, b_alpha, s0_f, s0_b):
    B, T = q.shape[:2]
    la = jax.nn.log_sigmoid(jnp.einsum('btzr,zrk->btzk', lr.astype(jnp.float32), w_alpha.astype(jnp.float32))
                            + b_alpha.astype(jnp.float32)) / GLA_NORMALIZER
    la = la.reshape(B, T, 2, B_HEADS, B_DK)
    q, k, v = q.astype(jnp.float32), k.astype(jnp.float32), v.astype(jnp.float32)
    o_f, s_f = gla_chunked(q, k, v, la[:, :, 0], s0_f.astype(jnp.float32))

    def flip(z):
        return jnp.flip(z, axis=1)

    o_b, s_b = gla_chunked(flip(q), flip(k), flip(v), flip(la[:, :, 1]), s0_b.astype(jnp.float32))
    return o_f + flip(o_b), s_f, s_b


def gla_output(o, r, gain):
    B, T = o.shape[:2]
    o = o * lax.rsqrt(jnp.mean(o * o, axis=-1, keepdims=True) + EPS)
    o = o.reshape(B, T, B_WIDTH) * gain.astype(jnp.float32)
    return (o * jax.nn.silu(r.astype(jnp.float32))).astype(r.dtype)


def even_project(h, w_in):
    B, T, _ = h.shape
    qa, ka, va, qb, kb, vb, rb, lr = split_cols(h @ w_in, SIZES_EVEN)
    qa = qa.reshape(B, T, A_KV_HEADS, A_GROUPS, HEAD_DIM)
    ka = ka.reshape(B, T, A_KV_HEADS, HEAD_DIM)
    va = va.reshape(B, T, A_KV_HEADS, HEAD_DIM)
    qb = qb.reshape(B, T, B_HEADS, B_DK) * (B_DK ** -0.5)
    kb = kb.reshape(B, T, B_HEADS, B_DK)
    vb = vb.reshape(B, T, B_HEADS, B_DV)
    lr = lr.reshape(B, T, 2, GLA_RANK)
    return qa, ka, va, qb, kb, vb, rb, lr


def even_ctx(h, w_in, sink, w_alpha, b_alpha, gla_gain, w_out):
    B = h.shape[0]
    qa, ka, va, qb, kb, vb, rb, lr = even_project(h, w_in)
    oa = ctx_attention(qa, ka, va, sink)
    zeros = jnp.zeros((B, B_HEADS, B_DK, B_DV), jnp.float32)
    ob, s_f, s_b = gla_bidir(qb, kb, vb, lr, w_alpha, b_alpha, zeros, zeros)
    out = jnp.concatenate([oa, gla_output(ob, rb, gla_gain)], axis=-1) @ w_out
    return out, ka, va, jnp.stack([s_f, s_b], axis=1)


def even_lat(h, kc, vc, s0, w_in, sink, w_alpha, b_alpha, gla_gain, w_out):
    qa, ka, va, qb, kb, vb, rb, lr = even_project(h, w_in)
    oa = latent_attention(rope_2d(qa), rope_2d(ka), va, kc, vc, sink)
    ob, _, _ = gla_bidir(qb, kb, vb, lr, w_alpha, b_alpha, s0[:, 0], s0[:, 1])
    return jnp.concatenate([oa, gla_output(ob, rb, gla_gain)], axis=-1) @ w_out


def conv_centred(x, w, b):
    T = x.shape[1]
    left = CONV_W // 2
    xp = jnp.pad(x, ((0, 0), (left, CONV_W - 1 - left), (0, 0)))
    y = xp[:, 0:T] * w[0]
    for i in range(1, CONV_W):
        y = y + xp[:, i:i + T] * w[i]
    return y + b


def blockdiag(x, w, b):
    B, T, _ = x.shape
    y = jnp.einsum('btnc,ncd->btnd', x.reshape(B, T, LRU_BLOCKS, LRU_BLOCK_W), w.astype(jnp.float32))
    return y.reshape(B, T, D_RNN) + b.astype(jnp.float32)


def _lin_combine(e1, e2):
    a1, b1 = e1
    a2, b2 = e2
    return a1 * a2, a2 * b1 + b2


def rglru_dir(x, w_a, b_a, w_x, b_x, lam, h0):
    xf = x.astype(jnp.float32)
    r = jax.nn.sigmoid(blockdiag(xf, w_a, b_a))
    i = jax.nn.sigmoid(blockdiag(xf, w_x, b_x))
    log_a = -LRU_C * r * jax.nn.softplus(-lam.astype(jnp.float32))
    a = jnp.exp(log_a)
    u = jnp.sqrt(-jnp.expm1(2.0 * log_a)) * (i * xf)
    u = u.at[:, 0].add(a[:, 0] * h0.astype(jnp.float32))
    _, hs = lax.associative_scan(_lin_combine, (a, u), axis=1)
    return hs, hs[:, -1]


def odd_mixer(h, h0, w_in, conv_w, conv_b, w_ga, b_ga, w_gx, b_gx, lam, w_out):
    g, u = jnp.split(h @ w_in, 2, axis=-1)
    u = conv_centred(u, conv_w, conv_b)
    y_f, hf = rglru_dir(u, w_ga[0], b_ga[0], w_gx[0], b_gx[0], lam[0], h0[:, 0])
    y_b, hb = rglru_dir(jnp.flip(u, axis=1), w_ga[1], b_ga[1], w_gx[1], b_gx[1], lam[1], h0[:, 1])
    y = y_f + jnp.flip(y_b, axis=1)
    out = (jax.nn.gelu(g).astype(jnp.float32) * y).astype(h.dtype) @ w_out
    return out, jnp.stack([hf, hb], axis=1)


def swiglu(h, w_in, w_out):
    g, u = jnp.split(h @ w_in, 2, axis=-1)
    return (jax.nn.silu(g) * u) @ w_out


def setup_inputs(seed: int = 0) -> dict:
    key = jax.random.key(seed)
    ks = iter(jax.random.split(key, 32))

    def nrm(shape, s):
        return jax.random.normal(next(ks), shape, jnp.float32) * s

    D = D_MODEL
    x_prompt = nrm((BATCH, SEQ, D), 1.0)
    x_sample = nrm((DEC_BATCH, DEC_SEQ, D), 1.0)
    c = nrm((DEC_BATCH, D), 1.0)
    cache_k = nrm((DEC_BATCH, N_EVEN, PAST_LEN, A_KV_HEADS, HEAD_DIM), 1.0)
    cache_v = nrm((DEC_BATCH, N_EVEN, PAST_LEN, A_KV_HEADS, HEAD_DIM), 1.0)
    state_gla = nrm((DEC_BATCH, N_EVEN, 2, B_HEADS, B_DK, B_DV), 0.5)
    state_lru = nrm((DEC_BATCH, N_ODD, 2, D_RNN), 0.5)
    c_ctx = nrm((D,), 1.0)
    w_ada = nrm((DEPTH, D, 6 * D), 0.5 * D ** -0.5)
    b_ada = nrm((DEPTH, 6 * D), 0.01)
    norm_mix = 1.0 + nrm((DEPTH, D), 0.05)
    norm_ffn = 1.0 + nrm((DEPTH, D), 0.05)
    w_in_even = nrm((N_EVEN, D, P_EVEN), D ** -0.5)
    attn_sink = nrm((N_EVEN, A_Q_HEADS), 1.0)
    w_alpha = nrm((N_EVEN, 2, GLA_RANK, B_HEADS * B_DK), GLA_RANK ** -0.5)
    b_alpha = nrm((N_EVEN, 2, B_HEADS * B_DK), 0.1)
    gla_gain = 1.0 + nrm((N_EVEN, B_WIDTH), 0.05)
    w_out_even = nrm((N_EVEN, MIX_WIDTH, D), MIX_WIDTH ** -0.5)
    w_in_odd = nrm((N_ODD, D, 2 * D_RNN), D ** -0.5)
    conv_w = nrm((N_ODD, CONV_W, D_RNN), CONV_W ** -0.5)
    conv_b = nrm((N_ODD, D_RNN), 0.01)
    w_gate_a = nrm((N_ODD, 2, LRU_BLOCKS, LRU_BLOCK_W, LRU_BLOCK_W), LRU_BLOCK_W ** -0.5)
    b_gate_a = nrm((N_ODD, 2, D_RNN), 0.01)
    w_gate_x = nrm((N_ODD, 2, LRU_BLOCKS, LRU_BLOCK_W, LRU_BLOCK_W), LRU_BLOCK_W ** -0.5)
    b_gate_x = nrm((N_ODD, 2, D_RNN), 0.01)
    a0 = jax.random.uniform(next(ks), (N_ODD, 2, D_RNN), jnp.float32, minval=0.9, maxval=0.999)
    lru_lambda = jnp.log(a0) - jnp.log1p(-a0)
    w_out_odd = nrm((N_ODD, D_RNN, D), D_RNN ** -0.5)
    w_ffn_in = nrm((DEPTH, D, 2 * D_FF), D ** -0.5)
    w_ffn_out = nrm((DEPTH, D_FF, D), D_FF ** -0.5)
    norm_final = 1.0 + nrm((D,), 0.05)
    return {'x_prompt': x_prompt, 'x_sample': x_sample, 'c': c,
            'cache_k': cache_k, 'cache_v': cache_v, 'state_gla': state_gla, 'state_lru': state_lru,
            'c_ctx': c_ctx, 'w_ada': w_ada, 'b_ada': b_ada, 'norm_mix': norm_mix, 'norm_ffn': norm_ffn,
            'w_in_even': w_in_even, 'attn_sink': attn_sink, 'w_alpha': w_alpha, 'b_alpha': b_alpha,
            'gla_gain': gla_gain, 'w_out_even': w_out_even, 'w_in_odd': w_in_odd, 'conv_w': conv_w,
            'conv_b': conv_b, 'w_gate_a': w_gate_a, 'b_gate_a': b_gate_a, 'w_gate_x': w_gate_x,
            'b_gate_x': b_gate_x, 'lru_lambda': lru_lambda, 'w_out_odd': w_out_odd,
            'w_ffn_in': w_ffn_in, 'w_ffn_out': w_ffn_out, 'norm_final': norm_final}


def reference(x_prompt, x_sample, c, cache_k, cache_v, state_gla, state_lru, c_ctx,
              w_ada, b_ada, norm_mix, norm_ffn, w_in_even, attn_sink, w_alpha, b_alpha, gla_gain,
              w_out_even, w_in_odd, conv_w, conv_b, w_gate_a, b_gate_a, w_gate_x, b_gate_x,
              lru_lambda, w_out_odd, w_ffn_in, w_ffn_out, norm_final):
    xp, xs = x_prompt, x_sample
    new_k, new_v, new_gla, new_lru = [], [], [], []
    for l in range(DEPTH):
        sh1c, sc1c, g1c, sh2c, sc2c, g2c = ada(c_ctx, w_ada[l], b_ada[l])
        sh1s, sc1s, g1s, sh2s, sc2s, g2s = (m[:, None] for m in ada(c, w_ada[l], b_ada[l]))
        hp = modulate(xp, norm_mix[l], sh1c, sc1c)
        hs = modulate(xs, norm_mix[l], sh1s, sc1s)
        if l % 2 == 0:
            e = l // 2
            ew = (w_in_even[e], attn_sink[e], w_alpha[e], b_alpha[e], gla_gain[e], w_out_even[e])
            op, k_c, v_c, s_c = even_ctx(hp, *ew)
            os_ = even_lat(hs, cache_k[:, e], cache_v[:, e], state_gla[:, e], *ew)
            new_k.append(k_c)
            new_v.append(v_c)
            new_gla.append(s_c)
        else:
            o = l // 2
            ow = (w_in_odd[o], conv_w[o], conv_b[o], w_gate_a[o], b_gate_a[o], w_gate_x[o],
                  b_gate_x[o], lru_lambda[o], w_out_odd[o])
            h0 = jnp.zeros((xp.shape[0], 2, D_RNN), jnp.float32)
            op, h_c = odd_mixer(hp, h0, *ow)
            os_, _ = odd_mixer(hs, state_lru[:, o], *ow)
            new_lru.append(h_c)
        xp = xp + g1c * op
        xs = xs + g1s * os_
        xp = xp + g2c * swiglu(modulate(xp, norm_ffn[l], sh2c, sc2c), w_ffn_in[l], w_ffn_out[l])
        xs = xs + g2s * swiglu(modulate(xs, norm_ffn[l], sh2s, sc2s), w_ffn_in[l], w_ffn_out[l])
    y_prompt = rmsnorm(xp, norm_final)
    y_sample = rmsnorm(xs, norm_final)
    new_cache_k = jnp.stack(new_k, axis=1)
    new_cache_v = jnp.stack(new_v, axis=1)
    new_state_gla = jnp.stack(new_gla, axis=1)
    new_state_lru = jnp.stack(new_lru, axis=1)
    return (y_prompt, y_sample, new_cache_k, new_cache_v, new_state_gla, new_state_lru)
```

```python
import functools
from typing import NamedTuple

import jax
import jax.numpy as jnp
import numpy as np
from jax import lax
from jax.experimental import pallas as pl
from jax.experimental.pallas import tpu as pltpu

F32 = jnp.float32
BF16 = jnp.bfloat16

D_MODEL = 1024
DEPTH = 4
EPS = 1e-6

HEAD_DIM = 64
A_Q_HEADS = 8
A_KV_HEADS = 2
A_GROUPS = A_Q_HEADS // A_KV_HEADS
A_WIDTH = A_Q_HEADS * HEAD_DIM
ATT_BLOCK = 128
ROPE_BASE = 10000.0
GRID_W = 64
NEG_INF = -1e30
PAST_LEN = 256

B_HEADS = 4
B_DK = 64
B_DV = 128
B_QK = B_HEADS * B_DK
B_WIDTH = B_HEADS * B_DV
GLA_RANK = 16
GLA_NORMALIZER = 16.0
GLA_CHUNK = 64

D_RNN = D_MODEL
LRU_BLOCK_W = 256
LRU_BLOCKS = D_RNN // LRU_BLOCK_W
CONV_W = 4
LRU_C = 8.0

D_FF = 2816

LANES = 128
SUBLANES = 8

ZA_W = A_WIDTH + 2 * 2 * LANES
ZB_LR = B_QK + B_QK + B_WIDTH + B_WIDTH
ZB_W = ZB_LR + LANES
P_PACK = ZA_W + ZB_W

TM = 512
TS = 256
FF_CHUNK = 1408

VMEM_LIMIT = 56 * 1024 * 1024


class Cfg(NamedTuple):
    nc: int
    tc: int
    nl: int
    tl: int

    @property
    def n_ctx(self):
        return self.nc * self.tc

    @property
    def n_lat(self):
        return self.nl * self.tl

    @property
    def n_tok(self):
        return self.n_ctx + self.n_lat


def _cparams(n_axes):
    return pltpu.CompilerParams(dimension_semantics=("arbitrary",) * n_axes,
                                vmem_limit_bytes=VMEM_LIMIT)


def _const_spec(shape):
    nd = len(shape)
    return pl.BlockSpec(shape, lambda *_: (0,) * nd)


def _silu(x):
    return x * jax.nn.sigmoid(x)


def _group_of_tile(cfg, i, tile):
    nct = cfg.n_ctx // tile
    per_lat = cfg.tl // tile
    return jnp.where(i < nct, 0, 1 + jnp.maximum(i - nct, 0) // per_lat)


def _modnorm(x, nw, shift, scale):
    ms = jnp.mean(x * x, axis=-1, keepdims=True)
    y = x * lax.rsqrt(ms + EPS) * nw
    return y * (1.0 + scale) + shift


def _mod_rows(mod_ref, g, first):
    return tuple(mod_ref[first + j, pl.ds(g, 1), :] for j in range(3))


def _ada_kernel(c_ref, w_ref, b_ref, o_ref):
    s = _silu(c_ref[...]).astype(BF16)
    o_ref[...] = jnp.dot(s, w_ref[...].astype(BF16), preferred_element_type=F32) + b_ref[...]


def _ada(cvec, w_ada, b_ada):
    b4 = b_ada.reshape(DEPTH, 6, 1, D_MODEL)
    return pl.pallas_call(
        _ada_kernel,
        grid=(DEPTH, 6),
        in_specs=[
            _const_spec((SUBLANES, D_MODEL)),
            pl.BlockSpec((None, D_MODEL, D_MODEL), lambda l, j: (l, 0, j)),
            pl.BlockSpec((None, None, 1, D_MODEL), lambda l, j: (l, j, 0, 0)),
        ],
        out_specs=pl.BlockSpec((None, None, SUBLANES, D_MODEL), lambda l, j: (l, j, 0, 0)),
        out_shape=jax.ShapeDtypeStruct((DEPTH, 6, SUBLANES, D_MODEL), F32),
        compiler_params=_cparams(2),
        name="ada",
    )(cvec, w_ada, b4)


def _mod_spec(l):
    return pl.BlockSpec((None, 6, SUBLANES, D_MODEL), lambda *_: (l, 0, 0, 0))


def _ffn_tile(x, nw, shift, scale, gate, win_ref, wout_ref):
    h = _modnorm(x, nw, shift, scale).astype(BF16)
    acc = None
    for c in range(D_FF // FF_CHUNK):
        lo, hi = c * FF_CHUNK, (c + 1) * FF_CHUNK
        g = jnp.dot(h, win_ref[:, lo:hi], preferred_element_type=F32)
        u = jnp.dot(h, win_ref[:, D_FF + lo:D_FF + hi], preferred_element_type=F32)
        a = (_silu(g) * u).astype(BF16)
        part = jnp.dot(a, wout_ref[lo:hi, :], preferred_element_type=F32)
        acc = part if acc is None else acc + part
    return x + gate * acc


def _ffn_kernel(x_ref, mod_ref, nw_ref, win_ref, wout_ref, nf_ref, o_ref, *, cfg, final):
    g = _group_of_tile(cfg, pl.program_id(0), TM)
    shift, scale, gate = _mod_rows(mod_ref, g, 3)
    y = _ffn_tile(x_ref[...], nw_ref[...], shift, scale, gate, win_ref, wout_ref)
    if final:
        ms = jnp.mean(y * y, axis=-1, keepdims=True)
        y = y * lax.rsqrt(ms + EPS) * nf_ref[...]
    o_ref[...] = y


def _ffn(cfg, l, x, mods, nw, win, wout, nf, final):
    n = cfg.n_tok
    tok = pl.BlockSpec((TM, D_MODEL), lambda i: (i, 0))
    return pl.pallas_call(
        functools.partial(_ffn_kernel, cfg=cfg, final=final),
        grid=(n // TM,),
        in_specs=[tok, _mod_spec(l), _const_spec((1, D_MODEL)),
                  pl.BlockSpec((D_MODEL, 2 * D_FF), lambda i: (0, 0), pipeline_mode=pl.Buffered(1)),
                  pl.BlockSpec((D_FF, D_MODEL), lambda i: (0, 0), pipeline_mode=pl.Buffered(1)),
                  _const_spec((1, D_MODEL))],
        out_specs=tok,
        out_shape=jax.ShapeDtypeStruct((n, D_MODEL), F32),
        compiler_params=_cparams(1),
        name=f"ffn{l}",
    )(x, mods, nw, win, wout, nf)


def _rope(x, tab_ref):
    cos, sin_a, sin_b = tab_ref[:, 0:LANES], tab_ref[:, LANES:2 * LANES], tab_ref[:, 2 * LANES:3 * LANES]
    parts = []
    for j in range(x.shape[1] // LANES):
        xs = x[:, j * LANES:(j + 1) * LANES]
        parts.append(xs * cos + pltpu.roll(xs, LANES - 16, 1) * sin_a + pltpu.roll(xs, 16, 1) * sin_b)
    return jnp.concatenate(parts, axis=1)


def _even_in_kernel(x_ref, mod_ref, nw_ref, w_ref, tab_ref, za_ref, zb_ref, kv_ref, *, cfg):
    i = pl.program_id(0)
    g = _group_of_tile(cfg, i, TM)
    shift, scale, _ = _mod_rows(mod_ref, g, 0)
    h = _modnorm(x_ref[...], nw_ref[...], shift, scale).astype(BF16)
    z = jnp.dot(h, w_ref[...], preferred_element_type=F32)
    q = z[:, 0:A_WIDTH] * (HEAD_DIM ** -0.5)
    k2 = z[:, A_WIDTH:A_WIDTH + 2 * LANES]
    v2 = z[:, A_WIDTH + 2 * LANES:ZA_W]
    za_ref[:, A_WIDTH + 2 * LANES:ZA_W] = v2.astype(BF16)
    zb_ref[:, 0:B_QK] = z[:, ZA_W:ZA_W + B_QK] * (B_DK ** -0.5)
    zb_ref[:, B_QK:ZB_W] = z[:, ZA_W + B_QK:P_PACK]

    @pl.when(g == 0)
    def _():
        za_ref[:, 0:A_WIDTH] = q.astype(BF16)
        za_ref[:, A_WIDTH:A_WIDTH + 2 * LANES] = k2.astype(BF16)
        lo = lax.broadcasted_iota(jnp.int32, (TM, LANES), 1) < HEAD_DIM
        kv_ref[:, 0:LANES] = jnp.where(lo, k2[:, 0:LANES], k2[:, LANES:2 * LANES])
        kv_ref[:, LANES:2 * LANES] = jnp.where(lo, v2[:, 0:LANES], v2[:, LANES:2 * LANES])

    @pl.when(g != 0)
    def _():
        za_ref[:, 0:A_WIDTH] = _rope(q, tab_ref).astype(BF16)
        za_ref[:, A_WIDTH:A_WIDTH + 2 * LANES] = _rope(k2, tab_ref).astype(BF16)


def _even_in(cfg, l, x, mods, nw, w_pack, rope_tab):
    n = cfg.n_tok
    nct = cfg.n_ctx // TM
    per_lat = cfg.tl // TM
    tok = lambda w: pl.BlockSpec((TM, w), lambda i: (i, 0))
    return pl.pallas_call(
        functools.partial(_even_in_kernel, cfg=cfg),
        grid=(n // TM,),
        in_specs=[tok(D_MODEL), _mod_spec(l), _const_spec((1, D_MODEL)),
                  pl.BlockSpec((D_MODEL, P_PACK), lambda i: (0, 0), pipeline_mode=pl.Buffered(1)),
                  pl.BlockSpec((TM, 3 * LANES), lambda i: (jnp.maximum(i - nct, 0) % per_lat, 0))],
        out_specs=[tok(ZA_W), tok(ZB_W),
                   pl.BlockSpec((TM, 2 * LANES), lambda i: (jnp.minimum(i, nct - 1), 0))],
        out_shape=[jax.ShapeDtypeStruct((n, ZA_W), BF16),
                   jax.ShapeDtypeStruct((n, ZB_W), F32),
                   jax.ShapeDtypeStruct((cfg.n_ctx, 2 * LANES), F32)],
        compiler_params=_cparams(1),
        name=f"even_in{l}",
    )(x, mods, nw, w_pack, rope_tab)


def _sink_attention(q_ref, k_all, v_all, mask, sink_ref, o_ref, kvh):
    tq = q_ref.shape[0]
    lo = lax.broadcasted_iota(jnp.int32, (tq, LANES), 1) < HEAD_DIM
    qs = []
    for gq in range(A_GROUPS):
        h = kvh * A_GROUPS + gq
        pair = q_ref[:, (h // 2) * LANES:(h // 2 + 1) * LANES]
        keep = lo if h % 2 == 0 else jnp.logical_not(lo)
        qs.append(jnp.where(keep, pair, jnp.zeros_like(pair)))
    qst = jnp.concatenate(qs, axis=0)
    s = lax.dot_general(qst, k_all, (((1,), (1,)), ((), ())), preferred_element_type=F32)
    if mask is not None:
        s = jnp.where(mask, s, NEG_INF)
    rows = lax.broadcasted_iota(jnp.int32, (A_GROUPS * tq, 1), 0) // tq
    sink = jnp.zeros((A_GROUPS * tq, 1), F32)
    for gq in range(A_GROUPS):
        sink = jnp.where(rows == gq, sink_ref[kvh * A_GROUPS + gq], sink)
    m = jnp.maximum(jnp.max(s, axis=-1, keepdims=True), sink)
    p = jnp.exp(s - m)
    den = jnp.sum(p, axis=-1, keepdims=True) + jnp.exp(sink - m)
    o = jnp.dot(p.astype(BF16), v_all, preferred_element_type=F32) / den
    for pr in range(2):
        even, odd = o[(2 * pr) * tq:(2 * pr + 1) * tq], o[(2 * pr + 1) * tq:(2 * pr + 2) * tq]
        col = (kvh * 2 + pr) * LANES
        o_ref[:, col:col + LANES] = jnp.where(lo, even, odd).astype(o_ref.dtype)


def _attn_ctx_kernel(sink_ref, q_ref, k_ref, v_ref, o_ref):
    for kvh in range(A_KV_HEADS):
        sl = slice(kvh * LANES, (kvh + 1) * LANES)
        _sink_attention(q_ref, k_ref[:, sl], v_ref[:, sl], None, sink_ref, o_ref, kvh)


def _attn_ctx(cfg, l, za, sink):
    tc = cfg.tc
    return pl.pallas_call(
        _attn_ctx_kernel,
        grid=(cfg.nc,),
        in_specs=[pl.BlockSpec(memory_space=pltpu.SMEM),
                  pl.BlockSpec((tc, A_WIDTH), lambda b: (b, 0)),
                  pl.BlockSpec((tc, 2 * LANES), lambda b: (b, A_WIDTH // (2 * LANES))),
                  pl.BlockSpec((tc, 2 * LANES), lambda b: (b, A_WIDTH // (2 * LANES) + 1))],
        out_specs=pl.BlockSpec((tc, A_WIDTH), lambda b: (b, 0)),
        out_shape=jax.ShapeDtypeStruct((cfg.n_tok, A_WIDTH), BF16),
        compiler_params=_cparams(1),
        name=f"attn_ctx{l}",
    )(sink, za, za, za)


def _attn_lat_kernel(sink_ref, q_ref, kp_ref, kc_ref, kn_ref, vp_ref, vc_ref, vn_ref,
                     ck_ref, cv_ref, oa_in_ref, o_ref):
    del oa_in_ref
    n = pl.program_id(1)
    nb = pl.num_programs(1)
    shape = (A_GROUPS * ATT_BLOCK, PAST_LEN + 3 * ATT_BLOCK)
    r = lax.broadcasted_iota(jnp.int32, shape, 0) % ATT_BLOCK
    c = lax.broadcasted_iota(jnp.int32, shape, 1) - PAST_LEN
    lo_thr = r + jnp.where(n == 0, ATT_BLOCK, 0)
    hi_thr = r - jnp.where(n == nb - 1, ATT_BLOCK, 0)
    bad_prev = jnp.logical_and(jnp.logical_and(c >= 0, c < ATT_BLOCK), c < lo_thr)
    bad_next = jnp.logical_and(c >= 2 * ATT_BLOCK, c - 2 * ATT_BLOCK > hi_thr)
    mask = jnp.logical_not(jnp.logical_or(bad_prev, bad_next))
    for kvh in range(A_KV_HEADS):
        sl = slice(kvh * LANES, (kvh + 1) * LANES)
        k_all = jnp.concatenate([ck_ref[:, sl], kp_ref[:, sl], kc_ref[:, sl], kn_ref[:, sl]], axis=0)
        v_all = jnp.concatenate([cv_ref[:, sl], vp_ref[:, sl], vc_ref[:, sl], vn_ref[:, sl]], axis=0)
        _sink_attention(q_ref, k_all, v_all, mask, sink_ref, o_ref, kvh)


def _attn_lat(cfg, l, za, cache_k2, cache_v2, sink, oa):
    nbl = cfg.tl // ATT_BLOCK
    base = cfg.n_ctx // ATT_BLOCK
    kcol = A_WIDTH // (2 * LANES)

    def blk(delta, col, width):
        def imap(b, n):
            return (base + b * nbl + jnp.clip(n + delta, 0, nbl - 1), col)
        return pl.BlockSpec((ATT_BLOCK, width), imap)

    cache = pl.BlockSpec((None, PAST_LEN, 2 * LANES), lambda b, n: (b, 0, 0))
    return pl.pallas_call(
        _attn_lat_kernel,
        grid=(cfg.nl, nbl),
        in_specs=[pl.BlockSpec(memory_space=pltpu.SMEM),
                  blk(0, 0, A_WIDTH),
                  blk(-1, kcol, 2 * LANES), blk(0, kcol, 2 * LANES), blk(1, kcol, 2 * LANES),
                  blk(-1, kcol + 1, 2 * LANES), blk(0, kcol + 1, 2 * LANES), blk(1, kcol + 1, 2 * LANES),
                  cache, cache,
                  pl.BlockSpec(memory_space=pl.ANY)],
        out_specs=blk(0, 0, A_WIDTH),
        out_shape=jax.ShapeDtypeStruct((cfg.n_tok, A_WIDTH), BF16),
        input_output_aliases={10: 0},
        compiler_params=_cparams(2),
        name=f"attn_lat{l}",
    )(sink, za, za, za, za, za, za, za, cache_k2, cache_v2, oa)


def _gla_chunk(q, k, v, la, st, reverse):
    cn = GLA_CHUNK
    row = lax.broadcasted_iota(jnp.int32, (cn, B_QK), 0)
    lane = lax.broadcasted_iota(jnp.int32, (cn, B_QK), 1)
    b = la
    s = 1
    while s < cn:
        if reverse:
            b = b + jnp.where(row < cn - s, pltpu.roll(b, cn - s, 0), 0.0)
        else:
            b = b + jnp.where(row >= s, pltpu.roll(b, s, 0), 0.0)
        s *= 2
    b_last = b[0:1, :] if reverse else b[cn - 1:cn, :]
    q_e = (q * jnp.exp(b)).astype(BF16)
    k_e = (k * jnp.exp(-b)).astype(BF16)
    k_s = (k * jnp.exp(b_last - b)).astype(BF16)
    vb = v.astype(BF16)
    zero = jnp.zeros_like(k_e)
    kbd = jnp.concatenate([jnp.where(lane // B_DK == h, k_e, zero) for h in range(B_HEADS)], axis=0)
    attn = lax.dot_general(q_e, kbd, (((1,), (1,)), ((), ())), preferred_element_type=F32)
    col = lane % cn
    tri = (col >= row) if reverse else (col <= row)
    attn = jnp.where(tri, attn, 0.0).astype(BF16)
    lane_v = lax.broadcasted_iota(jnp.int32, (cn, B_WIDTH), 1)
    zero_v = jnp.zeros_like(vb)
    vbd = jnp.concatenate([jnp.where(lane_v // B_DV == h, vb, zero_v) for h in range(B_HEADS)], axis=0)
    o = jnp.dot(attn, vbd, preferred_element_type=F32)
    o = o + lax.dot_general(q_e, st.astype(BF16), (((1,), (1,)), ((), ())), preferred_element_type=F32)
    upd = lax.dot_general(vb, k_s, (((0,), (0,)), ((), ())), preferred_element_type=F32)
    own = (lax.broadcasted_iota(jnp.int32, (B_WIDTH, B_QK), 0) // B_DV
           == lax.broadcasted_iota(jnp.int32, (B_WIDTH, B_QK), 1) // B_DK)
    st_new = st * jnp.exp(b_last) + jnp.where(own, upd, 0.0)
    return o, st_new


def _gla_log_decay(lr_pad, wal_ref, bal_ref, direction):
    z = jnp.dot(lr_pad.astype(BF16), wal_ref[:, direction * B_QK:(direction + 1) * B_QK],
                preferred_element_type=F32) + bal_ref[:, direction * B_QK:(direction + 1) * B_QK]
    return jax.nn.log_sigmoid(z) / GLA_NORMALIZER


def _gla_block(qkv_ref, lr_ref, wal_ref, bal_ref, st_ref, o_ref, direction):
    nchunk = qkv_ref.shape[0] // GLA_CHUNK
    la_all = _gla_log_decay(lr_ref[...], wal_ref, bal_ref, direction)
    order = range(nchunk - 1, -1, -1) if direction else range(nchunk)
    for ci in order:
        rs = slice(ci * GLA_CHUNK, (ci + 1) * GLA_CHUNK)
        o, st = _gla_chunk(qkv_ref[rs, 0:B_QK], qkv_ref[rs, B_QK:2 * B_QK],
                           qkv_ref[rs, 2 * B_QK:2 * B_QK + B_WIDTH], la_all[rs, :],
                           st_ref[...], bool(direction))
        st_ref[...] = st
        o_ref[rs, :] = o


def _state_to_blockdiag_t(s_ref):
    rows = []
    for h in range(B_HEADS):
        parts = [jnp.zeros((B_DK, B_DV), F32)] * B_HEADS
        parts[h] = s_ref[h]
        rows.append(jnp.concatenate(parts, axis=1))
    return jnp.concatenate(rows, axis=0).T


def _blockdiag_t_to_state(st, s_ref):
    s = st.T
    for h in range(B_HEADS):
        s_ref[h] = s[h * B_DK:(h + 1) * B_DK, h * B_DV:(h + 1) * B_DV]


def _gla_finish(o, r, gain):
    parts = []
    for h in range(B_HEADS):
        oh = o[:, h * B_DV:(h + 1) * B_DV]
        parts.append(oh * lax.rsqrt(jnp.mean(oh * oh, axis=-1, keepdims=True) + EPS))
    return (jnp.concatenate(parts, axis=1) * gain * _silu(r)).astype(BF16)


def _gla_ctx_kernel(qkv_ref, r_ref, lr_ref, wal_ref, bal_ref, gain_ref, ob_ref, snew_ref,
                    st_ref, of_ref, obk_ref):
    for direction, acc_ref in ((0, of_ref), (1, obk_ref)):
        st_ref[...] = jnp.zeros_like(st_ref)
        _gla_block(qkv_ref, lr_ref, wal_ref, bal_ref, st_ref, acc_ref, direction)
        _blockdiag_t_to_state(st_ref[...], snew_ref.at[direction])
    ob_ref[...] = _gla_finish(of_ref[...] + obk_ref[...], r_ref[...], gain_ref[...])


def _gla_ctx(cfg, l, zb, wal, bal, gain):
    tc = cfg.tc
    return pl.pallas_call(
        _gla_ctx_kernel,
        grid=(cfg.nc,),
        in_specs=[pl.BlockSpec((tc, 2 * B_QK + B_WIDTH), lambda b: (b, 0)),
                  pl.BlockSpec((tc, B_WIDTH), lambda b: (b, (2 * B_QK + B_WIDTH) // B_WIDTH)),
                  pl.BlockSpec((tc, LANES), lambda b: (b, ZB_LR // LANES)),
                  _const_spec((LANES, 2 * B_QK)), _const_spec((1, 2 * B_QK)), _const_spec((1, B_WIDTH))],
        out_specs=[pl.BlockSpec((tc, B_WIDTH), lambda b: (b, 0)),
                   pl.BlockSpec((None, 2, B_HEADS, B_DK, B_DV), lambda b: (b, 0, 0, 0, 0))],
        out_shape=[jax.ShapeDtypeStruct((cfg.n_tok, B_WIDTH), BF16),
                   jax.ShapeDtypeStruct((cfg.nc, 2, B_HEADS, B_DK, B_DV), F32)],
        scratch_shapes=[pltpu.VMEM((B_WIDTH, B_QK), F32),
                        pltpu.VMEM((tc, B_WIDTH), F32), pltpu.VMEM((tc, B_WIDTH), F32)],
        compiler_params=_cparams(1),
        name=f"gla_ctx{l}",
    )(zb, zb, zb, wal, bal, gain)


def _gla_lat_fwd_kernel(qkv_ref, lr_ref, wal_ref, bal_ref, s0_ref, of_ref, st_ref):
    @pl.when(pl.program_id(1) == 0)
    def _():
        st_ref[...] = _state_to_blockdiag_t(s0_ref)
    _gla_block(qkv_ref, lr_ref, wal_ref, bal_ref, st_ref, of_ref, 0)


def _gla_lat_bwd_kernel(qkv_ref, r_ref, lr_ref, wal_ref, bal_ref, gain_ref, s0_ref, of_ref,
                        ob_in_ref, ob_ref, st_ref, obk_ref):
    del ob_in_ref

    @pl.when(pl.program_id(1) == 0)
    def _():
        st_ref[...] = _state_to_blockdiag_t(s0_ref)
    _gla_block(qkv_ref, lr_ref, wal_ref, bal_ref, st_ref, obk_ref, 1)
    ob_ref[...] = _gla_finish(of_ref[...] + obk_ref[...], r_ref[...], gain_ref[...])


def _gla_lat(cfg, l, zb, wal, bal, gain, s0, ob):
    nbl = cfg.tl // TS
    base = cfg.n_ctx // TS

    def blk(width, col, rev):
        def imap(b, n):
            return (base + b * nbl + (nbl - 1 - n if rev else n), col)
        return pl.BlockSpec((TS, width), imap)

    def lat_blk(rev):
        return pl.BlockSpec((TS, B_WIDTH), lambda b, n: (b * nbl + (nbl - 1 - n if rev else n), 0))

    def state(direction):
        return pl.BlockSpec((None, None, B_HEADS, B_DK, B_DV), lambda b, n: (b, direction, 0, 0, 0))

    consts = [_const_spec((LANES, 2 * B_QK)), _const_spec((1, 2 * B_QK))]
    qkv_w = 2 * B_QK + B_WIDTH
    o_f = pl.pallas_call(
        _gla_lat_fwd_kernel,
        grid=(cfg.nl, nbl),
        in_specs=[blk(qkv_w, 0, False), blk(LANES, ZB_LR // LANES, False)] + consts + [state(0)],
        out_specs=lat_blk(False),
        out_shape=jax.ShapeDtypeStruct((cfg.n_lat, B_WIDTH), F32),
        scratch_shapes=[pltpu.VMEM((B_WIDTH, B_QK), F32)],
        compiler_params=_cparams(2),
        name=f"gla_fwd{l}",
    )(zb, zb, wal, bal, s0)
    return pl.pallas_call(
        _gla_lat_bwd_kernel,
        grid=(cfg.nl, nbl),
        in_specs=[blk(qkv_w, 0, True), blk(B_WIDTH, qkv_w // B_WIDTH, True),
                  blk(LANES, ZB_LR // LANES, True)] + consts
                 + [_const_spec((1, B_WIDTH)), state(1), lat_blk(True), pl.BlockSpec(memory_space=pl.ANY)],
        out_specs=blk(B_WIDTH, 0, True),
        out_shape=jax.ShapeDtypeStruct((cfg.n_tok, B_WIDTH), BF16),
        scratch_shapes=[pltpu.VMEM((B_WIDTH, B_QK), F32), pltpu.VMEM((TS, B_WIDTH), F32)],
        input_output_aliases={8: 0},
        compiler_params=_cparams(2),
        name=f"gla_bwd{l}",
    )(zb, zb, zb, wal, bal, gain, s0, o_f, ob)


def _even_out_kernel(x_ref, oa_ref, ob_ref, mod_ref, w_ref, o_ref, *, cfg):
    g = _group_of_tile(cfg, pl.program_id(0), TM)
    gate = mod_ref[2, pl.ds(g, 1), :]
    o = jnp.concatenate([oa_ref[...], ob_ref[...]], axis=1)
    o_ref[...] = x_ref[...] + gate * jnp.dot(o, w_ref[...], preferred_element_type=F32)


def _even_out(cfg, l, x, oa, ob, mods, w_out):
    n = cfg.n_tok
    tok = lambda w: pl.BlockSpec((TM, w), lambda i: (i, 0))
    return pl.pallas_call(
        functools.partial(_even_out_kernel, cfg=cfg),
        grid=(n // TM,),
        in_specs=[tok(D_MODEL), tok(A_WIDTH), tok(B_WIDTH), _mod_spec(l),
                  _const_spec((A_WIDTH + B_WIDTH, D_MODEL))],
        out_specs=tok(D_MODEL),
        out_shape=jax.ShapeDtypeStruct((n, D_MODEL), F32),
        compiler_params=_cparams(1),
        name=f"even_out{l}",
    )(x, oa, ob, mods, w_out)


def _odd_in_kernel(x_ref, mod_ref, nw_ref, w_ref, o_ref, *, cfg):
    g = _group_of_tile(cfg, pl.program_id(0), TM)
    shift, scale, _ = _mod_rows(mod_ref, g, 0)
    h = _modnorm(x_ref[...], nw_ref[...], shift, scale).astype(BF16)
    z = jnp.dot(h, w_ref[...], preferred_element_type=F32)
    o_ref[:, 0:D_RNN] = jax.nn.gelu(z[:, 0:D_RNN])
    o_ref[:, D_RNN:2 * D_RNN] = z[:, D_RNN:2 * D_RNN]


def _odd_in(cfg, l, x, mods, nw, w):
    n = cfg.n_tok
    return pl.pallas_call(
        functools.partial(_odd_in_kernel, cfg=cfg),
        grid=(n // TM,),
        in_specs=[pl.BlockSpec((TM, D_MODEL), lambda i: (i, 0)), _mod_spec(l), _const_spec((1, D_MODEL)),
                  pl.BlockSpec((D_MODEL, 2 * D_RNN), lambda i: (0, 0), pipeline_mode=pl.Buffered(1))],
        out_specs=pl.BlockSpec((TM, 2 * D_RNN), lambda i: (i, 0)),
        out_shape=jax.ShapeDtypeStruct((n, 2 * D_RNN), F32),
        compiler_params=_cparams(1),
        name=f"odd_in{l}",
    )(x, mods, nw, w)


def _block_position(cfg, ib):
    ncb = cfg.n_ctx // TS
    bpc = cfg.tc // TS
    bpl = cfg.tl // TS
    is_ctx = ib < ncb
    jl = jnp.maximum(ib - ncb, 0)
    pos = jnp.where(is_ctx, ib % bpc, jl % bpl)
    per = jnp.where(is_ctx, bpc, bpl)
    return is_ctx, jl // bpl, pos == 0, pos == per - 1


def _conv_centred(u, prev8, next8, first, last, cw_ref, cb_ref):
    tb = u.shape[0]
    row = lax.broadcasted_iota(jnp.int32, u.shape, 0)
    p2 = jnp.where(first, 0.0, prev8[6:7, :])
    p1 = jnp.where(first, 0.0, prev8[7:8, :])
    n0 = jnp.where(last, 0.0, next8[0:1, :])
    m1 = jnp.where(row == 0, p1, pltpu.roll(u, 1, 0))
    m2 = jnp.where(row == 0, p2, jnp.where(row == 1, p1, pltpu.roll(u, 2, 0)))
    q1 = jnp.where(row == tb - 1, n0, pltpu.roll(u, tb - 1, 0))
    return m2 * cw_ref[0:1, :] + m1 * cw_ref[1:2, :] + u * cw_ref[2:3, :] + q1 * cw_ref[3:4, :] + cb_ref[...]


def _lru_terms(uc, wg_ref, bg_ref, lam_ref):
    ucb = uc.astype(BF16)
    rs, is_ = [], []
    for nblk in range(LRU_BLOCKS):
        z = jnp.dot(ucb[:, nblk * LRU_BLOCK_W:(nblk + 1) * LRU_BLOCK_W], wg_ref[nblk],
                    preferred_element_type=F32)
        rs.append(z[:, 0:LRU_BLOCK_W])
        is_.append(z[:, LRU_BLOCK_W:2 * LRU_BLOCK_W])
    r = jax.nn.sigmoid(jnp.concatenate(rs, axis=1) + bg_ref[0:1, :])
    gi = jax.nn.sigmoid(jnp.concatenate(is_, axis=1) + bg_ref[1:2, :])
    log_a = -LRU_C * r * jax.nn.softplus(-lam_ref[...])
    a = jnp.exp(log_a)
    t = jnp.tanh(log_a)
    v = jnp.sqrt(-2.0 * t / (1.0 - t)) * (gi * uc)
    return a, v


def _lru_scan(a, v, carry, reverse):
    tb = a.shape[0]
    row8 = lax.broadcasted_iota(jnp.int32, a.shape, 0) % SUBLANES
    s = 1
    while s < SUBLANES:
        if reverse:
            m = row8 < SUBLANES - s
            a_sh, v_sh = pltpu.roll(a, tb - s, 0), pltpu.roll(v, tb - s, 0)
        else:
            m = row8 >= s
            a_sh, v_sh = pltpu.roll(a, s, 0), pltpu.roll(v, s, 0)
        v = v + a * jnp.where(m, v_sh, 0.0)
        a = a * jnp.where(m, a_sh, 1.0)
        s *= 2
    ngrp = tb // SUBLANES
    outs = [None] * ngrp
    for gi in (range(ngrp - 1, -1, -1) if reverse else range(ngrp)):
        rs = slice(gi * SUBLANES, (gi + 1) * SUBLANES)
        hg = v[rs] + a[rs] * carry
        outs[gi] = hg
        carry = hg[0:1] if reverse else hg[SUBLANES - 1:SUBLANES]
    return jnp.concatenate(outs, axis=0), carry


def _lru_block(u_ref, up_ref, un_ref, cw_ref, cb_ref, wg_ref, bg_ref, lam_ref, h0_ref, carry_ref,
               cfg, ib, direction):
    is_ctx, jseq, first, last = _block_position(cfg, ib)
    start = last if direction else first

    @pl.when(start)
    def _():
        carry_ref[...] = jnp.where(is_ctx, 0.0, h0_ref[jseq])
    uc = _conv_centred(u_ref[...], up_ref[...], un_ref[...], first, last, cw_ref, cb_ref)
    a, v = _lru_terms(uc, wg_ref, bg_ref, lam_ref)
    y, carry = _lru_scan(a, v, carry_ref[...], bool(direction))
    carry_ref[...] = carry
    return y


def _odd_fwd_kernel(u_ref, up_ref, un_ref, cw_ref, cb_ref, wg_ref, bg_ref, lam_ref, h0_ref,
                    yf_ref, hs_ref, carry_ref, *, cfg):
    y = _lru_block(u_ref, up_ref, un_ref, cw_ref, cb_ref, wg_ref, bg_ref, lam_ref, h0_ref, carry_ref,
                   cfg, pl.program_id(0), 0)
    yf_ref[...] = y
    hs_ref[...] = carry_ref[...]


def _odd_bwd_kernel(x_ref, g_ref, u_ref, up_ref, un_ref, yf_ref, mod_ref, cw_ref, cb_ref, wg_ref, bg_ref,
                    lam_ref, h0_ref, wout_ref, o_ref, hs_ref, carry_ref, *, cfg):
    nblk = pl.num_programs(0)
    ib = nblk - 1 - pl.program_id(0)
    y = _lru_block(u_ref, up_ref, un_ref, cw_ref, cb_ref, wg_ref, bg_ref, lam_ref, h0_ref, carry_ref,
                   cfg, ib, 1)
    hs_ref[...] = carry_ref[...]
    mix = (g_ref[...] * (yf_ref[...] + y)).astype(BF16)
    grp = _group_of_tile(cfg, ib, TS)
    gate = mod_ref[2, pl.ds(grp, 1), :]
    o_ref[...] = x_ref[...] + gate * jnp.dot(mix, wout_ref[...], preferred_element_type=F32)


def _odd_scans(cfg, l, x, gu, mods, cw, cb, wg, bg, lam, h0, w_out):
    n = cfg.n_tok
    nblk = n // TS
    r8 = TS // SUBLANES
    last8 = n // SUBLANES - 1

    def specs(rev):
        ib = (lambda i: nblk - 1 - i) if rev else (lambda i: i)
        cur = lambda col: pl.BlockSpec((TS, D_RNN), lambda i: (ib(i), col))
        prev = pl.BlockSpec((SUBLANES, D_RNN), lambda i: (jnp.maximum(ib(i) * r8 - 1, 0), 1))
        nxt = pl.BlockSpec((SUBLANES, D_RNN), lambda i: (jnp.minimum((ib(i) + 1) * r8, last8), 1))
        hs = pl.BlockSpec((None, 1, D_RNN), lambda i: (ib(i), 0, 0))
        return cur, prev, nxt, hs

    def weights(direction):
        return [_const_spec((CONV_W, D_RNN)), _const_spec((1, D_RNN)),
                pl.BlockSpec((None, LRU_BLOCKS, LRU_BLOCK_W, 2 * LRU_BLOCK_W), lambda i: (direction, 0, 0, 0)),
                pl.BlockSpec((None, 2, D_RNN), lambda i: (direction, 0, 0)),
                pl.BlockSpec((None, 1, D_RNN), lambda i: (direction, 0, 0)),
                pl.BlockSpec((None, cfg.nl, 1, D_RNN), lambda i: (direction, 0, 0, 0))]

    cur, prev, nxt, hs = specs(False)
    y_f, hs_f = pl.pallas_call(
        functools.partial(_odd_fwd_kernel, cfg=cfg),
        grid=(nblk,),
        in_specs=[cur(1), prev, nxt] + weights(0),
        out_specs=[cur(0), hs],
        out_shape=[jax.ShapeDtypeStruct((n, D_RNN), F32), jax.ShapeDtypeStruct((nblk, 1, D_RNN), F32)],
        scratch_shapes=[pltpu.VMEM((1, D_RNN), F32)],
        compiler_params=_cparams(1),
        name=f"lru_fwd{l}",
    )(gu, gu, gu, cw, cb, wg, bg, lam, h0)
    cur, prev, nxt, hs = specs(True)
    x1, hs_b = pl.pallas_call(
        functools.partial(_odd_bwd_kernel, cfg=cfg),
        grid=(nblk,),
        in_specs=[cur(0), cur(0), cur(1), prev, nxt, cur(0), _mod_spec(l)] + weights(1)
                 + [_const_spec((D_RNN, D_MODEL))],
        out_specs=[cur(0), hs],
        out_shape=[jax.ShapeDtypeStruct((n, D_MODEL), F32), jax.ShapeDtypeStruct((nblk, 1, D_RNN), F32)],
        scratch_shapes=[pltpu.VMEM((1, D_RNN), F32)],
        compiler_params=_cparams(1),
        name=f"lru_bwd{l}",
    )(x, gu, gu, gu, gu, y_f, mods, cw, cb, wg, bg, lam, h0, w_out)
    return x1, hs_f, hs_b


def _rope_table(tl):
    pos = np.arange(tl)
    nf = HEAD_DIM // 4
    inv = ROPE_BASE ** (-np.arange(nf, dtype=np.float32) / nf)
    ar = (pos // GRID_W).astype(np.float32)[:, None] * inv
    ac = (pos % GRID_W).astype(np.float32)[:, None] * inv
    ang = np.concatenate([ar, ar, ac, ac] * 2, axis=-1)
    lo = (np.arange(LANES) % (HEAD_DIM // 2)) < nf
    cos, sin = np.cos(ang), np.sin(ang)
    tab = np.concatenate([cos, np.where(lo, -sin, 0.0), np.where(lo, 0.0, sin)], axis=-1)
    return jnp.asarray(tab, F32)


def _dup_heads(w):
    lead = w.shape[:-1]
    w = w.reshape(lead + (A_KV_HEADS, 1, HEAD_DIM))
    return jnp.broadcast_to(w, lead + (A_KV_HEADS, 2, HEAD_DIM)).reshape(lead + (2 * LANES,))


def _pack_even_in(w):
    kv = A_KV_HEADS * HEAD_DIM
    q, k, v = w[:, :A_WIDTH], w[:, A_WIDTH:A_WIDTH + kv], w[:, A_WIDTH + kv:A_WIDTH + 2 * kv]
    rest = w[:, A_WIDTH + 2 * kv:]
    pad = jnp.zeros((w.shape[0], LANES - 2 * GLA_RANK), w.dtype)
    return jnp.concatenate([q, _dup_heads(k), _dup_heads(v), rest, pad], axis=1).astype(BF16)


def _pack_alpha(w_alpha, b_alpha):
    w = jnp.zeros((LANES, 2 * B_QK), F32)
    w = w.at[0:GLA_RANK, 0:B_QK].set(w_alpha[0]).at[GLA_RANK:2 * GLA_RANK, B_QK:].set(w_alpha[1])
    return w.astype(BF16), b_alpha.reshape(1, 2 * B_QK)


def _forward(cfg, x_prompt, x_sample, c, cache_k, cache_v, state_gla, state_lru, c_ctx,
             w_ada, b_ada, norm_mix, norm_ffn, w_in_even, attn_sink, w_alpha, b_alpha, gla_gain,
             w_out_even, w_in_odd, conv_w, conv_b, w_gate_a, b_gate_a, w_gate_x, b_gate_x,
             lru_lambda, w_out_odd, w_ffn_in, w_ffn_out, norm_final):
    assert cfg.tc == TS and cfg.tl % TM == 0 and cfg.n_ctx % TM == 0 and cfg.nl + 1 <= SUBLANES
    n_even = (DEPTH + 1) // 2
    x = jnp.concatenate([x_prompt.reshape(cfg.n_ctx, D_MODEL), x_sample.reshape(cfg.n_lat, D_MODEL)], axis=0)
    cvec = jnp.concatenate([c_ctx[None, :], c, jnp.zeros((SUBLANES - 1 - cfg.nl, D_MODEL), F32)], axis=0)
    mods = _ada(cvec, w_ada, b_ada)
    rope_tab = _rope_table(cfg.tl)
    nf = norm_final.reshape(1, D_MODEL)
    new_k, new_v, new_gla, new_lru = [], [], [], []
    for l in range(DEPTH):
        nw = norm_mix[l].reshape(1, D_MODEL)
        if l % 2 == 0:
            e = l // 2
            za, zb, kvc = _even_in(cfg, l, x, mods, nw, _pack_even_in(w_in_even[e]), rope_tab)
            new_k.append(kvc[:, 0:LANES].reshape(cfg.nc, cfg.tc, A_KV_HEADS, HEAD_DIM))
            new_v.append(kvc[:, LANES:].reshape(cfg.nc, cfg.tc, A_KV_HEADS, HEAD_DIM))
            ck2 = _dup_heads(cache_k[:, e].reshape(cfg.nl, PAST_LEN, A_KV_HEADS * HEAD_DIM)).astype(BF16)
            cv2 = _dup_heads(cache_v[:, e].reshape(cfg.nl, PAST_LEN, A_KV_HEADS * HEAD_DIM)).astype(BF16)
            oa = _attn_ctx(cfg, l, za, attn_sink[e])
            oa = _attn_lat(cfg, l, za, ck2, cv2, attn_sink[e], oa)
            wal, bal = _pack_alpha(w_alpha[e], b_alpha[e])
            gain = gla_gain[e].reshape(1, B_WIDTH)
            ob, s_new = _gla_ctx(cfg, l, zb, wal, bal, gain)
            ob = _gla_lat(cfg, l, zb, wal, bal, gain, state_gla[:, e], ob)
            new_gla.append(s_new)
            x = _even_out(cfg, l, x, oa, ob, mods, w_out_even[e].astype(BF16))
        else:
            o = l // 2
            gu = _odd_in(cfg, l, x, mods, nw, w_in_odd[o].astype(BF16))
            wg = jnp.concatenate([w_gate_a[o], w_gate_x[o]], axis=-1).astype(BF16)
            bg = jnp.stack([b_gate_a[o], b_gate_x[o]], axis=1)
            lam = lru_lambda[o].reshape(2, 1, D_RNN)
            h0 = jnp.transpose(state_lru[:, o], (1, 0, 2)).reshape(2, cfg.nl, 1, D_RNN)
            x, hs_f, hs_b = _odd_scans(cfg, l, x, gu, mods, conv_w[o], conv_b[o].reshape(1, D_RNN),
                                       wg, bg, lam, h0, w_out_odd[o].astype(BF16))
            new_lru.append(jnp.stack([hs_f[:cfg.nc, 0], hs_b[:cfg.nc, 0]], axis=1))
        x = _ffn(cfg, l, x, mods, norm_ffn[l].reshape(1, D_MODEL), w_ffn_in[l].astype(BF16),
                 w_ffn_out[l].astype(BF16), nf, final=(l == DEPTH - 1))
    y_prompt = x[:cfg.n_ctx].reshape(cfg.nc, cfg.tc, D_MODEL)
    y_sample = x[cfg.n_ctx:].reshape(cfg.nl, cfg.tl, D_MODEL)
    return (y_prompt, y_sample, jnp.stack(new_k, axis=1), jnp.stack(new_v, axis=1),
            jnp.stack(new_gla, axis=1), jnp.stack(new_lru, axis=1))


def kernel(x_prompt, x_sample, c, cache_k, cache_v, state_gla, state_lru, c_ctx, w_ada, b_ada, norm_mix, norm_ffn, w_in_even, attn_sink, w_alpha, b_alpha, gla_gain, w_out_even, w_in_odd, conv_w, conv_b, w_gate_a, b_gate_a, w_gate_x, b_gate_x, lru_lambda, w_out_odd, w_ffn_in, w_ffn_out, norm_final):
    cfg = Cfg(nc=x_prompt.shape[0], tc=x_prompt.shape[1], nl=x_sample.shape[0], tl=x_sample.shape[1])
    return _forward(cfg, x_prompt, x_sample, c, cache_k, cache_v, state_gla, state_lru, c_ctx,
                    w_ada, b_ada, norm_mix, norm_ffn, w_in_even, attn_sink, w_alpha, b_alpha, gla_gain,
                    w_out_even, w_in_odd, conv_w, conv_b, w_gate_a, b_gate_a, w_gate_x, b_gate_x,
                    lru_lambda, w_out_odd, w_ffn_in, w_ffn_out, norm_final)
```

```python
import functools
from typing import NamedTuple

import jax
import jax.numpy as jnp
import numpy as np
from jax import lax
from jax.experimental import pallas as pl
from jax.experimental.pallas import tpu as pltpu

F32 = jnp.float32
BF16 = jnp.bfloat16

D_MODEL = 1024
DEPTH = 4
EPS = 1e-6

HEAD_DIM = 64
A_Q_HEADS = 8
A_KV_HEADS = 2
A_GROUPS = A_Q_HEADS // A_KV_HEADS
A_WIDTH = A_Q_HEADS * HEAD_DIM
ATT_BLOCK = 128
ROPE_BASE = 10000.0
GRID_W = 64
NEG_INF = -1e30
PAST_LEN = 256

B_HEADS = 4
B_DK = 64
B_DV = 128
B_QK = B_HEADS * B_DK
B_WIDTH = B_HEADS * B_DV
GLA_RANK = 16
GLA_NORMALIZER = 16.0
GLA_CHUNK = 64

D_RNN = D_MODEL
LRU_BLOCK_W = 256
LRU_BLOCKS = D_RNN // LRU_BLOCK_W
CONV_W = 4
LRU_C = 8.0

D_FF = 2816

LANES = 128
SUBLANES = 8

ZA_W = A_WIDTH + 2 * 2 * LANES
ZB_LR = B_QK + B_QK + B_WIDTH + B_WIDTH
ZB_W = ZB_LR + LANES
P_PACK = ZA_W + ZB_W

TM = 512
TS = 256
FF_CHUNK = 1408

VMEM_LIMIT = 56 * 1024 * 1024


class Cfg(NamedTuple):
    nc: int
    tc: int
    nl: int
    tl: int

    @property
    def n_ctx(self):
        return self.nc * self.tc

    @property
    def n_lat(self):
        return self.nl * self.tl

    @property
    def n_tok(self):
        return self.n_ctx + self.n_lat


def _cparams(n_axes):
    return pltpu.CompilerParams(dimension_semantics=("arbitrary",) * n_axes,
                                vmem_limit_bytes=VMEM_LIMIT)


def _const_spec(shape):
    nd = len(shape)
    return pl.BlockSpec(shape, lambda *_: (0,) * nd)


def _silu(x):
    return x * jax.nn.sigmoid(x)


def _group_of_tile(cfg, i, tile):
    nct = cfg.n_ctx // tile
    per_lat = cfg.tl // tile
    return jnp.where(i < nct, 0, 1 + jnp.maximum(i - nct, 0) // per_lat)


def _modnorm(x, nw, shift, scale):
    ms = jnp.mean(x * x, axis=-1, keepdims=True)
    y = x * lax.rsqrt(ms + EPS) * nw
    return y * (1.0 + scale) + shift


def _mod_rows(mod_ref, g, first):
    return tuple(mod_ref[first + j, pl.ds(g, 1), :] for j in range(3))


def _ada_kernel(c_ref, w_ref, b_ref, o_ref):
    s = _silu(c_ref[...]).astype(BF16)
    o_ref[...] = jnp.dot(s, w_ref[...].astype(BF16), preferred_element_type=F32) + b_ref[...]


def _ada(cvec, w_ada, b_ada):
    b4 = b_ada.reshape(DEPTH, 6, 1, D_MODEL)
    return pl.pallas_call(
        _ada_kernel,
        grid=(DEPTH, 6),
        in_specs=[
            _const_spec((SUBLANES, D_MODEL)),
            pl.BlockSpec((None, D_MODEL, D_MODEL), lambda l, j: (l, 0, j)),
            pl.BlockSpec((None, None, 1, D_MODEL), lambda l, j: (l, j, 0, 0)),
        ],
        out_specs=pl.BlockSpec((None, None, SUBLANES, D_MODEL), lambda l, j: (l, j, 0, 0)),
        out_shape=jax.ShapeDtypeStruct((DEPTH, 6, SUBLANES, D_MODEL), F32),
        compiler_params=_cparams(2),
        name="ada",
    )(cvec, w_ada, b4)


def _mod_spec(l):
    return pl.BlockSpec((None, 6, SUBLANES, D_MODEL), lambda *_: (l, 0, 0, 0))


def _ffn_tile(x, nw, shift, scale, gate, win_ref, wout_ref):
    h = _modnorm(x, nw, shift, scale).astype(BF16)
    acc = None
    for c in range(D_FF // FF_CHUNK):
        lo, hi = c * FF_CHUNK, (c + 1) * FF_CHUNK
        g = jnp.dot(h, win_ref[:, lo:hi], preferred_element_type=F32)
        u = jnp.dot(h, win_ref[:, D_FF + lo:D_FF + hi], preferred_element_type=F32)
        a = (_silu(g) * u).astype(BF16)
        part = jnp.dot(a, wout_ref[lo:hi, :], preferred_element_type=F32)
        acc = part if acc is None else acc + part
    return x + gate * acc


def _x_args(cfg, x):
    if not isinstance(x, tuple):
        return [x], [pl.BlockSpec((TM, D_MODEL), lambda i: (i, 0))]
    nct = cfg.n_ctx // TM
    return list(x), [pl.BlockSpec((TM, D_MODEL), lambda i: (jnp.minimum(i, nct - 1), 0)),
                     pl.BlockSpec((TM, D_MODEL), lambda i: (jnp.maximum(i - nct, 0), 0))]


def _load_x(cfg, x_refs, i):
    if len(x_refs) == 1:
        return x_refs[0][...]
    return jnp.where(i < cfg.n_ctx // TM, x_refs[0][...], x_refs[1][...])


def _to_scan_order(x):
    nb, d = x.shape[0] // TS, x.shape[1]
    return jnp.swapaxes(x.reshape(nb, SUBLANES, TS // SUBLANES, d), 1, 2).reshape(nb * TS, d)


def _from_scan_order(x):
    nb, d = x.shape[0] // TS, x.shape[1]
    return jnp.swapaxes(x.reshape(nb, TS // SUBLANES, SUBLANES, d), 1, 2).reshape(nb * TS, d)


def _ffn_kernel(*refs, cfg, n_x, pre, final):
    x_refs, refs = refs[:n_x], refs[n_x:]
    i = pl.program_id(0)
    g = _group_of_tile(cfg, i, TM)
    x = _load_x(cfg, x_refs, i)
    if pre == "even":
        oa_ref, ob_ref, wmix_ref, refs = refs[0], refs[1], refs[2], refs[3:]
    else:
        d_ref, refs = refs[0], refs[1:]
    mod_ref, nw_ref, win_ref, wout_ref, nf_ref = refs[:5]
    outs = refs[5:]
    if pre == "even":
        o = jnp.concatenate([oa_ref[...], ob_ref[...]], axis=1)
        x = x + mod_ref[2, pl.ds(g, 1), :] * jnp.dot(o, wmix_ref[...], preferred_element_type=F32)
    else:
        x = x + _from_scan_order(d_ref[...])
    shift, scale, gate = _mod_rows(mod_ref, g, 3)
    y = _ffn_tile(x, nw_ref[...], shift, scale, gate, win_ref, wout_ref)
    if not final:
        outs[0][...] = y
        return
    ms = jnp.mean(y * y, axis=-1, keepdims=True)
    y = y * lax.rsqrt(ms + EPS) * nf_ref[...]
    is_ctx = i < cfg.n_ctx // TM

    @pl.when(is_ctx)
    def _():
        outs[0][...] = y

    @pl.when(jnp.logical_not(is_ctx))
    def _():
        outs[1][...] = y


def _ffn(cfg, l, x, pre_args, mods, nw, win, wout, nf, final):
    n = cfg.n_tok
    nct = cfg.n_ctx // TM
    tok = lambda w: pl.BlockSpec((TM, w), lambda i: (i, 0))
    x_ops, x_specs = _x_args(cfg, x)
    if l % 2 == 0:
        pre, pre_specs = "even", [tok(A_WIDTH), tok(B_WIDTH), _const_spec((A_WIDTH + B_WIDTH, D_MODEL))]
    else:
        pre, pre_specs = "odd", [tok(D_MODEL)]
    if final:
        out_specs = [pl.BlockSpec((TM, D_MODEL), lambda i: (jnp.minimum(i, nct - 1), 0)),
                     pl.BlockSpec((TM, D_MODEL), lambda i: (jnp.maximum(i - nct, 0), 0))]
        out_shape = [jax.ShapeDtypeStruct((cfg.n_ctx, D_MODEL), F32),
                     jax.ShapeDtypeStruct((cfg.n_lat, D_MODEL), F32)]
    else:
        out_specs, out_shape = tok(D_MODEL), jax.ShapeDtypeStruct((n, D_MODEL), F32)
    return pl.pallas_call(
        functools.partial(_ffn_kernel, cfg=cfg, n_x=len(x_ops), pre=pre, final=final),
        grid=(n // TM,),
        in_specs=x_specs + pre_specs + [
            _mod_spec(l), _const_spec((1, D_MODEL)),
            pl.BlockSpec((D_MODEL, 2 * D_FF), lambda i: (0, 0), pipeline_mode=pl.Buffered(1)),
            pl.BlockSpec((D_FF, D_MODEL), lambda i: (0, 0), pipeline_mode=pl.Buffered(1)),
            _const_spec((1, D_MODEL))],
        out_specs=out_specs,
        out_shape=out_shape,
        compiler_params=_cparams(1),
        name=f"ffn{l}",
    )(*x_ops, *pre_args, mods, nw, win, wout, nf)


def _rope(x, tab_ref):
    cos, sin_a, sin_b = tab_ref[:, 0:LANES], tab_ref[:, LANES:2 * LANES], tab_ref[:, 2 * LANES:3 * LANES]
    parts = []
    for j in range(x.shape[1] // LANES):
        xs = x[:, j * LANES:(j + 1) * LANES]
        parts.append(xs * cos + pltpu.roll(xs, LANES - 16, 1) * sin_a + pltpu.roll(xs, 16, 1) * sin_b)
    return jnp.concatenate(parts, axis=1)


def _even_in_kernel(*refs, cfg, n_x):
    x_refs, (mod_ref, nw_ref, w_ref, tab_ref, za_ref, zb_ref, k_ref, v_ref) = refs[:n_x], refs[n_x:]
    i = pl.program_id(0)
    g = _group_of_tile(cfg, i, TM)
    shift, scale, _ = _mod_rows(mod_ref, g, 0)
    h = _modnorm(_load_x(cfg, x_refs, i), nw_ref[...], shift, scale).astype(BF16)
    z = jnp.dot(h, w_ref[...], preferred_element_type=F32)
    q = z[:, 0:A_WIDTH] * (HEAD_DIM ** -0.5)
    k2 = z[:, A_WIDTH:A_WIDTH + 2 * LANES]
    v2 = z[:, A_WIDTH + 2 * LANES:ZA_W]
    za_ref[:, A_WIDTH + 2 * LANES:ZA_W] = v2.astype(BF16)
    zb_ref[:, 0:B_QK] = z[:, ZA_W:ZA_W + B_QK] * (B_DK ** -0.5)
    zb_ref[:, B_QK:ZB_W] = z[:, ZA_W + B_QK:P_PACK]

    @pl.when(g == 0)
    def _():
        za_ref[:, 0:A_WIDTH] = q.astype(BF16)
        za_ref[:, A_WIDTH:A_WIDTH + 2 * LANES] = k2.astype(BF16)
        lo = lax.broadcasted_iota(jnp.int32, (TM, LANES), 1) < HEAD_DIM
        k_ref[...] = jnp.where(lo, k2[:, 0:LANES], k2[:, LANES:2 * LANES])
        v_ref[...] = jnp.where(lo, v2[:, 0:LANES], v2[:, LANES:2 * LANES])

    @pl.when(g != 0)
    def _():
        za_ref[:, 0:A_WIDTH] = _rope(q, tab_ref).astype(BF16)
        za_ref[:, A_WIDTH:A_WIDTH + 2 * LANES] = _rope(k2, tab_ref).astype(BF16)


def _even_in(cfg, l, x, mods, nw, w_pack, rope_tab):
    n = cfg.n_tok
    nct = cfg.n_ctx // TM
    per_lat = cfg.tl // TM
    tok = lambda w: pl.BlockSpec((TM, w), lambda i: (i, 0))
    x_ops, x_specs = _x_args(cfg, x)
    ctx_kv = pl.BlockSpec((TM, LANES), lambda i: (jnp.minimum(i, nct - 1), 0))
    return pl.pallas_call(
        functools.partial(_even_in_kernel, cfg=cfg, n_x=len(x_ops)),
        grid=(n // TM,),
        in_specs=x_specs + [_mod_spec(l), _const_spec((1, D_MODEL)),
                            pl.BlockSpec((D_MODEL, P_PACK), lambda i: (0, 0), pipeline_mode=pl.Buffered(1)),
                            pl.BlockSpec((TM, 3 * LANES), lambda i: (jnp.maximum(i - nct, 0) % per_lat, 0))],
        out_specs=[tok(ZA_W), tok(ZB_W), ctx_kv, ctx_kv],
        out_shape=[jax.ShapeDtypeStruct((n, ZA_W), BF16),
                   jax.ShapeDtypeStruct((n, ZB_W), F32),
                   jax.ShapeDtypeStruct((cfg.n_ctx, LANES), F32),
                   jax.ShapeDtypeStruct((cfg.n_ctx, LANES), F32)],
        compiler_params=_cparams(1),
        name=f"even_in{l}",
    )(*x_ops, mods, nw, w_pack, rope_tab)


def _sink_attention(q_ref, k_all, v_all, mask, sink_ref, o_ref, kvh):
    tq = q_ref.shape[0]
    lo = lax.broadcasted_iota(jnp.int32, (tq, LANES), 1) < HEAD_DIM
    qs = []
    for gq in range(A_GROUPS):
        h = kvh * A_GROUPS + gq
        pair = q_ref[:, (h // 2) * LANES:(h // 2 + 1) * LANES]
        keep = lo if h % 2 == 0 else jnp.logical_not(lo)
        qs.append(jnp.where(keep, pair, jnp.zeros_like(pair)))
    qst = jnp.concatenate(qs, axis=0)
    s = lax.dot_general(qst, k_all, (((1,), (1,)), ((), ())), preferred_element_type=F32)
    if mask is not None:
        s = jnp.where(mask, s, NEG_INF)
    rows = lax.broadcasted_iota(jnp.int32, (A_GROUPS * tq, 1), 0) // tq
    sink = jnp.zeros((A_GROUPS * tq, 1), F32)
    for gq in range(A_GROUPS):
        sink = jnp.where(rows == gq, sink_ref[kvh * A_GROUPS + gq], sink)
    m = jnp.maximum(jnp.max(s, axis=-1, keepdims=True), sink)
    p = jnp.exp(s - m)
    den = jnp.sum(p, axis=-1, keepdims=True) + jnp.exp(sink - m)
    o = jnp.dot(p.astype(BF16), v_all, preferred_element_type=F32) / den
    for pr in range(2):
        even, odd = o[(2 * pr) * tq:(2 * pr + 1) * tq], o[(2 * pr + 1) * tq:(2 * pr + 2) * tq]
        col = (kvh * 2 + pr) * LANES
        o_ref[:, col:col + LANES] = jnp.where(lo, even, odd).astype(o_ref.dtype)


def _attn_ctx_kernel(sink_ref, q_ref, k_ref, v_ref, o_ref):
    for kvh in range(A_KV_HEADS):
        sl = slice(kvh * LANES, (kvh + 1) * LANES)
        _sink_attention(q_ref, k_ref[:, sl], v_ref[:, sl], None, sink_ref, o_ref, kvh)


def _attn_ctx(cfg, l, za, sink):
    tc = cfg.tc
    return pl.pallas_call(
        _attn_ctx_kernel,
        grid=(cfg.nc,),
        in_specs=[pl.BlockSpec(memory_space=pltpu.SMEM),
                  pl.BlockSpec((tc, A_WIDTH), lambda b: (b, 0)),
                  pl.BlockSpec((tc, 2 * LANES), lambda b: (b, A_WIDTH // (2 * LANES))),
                  pl.BlockSpec((tc, 2 * LANES), lambda b: (b, A_WIDTH // (2 * LANES) + 1))],
        out_specs=pl.BlockSpec((tc, A_WIDTH), lambda b: (b, 0)),
        out_shape=jax.ShapeDtypeStruct((cfg.n_tok, A_WIDTH), BF16),
        compiler_params=_cparams(1),
        name=f"attn_ctx{l}",
    )(sink, za, za, za)


def _attn_lat_kernel(sink_ref, q_ref, kp_ref, kc_ref, kn_ref, vp_ref, vc_ref, vn_ref,
                     ck_ref, cv_ref, oa_in_ref, o_ref):
    del oa_in_ref
    n = pl.program_id(1)
    nb = pl.num_programs(1)
    shape = (A_GROUPS * ATT_BLOCK, PAST_LEN + 3 * ATT_BLOCK)
    r = lax.broadcasted_iota(jnp.int32, shape, 0) % ATT_BLOCK
    c = lax.broadcasted_iota(jnp.int32, shape, 1) - PAST_LEN
    lo_thr = r + jnp.where(n == 0, ATT_BLOCK, 0)
    hi_thr = r - jnp.where(n == nb - 1, ATT_BLOCK, 0)
    bad_prev = jnp.logical_and(jnp.logical_and(c >= 0, c < ATT_BLOCK), c < lo_thr)
    bad_next = jnp.logical_and(c >= 2 * ATT_BLOCK, c - 2 * ATT_BLOCK > hi_thr)
    mask = jnp.logical_not(jnp.logical_or(bad_prev, bad_next))
    for kvh in range(A_KV_HEADS):
        sl = slice(kvh * LANES, (kvh + 1) * LANES)
        k_all = jnp.concatenate([ck_ref[:, sl], kp_ref[:, sl], kc_ref[:, sl], kn_ref[:, sl]], axis=0)
        v_all = jnp.concatenate([cv_ref[:, sl], vp_ref[:, sl], vc_ref[:, sl], vn_ref[:, sl]], axis=0)
        _sink_attention(q_ref, k_all, v_all, mask, sink_ref, o_ref, kvh)


def _attn_lat(cfg, l, za, cache_k2, cache_v2, sink, oa):
    nbl = cfg.tl // ATT_BLOCK
    base = cfg.n_ctx // ATT_BLOCK
    kcol = A_WIDTH // (2 * LANES)

    def blk(delta, col, width):
        def imap(b, n):
            return (base + b * nbl + jnp.clip(n + delta, 0, nbl - 1), col)
        return pl.BlockSpec((ATT_BLOCK, width), imap)

    cache = pl.BlockSpec((None, PAST_LEN, 2 * LANES), lambda b, n: (b, 0, 0))
    return pl.pallas_call(
        _attn_lat_kernel,
        grid=(cfg.nl, nbl),
        in_specs=[pl.BlockSpec(memory_space=pltpu.SMEM),
                  blk(0, 0, A_WIDTH),
                  blk(-1, kcol, 2 * LANES), blk(0, kcol, 2 * LANES), blk(1, kcol, 2 * LANES),
                  blk(-1, kcol + 1, 2 * LANES), blk(0, kcol + 1, 2 * LANES), blk(1, kcol + 1, 2 * LANES),
                  cache, cache,
                  pl.BlockSpec(memory_space=pl.ANY)],
        out_specs=blk(0, 0, A_WIDTH),
        out_shape=jax.ShapeDtypeStruct((cfg.n_tok, A_WIDTH), BF16),
        input_output_aliases={10: 0},
        compiler_params=_cparams(2),
        name=f"attn_lat{l}",
    )(sink, za, za, za, za, za, za, za, cache_k2, cache_v2, oa)


def _gla_chunk(q, k, v, la, st, reverse):
    cn = GLA_CHUNK
    row = lax.broadcasted_iota(jnp.int32, (cn, B_QK), 0)
    lane = lax.broadcasted_iota(jnp.int32, (cn, B_QK), 1)
    b = la
    s = 1
    while s < cn:
        if reverse:
            b = b + jnp.where(row < cn - s, pltpu.roll(b, cn - s, 0), 0.0)
        else:
            b = b + jnp.where(row >= s, pltpu.roll(b, s, 0), 0.0)
        s *= 2
    b_last = b[0:1, :] if reverse else b[cn - 1:cn, :]
    q_e = (q * jnp.exp(b)).astype(BF16)
    k_e = (k * jnp.exp(-b)).astype(BF16)
    k_s = (k * jnp.exp(b_last - b)).astype(BF16)
    vb = v.astype(BF16)
    zero = jnp.zeros_like(k_e)
    kbd = jnp.concatenate([jnp.where(lane // B_DK == h, k_e, zero) for h in range(B_HEADS)], axis=0)
    attn = lax.dot_general(q_e, kbd, (((1,), (1,)), ((), ())), preferred_element_type=F32)
    col = lane % cn
    tri = (col >= row) if reverse else (col <= row)
    attn = jnp.where(tri, attn, 0.0).astype(BF16)
    lane_v = lax.broadcasted_iota(jnp.int32, (cn, B_WIDTH), 1)
    zero_v = jnp.zeros_like(vb)
    vbd = jnp.concatenate([jnp.where(lane_v // B_DV == h, vb, zero_v) for h in range(B_HEADS)], axis=0)
    o = jnp.dot(attn, vbd, preferred_element_type=F32)
    o = o + lax.dot_general(q_e, st.astype(BF16), (((1,), (1,)), ((), ())), preferred_element_type=F32)
    upd = lax.dot_general(vb, k_s, (((0,), (0,)), ((), ())), preferred_element_type=F32)
    own = (lax.broadcasted_iota(jnp.int32, (B_WIDTH, B_QK), 0) // B_DV
           == lax.broadcasted_iota(jnp.int32, (B_WIDTH, B_QK), 1) // B_DK)
    st_new = st * jnp.exp(b_last) + jnp.where(own, upd, 0.0)
    return o, st_new


def _gla_log_decay(lr_pad, wal_ref, bal_ref, direction):
    z = jnp.dot(lr_pad.astype(BF16), wal_ref[:, direction * B_QK:(direction + 1) * B_QK],
                preferred_element_type=F32) + bal_ref[:, direction * B_QK:(direction + 1) * B_QK]
    return jax.nn.log_sigmoid(z) / GLA_NORMALIZER


def _gla_block(qkv_ref, lr_ref, wal_ref, bal_ref, st, direction):
    nchunk = qkv_ref.shape[0] // GLA_CHUNK
    la_all = _gla_log_decay(lr_ref[...], wal_ref, bal_ref, direction)
    outs = [None] * nchunk
    for ci in (range(nchunk - 1, -1, -1) if direction else range(nchunk)):
        rs = slice(ci * GLA_CHUNK, (ci + 1) * GLA_CHUNK)
        outs[ci], st = _gla_chunk(qkv_ref[rs, 0:B_QK], qkv_ref[rs, B_QK:2 * B_QK],
                                  qkv_ref[rs, 2 * B_QK:2 * B_QK + B_WIDTH], la_all[rs, :],
                                  st, bool(direction))
    return jnp.concatenate(outs, axis=0), st


def _state_to_blockdiag_t(s_ref):
    rows = []
    for h in range(B_HEADS):
        parts = [jnp.zeros((B_DK, B_DV), F32)] * B_HEADS
        parts[h] = s_ref[h]
        rows.append(jnp.concatenate(parts, axis=1))
    return jnp.concatenate(rows, axis=0).T


def _blockdiag_t_to_state(st, s_ref):
    s = st.T
    for h in range(B_HEADS):
        s_ref[h] = s[h * B_DK:(h + 1) * B_DK, h * B_DV:(h + 1) * B_DV]


def _gla_finish(o, r, gain):
    parts = []
    for h in range(B_HEADS):
        oh = o[:, h * B_DV:(h + 1) * B_DV]
        parts.append(oh * lax.rsqrt(jnp.mean(oh * oh, axis=-1, keepdims=True) + EPS))
    return (jnp.concatenate(parts, axis=1) * gain * _silu(r)).astype(BF16)


def _gla_ctx_kernel(qkv_ref, r_ref, lr_ref, wal_ref, bal_ref, gain_ref, ob_ref, snew_ref):
    zero = jnp.zeros((B_WIDTH, B_QK), F32)
    o_f, st_f = _gla_block(qkv_ref, lr_ref, wal_ref, bal_ref, zero, 0)
    o_b, st_b = _gla_block(qkv_ref, lr_ref, wal_ref, bal_ref, zero, 1)
    _blockdiag_t_to_state(st_f, snew_ref.at[0])
    _blockdiag_t_to_state(st_b, snew_ref.at[1])
    ob_ref[...] = _gla_finish(o_f + o_b, r_ref[...], gain_ref[...])


def _gla_ctx(cfg, l, zb, wal, bal, gain):
    tc = cfg.tc
    return pl.pallas_call(
        _gla_ctx_kernel,
        grid=(cfg.nc,),
        in_specs=[pl.BlockSpec((tc, 2 * B_QK + B_WIDTH), lambda b: (b, 0)),
                  pl.BlockSpec((tc, B_WIDTH), lambda b: (b, (2 * B_QK + B_WIDTH) // B_WIDTH)),
                  pl.BlockSpec((tc, LANES), lambda b: (b, ZB_LR // LANES)),
                  _const_spec((LANES, 2 * B_QK)), _const_spec((1, 2 * B_QK)), _const_spec((1, B_WIDTH))],
        out_specs=[pl.BlockSpec((tc, B_WIDTH), lambda b: (b, 0)),
                   pl.BlockSpec((None, 2, B_HEADS, B_DK, B_DV), lambda b: (b, 0, 0, 0, 0))],
        out_shape=[jax.ShapeDtypeStruct((cfg.n_tok, B_WIDTH), BF16),
                   jax.ShapeDtypeStruct((cfg.nc, 2, B_HEADS, B_DK, B_DV), F32)],
        compiler_params=_cparams(1),
        name=f"gla_ctx{l}",
    )(zb, zb, zb, wal, bal, gain)


def _gla_lat_kernel(qkvf_ref, lrf_ref, rf_ref, qkvb_ref, lrb_ref, rb_ref, wal_ref, bal_ref, gain_ref,
                    s0_ref, ob_in_ref, ob_ref, stf_ref, stb_ref, of_ref, obk_ref):
    del ob_in_ref
    n = pl.program_id(1)
    nb = pl.num_programs(1)

    @pl.when(n == 0)
    def _():
        stf_ref[...] = _state_to_blockdiag_t(s0_ref.at[0])
        stb_ref[...] = _state_to_blockdiag_t(s0_ref.at[1])
    o_f, st_f = _gla_block(qkvf_ref, lrf_ref, wal_ref, bal_ref, stf_ref[...], 0)
    o_b, st_b = _gla_block(qkvb_ref, lrb_ref, wal_ref, bal_ref, stb_ref[...], 1)
    stf_ref[...] = st_f
    stb_ref[...] = st_b
    rows_f = pl.ds(pl.multiple_of(n * TS, TS), TS)
    rows_b = pl.ds(pl.multiple_of((nb - 1 - n) * TS, TS), TS)
    of_ref[rows_f, :] = o_f
    obk_ref[rows_b, :] = o_b

    @pl.when(2 * n >= nb)
    def _():
        ob_ref[rows_f, :] = _gla_finish(o_f + obk_ref[rows_f, :], rf_ref[...], gain_ref[...])
        ob_ref[rows_b, :] = _gla_finish(of_ref[rows_b, :] + o_b, rb_ref[...], gain_ref[...])


def _gla_lat(cfg, l, zb, wal, bal, gain, s0, ob):
    nbl = cfg.tl // TS
    base = cfg.n_ctx // TS
    assert nbl % 2 == 0 and cfg.n_ctx % cfg.tl == 0

    def blk(width, col, rev):
        def imap(b, n):
            return (base + b * nbl + (nbl - 1 - n if rev else n), col)
        return pl.BlockSpec((TS, width), imap)

    qkv_w = 2 * B_QK + B_WIDTH
    per_dir = lambda rev: [blk(qkv_w, 0, rev), blk(LANES, ZB_LR // LANES, rev), blk(B_WIDTH, qkv_w // B_WIDTH, rev)]
    return pl.pallas_call(
        _gla_lat_kernel,
        grid=(cfg.nl, nbl),
        in_specs=per_dir(False) + per_dir(True) + [
            _const_spec((LANES, 2 * B_QK)), _const_spec((1, 2 * B_QK)), _const_spec((1, B_WIDTH)),
            pl.BlockSpec((None, 2, B_HEADS, B_DK, B_DV), lambda b, n: (b, 0, 0, 0, 0)),
            pl.BlockSpec(memory_space=pl.ANY)],
        out_specs=pl.BlockSpec((cfg.tl, B_WIDTH), lambda b, n: (cfg.n_ctx // cfg.tl + b, 0)),
        out_shape=jax.ShapeDtypeStruct((cfg.n_tok, B_WIDTH), BF16),
        scratch_shapes=[pltpu.VMEM((B_WIDTH, B_QK), F32), pltpu.VMEM((B_WIDTH, B_QK), F32),
                        pltpu.VMEM((cfg.tl, B_WIDTH), F32), pltpu.VMEM((cfg.tl, B_WIDTH), F32)],
        input_output_aliases={10: 0},
        compiler_params=_cparams(2),
        name=f"gla_lat{l}",
    )(zb, zb, zb, zb, zb, zb, wal, bal, gain, s0, ob)


LRU_CHUNK = TS // SUBLANES


def _odd_in_kernel(x_ref, mod_ref, nw_ref, w_ref, o_ref, *, cfg):
    g = _group_of_tile(cfg, pl.program_id(0), TM)
    shift, scale, _ = _mod_rows(mod_ref, g, 0)
    h = _modnorm(_to_scan_order(x_ref[...]), nw_ref[...], shift, scale).astype(BF16)
    z = jnp.dot(h, w_ref[...], preferred_element_type=F32)
    o_ref[:, 0:D_RNN] = jax.nn.gelu(z[:, 0:D_RNN])
    o_ref[:, D_RNN:2 * D_RNN] = z[:, D_RNN:2 * D_RNN]


def _odd_in(cfg, l, x, mods, nw, w):
    n = cfg.n_tok
    return pl.pallas_call(
        functools.partial(_odd_in_kernel, cfg=cfg),
        grid=(n // TM,),
        in_specs=[pl.BlockSpec((TM, D_MODEL), lambda i: (i, 0)), _mod_spec(l), _const_spec((1, D_MODEL)),
                  pl.BlockSpec((D_MODEL, 2 * D_RNN), lambda i: (0, 0), pipeline_mode=pl.Buffered(1))],
        out_specs=pl.BlockSpec((TM, 2 * D_RNN), lambda i: (i, 0)),
        out_shape=jax.ShapeDtypeStruct((n, 2 * D_RNN), F32),
        compiler_params=_cparams(1),
        name=f"odd_in{l}",
    )(x, mods, nw, w)


def _block_position(cfg, ib):
    ncb = cfg.n_ctx // TS
    bpc = cfg.tc // TS
    bpl = cfg.tl // TS
    is_ctx = ib < ncb
    jl = jnp.maximum(ib - ncb, 0)
    pos = jnp.where(is_ctx, ib % bpc, jl % bpl)
    per = jnp.where(is_ctx, bpc, bpl)
    return is_ctx, jl // bpl, pos == 0, pos == per - 1


def _conv_centred(u, prev16, next8, first, last, cw_ref, cb_ref):
    sub = lax.broadcasted_iota(jnp.int32, (SUBLANES, D_RNN), 0)
    lastrow = TS - SUBLANES

    def from_prev_chunk(own, other):
        return jnp.where(sub == 0, jnp.where(first, 0.0, pltpu.roll(other, 1, 0)), pltpu.roll(own, 1, 0))

    s30 = from_prev_chunk(u[lastrow - SUBLANES:lastrow], prev16[0:SUBLANES])
    s31 = from_prev_chunk(u[lastrow:TS], prev16[SUBLANES:2 * SUBLANES])
    n0 = jnp.where(sub == SUBLANES - 1, jnp.where(last, 0.0, pltpu.roll(next8, SUBLANES - 1, 0)),
                   pltpu.roll(u[0:SUBLANES], SUBLANES - 1, 0))
    m2 = jnp.concatenate([s30, s31, u[0:lastrow - SUBLANES]], axis=0)
    m1 = jnp.concatenate([s31, u[0:lastrow]], axis=0)
    p1 = jnp.concatenate([u[SUBLANES:TS], n0], axis=0)
    return m2 * cw_ref[0:1, :] + m1 * cw_ref[1:2, :] + u * cw_ref[2:3, :] + p1 * cw_ref[3:4, :] + cb_ref[...]


def _lru_terms(uc, wg_ref, bg_ref, lam_ref):
    ucb = uc.astype(BF16)
    rs, is_ = [], []
    for nblk in range(LRU_BLOCKS):
        z = jnp.dot(ucb[:, nblk * LRU_BLOCK_W:(nblk + 1) * LRU_BLOCK_W], wg_ref[nblk],
                    preferred_element_type=F32)
        rs.append(z[:, 0:LRU_BLOCK_W])
        is_.append(z[:, LRU_BLOCK_W:2 * LRU_BLOCK_W])
    tr = jnp.tanh(0.5 * (jnp.concatenate(rs, axis=1) + bg_ref[0:1, :]))
    ti = jnp.tanh(0.5 * (jnp.concatenate(is_, axis=1) + bg_ref[1:2, :]))
    k = (-0.5 * LRU_C) * jax.nn.softplus(-lam_ref[...])
    log_a = k + k * tr
    a = jnp.exp(log_a)
    t = jnp.tanh(log_a)
    w = -2.0 * t / (1.0 - t)
    root = jnp.where(w > 0.0, w * lax.rsqrt(w), 0.0)
    v = root * ((0.5 + 0.5 * ti) * uc)
    return a, v


def _lru_scan(a, v, carry, reverse):
    order = range(LRU_CHUNK - 1, -1, -1) if reverse else range(LRU_CHUNK)
    h = jnp.zeros((SUBLANES, D_RNN), F32)
    p = jnp.ones((SUBLANES, D_RNN), F32)
    hs, ps = [None] * LRU_CHUNK, [None] * LRU_CHUNK
    for j in order:
        aj = a[j * SUBLANES:(j + 1) * SUBLANES]
        h = aj * h + v[j * SUBLANES:(j + 1) * SUBLANES]
        p = aj * p
        hs[j], ps[j] = h, p
    entry = [None] * SUBLANES
    for c in (range(SUBLANES - 1, -1, -1) if reverse else range(SUBLANES)):
        entry[c] = carry
        carry = h[c:c + 1] + p[c:c + 1] * carry
    hm = jnp.concatenate(entry, axis=0)
    return jnp.concatenate([hs[j] + ps[j] * hm for j in range(LRU_CHUNK)], axis=0), carry


def _lru_block(u_ref, up_ref, un_ref, cw_ref, cb_ref, wg_ref, bg_ref, lam_ref, h0_ref, carry_ref,
               cfg, ib, direction):
    is_ctx, jseq, first, last = _block_position(cfg, ib)
    start = last if direction else first

    @pl.when(start)
    def _():
        carry_ref[...] = jnp.where(is_ctx, 0.0, h0_ref[jseq])
    uc = _conv_centred(u_ref[...], up_ref[...], un_ref[...], first, last, cw_ref, cb_ref)
    a, v = _lru_terms(uc, wg_ref, bg_ref, lam_ref)
    y, carry = _lru_scan(a, v, carry_ref[...], bool(direction))
    carry_ref[...] = carry
    return y


def _odd_fwd_kernel(u_ref, up_ref, un_ref, cw_ref, cb_ref, wg_ref, bg_ref, lam_ref, h0_ref,
                    yf_ref, hs_ref, carry_ref, *, cfg):
    y = _lru_block(u_ref, up_ref, un_ref, cw_ref, cb_ref, wg_ref, bg_ref, lam_ref, h0_ref, carry_ref,
                   cfg, pl.program_id(0), 0)
    yf_ref[...] = y
    hs_ref[...] = carry_ref[...]


def _odd_bwd_kernel(g_ref, u_ref, up_ref, un_ref, yf_ref, mod_ref, cw_ref, cb_ref, wg_ref, bg_ref,
                    lam_ref, h0_ref, wout_ref, o_ref, hs_ref, carry_ref, *, cfg):
    nblk = pl.num_programs(0)
    ib = nblk - 1 - pl.program_id(0)
    y = _lru_block(u_ref, up_ref, un_ref, cw_ref, cb_ref, wg_ref, bg_ref, lam_ref, h0_ref, carry_ref,
                   cfg, ib, 1)
    hs_ref[...] = carry_ref[...]
    mix = (g_ref[...] * (yf_ref[...] + y)).astype(BF16)
    grp = _group_of_tile(cfg, ib, TS)
    gate = mod_ref[2, pl.ds(grp, 1), :]
    o_ref[...] = gate * jnp.dot(mix, wout_ref[...], preferred_element_type=F32)


def _odd_scans(cfg, l, gu, mods, cw, cb, wg, bg, lam, h0, w_out):
    n = cfg.n_tok
    nblk = n // TS
    r8 = TS // SUBLANES
    last8 = n // SUBLANES - 1

    def specs(rev):
        ib = (lambda i: nblk - 1 - i) if rev else (lambda i: i)
        cur = lambda col: pl.BlockSpec((TS, D_RNN), lambda i: (ib(i), col))
        prev = pl.BlockSpec((2 * SUBLANES, D_RNN), lambda i: (jnp.maximum(ib(i) * (r8 // 2) - 1, 0), 1))
        nxt = pl.BlockSpec((SUBLANES, D_RNN), lambda i: (jnp.minimum((ib(i) + 1) * r8, last8), 1))
        hs = pl.BlockSpec((None, 1, D_RNN), lambda i: (ib(i), 0, 0))
        return cur, prev, nxt, hs

    def weights(direction):
        return [_const_spec((CONV_W, D_RNN)), _const_spec((1, D_RNN)),
                pl.BlockSpec((None, LRU_BLOCKS, LRU_BLOCK_W, 2 * LRU_BLOCK_W), lambda i: (direction, 0, 0, 0)),
                pl.BlockSpec((None, 2, D_RNN), lambda i: (direction, 0, 0)),
                pl.BlockSpec((None, 1, D_RNN), lambda i: (direction, 0, 0)),
                pl.BlockSpec((None, cfg.nl, 1, D_RNN), lambda i: (direction, 0, 0, 0))]

    cur, prev, nxt, hs = specs(False)
    y_f, hs_f = pl.pallas_call(
        functools.partial(_odd_fwd_kernel, cfg=cfg),
        grid=(nblk,),
        in_specs=[cur(1), prev, nxt] + weights(0),
        out_specs=[cur(0), hs],
        out_shape=[jax.ShapeDtypeStruct((n, D_RNN), F32), jax.ShapeDtypeStruct((nblk, 1, D_RNN), F32)],
        scratch_shapes=[pltpu.VMEM((1, D_RNN), F32)],
        compiler_params=_cparams(1),
        name=f"lru_fwd{l}",
    )(gu, gu, gu, cw, cb, wg, bg, lam, h0)
    cur, prev, nxt, hs = specs(True)
    upd, hs_b = pl.pallas_call(
        functools.partial(_odd_bwd_kernel, cfg=cfg),
        grid=(nblk,),
        in_specs=[cur(0), cur(1), prev, nxt, cur(0), _mod_spec(l)] + weights(1)
                 + [_const_spec((D_RNN, D_MODEL))],
        out_specs=[cur(0), hs],
        out_shape=[jax.ShapeDtypeStruct((n, D_MODEL), F32), jax.ShapeDtypeStruct((nblk, 1, D_RNN), F32)],
        scratch_shapes=[pltpu.VMEM((1, D_RNN), F32)],
        compiler_params=_cparams(1),
        name=f"lru_bwd{l}",
    )(gu, gu, gu, gu, y_f, mods, cw, cb, wg, bg, lam, h0, w_out)
    return upd, hs_f, hs_b


def _rope_table(tl):
    pos = np.arange(tl)
    nf = HEAD_DIM // 4
    inv = ROPE_BASE ** (-np.arange(nf, dtype=np.float32) / nf)
    ar = (pos // GRID_W).astype(np.float32)[:, None] * inv
    ac = (pos % GRID_W).astype(np.float32)[:, None] * inv
    ang = np.concatenate([ar, ar, ac, ac] * 2, axis=-1)
    lo = (np.arange(LANES) % (HEAD_DIM // 2)) < nf
    cos, sin = np.cos(ang), np.sin(ang)
    tab = np.concatenate([cos, np.where(lo, -sin, 0.0), np.where(lo, 0.0, sin)], axis=-1)
    return jnp.asarray(tab, F32)


def _dup_heads(w):
    lead = w.shape[:-1]
    w = w.reshape(lead + (A_KV_HEADS, 1, HEAD_DIM))
    return jnp.broadcast_to(w, lead + (A_KV_HEADS, 2, HEAD_DIM)).reshape(lead + (2 * LANES,))


def _pack_even_in(w):
    kv = A_KV_HEADS * HEAD_DIM
    q, k, v = w[:, :A_WIDTH], w[:, A_WIDTH:A_WIDTH + kv], w[:, A_WIDTH + kv:A_WIDTH + 2 * kv]
    rest = w[:, A_WIDTH + 2 * kv:]
    pad = jnp.zeros((w.shape[0], LANES - 2 * GLA_RANK), w.dtype)
    return jnp.concatenate([q, _dup_heads(k), _dup_heads(v), rest, pad], axis=1).astype(BF16)


def _pack_alpha(w_alpha, b_alpha):
    w = jnp.zeros((LANES, 2 * B_QK), F32)
    w = w.at[0:GLA_RANK, 0:B_QK].set(w_alpha[0]).at[GLA_RANK:2 * GLA_RANK, B_QK:].set(w_alpha[1])
    return w.astype(BF16), b_alpha.reshape(1, 2 * B_QK)


def _forward(cfg, x_prompt, x_sample, c, cache_k, cache_v, state_gla, state_lru, c_ctx,
             w_ada, b_ada, norm_mix, norm_ffn, w_in_even, attn_sink, w_alpha, b_alpha, gla_gain,
             w_out_even, w_in_odd, conv_w, conv_b, w_gate_a, b_gate_a, w_gate_x, b_gate_x,
             lru_lambda, w_out_odd, w_ffn_in, w_ffn_out, norm_final):
    assert cfg.tc == TS and cfg.tl % TM == 0 and cfg.n_ctx % TM == 0 and cfg.nl + 1 <= SUBLANES
    x = (x_prompt.reshape(cfg.n_ctx, D_MODEL), x_sample.reshape(cfg.n_lat, D_MODEL))
    cvec = jnp.concatenate([c_ctx[None, :], c, jnp.zeros((SUBLANES - 1 - cfg.nl, D_MODEL), F32)], axis=0)
    mods = _ada(cvec, w_ada, b_ada)
    rope_tab = _rope_table(cfg.tl)
    nf = norm_final.reshape(1, D_MODEL)
    new_k, new_v, new_gla, new_lru = [], [], [], []
    for l in range(DEPTH):
        nw = norm_mix[l].reshape(1, D_MODEL)
        if l % 2 == 0:
            e = l // 2
            za, zb, k_ctx, v_ctx = _even_in(cfg, l, x, mods, nw, _pack_even_in(w_in_even[e]), rope_tab)
            new_k.append(k_ctx.reshape(cfg.nc, cfg.tc, A_KV_HEADS, HEAD_DIM))
            new_v.append(v_ctx.reshape(cfg.nc, cfg.tc, A_KV_HEADS, HEAD_DIM))
            ck2 = _dup_heads(cache_k[:, e].reshape(cfg.nl, PAST_LEN, A_KV_HEADS * HEAD_DIM)).astype(BF16)
            cv2 = _dup_heads(cache_v[:, e].reshape(cfg.nl, PAST_LEN, A_KV_HEADS * HEAD_DIM)).astype(BF16)
            oa = _attn_ctx(cfg, l, za, attn_sink[e])
            oa = _attn_lat(cfg, l, za, ck2, cv2, attn_sink[e], oa)
            wal, bal = _pack_alpha(w_alpha[e], b_alpha[e])
            gain = gla_gain[e].reshape(1, B_WIDTH)
            ob, s_new = _gla_ctx(cfg, l, zb, wal, bal, gain)
            ob = _gla_lat(cfg, l, zb, wal, bal, gain, state_gla[:, e], ob)
            new_gla.append(s_new)
            pre_args = (oa, ob, w_out_even[e].astype(BF16))
        else:
            o = l // 2
            gu = _odd_in(cfg, l, x, mods, nw, w_in_odd[o].astype(BF16))
            wg = jnp.concatenate([w_gate_a[o], w_gate_x[o]], axis=-1).astype(BF16)
            bg = jnp.stack([b_gate_a[o], b_gate_x[o]], axis=1)
            lam = lru_lambda[o].reshape(2, 1, D_RNN)
            h0 = jnp.transpose(state_lru[:, o], (1, 0, 2)).reshape(2, cfg.nl, 1, D_RNN)
            upd, hs_f, hs_b = _odd_scans(cfg, l, gu, mods, conv_w[o], conv_b[o].reshape(1, D_RNN),
                                         wg, bg, lam, h0, w_out_odd[o].astype(BF16))
            new_lru.append(jnp.stack([hs_f[:cfg.nc, 0], hs_b[:cfg.nc, 0]], axis=1))
            pre_args = (upd,)
        x = _ffn(cfg, l, x, pre_args, mods, norm_ffn[l].reshape(1, D_MODEL), w_ffn_in[l].astype(BF16),
                 w_ffn_out[l].astype(BF16), nf, final=(l == DEPTH - 1))
    y_prompt = x[0].reshape(cfg.nc, cfg.tc, D_MODEL)
    y_sample = x[1].reshape(cfg.nl, cfg.tl, D_MODEL)
    return (y_prompt, y_sample, jnp.stack(new_k, axis=1), jnp.stack(new_v, axis=1),
            jnp.stack(new_gla, axis=1), jnp.stack(new_lru, axis=1))


def kernel(x_prompt, x_sample, c, cache_k, cache_v, state_gla, state_lru, c_ctx, w_ada, b_ada, norm_mix, norm_ffn, w_in_even, attn_sink, w_alpha, b_alpha, gla_gain, w_out_even, w_in_odd, conv_w, conv_b, w_gate_a, b_gate_a, w_gate_x, b_gate_x, lru_lambda, w_out_odd, w_ffn_in, w_ffn_out, norm_final):
    cfg = Cfg(nc=x_prompt.shape[0], tc=x_prompt.shape[1], nl=x_sample.shape[0], tl=x_sample.shape[1])
    return _forward(cfg, x_prompt, x_sample, c, cache_k, cache_v, state_gla, state_lru, c_ctx,
                    w_ada, b_ada, norm_mix, norm_ffn, w_in_even, attn_sink, w_alpha, b_alpha, gla_gain,
                    w_out_even, w_in_odd, conv_w, conv_b, w_gate_a, b_gate_a, w_gate_x, b_gate_x,
                    lru_lambda, w_out_odd, w_ffn_in, w_ffn_out, norm_final)
```

```python
import functools
from typing import NamedTuple

import jax
import jax.numpy as jnp
import numpy as np
from jax import lax
from jax.experimental import pallas as pl
from jax.experimental.pallas import tpu as pltpu

F32 = jnp.float32
BF16 = jnp.bfloat16

D_MODEL = 1024
DEPTH = 4
EPS = 1e-6

HEAD_DIM = 64
A_Q_HEADS = 8
A_KV_HEADS = 2
A_GROUPS = A_Q_HEADS // A_KV_HEADS
A_WIDTH = A_Q_HEADS * HEAD_DIM
ATT_BLOCK = 128
ROPE_BASE = 10000.0
GRID_W = 64
NEG_INF = -1e30
PAST_LEN = 256

B_HEADS = 4
B_DK = 64
B_DV = 128
B_QK = B_HEADS * B_DK
B_WIDTH = B_HEADS * B_DV
GLA_RANK = 16
GLA_NORMALIZER = 16.0
GLA_CHUNK = 64

D_RNN = D_MODEL
LRU_BLOCK_W = 256
LRU_BLOCKS = D_RNN // LRU_BLOCK_W
CONV_W = 4
LRU_C = 8.0

D_FF = 2816

LANES = 128
SUBLANES = 8

ZA_W = A_WIDTH + 2 * 2 * LANES
ZB_LR = B_QK + B_QK + B_WIDTH + B_WIDTH
ZB_W = ZB_LR + LANES
P_PACK = ZA_W + ZB_W

TM = 512
TS = 256
FF_CHUNK = 256

VMEM_LIMIT = 56 * 1024 * 1024


class Cfg(NamedTuple):
    nc: int
    tc: int
    nl: int
    tl: int

    @property
    def n_ctx(self):
        return self.nc * self.tc

    @property
    def n_lat(self):
        return self.nl * self.tl

    @property
    def n_tok(self):
        return self.n_ctx + self.n_lat


def _cparams(n_axes):
    return pltpu.CompilerParams(dimension_semantics=("arbitrary",) * n_axes,
                                vmem_limit_bytes=VMEM_LIMIT)


def _const_spec(shape):
    nd = len(shape)
    return pl.BlockSpec(shape, lambda *_: (0,) * nd)


def _silu(x):
    return x * jax.nn.sigmoid(x)


def _group_of_tile(cfg, i, tile):
    nct = cfg.n_ctx // tile
    per_lat = cfg.tl // tile
    return jnp.where(i < nct, 0, 1 + jnp.maximum(i - nct, 0) // per_lat)


def _modnorm(x, nw, shift, scale):
    ms = jnp.mean(x * x, axis=-1, keepdims=True)
    y = x * lax.rsqrt(ms + EPS) * nw
    return y * (1.0 + scale) + shift


def _mod_rows(mod_ref, g, first):
    return tuple(mod_ref[first + j, pl.ds(g, 1), :] for j in range(3))


def _ada_kernel(c_ref, w_ref, b_ref, o_ref):
    s = _silu(c_ref[...]).astype(BF16)
    o_ref[...] = jnp.dot(s, w_ref[...].astype(BF16), preferred_element_type=F32) + b_ref[...]


def _ada(cvec, w_ada, b_ada):
    b4 = b_ada.reshape(DEPTH, 6, 1, D_MODEL)
    return pl.pallas_call(
        _ada_kernel,
        grid=(DEPTH, 6),
        in_specs=[
            _const_spec((SUBLANES, D_MODEL)),
            pl.BlockSpec((None, D_MODEL, D_MODEL), lambda l, j: (l, 0, j)),
            pl.BlockSpec((None, None, 1, D_MODEL), lambda l, j: (l, j, 0, 0)),
        ],
        out_specs=pl.BlockSpec((None, None, SUBLANES, D_MODEL), lambda l, j: (l, j, 0, 0)),
        out_shape=jax.ShapeDtypeStruct((DEPTH, 6, SUBLANES, D_MODEL), F32),
        compiler_params=_cparams(2),
        name="ada",
    )(cvec, w_ada, b4)


def _mod_spec(l):
    return pl.BlockSpec((None, 6, SUBLANES, D_MODEL), lambda *_: (l, 0, 0, 0))


def _ffn_tile(x, nw, shift, scale, gate, win_ref, wout_ref):
    h = _modnorm(x, nw, shift, scale).astype(BF16)
    acc = None
    for c in range(D_FF // FF_CHUNK):
        lo, hi = c * FF_CHUNK, (c + 1) * FF_CHUNK
        g = jnp.dot(h, win_ref[:, lo:hi], preferred_element_type=F32)
        u = jnp.dot(h, win_ref[:, D_FF + lo:D_FF + hi], preferred_element_type=F32)
        a = (_silu(g) * u).astype(BF16)
        part = jnp.dot(a, wout_ref[lo:hi, :], preferred_element_type=F32)
        acc = part if acc is None else acc + part
    return x + gate * acc


def _x_args(cfg, x):
    if not isinstance(x, tuple):
        return [x], [pl.BlockSpec((TM, D_MODEL), lambda i: (i, 0))]
    nct = cfg.n_ctx // TM
    return list(x), [pl.BlockSpec((TM, D_MODEL), lambda i: (jnp.minimum(i, nct - 1), 0)),
                     pl.BlockSpec((TM, D_MODEL), lambda i: (jnp.maximum(i - nct, 0), 0))]


def _load_x(cfg, x_refs, i):
    if len(x_refs) == 1:
        return x_refs[0][...]
    return jnp.where(i < cfg.n_ctx // TM, x_refs[0][...], x_refs[1][...])


def _to_scan_order(x):
    nb, d = x.shape[0] // TS, x.shape[1]
    return jnp.swapaxes(x.reshape(nb, SUBLANES, TS // SUBLANES, d), 1, 2).reshape(nb * TS, d)


def _from_scan_order(x):
    nb, d = x.shape[0] // TS, x.shape[1]
    return jnp.swapaxes(x.reshape(nb, TS // SUBLANES, SUBLANES, d), 1, 2).reshape(nb * TS, d)


def _ffn_kernel(*refs, cfg, n_x, pre, final):
    x_refs, refs = refs[:n_x], refs[n_x:]
    i = pl.program_id(0)
    g = _group_of_tile(cfg, i, TM)
    x = _load_x(cfg, x_refs, i)
    if pre == "even":
        (oac_ref, oal_ref, obc_ref, obl_ref, wmix_ref), refs = refs[:5], refs[5:]
    else:
        d_ref, refs = refs[0], refs[1:]
    mod_ref, nw_ref, win_ref, wout_ref, nf_ref = refs[:5]
    outs = refs[5:]
    if pre == "even":
        is_ctx_tile = i < cfg.n_ctx // TM
        o = jnp.concatenate([jnp.where(is_ctx_tile, oac_ref[...], oal_ref[...]),
                             jnp.where(is_ctx_tile, obc_ref[...], obl_ref[...])], axis=1)
        x = x + mod_ref[2, pl.ds(g, 1), :] * jnp.dot(o, wmix_ref[...], preferred_element_type=F32)
    else:
        x = x + _from_scan_order(d_ref[...])
    shift, scale, gate = _mod_rows(mod_ref, g, 3)
    y = _ffn_tile(x, nw_ref[...], shift, scale, gate, win_ref, wout_ref)
    if not final:
        outs[0][...] = y
        return
    ms = jnp.mean(y * y, axis=-1, keepdims=True)
    y = y * lax.rsqrt(ms + EPS) * nf_ref[...]
    is_ctx = i < cfg.n_ctx // TM

    @pl.when(is_ctx)
    def _():
        outs[0][...] = y

    @pl.when(jnp.logical_not(is_ctx))
    def _():
        outs[1][...] = y


def _ffn(cfg, l, x, pre_args, mods, nw, win, wout, nf, final):
    n = cfg.n_tok
    nct = cfg.n_ctx // TM
    tok = lambda w: pl.BlockSpec((TM, w), lambda i: (i, 0))
    ctx = lambda w: pl.BlockSpec((TM, w), lambda i: (jnp.minimum(i, nct - 1), 0))
    lat = lambda w: pl.BlockSpec((TM, w), lambda i: (jnp.maximum(i - nct, 0), 0))
    x_ops, x_specs = _x_args(cfg, x)
    if l % 2 == 0:
        pre = "even"
        pre_specs = [ctx(A_WIDTH), lat(A_WIDTH), ctx(B_WIDTH), lat(B_WIDTH),
                     pl.BlockSpec((None, A_WIDTH + B_WIDTH, D_MODEL), lambda i: (l // 2, 0, 0))]
    else:
        pre, pre_specs = "odd", [tok(D_MODEL)]
    if final:
        out_specs = [pl.BlockSpec((TM, D_MODEL), lambda i: (jnp.minimum(i, nct - 1), 0)),
                     pl.BlockSpec((TM, D_MODEL), lambda i: (jnp.maximum(i - nct, 0), 0))]
        out_shape = [jax.ShapeDtypeStruct((cfg.n_ctx, D_MODEL), F32),
                     jax.ShapeDtypeStruct((cfg.n_lat, D_MODEL), F32)]
    else:
        out_specs, out_shape = tok(D_MODEL), jax.ShapeDtypeStruct((n, D_MODEL), F32)
    return pl.pallas_call(
        functools.partial(_ffn_kernel, cfg=cfg, n_x=len(x_ops), pre=pre, final=final),
        grid=(n // TM,),
        in_specs=x_specs + pre_specs + [
            _mod_spec(l), _const_spec((1, D_MODEL)),
            pl.BlockSpec((None, D_MODEL, 2 * D_FF), lambda i: (l, 0, 0), pipeline_mode=pl.Buffered(1)),
            pl.BlockSpec((None, D_FF, D_MODEL), lambda i: (l, 0, 0), pipeline_mode=pl.Buffered(1)),
            _const_spec((1, D_MODEL))],
        out_specs=out_specs,
        out_shape=out_shape,
        compiler_params=_cparams(1),
        name=f"ffn{l}",
    )(*x_ops, *pre_args, mods, nw, win, wout, nf)


def _rope(x, tab_ref):
    cos, sin_a, sin_b = tab_ref[:, 0:LANES], tab_ref[:, LANES:2 * LANES], tab_ref[:, 2 * LANES:3 * LANES]
    parts = []
    for j in range(x.shape[1] // LANES):
        xs = x[:, j * LANES:(j + 1) * LANES]
        parts.append(xs * cos + pltpu.roll(xs, LANES - 16, 1) * sin_a + pltpu.roll(xs, 16, 1) * sin_b)
    return jnp.concatenate(parts, axis=1)


def _even_in_kernel(*refs, cfg, n_x):
    x_refs, (mod_ref, nw_ref, w_ref, tab_ref, za_ref, zb_ref, k_ref, v_ref) = refs[:n_x], refs[n_x:]
    i = pl.program_id(0)
    g = _group_of_tile(cfg, i, TM)
    shift, scale, _ = _mod_rows(mod_ref, g, 0)
    h = _modnorm(_load_x(cfg, x_refs, i), nw_ref[...], shift, scale).astype(BF16)
    z = jnp.dot(h, w_ref[...], preferred_element_type=F32)
    q = z[:, 0:A_WIDTH] * (HEAD_DIM ** -0.5)
    k2 = z[:, A_WIDTH:A_WIDTH + 2 * LANES]
    v2 = z[:, A_WIDTH + 2 * LANES:ZA_W]
    za_ref[:, A_WIDTH + 2 * LANES:ZA_W] = v2.astype(BF16)
    zb_ref[:, 0:B_QK] = z[:, ZA_W:ZA_W + B_QK] * (B_DK ** -0.5)
    zb_ref[:, B_QK:ZB_W] = z[:, ZA_W + B_QK:P_PACK]

    @pl.when(g == 0)
    def _():
        za_ref[:, 0:A_WIDTH] = q.astype(BF16)
        za_ref[:, A_WIDTH:A_WIDTH + 2 * LANES] = k2.astype(BF16)
        lo = lax.broadcasted_iota(jnp.int32, (TM, LANES), 1) < HEAD_DIM
        k_ref[...] = jnp.where(lo, k2[:, 0:LANES], k2[:, LANES:2 * LANES])
        v_ref[...] = jnp.where(lo, v2[:, 0:LANES], v2[:, LANES:2 * LANES])

    @pl.when(g != 0)
    def _():
        za_ref[:, 0:A_WIDTH] = _rope(q, tab_ref).astype(BF16)
        za_ref[:, A_WIDTH:A_WIDTH + 2 * LANES] = _rope(k2, tab_ref).astype(BF16)


def _even_in(cfg, l, x, mods, nw, w_pack, rope_tab):
    n = cfg.n_tok
    nct = cfg.n_ctx // TM
    per_lat = cfg.tl // TM
    tok = lambda w: pl.BlockSpec((TM, w), lambda i: (i, 0))
    x_ops, x_specs = _x_args(cfg, x)
    ctx_kv = pl.BlockSpec((TM, LANES), lambda i: (jnp.minimum(i, nct - 1), 0))
    return pl.pallas_call(
        functools.partial(_even_in_kernel, cfg=cfg, n_x=len(x_ops)),
        grid=(n // TM,),
        in_specs=x_specs + [_mod_spec(l), _const_spec((1, D_MODEL)),
                            pl.BlockSpec((D_MODEL, P_PACK), lambda i: (0, 0), pipeline_mode=pl.Buffered(1)),
                            pl.BlockSpec((TM, 3 * LANES), lambda i: (jnp.maximum(i - nct, 0) % per_lat, 0))],
        out_specs=[tok(ZA_W), tok(ZB_W), ctx_kv, ctx_kv],
        out_shape=[jax.ShapeDtypeStruct((n, ZA_W), BF16),
                   jax.ShapeDtypeStruct((n, ZB_W), F32),
                   jax.ShapeDtypeStruct((cfg.n_ctx, LANES), F32),
                   jax.ShapeDtypeStruct((cfg.n_ctx, LANES), F32)],
        compiler_params=_cparams(1),
        name=f"even_in{l}",
    )(*x_ops, mods, nw, w_pack, rope_tab)


def _sink_attention(q, k_all, v_all, mask, sink_ref, kvh):
    tq = q.shape[0]
    lo = lax.broadcasted_iota(jnp.int32, (tq, LANES), 1) < HEAD_DIM
    qs = []
    for gq in range(A_GROUPS):
        h = kvh * A_GROUPS + gq
        pair = q[:, (h // 2) * LANES:(h // 2 + 1) * LANES]
        keep = lo if h % 2 == 0 else jnp.logical_not(lo)
        qs.append(jnp.where(keep, pair, jnp.zeros_like(pair)))
    qst = jnp.concatenate(qs, axis=0)
    s = lax.dot_general(qst, k_all, (((1,), (1,)), ((), ())), preferred_element_type=F32)
    if mask is not None:
        s = jnp.where(mask, s, NEG_INF)
    rows = lax.broadcasted_iota(jnp.int32, (A_GROUPS * tq, 1), 0) // tq
    sink = jnp.zeros((A_GROUPS * tq, 1), F32)
    for gq in range(A_GROUPS):
        sink = jnp.where(rows == gq, sink_ref[kvh * A_GROUPS + gq], sink)
    m = jnp.maximum(jnp.max(s, axis=-1, keepdims=True), sink)
    p = jnp.exp(s - m)
    den = jnp.sum(p, axis=-1, keepdims=True) + jnp.exp(sink - m)
    o = jnp.dot(p.astype(BF16), v_all, preferred_element_type=F32) / den
    pairs = [jnp.where(lo, o[(2 * pr) * tq:(2 * pr + 1) * tq], o[(2 * pr + 1) * tq:(2 * pr + 2) * tq])
             for pr in range(2)]
    return jnp.concatenate(pairs, axis=1).astype(BF16)


def _attn_ctx_kernel(sink_ref, q_ref, k_ref, v_ref, o_ref):
    q = q_ref[...]
    for kvh in range(A_KV_HEADS):
        sl = slice(kvh * LANES, (kvh + 1) * LANES)
        o_ref[:, 2 * kvh * LANES:2 * (kvh + 1) * LANES] = _sink_attention(
            q, k_ref[:, sl], v_ref[:, sl], None, sink_ref, kvh)


def _attn_ctx(cfg, l, za, sink):
    tc = cfg.tc
    return pl.pallas_call(
        _attn_ctx_kernel,
        grid=(cfg.nc,),
        in_specs=[pl.BlockSpec(memory_space=pltpu.SMEM),
                  pl.BlockSpec((tc, A_WIDTH), lambda b: (b, 0)),
                  pl.BlockSpec((tc, 2 * LANES), lambda b: (b, A_WIDTH // (2 * LANES))),
                  pl.BlockSpec((tc, 2 * LANES), lambda b: (b, A_WIDTH // (2 * LANES) + 1))],
        out_specs=pl.BlockSpec((tc, A_WIDTH), lambda b: (b, 0)),
        out_shape=jax.ShapeDtypeStruct((cfg.n_ctx, A_WIDTH), BF16),
        compiler_params=_cparams(1),
        name=f"attn_ctx{l}",
    )(sink, za, za, za)


ATT_STEP = 2 * ATT_BLOCK
ATT_WIN = 3 * ATT_BLOCK


def _attn_lat_kernel(sink_ref, q_ref, k_ref, v_ref, ck_ref, cv_ref, o_ref):
    n = pl.program_id(1)
    tl = k_ref.shape[0]
    shape = (A_GROUPS * ATT_BLOCK, PAST_LEN + ATT_WIN)
    r = lax.broadcasted_iota(jnp.int32, shape, 0) % ATT_BLOCK
    c = lax.broadcasted_iota(jnp.int32, shape, 1) - PAST_LEN
    for j in range(ATT_STEP // ATT_BLOCK):
        q0 = (n * (ATT_STEP // ATT_BLOCK) + j) * ATT_BLOCK
        start = pl.multiple_of(jnp.clip(q0 - ATT_BLOCK, 0, tl - ATT_WIN), ATT_BLOCK)
        dist = (q0 - start) + r - c
        mask = jnp.logical_or(c < 0, jnp.abs(dist) <= ATT_BLOCK)
        q = q_ref[j * ATT_BLOCK:(j + 1) * ATT_BLOCK, :]
        for kvh in range(A_KV_HEADS):
            sl = slice(kvh * LANES, (kvh + 1) * LANES)
            k_all = jnp.concatenate([ck_ref[:, sl], k_ref[pl.ds(start, ATT_WIN), sl]], axis=0)
            v_all = jnp.concatenate([cv_ref[:, sl], v_ref[pl.ds(start, ATT_WIN), sl]], axis=0)
            o_ref[j * ATT_BLOCK:(j + 1) * ATT_BLOCK, 2 * kvh * LANES:2 * (kvh + 1) * LANES] = _sink_attention(
                q, k_all, v_all, mask, sink_ref, kvh)


def _attn_lat(cfg, l, za, cache_k2, cache_v2, sink):
    assert cfg.n_ctx % cfg.tl == 0 and cfg.tl % ATT_STEP == 0 and cfg.tl >= ATT_WIN
    steps = cfg.tl // ATT_STEP
    base = cfg.n_ctx // ATT_STEP
    kcol = A_WIDTH // (2 * LANES)
    seq = lambda col: pl.BlockSpec((cfg.tl, 2 * LANES), lambda b, n: (cfg.n_ctx // cfg.tl + b, col))
    cache = pl.BlockSpec((None, PAST_LEN, 2 * LANES), lambda b, n: (b, 0, 0))
    return pl.pallas_call(
        _attn_lat_kernel,
        grid=(cfg.nl, steps),
        in_specs=[pl.BlockSpec(memory_space=pltpu.SMEM),
                  pl.BlockSpec((ATT_STEP, A_WIDTH), lambda b, n: (base + b * steps + n, 0)),
                  seq(kcol), seq(kcol + 1), cache, cache],
        out_specs=pl.BlockSpec((ATT_STEP, A_WIDTH), lambda b, n: (b * steps + n, 0)),
        out_shape=jax.ShapeDtypeStruct((cfg.n_lat, A_WIDTH), BF16),
        compiler_params=_cparams(2),
        name=f"attn_lat{l}",
    )(sink, za, za, za, cache_k2, cache_v2)


def _gla_chunk(q, k, v, la, st, reverse):
    cn = GLA_CHUNK
    row = lax.broadcasted_iota(jnp.int32, (cn, B_QK), 0)
    lane = lax.broadcasted_iota(jnp.int32, (cn, B_QK), 1)
    b = la
    s = 1
    while s < cn:
        if reverse:
            b = b + jnp.where(row < cn - s, pltpu.roll(b, cn - s, 0), 0.0)
        else:
            b = b + jnp.where(row >= s, pltpu.roll(b, s, 0), 0.0)
        s *= 2
    b_last = b[0:1, :] if reverse else b[cn - 1:cn, :]
    q_e = (q * jnp.exp(b)).astype(BF16)
    k_e = (k * jnp.exp(-b)).astype(BF16)
    k_s = (k * jnp.exp(b_last - b)).astype(BF16)
    vb = v.astype(BF16)
    zero = jnp.zeros_like(k_e)
    kbd = jnp.concatenate([jnp.where(lane // B_DK == h, k_e, zero) for h in range(B_HEADS)], axis=0)
    attn = lax.dot_general(q_e, kbd, (((1,), (1,)), ((), ())), preferred_element_type=F32)
    col = lane % cn
    tri = (col >= row) if reverse else (col <= row)
    attn = jnp.where(tri, attn, 0.0).astype(BF16)
    zero_v = jnp.zeros((cn, B_DV), BF16)
    vbd = jnp.concatenate(
        [jnp.concatenate([vb[:, h * B_DV:(h + 1) * B_DV] if g == h else zero_v for g in range(B_HEADS)], axis=1)
         for h in range(B_HEADS)], axis=0)
    o = jnp.dot(attn, vbd, preferred_element_type=F32)
    o = o + lax.dot_general(q_e, st.astype(BF16), (((1,), (1,)), ((), ())), preferred_element_type=F32)
    upd = lax.dot_general(vb, k_s, (((0,), (0,)), ((), ())), preferred_element_type=F32)
    own = (lax.broadcasted_iota(jnp.int32, (B_WIDTH, B_QK), 0) // B_DV
           == lax.broadcasted_iota(jnp.int32, (B_WIDTH, B_QK), 1) // B_DK)
    st_new = st * jnp.exp(b_last) + jnp.where(own, upd, 0.0)
    return o, st_new


def _gla_log_decay(lr_pad, wal_ref, bal_ref, direction):
    z = jnp.dot(lr_pad.astype(BF16), wal_ref[:, direction * B_QK:(direction + 1) * B_QK],
                preferred_element_type=F32) + bal_ref[:, direction * B_QK:(direction + 1) * B_QK]
    return jax.nn.log_sigmoid(z) / GLA_NORMALIZER


def _gla_block(qkv_ref, lr_ref, wal_ref, bal_ref, st, direction):
    nchunk = qkv_ref.shape[0] // GLA_CHUNK
    la_all = _gla_log_decay(lr_ref[...], wal_ref, bal_ref, direction)
    outs = [None] * nchunk
    for ci in (range(nchunk - 1, -1, -1) if direction else range(nchunk)):
        rs = slice(ci * GLA_CHUNK, (ci + 1) * GLA_CHUNK)
        outs[ci], st = _gla_chunk(qkv_ref[rs, 0:B_QK], qkv_ref[rs, B_QK:2 * B_QK],
                                  qkv_ref[rs, 2 * B_QK:2 * B_QK + B_WIDTH], la_all[rs, :],
                                  st, bool(direction))
    return jnp.concatenate(outs, axis=0), st


def _state_to_blockdiag_t(s_ref):
    rows = []
    for h in range(B_HEADS):
        parts = [jnp.zeros((B_DK, B_DV), F32)] * B_HEADS
        parts[h] = s_ref[h]
        rows.append(jnp.concatenate(parts, axis=1))
    return jnp.concatenate(rows, axis=0).T


def _blockdiag_t_to_state(st, s_ref):
    s = st.T
    for h in range(B_HEADS):
        s_ref[h] = s[h * B_DK:(h + 1) * B_DK, h * B_DV:(h + 1) * B_DV]


def _gla_finish(o, r, gain):
    parts = []
    for h in range(B_HEADS):
        oh = o[:, h * B_DV:(h + 1) * B_DV]
        parts.append(oh * lax.rsqrt(jnp.mean(oh * oh, axis=-1, keepdims=True) + EPS))
    return (jnp.concatenate(parts, axis=1) * gain * _silu(r)).astype(BF16)


def _gla_ctx_kernel(qkv_ref, r_ref, lr_ref, wal_ref, bal_ref, gain_ref, ob_ref, snew_ref):
    zero = jnp.zeros((B_WIDTH, B_QK), F32)
    o_f, st_f = _gla_block(qkv_ref, lr_ref, wal_ref, bal_ref, zero, 0)
    o_b, st_b = _gla_block(qkv_ref, lr_ref, wal_ref, bal_ref, zero, 1)
    _blockdiag_t_to_state(st_f, snew_ref.at[0])
    _blockdiag_t_to_state(st_b, snew_ref.at[1])
    ob_ref[...] = _gla_finish(o_f + o_b, r_ref[...], gain_ref[...])


def _gla_ctx(cfg, l, zb, wal, bal, gain):
    tc = cfg.tc
    return pl.pallas_call(
        _gla_ctx_kernel,
        grid=(cfg.nc,),
        in_specs=[pl.BlockSpec((tc, 2 * B_QK + B_WIDTH), lambda b: (b, 0)),
                  pl.BlockSpec((tc, B_WIDTH), lambda b: (b, (2 * B_QK + B_WIDTH) // B_WIDTH)),
                  pl.BlockSpec((tc, LANES), lambda b: (b, ZB_LR // LANES)),
                  _const_spec((LANES, 2 * B_QK)), _const_spec((1, 2 * B_QK)), _const_spec((1, B_WIDTH))],
        out_specs=[pl.BlockSpec((tc, B_WIDTH), lambda b: (b, 0)),
                   pl.BlockSpec((None, 2, B_HEADS, B_DK, B_DV), lambda b: (b, 0, 0, 0, 0))],
        out_shape=[jax.ShapeDtypeStruct((cfg.n_ctx, B_WIDTH), BF16),
                   jax.ShapeDtypeStruct((cfg.nc, 2, B_HEADS, B_DK, B_DV), F32)],
        compiler_params=_cparams(1),
        name=f"gla_ctx{l}",
    )(zb, zb, zb, wal, bal, gain)


def _gla_lat_kernel(qkvf_ref, lrf_ref, rf_ref, qkvb_ref, lrb_ref, rb_ref, wal_ref, bal_ref, gain_ref,
                    s0_ref, ob_ref, stf_ref, stb_ref, of_ref, obk_ref):
    n = pl.program_id(1)
    nb = pl.num_programs(1)

    @pl.when(n == 0)
    def _():
        stf_ref[...] = _state_to_blockdiag_t(s0_ref.at[0])
        stb_ref[...] = _state_to_blockdiag_t(s0_ref.at[1])
    o_f, st_f = _gla_block(qkvf_ref, lrf_ref, wal_ref, bal_ref, stf_ref[...], 0)
    o_b, st_b = _gla_block(qkvb_ref, lrb_ref, wal_ref, bal_ref, stb_ref[...], 1)
    stf_ref[...] = st_f
    stb_ref[...] = st_b
    rows_f = pl.ds(pl.multiple_of(n * TS, TS), TS)
    rows_b = pl.ds(pl.multiple_of((nb - 1 - n) * TS, TS), TS)
    of_ref[rows_f, :] = o_f
    obk_ref[rows_b, :] = o_b

    @pl.when(2 * n >= nb)
    def _():
        ob_ref[rows_f, :] = _gla_finish(o_f + obk_ref[rows_f, :], rf_ref[...], gain_ref[...])
        ob_ref[rows_b, :] = _gla_finish(of_ref[rows_b, :] + o_b, rb_ref[...], gain_ref[...])


def _gla_lat(cfg, l, zb, wal, bal, gain, s0):
    nbl = cfg.tl // TS
    base = cfg.n_ctx // TS
    assert nbl % 2 == 0

    def blk(width, col, rev):
        def imap(b, n):
            return (base + b * nbl + (nbl - 1 - n if rev else n), col)
        return pl.BlockSpec((TS, width), imap)

    qkv_w = 2 * B_QK + B_WIDTH
    per_dir = lambda rev: [blk(qkv_w, 0, rev), blk(LANES, ZB_LR // LANES, rev), blk(B_WIDTH, qkv_w // B_WIDTH, rev)]
    return pl.pallas_call(
        _gla_lat_kernel,
        grid=(cfg.nl, nbl),
        in_specs=per_dir(False) + per_dir(True) + [
            _const_spec((LANES, 2 * B_QK)), _const_spec((1, 2 * B_QK)), _const_spec((1, B_WIDTH)),
            pl.BlockSpec((None, 2, B_HEADS, B_DK, B_DV), lambda b, n: (b, 0, 0, 0, 0))],
        out_specs=pl.BlockSpec((cfg.tl, B_WIDTH), lambda b, n: (b, 0)),
        out_shape=jax.ShapeDtypeStruct((cfg.n_lat, B_WIDTH), BF16),
        scratch_shapes=[pltpu.VMEM((B_WIDTH, B_QK), F32), pltpu.VMEM((B_WIDTH, B_QK), F32),
                        pltpu.VMEM((cfg.tl, B_WIDTH), F32), pltpu.VMEM((cfg.tl, B_WIDTH), F32)],
        compiler_params=_cparams(2),
        name=f"gla_lat{l}",
    )(zb, zb, zb, zb, zb, zb, wal, bal, gain, s0)


LRU_CHUNK = TS // SUBLANES


def _odd_in_kernel(x_ref, mod_ref, nw_ref, w_ref, o_ref, *, cfg):
    g = _group_of_tile(cfg, pl.program_id(0), TM)
    shift, scale, _ = _mod_rows(mod_ref, g, 0)
    h = _modnorm(_to_scan_order(x_ref[...]), nw_ref[...], shift, scale).astype(BF16)
    z = jnp.dot(h, w_ref[...], preferred_element_type=F32)
    o_ref[:, 0:D_RNN] = jax.nn.gelu(z[:, 0:D_RNN])
    o_ref[:, D_RNN:2 * D_RNN] = z[:, D_RNN:2 * D_RNN]


def _odd_in(cfg, l, x, mods, nw, w):
    n = cfg.n_tok
    return pl.pallas_call(
        functools.partial(_odd_in_kernel, cfg=cfg),
        grid=(n // TM,),
        in_specs=[pl.BlockSpec((TM, D_MODEL), lambda i: (i, 0)), _mod_spec(l), _const_spec((1, D_MODEL)),
                  pl.BlockSpec((D_MODEL, 2 * D_RNN), lambda i: (0, 0), pipeline_mode=pl.Buffered(1))],
        out_specs=pl.BlockSpec((TM, 2 * D_RNN), lambda i: (i, 0)),
        out_shape=jax.ShapeDtypeStruct((n, 2 * D_RNN), F32),
        compiler_params=_cparams(1),
        name=f"odd_in{l}",
    )(x, mods, nw, w)


def _block_position(cfg, ib):
    ncb = cfg.n_ctx // TS
    bpc = cfg.tc // TS
    bpl = cfg.tl // TS
    is_ctx = ib < ncb
    jl = jnp.maximum(ib - ncb, 0)
    pos = jnp.where(is_ctx, ib % bpc, jl % bpl)
    per = jnp.where(is_ctx, bpc, bpl)
    return is_ctx, jl // bpl, pos == 0, pos == per - 1


def _conv_centred(u, prev16, next8, first, last, cw_ref, cb_ref):
    sub = lax.broadcasted_iota(jnp.int32, (SUBLANES, D_RNN), 0)
    lastrow = TS - SUBLANES

    def from_prev_chunk(own, other):
        return jnp.where(sub == 0, jnp.where(first, 0.0, pltpu.roll(other, 1, 0)), pltpu.roll(own, 1, 0))

    s30 = from_prev_chunk(u[lastrow - SUBLANES:lastrow], prev16[0:SUBLANES])
    s31 = from_prev_chunk(u[lastrow:TS], prev16[SUBLANES:2 * SUBLANES])
    n0 = jnp.where(sub == SUBLANES - 1, jnp.where(last, 0.0, pltpu.roll(next8, SUBLANES - 1, 0)),
                   pltpu.roll(u[0:SUBLANES], SUBLANES - 1, 0))
    m2 = jnp.concatenate([s30, s31, u[0:lastrow - SUBLANES]], axis=0)
    m1 = jnp.concatenate([s31, u[0:lastrow]], axis=0)
    p1 = jnp.concatenate([u[SUBLANES:TS], n0], axis=0)
    return m2 * cw_ref[0:1, :] + m1 * cw_ref[1:2, :] + u * cw_ref[2:3, :] + p1 * cw_ref[3:4, :] + cb_ref[...]


def _lru_terms(uc, wg_ref, bg_ref, lam_ref):
    ucb = uc.astype(BF16)
    rs, is_ = [], []
    for nblk in range(LRU_BLOCKS):
        z = jnp.dot(ucb[:, nblk * LRU_BLOCK_W:(nblk + 1) * LRU_BLOCK_W], wg_ref[nblk],
                    preferred_element_type=F32)
        rs.append(z[:, 0:LRU_BLOCK_W])
        is_.append(z[:, LRU_BLOCK_W:2 * LRU_BLOCK_W])
    tr = jnp.tanh(jnp.concatenate(rs, axis=1) + bg_ref[0:1, :])
    ti = jnp.tanh(jnp.concatenate(is_, axis=1) + bg_ref[1:2, :])
    k = (-0.5 * LRU_C) * jax.nn.softplus(-lam_ref[...])
    a = jnp.exp(k + k * tr)
    w = 1.0 - a * a
    root = jnp.where(w > 0.0, w * lax.rsqrt(w), 0.0)
    v = root * ((0.5 + 0.5 * ti) * uc)
    return a, v


def _lru_scan(a, v, carry, reverse):
    order = range(LRU_CHUNK - 1, -1, -1) if reverse else range(LRU_CHUNK)
    h = jnp.zeros((SUBLANES, D_RNN), F32)
    p = jnp.ones((SUBLANES, D_RNN), F32)
    hs, ps = [None] * LRU_CHUNK, [None] * LRU_CHUNK
    for j in order:
        aj = a[j * SUBLANES:(j + 1) * SUBLANES]
        h = aj * h + v[j * SUBLANES:(j + 1) * SUBLANES]
        p = aj * p
        hs[j], ps[j] = h, p
    entry = [None] * SUBLANES
    for c in (range(SUBLANES - 1, -1, -1) if reverse else range(SUBLANES)):
        entry[c] = carry
        carry = h[c:c + 1] + p[c:c + 1] * carry
    hm = jnp.concatenate(entry, axis=0)
    return jnp.concatenate([hs[j] + ps[j] * hm for j in range(LRU_CHUNK)], axis=0), carry


def _lru_block(u_ref, up_ref, un_ref, cw_ref, cb_ref, wg_ref, bg_ref, lam_ref, h0_ref, carry_ref,
               cfg, ib, direction):
    is_ctx, jseq, first, last = _block_position(cfg, ib)
    start = last if direction else first

    @pl.when(start)
    def _():
        carry_ref[...] = jnp.where(is_ctx, 0.0, h0_ref[jseq])
    uc = _conv_centred(u_ref[...], up_ref[...], un_ref[...], first, last, cw_ref, cb_ref)
    a, v = _lru_terms(uc, wg_ref, bg_ref, lam_ref)
    y, carry = _lru_scan(a, v, carry_ref[...], bool(direction))
    carry_ref[...] = carry
    return y


def _odd_fwd_kernel(u_ref, up_ref, un_ref, cw_ref, cb_ref, wg_ref, bg_ref, lam_ref, h0_ref,
                    yf_ref, hs_ref, carry_ref, *, cfg):
    y = _lru_block(u_ref, up_ref, un_ref, cw_ref, cb_ref, wg_ref, bg_ref, lam_ref, h0_ref, carry_ref,
                   cfg, pl.program_id(0), 0)
    yf_ref[...] = y
    hs_ref[...] = carry_ref[...]


def _odd_bwd_kernel(g_ref, u_ref, up_ref, un_ref, yf_ref, mod_ref, cw_ref, cb_ref, wg_ref, bg_ref,
                    lam_ref, h0_ref, wout_ref, o_ref, hs_ref, carry_ref, *, cfg):
    nblk = pl.num_programs(0)
    ib = nblk - 1 - pl.program_id(0)
    y = _lru_block(u_ref, up_ref, un_ref, cw_ref, cb_ref, wg_ref, bg_ref, lam_ref, h0_ref, carry_ref,
                   cfg, ib, 1)
    hs_ref[...] = carry_ref[...]
    mix = (g_ref[...] * (yf_ref[...] + y)).astype(BF16)
    grp = _group_of_tile(cfg, ib, TS)
    gate = mod_ref[2, pl.ds(grp, 1), :]
    o_ref[...] = gate * jnp.dot(mix, wout_ref[...], preferred_element_type=F32)


def _odd_scans(cfg, l, gu, mods, cw, cb, wg, bg, lam, h0, w_out):
    n = cfg.n_tok
    nblk = n // TS
    r8 = TS // SUBLANES
    last8 = n // SUBLANES - 1

    def specs(rev):
        ib = (lambda i: nblk - 1 - i) if rev else (lambda i: i)
        cur = lambda col: pl.BlockSpec((TS, D_RNN), lambda i: (ib(i), col))
        prev = pl.BlockSpec((2 * SUBLANES, D_RNN), lambda i: (jnp.maximum(ib(i) * (r8 // 2) - 1, 0), 1))
        nxt = pl.BlockSpec((SUBLANES, D_RNN), lambda i: (jnp.minimum((ib(i) + 1) * r8, last8), 1))
        hs = pl.BlockSpec((None, 1, D_RNN), lambda i: (ib(i), 0, 0))
        return cur, prev, nxt, hs

    def weights(direction):
        return [_const_spec((CONV_W, D_RNN)), _const_spec((1, D_RNN)),
                pl.BlockSpec((None, LRU_BLOCKS, LRU_BLOCK_W, 2 * LRU_BLOCK_W), lambda i: (direction, 0, 0, 0)),
                pl.BlockSpec((None, 2, D_RNN), lambda i: (direction, 0, 0)),
                pl.BlockSpec((None, 1, D_RNN), lambda i: (direction, 0, 0)),
                pl.BlockSpec((None, cfg.nl, 1, D_RNN), lambda i: (direction, 0, 0, 0))]

    cur, prev, nxt, hs = specs(False)
    y_f, hs_f = pl.pallas_call(
        functools.partial(_odd_fwd_kernel, cfg=cfg),
        grid=(nblk,),
        in_specs=[cur(1), prev, nxt] + weights(0),
        out_specs=[cur(0), hs],
        out_shape=[jax.ShapeDtypeStruct((n, D_RNN), F32), jax.ShapeDtypeStruct((nblk, 1, D_RNN), F32)],
        scratch_shapes=[pltpu.VMEM((1, D_RNN), F32)],
        compiler_params=_cparams(1),
        name=f"lru_fwd{l}",
    )(gu, gu, gu, cw, cb, wg, bg, lam, h0)
    cur, prev, nxt, hs = specs(True)
    upd, hs_b = pl.pallas_call(
        functools.partial(_odd_bwd_kernel, cfg=cfg),
        grid=(nblk,),
        in_specs=[cur(0), cur(1), prev, nxt, cur(0), _mod_spec(l)] + weights(1)
                 + [_const_spec((D_RNN, D_MODEL))],
        out_specs=[cur(0), hs],
        out_shape=[jax.ShapeDtypeStruct((n, D_MODEL), F32), jax.ShapeDtypeStruct((nblk, 1, D_RNN), F32)],
        scratch_shapes=[pltpu.VMEM((1, D_RNN), F32)],
        compiler_params=_cparams(1),
        name=f"lru_bwd{l}",
    )(gu, gu, gu, gu, y_f, mods, cw, cb, wg, bg, lam, h0, w_out)
    return upd, hs_f, hs_b


def _rope_table(tl):
    pos = np.arange(tl)
    nf = HEAD_DIM // 4
    inv = ROPE_BASE ** (-np.arange(nf, dtype=np.float32) / nf)
    ar = (pos // GRID_W).astype(np.float32)[:, None] * inv
    ac = (pos % GRID_W).astype(np.float32)[:, None] * inv
    ang = np.concatenate([ar, ar, ac, ac] * 2, axis=-1)
    lo = (np.arange(LANES) % (HEAD_DIM // 2)) < nf
    cos, sin = np.cos(ang), np.sin(ang)
    tab = np.concatenate([cos, np.where(lo, -sin, 0.0), np.where(lo, 0.0, sin)], axis=-1)
    return jnp.asarray(tab, F32)


def _dup_heads(w):
    lead = w.shape[:-1]
    w = w.reshape(lead + (A_KV_HEADS, 1, HEAD_DIM))
    return jnp.broadcast_to(w, lead + (A_KV_HEADS, 2, HEAD_DIM)).reshape(lead + (2 * LANES,))


def _pack_even_in(w):
    kv = A_KV_HEADS * HEAD_DIM
    q, k, v = w[:, :A_WIDTH], w[:, A_WIDTH:A_WIDTH + kv], w[:, A_WIDTH + kv:A_WIDTH + 2 * kv]
    rest = w[:, A_WIDTH + 2 * kv:]
    pad = jnp.zeros((w.shape[0], LANES - 2 * GLA_RANK), w.dtype)
    return jnp.concatenate([q, _dup_heads(k), _dup_heads(v), rest, pad], axis=1).astype(BF16)


def _pack_alpha(w_alpha, b_alpha):
    w = jnp.zeros((LANES, 2 * B_QK), F32)
    w = w.at[0:GLA_RANK, 0:B_QK].set(w_alpha[0]).at[GLA_RANK:2 * GLA_RANK, B_QK:].set(w_alpha[1])
    return w.astype(BF16), b_alpha.reshape(1, 2 * B_QK)


def _forward(cfg, x_prompt, x_sample, c, cache_k, cache_v, state_gla, state_lru, c_ctx,
             w_ada, b_ada, norm_mix, norm_ffn, w_in_even, attn_sink, w_alpha, b_alpha, gla_gain,
             w_out_even, w_in_odd, conv_w, conv_b, w_gate_a, b_gate_a, w_gate_x, b_gate_x,
             lru_lambda, w_out_odd, w_ffn_in, w_ffn_out, norm_final):
    assert cfg.tc == TS and cfg.tl % TM == 0 and cfg.n_ctx % TM == 0 and cfg.nl + 1 <= SUBLANES
    x = (x_prompt.reshape(cfg.n_ctx, D_MODEL), x_sample.reshape(cfg.n_lat, D_MODEL))
    cvec = jnp.concatenate([c_ctx[None, :], c, jnp.zeros((SUBLANES - 1 - cfg.nl, D_MODEL), F32)], axis=0)
    mods = _ada(cvec, w_ada, b_ada)
    rope_tab = _rope_table(cfg.tl)
    nf = norm_final.reshape(1, D_MODEL)
    w_ffn_in_bf, w_ffn_out_bf, w_out_even_bf = (w.astype(BF16) for w in (w_ffn_in, w_ffn_out, w_out_even))
    new_k, new_v, new_gla, new_lru = [], [], [], []
    for l in range(DEPTH):
        nw = norm_mix[l].reshape(1, D_MODEL)
        if l % 2 == 0:
            e = l // 2
            za, zb, k_ctx, v_ctx = _even_in(cfg, l, x, mods, nw, _pack_even_in(w_in_even[e]), rope_tab)
            new_k.append(k_ctx.reshape(cfg.nc, cfg.tc, A_KV_HEADS, HEAD_DIM))
            new_v.append(v_ctx.reshape(cfg.nc, cfg.tc, A_KV_HEADS, HEAD_DIM))
            ck2 = _dup_heads(cache_k[:, e].reshape(cfg.nl, PAST_LEN, A_KV_HEADS * HEAD_DIM)).astype(BF16)
            cv2 = _dup_heads(cache_v[:, e].reshape(cfg.nl, PAST_LEN, A_KV_HEADS * HEAD_DIM)).astype(BF16)
            oa_ctx = _attn_ctx(cfg, l, za, attn_sink[e])
            oa_lat = _attn_lat(cfg, l, za, ck2, cv2, attn_sink[e])
            wal, bal = _pack_alpha(w_alpha[e], b_alpha[e])
            gain = gla_gain[e].reshape(1, B_WIDTH)
            ob_ctx, s_new = _gla_ctx(cfg, l, zb, wal, bal, gain)
            ob_lat = _gla_lat(cfg, l, zb, wal, bal, gain, state_gla[:, e])
            new_gla.append(s_new)
            pre_args = (oa_ctx, oa_lat, ob_ctx, ob_lat, w_out_even_bf)
        else:
            o = l // 2
            gu = _odd_in(cfg, l, x, mods, nw, w_in_odd[o].astype(BF16))
            wg = (0.5 * jnp.concatenate([w_gate_a[o], w_gate_x[o]], axis=-1)).astype(BF16)
            bg = 0.5 * jnp.stack([b_gate_a[o], b_gate_x[o]], axis=1)
            lam = lru_lambda[o].reshape(2, 1, D_RNN)
            h0 = jnp.transpose(state_lru[:, o], (1, 0, 2)).reshape(2, cfg.nl, 1, D_RNN)
            upd, hs_f, hs_b = _odd_scans(cfg, l, gu, mods, conv_w[o], conv_b[o].reshape(1, D_RNN),
                                         wg, bg, lam, h0, w_out_odd[o].astype(BF16))
            new_lru.append(jnp.stack([hs_f[:cfg.nc, 0], hs_b[:cfg.nc, 0]], axis=1))
            pre_args = (upd,)
        x = _ffn(cfg, l, x, pre_args, mods, norm_ffn[l].reshape(1, D_MODEL), w_ffn_in_bf, w_ffn_out_bf, nf,
                 final=(l == DEPTH - 1))
    y_prompt = x[0].reshape(cfg.nc, cfg.tc, D_MODEL)
    y_sample = x[1].reshape(cfg.nl, cfg.tl, D_MODEL)
    return (y_prompt, y_sample, jnp.stack(new_k, axis=1), jnp.stack(new_v, axis=1),
            jnp.stack(new_gla, axis=1), jnp.stack(new_lru, axis=1))


def kernel(x_prompt, x_sample, c, cache_k, cache_v, state_gla, state_lru, c_ctx, w_ada, b_ada, norm_mix, norm_ffn, w_in_even, attn_sink, w_alpha, b_alpha, gla_gain, w_out_even, w_in_odd, conv_w, conv_b, w_gate_a, b_gate_a, w_gate_x, b_gate_x, lru_lambda, w_out_odd, w_ffn_in, w_ffn_out, norm_final):
    cfg = Cfg(nc=x_prompt.shape[0], tc=x_prompt.shape[1], nl=x_sample.shape[0], tl=x_sample.shape[1])
    return _forward(cfg, x_prompt, x_sample, c, cache_k, cache_v, state_gla, state_lru, c_ctx,
                    w_ada, b_ada, norm_mix, norm_ffn, w_in_even, attn_sink, w_alpha, b_alpha, gla_gain,
                    w_out_even, w_in_odd, conv_w, conv_b, w_gate_a, b_gate_a, w_gate_x, b_gate_x,
                    lru_lambda, w_out_odd, w_ffn_in, w_ffn_out, norm_final)
```

```python
import functools
from typing import NamedTuple

import jax
import jax.numpy as jnp
import numpy as np
from jax import lax
from jax.experimental import pallas as pl
from jax.experimental.pallas import tpu as pltpu

F32 = jnp.float32
BF16 = jnp.bfloat16

D_MODEL = 1024
DEPTH = 4
EPS = 1e-6

HEAD_DIM = 64
A_Q_HEADS = 8
A_KV_HEADS = 2
A_GROUPS = A_Q_HEADS // A_KV_HEADS
A_WIDTH = A_Q_HEADS * HEAD_DIM
ATT_BLOCK = 128
ROPE_BASE = 10000.0
GRID_W = 64
NEG_INF = -1e30
PAST_LEN = 256

B_HEADS = 4
B_DK = 64
B_DV = 128
B_QK = B_HEADS * B_DK
B_WIDTH = B_HEADS * B_DV
GLA_RANK = 16
GLA_NORMALIZER = 16.0
GLA_CHUNK = 64

D_RNN = D_MODEL
LRU_BLOCK_W = 256
LRU_BLOCKS = D_RNN // LRU_BLOCK_W
CONV_W = 4
LRU_C = 8.0

D_FF = 2816

LANES = 128
SUBLANES = 8

ZA_W = A_WIDTH + 2 * 2 * LANES
ZB_LR = B_QK + B_QK + B_WIDTH + B_WIDTH
ZB_W = ZB_LR + LANES
P_PACK = ZA_W + ZB_W

TM = 512
TS = 256
FF_CHUNK = 256

VMEM_LIMIT = 56 * 1024 * 1024


class Cfg(NamedTuple):
    nc: int
    tc: int
    nl: int
    tl: int

    @property
    def n_ctx(self):
        return self.nc * self.tc

    @property
    def n_lat(self):
        return self.nl * self.tl

    @property
    def n_tok(self):
        return self.n_ctx + self.n_lat


def _cparams(n_axes):
    return pltpu.CompilerParams(dimension_semantics=("arbitrary",) * n_axes,
                                vmem_limit_bytes=VMEM_LIMIT)


def _const_spec(shape):
    nd = len(shape)
    return pl.BlockSpec(shape, lambda *_: (0,) * nd)


def _silu(x):
    return x * jax.nn.sigmoid(x)


def _group_of_tile(cfg, i, tile):
    nct = cfg.n_ctx // tile
    per_lat = cfg.tl // tile
    return jnp.where(i < nct, 0, 1 + jnp.maximum(i - nct, 0) // per_lat)


def _modnorm(x, nw, shift, scale):
    ms = jnp.mean(x * x, axis=-1, keepdims=True)
    y = x * lax.rsqrt(ms + EPS) * nw
    return y * (1.0 + scale) + shift


def _mod_rows(mod_ref, g, first):
    return tuple(mod_ref[first + j, pl.ds(g, 1), :] for j in range(3))


def _ada_kernel(c_ref, w_ref, b_ref, o_ref):
    s = _silu(c_ref[...]).astype(BF16)
    o_ref[...] = jnp.dot(s, w_ref[...].astype(BF16), preferred_element_type=F32) + b_ref[...]


def _ada(cvec, w_ada, b_ada):
    b4 = b_ada.reshape(DEPTH, 6, 1, D_MODEL)
    return pl.pallas_call(
        _ada_kernel,
        grid=(DEPTH, 6),
        in_specs=[
            _const_spec((SUBLANES, D_MODEL)),
            pl.BlockSpec((None, D_MODEL, D_MODEL), lambda l, j: (l, 0, j)),
            pl.BlockSpec((None, None, 1, D_MODEL), lambda l, j: (l, j, 0, 0)),
        ],
        out_specs=pl.BlockSpec((None, None, SUBLANES, D_MODEL), lambda l, j: (l, j, 0, 0)),
        out_shape=jax.ShapeDtypeStruct((DEPTH, 6, SUBLANES, D_MODEL), F32),
        compiler_params=_cparams(2),
        name="ada",
    )(cvec, w_ada, b4)


def _mod_spec(l):
    return pl.BlockSpec((None, 6, SUBLANES, D_MODEL), lambda *_: (l, 0, 0, 0))


def _ffn_tile(x, nw, shift, scale, gate, win_ref, wout_ref):
    h = _modnorm(x, nw, shift, scale).astype(BF16)
    acc = None
    for c in range(D_FF // FF_CHUNK):
        lo, hi = c * FF_CHUNK, (c + 1) * FF_CHUNK
        g = jnp.dot(h, win_ref[:, lo:hi], preferred_element_type=F32)
        u = jnp.dot(h, win_ref[:, D_FF + lo:D_FF + hi], preferred_element_type=F32)
        a = (_silu(g) * u).astype(BF16)
        part = jnp.dot(a, wout_ref[lo:hi, :], preferred_element_type=F32)
        acc = part if acc is None else acc + part
    return x + gate * acc


def _x_args(cfg, x):
    if not isinstance(x, tuple):
        return [x], [pl.BlockSpec((TM, D_MODEL), lambda i: (i, 0))]
    nct = cfg.n_ctx // TM
    return list(x), [pl.BlockSpec((TM, D_MODEL), lambda i: (jnp.minimum(i, nct - 1), 0)),
                     pl.BlockSpec((TM, D_MODEL), lambda i: (jnp.maximum(i - nct, 0), 0))]


def _load_x(cfg, x_refs, i):
    if len(x_refs) == 1:
        return x_refs[0][...]
    return jnp.where(i < cfg.n_ctx // TM, x_refs[0][...], x_refs[1][...])


def _to_scan_order(x):
    nb, d = x.shape[0] // TS, x.shape[1]
    return jnp.swapaxes(x.reshape(nb, SUBLANES, TS // SUBLANES, d), 1, 2).reshape(nb * TS, d)


def _from_scan_order(x):
    nb, d = x.shape[0] // TS, x.shape[1]
    return jnp.swapaxes(x.reshape(nb, TS // SUBLANES, SUBLANES, d), 1, 2).reshape(nb * TS, d)


def _ffn_kernel(*refs, cfg, n_x, pre, final):
    x_refs, refs = refs[:n_x], refs[n_x:]
    i = pl.program_id(0)
    g = _group_of_tile(cfg, i, TM)
    x = _load_x(cfg, x_refs, i)
    if pre == "even":
        (oac_ref, oal_ref, obc_ref, obl_ref, wmix_ref), refs = refs[:5], refs[5:]
    else:
        d_ref, refs = refs[0], refs[1:]
    mod_ref, nw_ref, win_ref, wout_ref, nf_ref = refs[:5]
    outs = refs[5:]
    if pre == "even":
        is_ctx_tile = i < cfg.n_ctx // TM
        o = jnp.concatenate([jnp.where(is_ctx_tile, oac_ref[...], oal_ref[...]),
                             jnp.where(is_ctx_tile, obc_ref[...], obl_ref[...])], axis=1)
        x = x + mod_ref[2, pl.ds(g, 1), :] * jnp.dot(o, wmix_ref[...], preferred_element_type=F32)
    else:
        x = x + _from_scan_order(d_ref[...])
    shift, scale, gate = _mod_rows(mod_ref, g, 3)
    y = _ffn_tile(x, nw_ref[...], shift, scale, gate, win_ref, wout_ref)
    if not final:
        outs[0][...] = y
        return
    ms = jnp.mean(y * y, axis=-1, keepdims=True)
    y = y * lax.rsqrt(ms + EPS) * nf_ref[...]
    is_ctx = i < cfg.n_ctx // TM

    @pl.when(is_ctx)
    def _():
        outs[0][...] = y

    @pl.when(jnp.logical_not(is_ctx))
    def _():
        outs[1][...] = y


def _ffn(cfg, l, x, pre_args, mods, nw, win, wout, nf, final):
    n = cfg.n_tok
    nct = cfg.n_ctx // TM
    tok = lambda w: pl.BlockSpec((TM, w), lambda i: (i, 0))
    ctx = lambda w: pl.BlockSpec((TM, w), lambda i: (jnp.minimum(i, nct - 1), 0))
    lat = lambda w: pl.BlockSpec((TM, w), lambda i: (jnp.maximum(i - nct, 0), 0))
    x_ops, x_specs = _x_args(cfg, x)
    if l % 2 == 0:
        pre = "even"
        pre_specs = [ctx(A_WIDTH), lat(A_WIDTH), ctx(B_WIDTH), lat(B_WIDTH),
                     pl.BlockSpec((None, A_WIDTH + B_WIDTH, D_MODEL), lambda i: (l // 2, 0, 0))]
    else:
        pre, pre_specs = "odd", [tok(D_MODEL)]
    if final:
        out_specs = [pl.BlockSpec((TM, D_MODEL), lambda i: (jnp.minimum(i, nct - 1), 0)),
                     pl.BlockSpec((TM, D_MODEL), lambda i: (jnp.maximum(i - nct, 0), 0))]
        out_shape = [jax.ShapeDtypeStruct((cfg.n_ctx, D_MODEL), F32),
                     jax.ShapeDtypeStruct((cfg.n_lat, D_MODEL), F32)]
    else:
        out_specs, out_shape = tok(D_MODEL), jax.ShapeDtypeStruct((n, D_MODEL), F32)
    return pl.pallas_call(
        functools.partial(_ffn_kernel, cfg=cfg, n_x=len(x_ops), pre=pre, final=final),
        grid=(n // TM,),
        in_specs=x_specs + pre_specs + [
            _mod_spec(l), _const_spec((1, D_MODEL)),
            pl.BlockSpec((None, D_MODEL, 2 * D_FF), lambda i: (l, 0, 0), pipeline_mode=pl.Buffered(1)),
            pl.BlockSpec((None, D_FF, D_MODEL), lambda i: (l, 0, 0), pipeline_mode=pl.Buffered(1)),
            _const_spec((1, D_MODEL))],
        out_specs=out_specs,
        out_shape=out_shape,
        compiler_params=_cparams(1),
        name=f"ffn{l}",
    )(*x_ops, *pre_args, mods, nw, win, wout, nf)


def _rope(x, tab_ref):
    cos, sin_a, sin_b = tab_ref[:, 0:LANES], tab_ref[:, LANES:2 * LANES], tab_ref[:, 2 * LANES:3 * LANES]
    parts = []
    for j in range(x.shape[1] // LANES):
        xs = x[:, j * LANES:(j + 1) * LANES]
        parts.append(xs * cos + pltpu.roll(xs, LANES - 16, 1) * sin_a + pltpu.roll(xs, 16, 1) * sin_b)
    return jnp.concatenate(parts, axis=1)


def _even_in_kernel(*refs, cfg, n_x):
    x_refs, (mod_ref, nw_ref, w_ref, tab_ref, za_ref, zb_ref, k_ref, v_ref) = refs[:n_x], refs[n_x:]
    i = pl.program_id(0)
    g = _group_of_tile(cfg, i, TM)
    shift, scale, _ = _mod_rows(mod_ref, g, 0)
    h = _modnorm(_load_x(cfg, x_refs, i), nw_ref[...], shift, scale).astype(BF16)
    z = jnp.dot(h, w_ref[...], preferred_element_type=F32)
    q = z[:, 0:A_WIDTH] * (HEAD_DIM ** -0.5)
    k2 = z[:, A_WIDTH:A_WIDTH + 2 * LANES]
    v2 = z[:, A_WIDTH + 2 * LANES:ZA_W]
    za_ref[:, A_WIDTH + 2 * LANES:ZA_W] = v2.astype(BF16)
    zb_ref[:, 0:B_QK] = z[:, ZA_W:ZA_W + B_QK] * (B_DK ** -0.5)
    zb_ref[:, B_QK:ZB_W] = z[:, ZA_W + B_QK:P_PACK]

    @pl.when(g == 0)
    def _():
        za_ref[:, 0:A_WIDTH] = q.astype(BF16)
        za_ref[:, A_WIDTH:A_WIDTH + 2 * LANES] = k2.astype(BF16)
        lo = lax.broadcasted_iota(jnp.int32, (TM, LANES), 1) < HEAD_DIM
        k_ref[...] = jnp.where(lo, k2[:, 0:LANES], k2[:, LANES:2 * LANES])
        v_ref[...] = jnp.where(lo, v2[:, 0:LANES], v2[:, LANES:2 * LANES])

    @pl.when(g != 0)
    def _():
        za_ref[:, 0:A_WIDTH] = _rope(q, tab_ref).astype(BF16)
        za_ref[:, A_WIDTH:A_WIDTH + 2 * LANES] = _rope(k2, tab_ref).astype(BF16)


def _even_in(cfg, l, x, mods, nw, w_pack, rope_tab):
    n = cfg.n_tok
    nct = cfg.n_ctx // TM
    per_lat = cfg.tl // TM
    tok = lambda w: pl.BlockSpec((TM, w), lambda i: (i, 0))
    x_ops, x_specs = _x_args(cfg, x)
    ctx_kv = pl.BlockSpec((TM, LANES), lambda i: (jnp.minimum(i, nct - 1), 0))
    return pl.pallas_call(
        functools.partial(_even_in_kernel, cfg=cfg, n_x=len(x_ops)),
        grid=(n // TM,),
        in_specs=x_specs + [_mod_spec(l), _const_spec((1, D_MODEL)),
                            pl.BlockSpec((D_MODEL, P_PACK), lambda i: (0, 0), pipeline_mode=pl.Buffered(1)),
                            pl.BlockSpec((TM, 3 * LANES), lambda i: (jnp.maximum(i - nct, 0) % per_lat, 0))],
        out_specs=[tok(ZA_W), tok(ZB_W), ctx_kv, ctx_kv],
        out_shape=[jax.ShapeDtypeStruct((n, ZA_W), BF16),
                   jax.ShapeDtypeStruct((n, ZB_W), F32),
                   jax.ShapeDtypeStruct((cfg.n_ctx, LANES), F32),
                   jax.ShapeDtypeStruct((cfg.n_ctx, LANES), F32)],
        compiler_params=_cparams(1),
        name=f"even_in{l}",
    )(*x_ops, mods, nw, w_pack, rope_tab)


def _sink_attention(units, sink_ref):
    scores = []
    for q, k_all, _, mask, kvh in units:
        tq = q.shape[0]
        lo = lax.broadcasted_iota(jnp.int32, (tq, LANES), 1) < HEAD_DIM
        qs = []
        for gq in range(A_GROUPS):
            h = kvh * A_GROUPS + gq
            pair = q[:, (h // 2) * LANES:(h // 2 + 1) * LANES]
            keep = lo if h % 2 == 0 else jnp.logical_not(lo)
            qs.append(jnp.where(keep, pair, jnp.zeros_like(pair)))
        qst = jnp.concatenate(qs, axis=0)
        s = lax.dot_general(qst, k_all, (((1,), (1,)), ((), ())), preferred_element_type=F32)
        scores.append(s if mask is None else jnp.where(mask, s, NEG_INF))
    probs = []
    for (q, _, _, _, kvh), s in zip(units, scores):
        tq = q.shape[0]
        rows = lax.broadcasted_iota(jnp.int32, (A_GROUPS * tq, 1), 0) // tq
        sink = jnp.zeros((A_GROUPS * tq, 1), F32)
        for gq in range(A_GROUPS):
            sink = jnp.where(rows == gq, sink_ref[kvh * A_GROUPS + gq], sink)
        m = jnp.maximum(jnp.max(s, axis=-1, keepdims=True), sink)
        p = jnp.exp(s - m)
        den = jnp.sum(p, axis=-1, keepdims=True) + jnp.exp(sink - m)
        probs.append((p.astype(BF16), den))
    outs = []
    for (q, _, v_all, _, _), (p, den) in zip(units, probs):
        tq = q.shape[0]
        lo = lax.broadcasted_iota(jnp.int32, (tq, LANES), 1) < HEAD_DIM
        o = jnp.dot(p, v_all, preferred_element_type=F32) / den
        pairs = [jnp.where(lo, o[(2 * pr) * tq:(2 * pr + 1) * tq], o[(2 * pr + 1) * tq:(2 * pr + 2) * tq])
                 for pr in range(2)]
        outs.append(jnp.concatenate(pairs, axis=1).astype(BF16))
    return outs


def _attn_ctx_kernel(sink_ref, q_ref, k_ref, v_ref, o_ref):
    q = q_ref[...]
    units = [(q, k_ref[:, kvh * LANES:(kvh + 1) * LANES], v_ref[:, kvh * LANES:(kvh + 1) * LANES], None, kvh)
             for kvh in range(A_KV_HEADS)]
    for kvh, o in enumerate(_sink_attention(units, sink_ref)):
        o_ref[:, 2 * kvh * LANES:2 * (kvh + 1) * LANES] = o


def _attn_ctx(cfg, l, za, sink):
    tc = cfg.tc
    return pl.pallas_call(
        _attn_ctx_kernel,
        grid=(cfg.nc,),
        in_specs=[pl.BlockSpec(memory_space=pltpu.SMEM),
                  pl.BlockSpec((tc, A_WIDTH), lambda b: (b, 0)),
                  pl.BlockSpec((tc, 2 * LANES), lambda b: (b, A_WIDTH // (2 * LANES))),
                  pl.BlockSpec((tc, 2 * LANES), lambda b: (b, A_WIDTH // (2 * LANES) + 1))],
        out_specs=pl.BlockSpec((tc, A_WIDTH), lambda b: (b, 0)),
        out_shape=jax.ShapeDtypeStruct((cfg.n_ctx, A_WIDTH), BF16),
        compiler_params=_cparams(1),
        name=f"attn_ctx{l}",
    )(sink, za, za, za)


ATT_STEP = 2 * ATT_BLOCK
ATT_WIN = 3 * ATT_BLOCK


def _attn_lat_kernel(sink_ref, q_ref, k_ref, v_ref, ck_ref, cv_ref, o_ref):
    n = pl.program_id(1)
    tl = k_ref.shape[0]
    shape = (A_GROUPS * ATT_BLOCK, PAST_LEN + ATT_WIN)
    r = lax.broadcasted_iota(jnp.int32, shape, 0) % ATT_BLOCK
    c = lax.broadcasted_iota(jnp.int32, shape, 1) - PAST_LEN
    units = []
    for j in range(ATT_STEP // ATT_BLOCK):
        q0 = (n * (ATT_STEP // ATT_BLOCK) + j) * ATT_BLOCK
        start = pl.multiple_of(jnp.clip(q0 - ATT_BLOCK, 0, tl - ATT_WIN), ATT_BLOCK)
        dist = (q0 - start) + r - c
        mask = jnp.logical_or(c < 0, jnp.abs(dist) <= ATT_BLOCK)
        q = q_ref[j * ATT_BLOCK:(j + 1) * ATT_BLOCK, :]
        for kvh in range(A_KV_HEADS):
            sl = slice(kvh * LANES, (kvh + 1) * LANES)
            k_all = jnp.concatenate([ck_ref[:, sl], k_ref[pl.ds(start, ATT_WIN), sl]], axis=0)
            v_all = jnp.concatenate([cv_ref[:, sl], v_ref[pl.ds(start, ATT_WIN), sl]], axis=0)
            units.append((q, k_all, v_all, mask, kvh))
    for u, o in enumerate(_sink_attention(units, sink_ref)):
        j, kvh = divmod(u, A_KV_HEADS)
        o_ref[j * ATT_BLOCK:(j + 1) * ATT_BLOCK, 2 * kvh * LANES:2 * (kvh + 1) * LANES] = o


def _attn_lat(cfg, l, za, cache_k2, cache_v2, sink):
    assert cfg.n_ctx % cfg.tl == 0 and cfg.tl % ATT_STEP == 0 and cfg.tl >= ATT_WIN
    steps = cfg.tl // ATT_STEP
    base = cfg.n_ctx // ATT_STEP
    kcol = A_WIDTH // (2 * LANES)
    seq = lambda col: pl.BlockSpec((cfg.tl, 2 * LANES), lambda b, n: (cfg.n_ctx // cfg.tl + b, col))
    cache = pl.BlockSpec((None, PAST_LEN, 2 * LANES), lambda b, n: (b, 0, 0))
    return pl.pallas_call(
        _attn_lat_kernel,
        grid=(cfg.nl, steps),
        in_specs=[pl.BlockSpec(memory_space=pltpu.SMEM),
                  pl.BlockSpec((ATT_STEP, A_WIDTH), lambda b, n: (base + b * steps + n, 0)),
                  seq(kcol), seq(kcol + 1), cache, cache],
        out_specs=pl.BlockSpec((ATT_STEP, A_WIDTH), lambda b, n: (b * steps + n, 0)),
        out_shape=jax.ShapeDtypeStruct((cfg.n_lat, A_WIDTH), BF16),
        compiler_params=_cparams(2),
        name=f"attn_lat{l}",
    )(sink, za, za, za, cache_k2, cache_v2)


def _gla_chunk_prepare(q, k, v, la, reverse):
    cn = GLA_CHUNK
    row = lax.broadcasted_iota(jnp.int32, (cn, B_QK), 0)
    b = la
    s = 1
    while s < cn:
        if reverse:
            b = b + jnp.where(row < cn - s, pltpu.roll(b, cn - s, 0), 0.0)
        else:
            b = b + jnp.where(row >= s, pltpu.roll(b, s, 0), 0.0)
        s *= 2
    b_last = b[0:1, :] if reverse else b[cn - 1:cn, :]
    q_e = q * jnp.exp(b)
    k_e = (k * jnp.exp(-b)).astype(BF16)
    k_s = (k * jnp.exp(b_last - b)).astype(BF16)
    vb = v.astype(BF16)
    row_t = row[:, 0:LANES]
    lane_t = lax.broadcasted_iota(jnp.int32, (cn, LANES), 1)
    lo = lane_t < B_DK
    col = lane_t % cn
    tri = (col >= row_t) if reverse else (col <= row_t)
    q_tiles, att_tiles = [], []
    for t in range(B_HEADS // 2):
        sl = slice(t * LANES, (t + 1) * LANES)
        k_t = k_e[:, sl]
        zero = jnp.zeros_like(k_t)
        kbd = jnp.concatenate([jnp.where(lo, k_t, zero), jnp.where(lo, zero, k_t)], axis=0)
        q_t = q_e[:, sl]
        att = lax.dot_general(q_t.astype(BF16), kbd, (((1,), (1,)), ((), ())), preferred_element_type=F32)
        q_tiles.append(q_t)
        att_tiles.append(jnp.where(tri, att, 0.0))
    upd = lax.dot_general(k_s, vb, (((0,), (0,)), ((), ())), preferred_element_type=F32)
    own = jnp.concatenate([upd[h * B_DK:(h + 1) * B_DK, h * B_DV:(h + 1) * B_DV] for h in range(B_HEADS)], axis=0)
    decay = jnp.broadcast_to(jnp.exp(b_last), (LANES, B_QK)).T
    return q_tiles, att_tiles, vb, own, decay


def _gla_chunk_output(q_tiles, att_tiles, vb, st):
    lo = lax.broadcasted_iota(jnp.int32, (GLA_CHUNK, LANES), 1) < B_DK
    st_b = st.astype(BF16)
    outs = []
    for h in range(B_HEADS):
        att, q_t = att_tiles[h // 2], q_tiles[h // 2]
        x = (jnp.where(lo, att, pltpu.roll(q_t, B_DK, 1)) if h % 2 == 0
             else jnp.where(lo, pltpu.roll(att, B_DK, 1), q_t))
        rhs = jnp.concatenate([vb[:, h * B_DV:(h + 1) * B_DV], st_b[h * B_DK:(h + 1) * B_DK, :]], axis=0)
        outs.append(jnp.dot(x.astype(BF16), rhs, preferred_element_type=F32))
    return jnp.concatenate(outs, axis=1)


def _gla_log_decay(lr_pad, wal_ref, bal_ref, direction):
    z = jnp.dot(lr_pad.astype(BF16), wal_ref[:, direction * B_QK:(direction + 1) * B_QK],
                preferred_element_type=F32) + bal_ref[:, direction * B_QK:(direction + 1) * B_QK]
    return jax.nn.log_sigmoid(z) / GLA_NORMALIZER


def _gla_blocks(jobs, wal_ref, bal_ref):
    prepared = []
    for qkv_ref, lr_ref, _, direction in jobs:
        nchunk = qkv_ref.shape[0] // GLA_CHUNK
        la_all = _gla_log_decay(lr_ref[...], wal_ref, bal_ref, direction)
        chunks = {}
        for ci in range(nchunk):
            rs = slice(ci * GLA_CHUNK, (ci + 1) * GLA_CHUNK)
            chunks[ci] = _gla_chunk_prepare(qkv_ref[rs, 0:B_QK], qkv_ref[rs, B_QK:2 * B_QK],
                                            qkv_ref[rs, 2 * B_QK:2 * B_QK + B_WIDTH], la_all[rs, :],
                                            bool(direction))
        prepared.append(chunks)
    results = []
    for (qkv_ref, _, st, direction), chunks in zip(jobs, prepared):
        nchunk = len(chunks)
        states = {}
        for ci in (range(nchunk - 1, -1, -1) if direction else range(nchunk)):
            states[ci] = st
            _, _, _, own, decay = chunks[ci]
            st = st * decay + own
        outs = [_gla_chunk_output(*chunks[ci][:3], states[ci]) for ci in range(nchunk)]
        results.append((jnp.concatenate(outs, axis=0), st))
    return results


def _stack_state(s_ref):
    return s_ref[...].reshape(B_QK, B_DV)


def _unstack_state(st, s_ref):
    s_ref[...] = st.reshape(B_HEADS, B_DK, B_DV)


def _gla_finish(o, r, gain):
    parts = []
    for h in range(B_HEADS):
        oh = o[:, h * B_DV:(h + 1) * B_DV]
        parts.append(oh * lax.rsqrt(jnp.mean(oh * oh, axis=-1, keepdims=True) + EPS))
    return (jnp.concatenate(parts, axis=1) * gain * _silu(r)).astype(BF16)


def _gla_ctx_kernel(qkv_ref, r_ref, lr_ref, wal_ref, bal_ref, gain_ref, ob_ref, snew_ref):
    zero = jnp.zeros((B_QK, B_DV), F32)
    (o_f, st_f), (o_b, st_b) = _gla_blocks([(qkv_ref, lr_ref, zero, 0), (qkv_ref, lr_ref, zero, 1)],
                                           wal_ref, bal_ref)
    _unstack_state(st_f, snew_ref.at[0])
    _unstack_state(st_b, snew_ref.at[1])
    ob_ref[...] = _gla_finish(o_f + o_b, r_ref[...], gain_ref[...])


def _gla_ctx(cfg, l, zb, wal, bal, gain):
    tc = cfg.tc
    return pl.pallas_call(
        _gla_ctx_kernel,
        grid=(cfg.nc,),
        in_specs=[pl.BlockSpec((tc, 2 * B_QK + B_WIDTH), lambda b: (b, 0)),
                  pl.BlockSpec((tc, B_WIDTH), lambda b: (b, (2 * B_QK + B_WIDTH) // B_WIDTH)),
                  pl.BlockSpec((tc, LANES), lambda b: (b, ZB_LR // LANES)),
                  _const_spec((LANES, 2 * B_QK)), _const_spec((1, 2 * B_QK)), _const_spec((1, B_WIDTH))],
        out_specs=[pl.BlockSpec((tc, B_WIDTH), lambda b: (b, 0)),
                   pl.BlockSpec((None, 2, B_HEADS, B_DK, B_DV), lambda b: (b, 0, 0, 0, 0))],
        out_shape=[jax.ShapeDtypeStruct((cfg.n_ctx, B_WIDTH), BF16),
                   jax.ShapeDtypeStruct((cfg.nc, 2, B_HEADS, B_DK, B_DV), F32)],
        compiler_params=_cparams(1),
        name=f"gla_ctx{l}",
    )(zb, zb, zb, wal, bal, gain)


def _gla_lat_kernel(qkvf_ref, lrf_ref, rf_ref, qkvb_ref, lrb_ref, rb_ref, wal_ref, bal_ref, gain_ref,
                    s0_ref, ob_ref, stf_ref, stb_ref, of_ref, obk_ref):
    n = pl.program_id(1)
    nb = pl.num_programs(1)

    @pl.when(n == 0)
    def _():
        stf_ref[...] = _stack_state(s0_ref.at[0])
        stb_ref[...] = _stack_state(s0_ref.at[1])
    (o_f, st_f), (o_b, st_b) = _gla_blocks([(qkvf_ref, lrf_ref, stf_ref[...], 0),
                                            (qkvb_ref, lrb_ref, stb_ref[...], 1)], wal_ref, bal_ref)
    stf_ref[...] = st_f
    stb_ref[...] = st_b
    rows_f = pl.ds(pl.multiple_of(n * TS, TS), TS)
    rows_b = pl.ds(pl.multiple_of((nb - 1 - n) * TS, TS), TS)
    of_ref[rows_f, :] = o_f
    obk_ref[rows_b, :] = o_b

    @pl.when(2 * n >= nb)
    def _():
        ob_ref[rows_f, :] = _gla_finish(o_f + obk_ref[rows_f, :], rf_ref[...], gain_ref[...])
        ob_ref[rows_b, :] = _gla_finish(of_ref[rows_b, :] + o_b, rb_ref[...], gain_ref[...])


def _gla_lat(cfg, l, zb, wal, bal, gain, s0):
    nbl = cfg.tl // TS
    base = cfg.n_ctx // TS
    assert nbl % 2 == 0

    def blk(width, col, rev):
        def imap(b, n):
            return (base + b * nbl + (nbl - 1 - n if rev else n), col)
        return pl.BlockSpec((TS, width), imap)

    qkv_w = 2 * B_QK + B_WIDTH
    per_dir = lambda rev: [blk(qkv_w, 0, rev), blk(LANES, ZB_LR // LANES, rev), blk(B_WIDTH, qkv_w // B_WIDTH, rev)]
    return pl.pallas_call(
        _gla_lat_kernel,
        grid=(cfg.nl, nbl),
        in_specs=per_dir(False) + per_dir(True) + [
            _const_spec((LANES, 2 * B_QK)), _const_spec((1, 2 * B_QK)), _const_spec((1, B_WIDTH)),
            pl.BlockSpec((None, 2, B_HEADS, B_DK, B_DV), lambda b, n: (b, 0, 0, 0, 0))],
        out_specs=pl.BlockSpec((cfg.tl, B_WIDTH), lambda b, n: (b, 0)),
        out_shape=jax.ShapeDtypeStruct((cfg.n_lat, B_WIDTH), BF16),
        scratch_shapes=[pltpu.VMEM((B_QK, B_DV), F32), pltpu.VMEM((B_QK, B_DV), F32),
                        pltpu.VMEM((cfg.tl, B_WIDTH), F32), pltpu.VMEM((cfg.tl, B_WIDTH), F32)],
        compiler_params=_cparams(2),
        name=f"gla_lat{l}",
    )(zb, zb, zb, zb, zb, zb, wal, bal, gain, s0)


LRU_CHUNK = TS // SUBLANES


TM_ODD_IN = 1024


def _odd_in_kernel(x_ref, mod_ref, nw_ref, w_ref, o_ref, *, cfg):
    g = _group_of_tile(cfg, pl.program_id(0), TM_ODD_IN)
    shift, scale, _ = _mod_rows(mod_ref, g, 0)
    h = _modnorm(_to_scan_order(x_ref[...]), nw_ref[...], shift, scale).astype(BF16)
    z = jnp.dot(h, w_ref[...], preferred_element_type=F32)
    o_ref[:, 0:D_RNN] = jax.nn.gelu(z[:, 0:D_RNN])
    o_ref[:, D_RNN:2 * D_RNN] = z[:, D_RNN:2 * D_RNN]


def _odd_in(cfg, l, x, mods, nw, w):
    n = cfg.n_tok
    return pl.pallas_call(
        functools.partial(_odd_in_kernel, cfg=cfg),
        grid=(n // TM_ODD_IN,),
        in_specs=[pl.BlockSpec((TM_ODD_IN, D_MODEL), lambda i: (i, 0)), _mod_spec(l), _const_spec((1, D_MODEL)),
                  pl.BlockSpec((D_MODEL, 2 * D_RNN), lambda i: (0, 0), pipeline_mode=pl.Buffered(1))],
        out_specs=pl.BlockSpec((TM_ODD_IN, 2 * D_RNN), lambda i: (i, 0)),
        out_shape=jax.ShapeDtypeStruct((n, 2 * D_RNN), F32),
        compiler_params=_cparams(1),
        name=f"odd_in{l}",
    )(x, mods, nw, w)


def _block_position(cfg, ib):
    ncb = cfg.n_ctx // TS
    bpc = cfg.tc // TS
    bpl = cfg.tl // TS
    is_ctx = ib < ncb
    jl = jnp.maximum(ib - ncb, 0)
    pos = jnp.where(is_ctx, ib % bpc, jl % bpl)
    per = jnp.where(is_ctx, bpc, bpl)
    return is_ctx, jl // bpl, pos == 0, pos == per - 1


def _conv_centred(u, prev16, next8, first, last, cw_ref, cb_ref):
    sub = lax.broadcasted_iota(jnp.int32, (SUBLANES, D_RNN), 0)
    lastrow = TS - SUBLANES

    def from_prev_chunk(own, other):
        return jnp.where(sub == 0, jnp.where(first, 0.0, pltpu.roll(other, 1, 0)), pltpu.roll(own, 1, 0))

    s30 = from_prev_chunk(u[lastrow - SUBLANES:lastrow], prev16[0:SUBLANES])
    s31 = from_prev_chunk(u[lastrow:TS], prev16[SUBLANES:2 * SUBLANES])
    n0 = jnp.where(sub == SUBLANES - 1, jnp.where(last, 0.0, pltpu.roll(next8, SUBLANES - 1, 0)),
                   pltpu.roll(u[0:SUBLANES], SUBLANES - 1, 0))
    m2 = jnp.concatenate([s30, s31, u[0:lastrow - SUBLANES]], axis=0)
    m1 = jnp.concatenate([s31, u[0:lastrow]], axis=0)
    p1 = jnp.concatenate([u[SUBLANES:TS], n0], axis=0)
    return m2 * cw_ref[0:1, :] + m1 * cw_ref[1:2, :] + u * cw_ref[2:3, :] + p1 * cw_ref[3:4, :] + cb_ref[...]


def _lru_terms(uc, wg_ref, bg_ref, lam_ref):
    ucb = uc.astype(BF16)
    rs, is_ = [], []
    for nblk in range(LRU_BLOCKS):
        z = jnp.dot(ucb[:, nblk * LRU_BLOCK_W:(nblk + 1) * LRU_BLOCK_W], wg_ref[nblk],
                    preferred_element_type=F32)
        rs.append(z[:, 0:LRU_BLOCK_W])
        is_.append(z[:, LRU_BLOCK_W:2 * LRU_BLOCK_W])
    tr = jnp.tanh(jnp.concatenate(rs, axis=1) + bg_ref[0:1, :])
    ti = jnp.tanh(jnp.concatenate(is_, axis=1) + bg_ref[1:2, :])
    k = (-0.5 * LRU_C) * jax.nn.softplus(-lam_ref[...])
    a = jnp.exp(k + k * tr)
    w = 1.0 - a * a
    root = jnp.where(w > 0.0, w * lax.rsqrt(w), 0.0)
    v = root * ((0.5 + 0.5 * ti) * uc)
    return a, v


def _lru_scan(a, v, carry, reverse):
    order = range(LRU_CHUNK - 1, -1, -1) if reverse else range(LRU_CHUNK)
    h = jnp.zeros((SUBLANES, D_RNN), F32)
    p = jnp.ones((SUBLANES, D_RNN), F32)
    hs, ps = [None] * LRU_CHUNK, [None] * LRU_CHUNK
    for j in order:
        aj = a[j * SUBLANES:(j + 1) * SUBLANES]
        h = aj * h + v[j * SUBLANES:(j + 1) * SUBLANES]
        p = aj * p
        hs[j], ps[j] = h, p
    entry = [None] * SUBLANES
    for c in (range(SUBLANES - 1, -1, -1) if reverse else range(SUBLANES)):
        entry[c] = carry
        carry = h[c:c + 1] + p[c:c + 1] * carry
    hm = jnp.concatenate(entry, axis=0)
    return jnp.concatenate([hs[j] + ps[j] * hm for j in range(LRU_CHUNK)], axis=0), carry


def _lru_block(uc, wg_ref, bg_ref, lam_ref, h0_ref, carry_ref, cfg, ib, direction):
    is_ctx, jseq, first, last = _block_position(cfg, ib)
    start = last if direction else first

    @pl.when(start)
    def _():
        carry_ref[...] = jnp.where(is_ctx, 0.0, h0_ref[jseq])
    a, v = _lru_terms(uc, wg_ref, bg_ref, lam_ref)
    y, carry = _lru_scan(a, v, carry_ref[...], bool(direction))
    carry_ref[...] = carry
    return y


def _odd_fwd_kernel(u_ref, up_ref, un_ref, cw_ref, cb_ref, wg_ref, bg_ref, lam_ref, h0_ref,
                    yf_ref, uc_ref, hs_ref, carry_ref, *, cfg):
    ib = pl.program_id(0)
    _, _, first, last = _block_position(cfg, ib)
    uc = _conv_centred(u_ref[...], up_ref[...], un_ref[...], first, last, cw_ref, cb_ref)
    uc_ref[...] = uc
    yf_ref[...] = _lru_block(uc, wg_ref, bg_ref, lam_ref, h0_ref, carry_ref, cfg, ib, 0)
    hs_ref[...] = carry_ref[...]


def _odd_bwd_kernel(g_ref, uc_ref, yf_ref, mod_ref, wg_ref, bg_ref, lam_ref, h0_ref, wout_ref,
                    o_ref, hs_ref, carry_ref, *, cfg):
    nblk = pl.num_programs(0)
    ib = nblk - 1 - pl.program_id(0)
    y = _lru_block(uc_ref[...], wg_ref, bg_ref, lam_ref, h0_ref, carry_ref, cfg, ib, 1)
    hs_ref[...] = carry_ref[...]
    mix = (g_ref[...] * (yf_ref[...] + y)).astype(BF16)
    grp = _group_of_tile(cfg, ib, TS)
    gate = mod_ref[2, pl.ds(grp, 1), :]
    o_ref[...] = gate * jnp.dot(mix, wout_ref[...], preferred_element_type=F32)


def _odd_scans(cfg, l, gu, mods, cw, cb, wg, bg, lam, h0, w_out):
    n = cfg.n_tok
    nblk = n // TS
    r8 = TS // SUBLANES
    last8 = n // SUBLANES - 1

    def specs(rev):
        ib = (lambda i: nblk - 1 - i) if rev else (lambda i: i)
        cur = lambda col: pl.BlockSpec((TS, D_RNN), lambda i: (ib(i), col))
        prev = pl.BlockSpec((2 * SUBLANES, D_RNN), lambda i: (jnp.maximum(ib(i) * (r8 // 2) - 1, 0), 1))
        nxt = pl.BlockSpec((SUBLANES, D_RNN), lambda i: (jnp.minimum((ib(i) + 1) * r8, last8), 1))
        hs = pl.BlockSpec((None, 1, D_RNN), lambda i: (ib(i), 0, 0))
        return cur, prev, nxt, hs

    def weights(direction):
        return [pl.BlockSpec((None, LRU_BLOCKS, LRU_BLOCK_W, 2 * LRU_BLOCK_W), lambda i: (direction, 0, 0, 0)),
                pl.BlockSpec((None, 2, D_RNN), lambda i: (direction, 0, 0)),
                pl.BlockSpec((None, 1, D_RNN), lambda i: (direction, 0, 0)),
                pl.BlockSpec((None, cfg.nl, 1, D_RNN), lambda i: (direction, 0, 0, 0))]

    cur, prev, nxt, hs = specs(False)
    tok_shape = jax.ShapeDtypeStruct((n, D_RNN), F32)
    y_f, uc, hs_f = pl.pallas_call(
        functools.partial(_odd_fwd_kernel, cfg=cfg),
        grid=(nblk,),
        in_specs=[cur(1), prev, nxt, _const_spec((CONV_W, D_RNN)), _const_spec((1, D_RNN))] + weights(0),
        out_specs=[cur(0), cur(0), hs],
        out_shape=[tok_shape, tok_shape, jax.ShapeDtypeStruct((nblk, 1, D_RNN), F32)],
        scratch_shapes=[pltpu.VMEM((1, D_RNN), F32)],
        compiler_params=_cparams(1),
        name=f"lru_fwd{l}",
    )(gu, gu, gu, cw, cb, wg, bg, lam, h0)
    cur, _, _, hs = specs(True)
    upd, hs_b = pl.pallas_call(
        functools.partial(_odd_bwd_kernel, cfg=cfg),
        grid=(nblk,),
        in_specs=[cur(0), cur(0), cur(0), _mod_spec(l)] + weights(1) + [_const_spec((D_RNN, D_MODEL))],
        out_specs=[cur(0), hs],
        out_shape=[tok_shape, jax.ShapeDtypeStruct((nblk, 1, D_RNN), F32)],
        scratch_shapes=[pltpu.VMEM((1, D_RNN), F32)],
        compiler_params=_cparams(1),
        name=f"lru_bwd{l}",
    )(gu, uc, y_f, mods, wg, bg, lam, h0, w_out)
    return upd, hs_f, hs_b


def _rope_table(tl):
    pos = np.arange(tl)
    nf = HEAD_DIM // 4
    inv = ROPE_BASE ** (-np.arange(nf, dtype=np.float32) / nf)
    ar = (pos // GRID_W).astype(np.float32)[:, None] * inv
    ac = (pos % GRID_W).astype(np.float32)[:, None] * inv
    ang = np.concatenate([ar, ar, ac, ac] * 2, axis=-1)
    lo = (np.arange(LANES) % (HEAD_DIM // 2)) < nf
    cos, sin = np.cos(ang), np.sin(ang)
    tab = np.concatenate([cos, np.where(lo, -sin, 0.0), np.where(lo, 0.0, sin)], axis=-1)
    return jnp.asarray(tab, F32)


def _dup_heads(w):
    lead = w.shape[:-1]
    w = w.reshape(lead + (A_KV_HEADS, 1, HEAD_DIM))
    return jnp.broadcast_to(w, lead + (A_KV_HEADS, 2, HEAD_DIM)).reshape(lead + (2 * LANES,))


def _pack_even_in(w):
    kv = A_KV_HEADS * HEAD_DIM
    q, k, v = w[:, :A_WIDTH], w[:, A_WIDTH:A_WIDTH + kv], w[:, A_WIDTH + kv:A_WIDTH + 2 * kv]
    rest = w[:, A_WIDTH + 2 * kv:]
    pad = jnp.zeros((w.shape[0], LANES - 2 * GLA_RANK), w.dtype)
    return jnp.concatenate([q, _dup_heads(k), _dup_heads(v), rest, pad], axis=1).astype(BF16)


def _pack_alpha(w_alpha, b_alpha):
    w = jnp.zeros((LANES, 2 * B_QK), F32)
    w = w.at[0:GLA_RANK, 0:B_QK].set(w_alpha[0]).at[GLA_RANK:2 * GLA_RANK, B_QK:].set(w_alpha[1])
    return w.astype(BF16), b_alpha.reshape(1, 2 * B_QK)


def _forward(cfg, x_prompt, x_sample, c, cache_k, cache_v, state_gla, state_lru, c_ctx,
             w_ada, b_ada, norm_mix, norm_ffn, w_in_even, attn_sink, w_alpha, b_alpha, gla_gain,
             w_out_even, w_in_odd, conv_w, conv_b, w_gate_a, b_gate_a, w_gate_x, b_gate_x,
             lru_lambda, w_out_odd, w_ffn_in, w_ffn_out, norm_final):
    assert cfg.tc == TS and cfg.tl % TM == 0 and cfg.n_ctx % TM == 0 and cfg.nl + 1 <= SUBLANES
    x = (x_prompt.reshape(cfg.n_ctx, D_MODEL), x_sample.reshape(cfg.n_lat, D_MODEL))
    cvec = jnp.concatenate([c_ctx[None, :], c, jnp.zeros((SUBLANES - 1 - cfg.nl, D_MODEL), F32)], axis=0)
    mods = _ada(cvec, w_ada, b_ada)
    rope_tab = _rope_table(cfg.tl)
    nf = norm_final.reshape(1, D_MODEL)
    w_ffn_in_bf, w_ffn_out_bf, w_out_even_bf = (w.astype(BF16) for w in (w_ffn_in, w_ffn_out, w_out_even))
    new_k, new_v, new_gla, new_lru = [], [], [], []
    for l in range(DEPTH):
        nw = norm_mix[l].reshape(1, D_MODEL)
        if l % 2 == 0:
            e = l // 2
            za, zb, k_ctx, v_ctx = _even_in(cfg, l, x, mods, nw, _pack_even_in(w_in_even[e]), rope_tab)
            new_k.append(k_ctx.reshape(cfg.nc, cfg.tc, A_KV_HEADS, HEAD_DIM))
            new_v.append(v_ctx.reshape(cfg.nc, cfg.tc, A_KV_HEADS, HEAD_DIM))
            ck2 = _dup_heads(cache_k[:, e].reshape(cfg.nl, PAST_LEN, A_KV_HEADS * HEAD_DIM)).astype(BF16)
            cv2 = _dup_heads(cache_v[:, e].reshape(cfg.nl, PAST_LEN, A_KV_HEADS * HEAD_DIM)).astype(BF16)
            oa_ctx = _attn_ctx(cfg, l, za, attn_sink[e])
            oa_lat = _attn_lat(cfg, l, za, ck2, cv2, attn_sink[e])
            wal, bal = _pack_alpha(w_alpha[e], b_alpha[e])
            gain = gla_gain[e].reshape(1, B_WIDTH)
            ob_ctx, s_new = _gla_ctx(cfg, l, zb, wal, bal, gain)
            ob_lat = _gla_lat(cfg, l, zb, wal, bal, gain, state_gla[:, e])
            new_gla.append(s_new)
            pre_args = (oa_ctx, oa_lat, ob_ctx, ob_lat, w_out_even_bf)
        else:
            o = l // 2
            gu = _odd_in(cfg, l, x, mods, nw, w_in_odd[o].astype(BF16))
            wg = (0.5 * jnp.concatenate([w_gate_a[o], w_gate_x[o]], axis=-1)).astype(BF16)
            bg = 0.5 * jnp.stack([b_gate_a[o], b_gate_x[o]], axis=1)
            lam = lru_lambda[o].reshape(2, 1, D_RNN)
            h0 = jnp.transpose(state_lru[:, o], (1, 0, 2)).reshape(2, cfg.nl, 1, D_RNN)
            upd, hs_f, hs_b = _odd_scans(cfg, l, gu, mods, conv_w[o], conv_b[o].reshape(1, D_RNN),
                                         wg, bg, lam, h0, w_out_odd[o].astype(BF16))
            new_lru.append(jnp.stack([hs_f[:cfg.nc, 0], hs_b[:cfg.nc, 0]], axis=1))
            pre_args = (upd,)
        x = _ffn(cfg, l, x, pre_args, mods, norm_ffn[l].reshape(1, D_MODEL), w_ffn_in_bf, w_ffn_out_bf, nf,
                 final=(l == DEPTH - 1))
    y_prompt = x[0].reshape(cfg.nc, cfg.tc, D_MODEL)
    y_sample = x[1].reshape(cfg.nl, cfg.tl, D_MODEL)
    return (y_prompt, y_sample, jnp.stack(new_k, axis=1), jnp.stack(new_v, axis=1),
            jnp.stack(new_gla, axis=1), jnp.stack(new_lru, axis=1))


def kernel(x_prompt, x_sample, c, cache_k, cache_v, state_gla, state_lru, c_ctx, w_ada, b_ada, norm_mix, norm_ffn, w_in_even, attn_sink, w_alpha, b_alpha, gla_gain, w_out_even, w_in_odd, conv_w, conv_b, w_gate_a, b_gate_a, w_gate_x, b_gate_x, lru_lambda, w_out_odd, w_ffn_in, w_ffn_out, norm_final):
    cfg = Cfg(nc=x_prompt.shape[0], tc=x_prompt.shape[1], nl=x_sample.shape[0], tl=x_sample.shape[1])
    return _forward(cfg, x_prompt, x_sample, c, cache_k, cache_v, state_gla, state_lru, c_ctx,
                    w_ada, b_ada, norm_mix, norm_ffn, w_in_even, attn_sink, w_alpha, b_alpha, gla_gain,
                    w_out_even, w_in_odd, conv_w, conv_b, w_gate_a, b_gate_a, w_gate_x, b_gate_x,
                    lru_lambda, w_out_odd, w_ffn_in, w_ffn_out, norm_final)
```

```python
import functools
from typing import NamedTuple

import jax
import jax.numpy as jnp
import numpy as np
from jax import lax
from jax.experimental import pallas as pl
from jax.experimental.pallas import tpu as pltpu

F32 = jnp.float32
BF16 = jnp.bfloat16

D_MODEL = 1024
DEPTH = 4
EPS = 1e-6

HEAD_DIM = 64
A_Q_HEADS = 8
A_KV_HEADS = 2
A_GROUPS = A_Q_HEADS // A_KV_HEADS
A_WIDTH = A_Q_HEADS * HEAD_DIM
ATT_BLOCK = 128
ROPE_BASE = 10000.0
GRID_W = 64
NEG_INF = -1e30
PAST_LEN = 256

B_HEADS = 4
B_DK = 64
B_DV = 128
B_QK = B_HEADS * B_DK
B_WIDTH = B_HEADS * B_DV
GLA_RANK = 16
GLA_NORMALIZER = 16.0
GLA_CHUNK = 64

D_RNN = D_MODEL
LRU_BLOCK_W = 256
LRU_BLOCKS = D_RNN // LRU_BLOCK_W
CONV_W = 4
LRU_C = 8.0

D_FF = 2816

LANES = 128
SUBLANES = 8

ZA_W = A_WIDTH + 2 * 2 * LANES
ZB_LR = B_QK + B_QK + B_WIDTH + B_WIDTH
ZB_W = ZB_LR + LANES
P_PACK = ZA_W + ZB_W

TM = 512
TS = 256
FF_CHUNK = 256

VMEM_LIMIT = 56 * 1024 * 1024


class Cfg(NamedTuple):
    nc: int
    tc: int
    nl: int
    tl: int

    @property
    def n_ctx(self):
        return self.nc * self.tc

    @property
    def n_lat(self):
        return self.nl * self.tl

    @property
    def n_tok(self):
        return self.n_ctx + self.n_lat


def _cparams(n_axes):
    return pltpu.CompilerParams(dimension_semantics=("arbitrary",) * n_axes,
                                vmem_limit_bytes=VMEM_LIMIT)


def _const_spec(shape):
    nd = len(shape)
    return pl.BlockSpec(shape, lambda *_: (0,) * nd)


def _silu(x):
    return x * jax.nn.sigmoid(x)


def _group_of_tile(cfg, i, tile):
    nct = cfg.n_ctx // tile
    per_lat = cfg.tl // tile
    return jnp.where(i < nct, 0, 1 + jnp.maximum(i - nct, 0) // per_lat)


def _modnorm(x, nw, shift, scale):
    ms = jnp.mean(x * x, axis=-1, keepdims=True)
    y = x * lax.rsqrt(ms + EPS) * nw
    return y * (1.0 + scale) + shift


def _mod_rows(mod_ref, g, first):
    return tuple(mod_ref[first + j, pl.ds(g, 1), :] for j in range(3))


def _ada_kernel(c_ref, w_ref, b_ref, o_ref):
    s = _silu(c_ref[...]).astype(BF16)
    o_ref[...] = jnp.dot(s, w_ref[...].astype(BF16), preferred_element_type=F32) + b_ref[...]


def _ada(cvec, w_ada, b_ada):
    b4 = b_ada.reshape(DEPTH, 6, 1, D_MODEL)
    return pl.pallas_call(
        _ada_kernel,
        grid=(DEPTH, 6),
        in_specs=[
            _const_spec((SUBLANES, D_MODEL)),
            pl.BlockSpec((None, D_MODEL, D_MODEL), lambda l, j: (l, 0, j)),
            pl.BlockSpec((None, None, 1, D_MODEL), lambda l, j: (l, j, 0, 0)),
        ],
        out_specs=pl.BlockSpec((None, None, SUBLANES, D_MODEL), lambda l, j: (l, j, 0, 0)),
        out_shape=jax.ShapeDtypeStruct((DEPTH, 6, SUBLANES, D_MODEL), F32),
        compiler_params=_cparams(2),
        name="ada",
    )(cvec, w_ada, b4)


def _mod_spec(l):
    return pl.BlockSpec((None, 6, SUBLANES, D_MODEL), lambda *_: (l, 0, 0, 0))


def _ffn_tile(x, nw, shift, scale, gate, win_ref, wout_ref):
    h = _modnorm(x, nw, shift, scale).astype(BF16)
    acc = None
    for c in range(D_FF // FF_CHUNK):
        lo, hi = c * FF_CHUNK, (c + 1) * FF_CHUNK
        g = jnp.dot(h, win_ref[:, lo:hi], preferred_element_type=F32)
        u = jnp.dot(h, win_ref[:, D_FF + lo:D_FF + hi], preferred_element_type=F32)
        a = (_silu(g) * u).astype(BF16)
        part = jnp.dot(a, wout_ref[lo:hi, :], preferred_element_type=F32)
        acc = part if acc is None else acc + part
    return x + gate * acc


def _x_args(cfg, x):
    if not isinstance(x, tuple):
        return [x], [pl.BlockSpec((TM, D_MODEL), lambda i: (i, 0))]
    nct = cfg.n_ctx // TM
    return list(x), [pl.BlockSpec((TM, D_MODEL), lambda i: (jnp.minimum(i, nct - 1), 0)),
                     pl.BlockSpec((TM, D_MODEL), lambda i: (jnp.maximum(i - nct, 0), 0))]


def _load_x(cfg, x_refs, i):
    if len(x_refs) == 1:
        return x_refs[0][...]
    return jnp.where(i < cfg.n_ctx // TM, x_refs[0][...], x_refs[1][...])


def _to_scan_order(x):
    nb, d = x.shape[0] // TS, x.shape[1]
    return jnp.swapaxes(x.reshape(nb, SUBLANES, TS // SUBLANES, d), 1, 2).reshape(nb * TS, d)


def _from_scan_order(x):
    nb, d = x.shape[0] // TS, x.shape[1]
    return jnp.swapaxes(x.reshape(nb, TS // SUBLANES, SUBLANES, d), 1, 2).reshape(nb * TS, d)


def _ffn_kernel(*refs, cfg, n_x, pre, final):
    x_refs, refs = refs[:n_x], refs[n_x:]
    i = pl.program_id(0)
    g = _group_of_tile(cfg, i, TM)
    x = _load_x(cfg, x_refs, i)
    if pre == "even":
        (oac_ref, oal_ref, obc_ref, obl_ref, wmix_ref), refs = refs[:5], refs[5:]
    else:
        d_ref, refs = refs[0], refs[1:]
    mod_ref, nw_ref, win_ref, wout_ref, nf_ref = refs[:5]
    outs = refs[5:]
    if pre == "even":
        is_ctx_tile = i < cfg.n_ctx // TM
        o = jnp.concatenate([jnp.where(is_ctx_tile, oac_ref[...], oal_ref[...]),
                             jnp.where(is_ctx_tile, obc_ref[...], obl_ref[...])], axis=1)
        x = x + mod_ref[2, pl.ds(g, 1), :] * jnp.dot(o, wmix_ref[...], preferred_element_type=F32)
    else:
        x = x + _from_scan_order(d_ref[...])
    shift, scale, gate = _mod_rows(mod_ref, g, 3)
    y = _ffn_tile(x, nw_ref[...], shift, scale, gate, win_ref, wout_ref)
    if not final:
        outs[0][...] = y
        return
    ms = jnp.mean(y * y, axis=-1, keepdims=True)
    y = y * lax.rsqrt(ms + EPS) * nf_ref[...]
    is_ctx = i < cfg.n_ctx // TM

    @pl.when(is_ctx)
    def _():
        outs[0][...] = y

    @pl.when(jnp.logical_not(is_ctx))
    def _():
        outs[1][...] = y


def _ffn(cfg, l, x, pre_args, mods, nw, win, wout, nf, final):
    n = cfg.n_tok
    nct = cfg.n_ctx // TM
    tok = lambda w: pl.BlockSpec((TM, w), lambda i: (i, 0))
    ctx = lambda w: pl.BlockSpec((TM, w), lambda i: (jnp.minimum(i, nct - 1), 0))
    lat = lambda w: pl.BlockSpec((TM, w), lambda i: (jnp.maximum(i - nct, 0), 0))
    x_ops, x_specs = _x_args(cfg, x)
    if l % 2 == 0:
        pre = "even"
        pre_specs = [ctx(A_WIDTH), lat(A_WIDTH), ctx(B_WIDTH), lat(B_WIDTH),
                     pl.BlockSpec((None, A_WIDTH + B_WIDTH, D_MODEL), lambda i: (l // 2, 0, 0))]
    else:
        pre, pre_specs = "odd", [tok(D_MODEL)]
    if final:
        out_specs = [pl.BlockSpec((TM, D_MODEL), lambda i: (jnp.minimum(i, nct - 1), 0)),
                     pl.BlockSpec((TM, D_MODEL), lambda i: (jnp.maximum(i - nct, 0), 0))]
        out_shape = [jax.ShapeDtypeStruct((cfg.n_ctx, D_MODEL), F32),
                     jax.ShapeDtypeStruct((cfg.n_lat, D_MODEL), F32)]
    else:
        out_specs, out_shape = tok(D_MODEL), jax.ShapeDtypeStruct((n, D_MODEL), F32)
    return pl.pallas_call(
        functools.partial(_ffn_kernel, cfg=cfg, n_x=len(x_ops), pre=pre, final=final),
        grid=(n // TM,),
        in_specs=x_specs + pre_specs + [
            _mod_spec(l), _const_spec((1, D_MODEL)),
            pl.BlockSpec((None, D_MODEL, 2 * D_FF), lambda i: (l, 0, 0), pipeline_mode=pl.Buffered(1)),
            pl.BlockSpec((None, D_FF, D_MODEL), lambda i: (l, 0, 0), pipeline_mode=pl.Buffered(1)),
            _const_spec((1, D_MODEL))],
        out_specs=out_specs,
        out_shape=out_shape,
        compiler_params=_cparams(1),
        name=f"ffn{l}",
    )(*x_ops, *pre_args, mods, nw, win, wout, nf)


def _rope(x, tab_ref):
    cos, sin_a, sin_b = tab_ref[:, 0:LANES], tab_ref[:, LANES:2 * LANES], tab_ref[:, 2 * LANES:3 * LANES]
    parts = []
    for j in range(x.shape[1] // LANES):
        xs = x[:, j * LANES:(j + 1) * LANES]
        parts.append(xs * cos + pltpu.roll(xs, LANES - 16, 1) * sin_a + pltpu.roll(xs, 16, 1) * sin_b)
    return jnp.concatenate(parts, axis=1)


def _even_in_kernel(*refs, cfg, n_x):
    x_refs, (mod_ref, nw_ref, w_ref, tab_ref, za_ref, zb_ref, k_ref, v_ref) = refs[:n_x], refs[n_x:]
    i = pl.program_id(0)
    g = _group_of_tile(cfg, i, TM)
    shift, scale, _ = _mod_rows(mod_ref, g, 0)
    h = _modnorm(_load_x(cfg, x_refs, i), nw_ref[...], shift, scale).astype(BF16)
    z = jnp.dot(h, w_ref[...], preferred_element_type=F32)
    q = z[:, 0:A_WIDTH] * (HEAD_DIM ** -0.5)
    k2 = z[:, A_WIDTH:A_WIDTH + 2 * LANES]
    v2 = z[:, A_WIDTH + 2 * LANES:ZA_W]
    za_ref[:, A_WIDTH + 2 * LANES:ZA_W] = v2.astype(BF16)
    zb_ref[:, 0:B_QK] = z[:, ZA_W:ZA_W + B_QK] * (B_DK ** -0.5)
    zb_ref[:, B_QK:ZB_W] = z[:, ZA_W + B_QK:P_PACK]

    @pl.when(g == 0)
    def _():
        za_ref[:, 0:A_WIDTH] = q.astype(BF16)
        za_ref[:, A_WIDTH:A_WIDTH + 2 * LANES] = k2.astype(BF16)
        lo = lax.broadcasted_iota(jnp.int32, (TM, LANES), 1) < HEAD_DIM
        k_ref[...] = jnp.where(lo, k2[:, 0:LANES], k2[:, LANES:2 * LANES])
        v_ref[...] = jnp.where(lo, v2[:, 0:LANES], v2[:, LANES:2 * LANES])

    @pl.when(g != 0)
    def _():
        za_ref[:, 0:A_WIDTH] = _rope(q, tab_ref).astype(BF16)
        za_ref[:, A_WIDTH:A_WIDTH + 2 * LANES] = _rope(k2, tab_ref).astype(BF16)


def _even_in(cfg, l, x, mods, nw, w_pack, rope_tab):
    n = cfg.n_tok
    nct = cfg.n_ctx // TM
    per_lat = cfg.tl // TM
    tok = lambda w: pl.BlockSpec((TM, w), lambda i: (i, 0))
    x_ops, x_specs = _x_args(cfg, x)
    ctx_kv = pl.BlockSpec((TM, LANES), lambda i: (jnp.minimum(i, nct - 1), 0))
    return pl.pallas_call(
        functools.partial(_even_in_kernel, cfg=cfg, n_x=len(x_ops)),
        grid=(n // TM,),
        in_specs=x_specs + [_mod_spec(l), _const_spec((1, D_MODEL)),
                            pl.BlockSpec((D_MODEL, P_PACK), lambda i: (0, 0), pipeline_mode=pl.Buffered(1)),
                            pl.BlockSpec((TM, 3 * LANES), lambda i: (jnp.maximum(i - nct, 0) % per_lat, 0))],
        out_specs=[tok(ZA_W), tok(ZB_W), ctx_kv, ctx_kv],
        out_shape=[jax.ShapeDtypeStruct((n, ZA_W), BF16),
                   jax.ShapeDtypeStruct((n, ZB_W), F32),
                   jax.ShapeDtypeStruct((cfg.n_ctx, LANES), F32),
                   jax.ShapeDtypeStruct((cfg.n_ctx, LANES), F32)],
        compiler_params=_cparams(1),
        name=f"even_in{l}",
    )(*x_ops, mods, nw, w_pack, rope_tab)


def _sink_attention(units, sink_ref):
    scores = []
    for q, k_all, _, mask, kvh in units:
        tq = q.shape[0]
        lo = lax.broadcasted_iota(jnp.int32, (tq, LANES), 1) < HEAD_DIM
        qs = []
        for gq in range(A_GROUPS):
            h = kvh * A_GROUPS + gq
            pair = q[:, (h // 2) * LANES:(h // 2 + 1) * LANES]
            keep = lo if h % 2 == 0 else jnp.logical_not(lo)
            qs.append(jnp.where(keep, pair, jnp.zeros_like(pair)))
        qst = jnp.concatenate(qs, axis=0)
        s = lax.dot_general(qst, k_all, (((1,), (1,)), ((), ())), preferred_element_type=F32)
        scores.append(s if mask is None else jnp.where(mask, s, NEG_INF))
    probs = []
    for (q, _, _, _, kvh), s in zip(units, scores):
        tq = q.shape[0]
        rows = lax.broadcasted_iota(jnp.int32, (A_GROUPS * tq, 1), 0) // tq
        sink = jnp.zeros((A_GROUPS * tq, 1), F32)
        for gq in range(A_GROUPS):
            sink = jnp.where(rows == gq, sink_ref[kvh * A_GROUPS + gq], sink)
        m = jnp.maximum(jnp.max(s, axis=-1, keepdims=True), sink)
        p = jnp.exp(s - m)
        den = jnp.sum(p, axis=-1, keepdims=True) + jnp.exp(sink - m)
        probs.append((p.astype(BF16), den))
    outs = []
    for (q, _, v_all, _, _), (p, den) in zip(units, probs):
        tq = q.shape[0]
        lo = lax.broadcasted_iota(jnp.int32, (tq, LANES), 1) < HEAD_DIM
        o = jnp.dot(p, v_all, preferred_element_type=F32) / den
        pairs = [jnp.where(lo, o[(2 * pr) * tq:(2 * pr + 1) * tq], o[(2 * pr + 1) * tq:(2 * pr + 2) * tq])
                 for pr in range(2)]
        outs.append(jnp.concatenate(pairs, axis=1).astype(BF16))
    return outs


def _attn_ctx_kernel(sink_ref, q_ref, k_ref, v_ref, o_ref):
    q = q_ref[...]
    units = [(q, k_ref[:, kvh * LANES:(kvh + 1) * LANES], v_ref[:, kvh * LANES:(kvh + 1) * LANES], None, kvh)
             for kvh in range(A_KV_HEADS)]
    for kvh, o in enumerate(_sink_attention(units, sink_ref)):
        o_ref[:, 2 * kvh * LANES:2 * (kvh + 1) * LANES] = o


def _attn_ctx(cfg, l, za, sink):
    tc = cfg.tc
    return pl.pallas_call(
        _attn_ctx_kernel,
        grid=(cfg.nc,),
        in_specs=[pl.BlockSpec(memory_space=pltpu.SMEM),
                  pl.BlockSpec((tc, A_WIDTH), lambda b: (b, 0)),
                  pl.BlockSpec((tc, 2 * LANES), lambda b: (b, A_WIDTH // (2 * LANES))),
                  pl.BlockSpec((tc, 2 * LANES), lambda b: (b, A_WIDTH // (2 * LANES) + 1))],
        out_specs=pl.BlockSpec((tc, A_WIDTH), lambda b: (b, 0)),
        out_shape=jax.ShapeDtypeStruct((cfg.n_ctx, A_WIDTH), BF16),
        compiler_params=_cparams(1),
        name=f"attn_ctx{l}",
    )(sink, za, za, za)


ATT_STEP = 4 * ATT_BLOCK
ATT_WIN = 3 * ATT_BLOCK


def _attn_lat_kernel(sink_ref, q_ref, k_ref, v_ref, ck_ref, cv_ref, o_ref):
    n = pl.program_id(1)
    tl = k_ref.shape[0]
    shape = (A_GROUPS * ATT_BLOCK, PAST_LEN + ATT_WIN)
    r = lax.broadcasted_iota(jnp.int32, shape, 0) % ATT_BLOCK
    c = lax.broadcasted_iota(jnp.int32, shape, 1) - PAST_LEN
    units = []
    for j in range(ATT_STEP // ATT_BLOCK):
        q0 = (n * (ATT_STEP // ATT_BLOCK) + j) * ATT_BLOCK
        start = pl.multiple_of(jnp.clip(q0 - ATT_BLOCK, 0, tl - ATT_WIN), ATT_BLOCK)
        dist = (q0 - start) + r - c
        mask = jnp.logical_or(c < 0, jnp.abs(dist) <= ATT_BLOCK)
        q = q_ref[j * ATT_BLOCK:(j + 1) * ATT_BLOCK, :]
        for kvh in range(A_KV_HEADS):
            sl = slice(kvh * LANES, (kvh + 1) * LANES)
            k_all = jnp.concatenate([ck_ref[:, sl], k_ref[pl.ds(start, ATT_WIN), sl]], axis=0)
            v_all = jnp.concatenate([cv_ref[:, sl], v_ref[pl.ds(start, ATT_WIN), sl]], axis=0)
            units.append((q, k_all, v_all, mask, kvh))
    for u, o in enumerate(_sink_attention(units, sink_ref)):
        j, kvh = divmod(u, A_KV_HEADS)
        o_ref[j * ATT_BLOCK:(j + 1) * ATT_BLOCK, 2 * kvh * LANES:2 * (kvh + 1) * LANES] = o


def _attn_lat(cfg, l, za, cache_k2, cache_v2, sink):
    assert cfg.n_ctx % cfg.tl == 0 and cfg.tl % ATT_STEP == 0 and cfg.tl >= ATT_WIN
    steps = cfg.tl // ATT_STEP
    base = cfg.n_ctx // ATT_STEP
    kcol = A_WIDTH // (2 * LANES)
    seq = lambda col: pl.BlockSpec((cfg.tl, 2 * LANES), lambda b, n: (cfg.n_ctx // cfg.tl + b, col))
    cache = pl.BlockSpec((None, PAST_LEN, 2 * LANES), lambda b, n: (b, 0, 0))
    return pl.pallas_call(
        _attn_lat_kernel,
        grid=(cfg.nl, steps),
        in_specs=[pl.BlockSpec(memory_space=pltpu.SMEM),
                  pl.BlockSpec((ATT_STEP, A_WIDTH), lambda b, n: (base + b * steps + n, 0)),
                  seq(kcol), seq(kcol + 1), cache, cache],
        out_specs=pl.BlockSpec((ATT_STEP, A_WIDTH), lambda b, n: (b * steps + n, 0)),
        out_shape=jax.ShapeDtypeStruct((cfg.n_lat, A_WIDTH), BF16),
        compiler_params=_cparams(2),
        name=f"attn_lat{l}",
    )(sink, za, za, za, cache_k2, cache_v2)


def _gla_chunk_prepare(q, k, v, la, reverse):
    cn = GLA_CHUNK
    row = lax.broadcasted_iota(jnp.int32, (cn, B_QK), 0)
    b = la
    s = 1
    while s < cn:
        if reverse:
            b = b + jnp.where(row < cn - s, pltpu.roll(b, cn - s, 0), 0.0)
        else:
            b = b + jnp.where(row >= s, pltpu.roll(b, s, 0), 0.0)
        s *= 2
    b_last = b[0:1, :] if reverse else b[cn - 1:cn, :]
    q_e = q * jnp.exp(b)
    k_e = (k * jnp.exp(-b)).astype(BF16)
    k_s = (k * jnp.exp(b_last - b)).astype(BF16)
    vb = v.astype(BF16)
    row_t = row[:, 0:LANES]
    lane_t = lax.broadcasted_iota(jnp.int32, (cn, LANES), 1)
    lo = lane_t < B_DK
    col = lane_t % cn
    tri = (col >= row_t) if reverse else (col <= row_t)
    q_tiles, att_tiles = [], []
    for t in range(B_HEADS // 2):
        sl = slice(t * LANES, (t + 1) * LANES)
        k_t = k_e[:, sl]
        zero = jnp.zeros_like(k_t)
        kbd = jnp.concatenate([jnp.where(lo, k_t, zero), jnp.where(lo, zero, k_t)], axis=0)
        q_t = q_e[:, sl]
        att = lax.dot_general(q_t.astype(BF16), kbd, (((1,), (1,)), ((), ())), preferred_element_type=F32)
        q_tiles.append(q_t)
        att_tiles.append(jnp.where(tri, att, 0.0))
    upd = lax.dot_general(k_s, vb, (((0,), (0,)), ((), ())), preferred_element_type=F32)
    own = jnp.concatenate([upd[h * B_DK:(h + 1) * B_DK, h * B_DV:(h + 1) * B_DV] for h in range(B_HEADS)], axis=0)
    decay = jnp.broadcast_to(jnp.exp(b_last), (LANES, B_QK)).T
    return q_tiles, att_tiles, vb, own, decay


def _gla_chunk_output(q_tiles, att_tiles, vb, st):
    lo = lax.broadcasted_iota(jnp.int32, (GLA_CHUNK, LANES), 1) < B_DK
    st_b = st.astype(BF16)
    outs = []
    for h in range(B_HEADS):
        att, q_t = att_tiles[h // 2], q_tiles[h // 2]
        x = (jnp.where(lo, att, pltpu.roll(q_t, B_DK, 1)) if h % 2 == 0
             else jnp.where(lo, pltpu.roll(att, B_DK, 1), q_t))
        rhs = jnp.concatenate([vb[:, h * B_DV:(h + 1) * B_DV], st_b[h * B_DK:(h + 1) * B_DK, :]], axis=0)
        outs.append(jnp.dot(x.astype(BF16), rhs, preferred_element_type=F32))
    return jnp.concatenate(outs, axis=1)


def _gla_log_decay(lr_pad, wal_ref, bal_ref, direction):
    z = jnp.dot(lr_pad.astype(BF16), wal_ref[:, direction * B_QK:(direction + 1) * B_QK],
                preferred_element_type=F32) + bal_ref[:, direction * B_QK:(direction + 1) * B_QK]
    return jax.nn.log_sigmoid(z) / GLA_NORMALIZER


def _gla_blocks(jobs, wal_ref, bal_ref):
    prepared = []
    for qkv_ref, lr_ref, _, direction in jobs:
        nchunk = qkv_ref.shape[0] // GLA_CHUNK
        la_all = _gla_log_decay(lr_ref[...], wal_ref, bal_ref, direction)
        chunks = {}
        for ci in range(nchunk):
            rs = slice(ci * GLA_CHUNK, (ci + 1) * GLA_CHUNK)
            chunks[ci] = _gla_chunk_prepare(qkv_ref[rs, 0:B_QK], qkv_ref[rs, B_QK:2 * B_QK],
                                            qkv_ref[rs, 2 * B_QK:2 * B_QK + B_WIDTH], la_all[rs, :],
                                            bool(direction))
        prepared.append(chunks)
    results = []
    for (qkv_ref, _, st, direction), chunks in zip(jobs, prepared):
        nchunk = len(chunks)
        states = {}
        for ci in (range(nchunk - 1, -1, -1) if direction else range(nchunk)):
            states[ci] = st
            _, _, _, own, decay = chunks[ci]
            st = st * decay + own
        outs = [_gla_chunk_output(*chunks[ci][:3], states[ci]) for ci in range(nchunk)]
        results.append((jnp.concatenate(outs, axis=0), st))
    return results


def _stack_state(s_ref):
    return s_ref[...].reshape(B_QK, B_DV)


def _unstack_state(st, s_ref):
    s_ref[...] = st.reshape(B_HEADS, B_DK, B_DV)


def _gla_finish(o, r, gain):
    parts = []
    for h in range(B_HEADS):
        oh = o[:, h * B_DV:(h + 1) * B_DV]
        parts.append(oh * lax.rsqrt(jnp.mean(oh * oh, axis=-1, keepdims=True) + EPS))
    return (jnp.concatenate(parts, axis=1) * gain * _silu(r)).astype(BF16)


def _gla_ctx_kernel(qkv_ref, r_ref, lr_ref, wal_ref, bal_ref, gain_ref, ob_ref, snew_ref):
    zero = jnp.zeros((B_QK, B_DV), F32)
    (o_f, st_f), (o_b, st_b) = _gla_blocks([(qkv_ref, lr_ref, zero, 0), (qkv_ref, lr_ref, zero, 1)],
                                           wal_ref, bal_ref)
    _unstack_state(st_f, snew_ref.at[0])
    _unstack_state(st_b, snew_ref.at[1])
    ob_ref[...] = _gla_finish(o_f + o_b, r_ref[...], gain_ref[...])


def _gla_ctx(cfg, l, zb, wal, bal, gain):
    tc = cfg.tc
    return pl.pallas_call(
        _gla_ctx_kernel,
        grid=(cfg.nc,),
        in_specs=[pl.BlockSpec((tc, 2 * B_QK + B_WIDTH), lambda b: (b, 0)),
                  pl.BlockSpec((tc, B_WIDTH), lambda b: (b, (2 * B_QK + B_WIDTH) // B_WIDTH)),
                  pl.BlockSpec((tc, LANES), lambda b: (b, ZB_LR // LANES)),
                  _const_spec((LANES, 2 * B_QK)), _const_spec((1, 2 * B_QK)), _const_spec((1, B_WIDTH))],
        out_specs=[pl.BlockSpec((tc, B_WIDTH), lambda b: (b, 0)),
                   pl.BlockSpec((None, 2, B_HEADS, B_DK, B_DV), lambda b: (b, 0, 0, 0, 0))],
        out_shape=[jax.ShapeDtypeStruct((cfg.n_ctx, B_WIDTH), BF16),
                   jax.ShapeDtypeStruct((cfg.nc, 2, B_HEADS, B_DK, B_DV), F32)],
        compiler_params=_cparams(1),
        name=f"gla_ctx{l}",
    )(zb, zb, zb, wal, bal, gain)


GLA_STEP = 512


def _gla_lat_kernel(qkvf_ref, lrf_ref, rf_ref, qkvb_ref, lrb_ref, rb_ref, wal_ref, bal_ref, gain_ref,
                    s0_ref, ob_ref, stf_ref, stb_ref, of_ref, obk_ref):
    n = pl.program_id(1)
    nb = pl.num_programs(1)

    @pl.when(n == 0)
    def _():
        stf_ref[...] = _stack_state(s0_ref.at[0])
        stb_ref[...] = _stack_state(s0_ref.at[1])
    (o_f, st_f), (o_b, st_b) = _gla_blocks([(qkvf_ref, lrf_ref, stf_ref[...], 0),
                                            (qkvb_ref, lrb_ref, stb_ref[...], 1)], wal_ref, bal_ref)
    stf_ref[...] = st_f
    stb_ref[...] = st_b
    rows_f = pl.ds(pl.multiple_of(n * GLA_STEP, GLA_STEP), GLA_STEP)
    rows_b = pl.ds(pl.multiple_of((nb - 1 - n) * GLA_STEP, GLA_STEP), GLA_STEP)
    of_ref[rows_f, :] = o_f
    obk_ref[rows_b, :] = o_b

    @pl.when(2 * n >= nb)
    def _():
        ob_ref[rows_f, :] = _gla_finish(o_f + obk_ref[rows_f, :], rf_ref[...], gain_ref[...])
        ob_ref[rows_b, :] = _gla_finish(of_ref[rows_b, :] + o_b, rb_ref[...], gain_ref[...])


def _gla_lat(cfg, l, zb, wal, bal, gain, s0):
    nbl = cfg.tl // GLA_STEP
    base = cfg.n_ctx // GLA_STEP
    assert nbl % 2 == 0 and cfg.tl % GLA_STEP == 0 and cfg.n_ctx % GLA_STEP == 0

    def blk(width, col, rev):
        def imap(b, n):
            return (base + b * nbl + (nbl - 1 - n if rev else n), col)
        return pl.BlockSpec((GLA_STEP, width), imap)

    qkv_w = 2 * B_QK + B_WIDTH
    per_dir = lambda rev: [blk(qkv_w, 0, rev), blk(LANES, ZB_LR // LANES, rev), blk(B_WIDTH, qkv_w // B_WIDTH, rev)]
    return pl.pallas_call(
        _gla_lat_kernel,
        grid=(cfg.nl, nbl),
        in_specs=per_dir(False) + per_dir(True) + [
            _const_spec((LANES, 2 * B_QK)), _const_spec((1, 2 * B_QK)), _const_spec((1, B_WIDTH)),
            pl.BlockSpec((None, 2, B_HEADS, B_DK, B_DV), lambda b, n: (b, 0, 0, 0, 0))],
        out_specs=pl.BlockSpec((cfg.tl, B_WIDTH), lambda b, n: (b, 0)),
        out_shape=jax.ShapeDtypeStruct((cfg.n_lat, B_WIDTH), BF16),
        scratch_shapes=[pltpu.VMEM((B_QK, B_DV), F32), pltpu.VMEM((B_QK, B_DV), F32),
                        pltpu.VMEM((cfg.tl, B_WIDTH), F32), pltpu.VMEM((cfg.tl, B_WIDTH), F32)],
        compiler_params=_cparams(2),
        name=f"gla_lat{l}",
    )(zb, zb, zb, zb, zb, zb, wal, bal, gain, s0)


LRU_CHUNK = TS // SUBLANES


TM_ODD_IN = 1024


def _odd_in_kernel(x_ref, mod_ref, nw_ref, w_ref, o_ref, *, cfg):
    g = _group_of_tile(cfg, pl.program_id(0), TM_ODD_IN)
    shift, scale, _ = _mod_rows(mod_ref, g, 0)
    h = _modnorm(_to_scan_order(x_ref[...]), nw_ref[...], shift, scale).astype(BF16)
    z = jnp.dot(h, w_ref[...], preferred_element_type=F32)
    o_ref[:, 0:D_RNN] = jax.nn.gelu(z[:, 0:D_RNN])
    o_ref[:, D_RNN:2 * D_RNN] = z[:, D_RNN:2 * D_RNN]


def _odd_in(cfg, l, x, mods, nw, w):
    n = cfg.n_tok
    return pl.pallas_call(
        functools.partial(_odd_in_kernel, cfg=cfg),
        grid=(n // TM_ODD_IN,),
        in_specs=[pl.BlockSpec((TM_ODD_IN, D_MODEL), lambda i: (i, 0)), _mod_spec(l), _const_spec((1, D_MODEL)),
                  pl.BlockSpec((D_MODEL, 2 * D_RNN), lambda i: (0, 0), pipeline_mode=pl.Buffered(1))],
        out_specs=pl.BlockSpec((TM_ODD_IN, 2 * D_RNN), lambda i: (i, 0)),
        out_shape=jax.ShapeDtypeStruct((n, 2 * D_RNN), F32),
        compiler_params=_cparams(1),
        name=f"odd_in{l}",
    )(x, mods, nw, w)


def _block_position(cfg, ib):
    ncb = cfg.n_ctx // TS
    bpc = cfg.tc // TS
    bpl = cfg.tl // TS
    is_ctx = ib < ncb
    jl = jnp.maximum(ib - ncb, 0)
    pos = jnp.where(is_ctx, ib % bpc, jl % bpl)
    per = jnp.where(is_ctx, bpc, bpl)
    return is_ctx, jl // bpl, pos == 0, pos == per - 1


def _conv_centred(u, prev16, next8, first, last, cw_ref, cb_ref):
    sub = lax.broadcasted_iota(jnp.int32, (SUBLANES, D_RNN), 0)
    lastrow = TS - SUBLANES

    def from_prev_chunk(own, other):
        return jnp.where(sub == 0, jnp.where(first, 0.0, pltpu.roll(other, 1, 0)), pltpu.roll(own, 1, 0))

    s30 = from_prev_chunk(u[lastrow - SUBLANES:lastrow], prev16[0:SUBLANES])
    s31 = from_prev_chunk(u[lastrow:TS], prev16[SUBLANES:2 * SUBLANES])
    n0 = jnp.where(sub == SUBLANES - 1, jnp.where(last, 0.0, pltpu.roll(next8, SUBLANES - 1, 0)),
                   pltpu.roll(u[0:SUBLANES], SUBLANES - 1, 0))
    m2 = jnp.concatenate([s30, s31, u[0:lastrow - SUBLANES]], axis=0)
    m1 = jnp.concatenate([s31, u[0:lastrow]], axis=0)
    p1 = jnp.concatenate([u[SUBLANES:TS], n0], axis=0)
    return m2 * cw_ref[0:1, :] + m1 * cw_ref[1:2, :] + u * cw_ref[2:3, :] + p1 * cw_ref[3:4, :] + cb_ref[...]


def _lru_terms(uc, wg_ref, bg_ref, lam_ref):
    ucb = uc.astype(BF16)
    rs, is_ = [], []
    for nblk in range(LRU_BLOCKS):
        z = jnp.dot(ucb[:, nblk * LRU_BLOCK_W:(nblk + 1) * LRU_BLOCK_W], wg_ref[nblk],
                    preferred_element_type=F32)
        rs.append(z[:, 0:LRU_BLOCK_W])
        is_.append(z[:, LRU_BLOCK_W:2 * LRU_BLOCK_W])
    tr = jnp.tanh(jnp.concatenate(rs, axis=1) + bg_ref[0:1, :])
    ti = jnp.tanh(jnp.concatenate(is_, axis=1) + bg_ref[1:2, :])
    k = (-0.5 * LRU_C) * jax.nn.softplus(-lam_ref[...])
    a = jnp.exp(k + k * tr)
    w = 1.0 - a * a
    root = jnp.where(w > 0.0, w * lax.rsqrt(w), 0.0)
    v = root * ((0.5 + 0.5 * ti) * uc)
    return a, v


def _lru_scan(a, v, carry, reverse):
    order = range(LRU_CHUNK - 1, -1, -1) if reverse else range(LRU_CHUNK)
    h = jnp.zeros((SUBLANES, D_RNN), F32)
    p = jnp.ones((SUBLANES, D_RNN), F32)
    hs, ps = [None] * LRU_CHUNK, [None] * LRU_CHUNK
    for j in order:
        aj = a[j * SUBLANES:(j + 1) * SUBLANES]
        h = aj * h + v[j * SUBLANES:(j + 1) * SUBLANES]
        p = aj * p
        hs[j], ps[j] = h, p
    entry = [None] * SUBLANES
    for c in (range(SUBLANES - 1, -1, -1) if reverse else range(SUBLANES)):
        entry[c] = carry
        carry = h[c:c + 1] + p[c:c + 1] * carry
    hm = jnp.concatenate(entry, axis=0)
    return jnp.concatenate([hs[j] + ps[j] * hm for j in range(LRU_CHUNK)], axis=0), carry


LRU_SUB = 2
LRU_STEP = LRU_SUB * TS


def _lru_scans(terms, h0_ref, carry_ref, hs_ref, cfg, tile, direction):
    carry = carry_ref[...]
    ys = [None] * LRU_SUB
    for sb in (range(LRU_SUB - 1, -1, -1) if direction else range(LRU_SUB)):
        is_ctx, jseq, first, last = _block_position(cfg, tile * LRU_SUB + sb)
        start = last if direction else first
        carry = jnp.where(start, jnp.where(is_ctx, 0.0, h0_ref[jseq]), carry)
        ys[sb], carry = _lru_scan(*terms[sb], carry, bool(direction))
        hs_ref[sb] = carry
    carry_ref[...] = carry
    return ys


def _odd_fwd_kernel(u_ref, up_ref, un_ref, cw_ref, cb_ref, wg_ref, bg_ref, lam_ref, h0_ref,
                    yf_ref, uc_ref, hs_ref, carry_ref, *, cfg):
    tile = pl.program_id(0)
    terms = []
    for sb in range(LRU_SUB):
        _, _, first, last = _block_position(cfg, tile * LRU_SUB + sb)
        rows = slice(sb * TS, (sb + 1) * TS)
        prev16 = up_ref[...] if sb == 0 else u_ref[sb * TS - 2 * SUBLANES:sb * TS, :]
        next8 = un_ref[...] if sb == LRU_SUB - 1 else u_ref[(sb + 1) * TS:(sb + 1) * TS + SUBLANES, :]
        uc = _conv_centred(u_ref[rows, :], prev16, next8, first, last, cw_ref, cb_ref)
        uc_ref[rows, :] = uc
        terms.append(_lru_terms(uc, wg_ref, bg_ref, lam_ref))
    for sb, y in enumerate(_lru_scans(terms, h0_ref, carry_ref, hs_ref, cfg, tile, 0)):
        yf_ref[sb * TS:(sb + 1) * TS, :] = y


def _odd_bwd_kernel(g_ref, uc_ref, yf_ref, mod_ref, wg_ref, bg_ref, lam_ref, h0_ref, wout_ref,
                    o_ref, hs_ref, carry_ref, *, cfg):
    tile = pl.num_programs(0) - 1 - pl.program_id(0)
    terms = [_lru_terms(uc_ref[sb * TS:(sb + 1) * TS, :], wg_ref, bg_ref, lam_ref) for sb in range(LRU_SUB)]
    y = jnp.concatenate(_lru_scans(terms, h0_ref, carry_ref, hs_ref, cfg, tile, 1), axis=0)
    mix = (g_ref[...] * (yf_ref[...] + y)).astype(BF16)
    grp = _group_of_tile(cfg, tile, LRU_STEP)
    gate = mod_ref[2, pl.ds(grp, 1), :]
    o_ref[...] = gate * jnp.dot(mix, wout_ref[...], preferred_element_type=F32)


def _odd_scans(cfg, l, gu, mods, cw, cb, wg, bg, lam, h0, w_out):
    n = cfg.n_tok
    assert n % LRU_STEP == 0 and cfg.n_ctx % LRU_STEP == 0 and cfg.tl % LRU_STEP == 0
    nblk = n // TS
    nstep = n // LRU_STEP
    r8 = LRU_STEP // SUBLANES
    last8 = n // SUBLANES - 1

    def specs(rev):
        it = (lambda i: nstep - 1 - i) if rev else (lambda i: i)
        cur = lambda col: pl.BlockSpec((LRU_STEP, D_RNN), lambda i: (it(i), col))
        cur_in = cur
        prev = pl.BlockSpec((2 * SUBLANES, D_RNN), lambda i: (jnp.maximum(it(i) * (r8 // 2) - 1, 0), 1))
        nxt = pl.BlockSpec((SUBLANES, D_RNN), lambda i: (jnp.minimum((it(i) + 1) * r8, last8), 1))
        hs = pl.BlockSpec((LRU_SUB, 1, D_RNN), lambda i: (it(i), 0, 0))
        return cur, cur_in, prev, nxt, hs

    def weights(direction):
        return [pl.BlockSpec((None, LRU_BLOCKS, LRU_BLOCK_W, 2 * LRU_BLOCK_W), lambda i: (direction, 0, 0, 0)),
                pl.BlockSpec((None, 2, D_RNN), lambda i: (direction, 0, 0)),
                pl.BlockSpec((None, 1, D_RNN), lambda i: (direction, 0, 0)),
                pl.BlockSpec((None, cfg.nl, 1, D_RNN), lambda i: (direction, 0, 0, 0))]

    cur, cur_in, prev, nxt, hs = specs(False)
    tok_shape = jax.ShapeDtypeStruct((n, D_RNN), F32)
    y_f, uc, hs_f = pl.pallas_call(
        functools.partial(_odd_fwd_kernel, cfg=cfg),
        grid=(nstep,),
        in_specs=[cur_in(1), prev, nxt, _const_spec((CONV_W, D_RNN)), _const_spec((1, D_RNN))] + weights(0),
        out_specs=[cur(0), cur(0), hs],
        out_shape=[tok_shape, tok_shape, jax.ShapeDtypeStruct((nblk, 1, D_RNN), F32)],
        scratch_shapes=[pltpu.VMEM((1, D_RNN), F32)],
        compiler_params=_cparams(1),
        name=f"lru_fwd{l}",
    )(gu, gu, gu, cw, cb, wg, bg, lam, h0)
    cur, cur_in, _, _, hs = specs(True)
    upd, hs_b = pl.pallas_call(
        functools.partial(_odd_bwd_kernel, cfg=cfg),
        grid=(nstep,),
        in_specs=[cur_in(0), cur_in(0), cur_in(0), _mod_spec(l)] + weights(1) + [_const_spec((D_RNN, D_MODEL))],
        out_specs=[cur(0), hs],
        out_shape=[tok_shape, jax.ShapeDtypeStruct((nblk, 1, D_RNN), F32)],
        scratch_shapes=[pltpu.VMEM((1, D_RNN), F32)],
        compiler_params=_cparams(1),
        name=f"lru_bwd{l}",
    )(gu, uc, y_f, mods, wg, bg, lam, h0, w_out)
    return upd, hs_f, hs_b


def _rope_table(tl):
    pos = np.arange(tl)
    nf = HEAD_DIM // 4
    inv = ROPE_BASE ** (-np.arange(nf, dtype=np.float32) / nf)
    ar = (pos // GRID_W).astype(np.float32)[:, None] * inv
    ac = (pos % GRID_W).astype(np.float32)[:, None] * inv
    ang = np.concatenate([ar, ar, ac, ac] * 2, axis=-1)
    lo = (np.arange(LANES) % (HEAD_DIM // 2)) < nf
    cos, sin = np.cos(ang), np.sin(ang)
    tab = np.concatenate([cos, np.where(lo, -sin, 0.0), np.where(lo, 0.0, sin)], axis=-1)
    return jnp.asarray(tab, F32)


def _dup_heads(w):
    lead = w.shape[:-1]
    w = w.reshape(lead + (A_KV_HEADS, 1, HEAD_DIM))
    return jnp.broadcast_to(w, lead + (A_KV_HEADS, 2, HEAD_DIM)).reshape(lead + (2 * LANES,))


def _pack_even_in(w):
    kv = A_KV_HEADS * HEAD_DIM
    q, k, v = w[:, :A_WIDTH], w[:, A_WIDTH:A_WIDTH + kv], w[:, A_WIDTH + kv:A_WIDTH + 2 * kv]
    rest = w[:, A_WIDTH + 2 * kv:]
    pad = jnp.zeros((w.shape[0], LANES - 2 * GLA_RANK), w.dtype)
    return jnp.concatenate([q, _dup_heads(k), _dup_heads(v), rest, pad], axis=1).astype(BF16)


def _pack_alpha(w_alpha, b_alpha):
    w = jnp.zeros((LANES, 2 * B_QK), F32)
    w = w.at[0:GLA_RANK, 0:B_QK].set(w_alpha[0]).at[GLA_RANK:2 * GLA_RANK, B_QK:].set(w_alpha[1])
    return w.astype(BF16), b_alpha.reshape(1, 2 * B_QK)


def _forward(cfg, x_prompt, x_sample, c, cache_k, cache_v, state_gla, state_lru, c_ctx,
             w_ada, b_ada, norm_mix, norm_ffn, w_in_even, attn_sink, w_alpha, b_alpha, gla_gain,
             w_out_even, w_in_odd, conv_w, conv_b, w_gate_a, b_gate_a, w_gate_x, b_gate_x,
             lru_lambda, w_out_odd, w_ffn_in, w_ffn_out, norm_final):
    assert cfg.tc == TS and cfg.tl % TM == 0 and cfg.n_ctx % TM == 0 and cfg.nl + 1 <= SUBLANES
    x = (x_prompt.reshape(cfg.n_ctx, D_MODEL), x_sample.reshape(cfg.n_lat, D_MODEL))
    cvec = jnp.concatenate([c_ctx[None, :], c, jnp.zeros((SUBLANES - 1 - cfg.nl, D_MODEL), F32)], axis=0)
    mods = _ada(cvec, w_ada, b_ada)
    rope_tab = _rope_table(cfg.tl)
    nf = norm_final.reshape(1, D_MODEL)
    w_ffn_in_bf, w_ffn_out_bf, w_out_even_bf = (w.astype(BF16) for w in (w_ffn_in, w_ffn_out, w_out_even))
    new_k, new_v, new_gla, new_lru = [], [], [], []
    for l in range(DEPTH):
        nw = norm_mix[l].reshape(1, D_MODEL)
        if l % 2 == 0:
            e = l // 2
            za, zb, k_ctx, v_ctx = _even_in(cfg, l, x, mods, nw, _pack_even_in(w_in_even[e]), rope_tab)
            new_k.append(k_ctx.reshape(cfg.nc, cfg.tc, A_KV_HEADS, HEAD_DIM))
            new_v.append(v_ctx.reshape(cfg.nc, cfg.tc, A_KV_HEADS, HEAD_DIM))
            ck2 = _dup_heads(cache_k[:, e].reshape(cfg.nl, PAST_LEN, A_KV_HEADS * HEAD_DIM)).astype(BF16)
            cv2 = _dup_heads(cache_v[:, e].reshape(cfg.nl, PAST_LEN, A_KV_HEADS * HEAD_DIM)).astype(BF16)
            oa_ctx = _attn_ctx(cfg, l, za, attn_sink[e])
            oa_lat = _attn_lat(cfg, l, za, ck2, cv2, attn_sink[e])
            wal, bal = _pack_alpha(w_alpha[e], b_alpha[e])
            gain = gla_gain[e].reshape(1, B_WIDTH)
            ob_ctx, s_new = _gla_ctx(cfg, l, zb, wal, bal, gain)
            ob_lat = _gla_lat(cfg, l, zb, wal, bal, gain, state_gla[:, e])
            new_gla.append(s_new)
            pre_args = (oa_ctx, oa_lat, ob_ctx, ob_lat, w_out_even_bf)
        else:
            o = l // 2
            gu = _odd_in(cfg, l, x, mods, nw, w_in_odd[o].astype(BF16))
            wg = (0.5 * jnp.concatenate([w_gate_a[o], w_gate_x[o]], axis=-1)).astype(BF16)
            bg = 0.5 * jnp.stack([b_gate_a[o], b_gate_x[o]], axis=1)
            lam = lru_lambda[o].reshape(2, 1, D_RNN)
            h0 = jnp.transpose(state_lru[:, o], (1, 0, 2)).reshape(2, cfg.nl, 1, D_RNN)
            upd, hs_f, hs_b = _odd_scans(cfg, l, gu, mods, conv_w[o], conv_b[o].reshape(1, D_RNN),
                                         wg, bg, lam, h0, w_out_odd[o].astype(BF16))
            new_lru.append(jnp.stack([hs_f[:cfg.nc, 0], hs_b[:cfg.nc, 0]], axis=1))
            pre_args = (upd,)
        x = _ffn(cfg, l, x, pre_args, mods, norm_ffn[l].reshape(1, D_MODEL), w_ffn_in_bf, w_ffn_out_bf, nf,
                 final=(l == DEPTH - 1))
    y_prompt = x[0].reshape(cfg.nc, cfg.tc, D_MODEL)
    y_sample = x[1].reshape(cfg.nl, cfg.tl, D_MODEL)
    return (y_prompt, y_sample, jnp.stack(new_k, axis=1), jnp.stack(new_v, axis=1),
            jnp.stack(new_gla, axis=1), jnp.stack(new_lru, axis=1))


def kernel(x_prompt, x_sample, c, cache_k, cache_v, state_gla, state_lru, c_ctx, w_ada, b_ada, norm_mix, norm_ffn, w_in_even, attn_sink, w_alpha, b_alpha, gla_gain, w_out_even, w_in_odd, conv_w, conv_b, w_gate_a, b_gate_a, w_gate_x, b_gate_x, lru_lambda, w_out_odd, w_ffn_in, w_ffn_out, norm_final):
    cfg = Cfg(nc=x_prompt.shape[0], tc=x_prompt.shape[1], nl=x_sample.shape[0], tl=x_sample.shape[1])
    return _forward(cfg, x_prompt, x_sample, c, cache_k, cache_v, state_gla, state_lru, c_ctx,
                    w_ada, b_ada, norm_mix, norm_ffn, w_in_even, attn_sink, w_alpha, b_alpha, gla_gain,
                    w_out_even, w_in_odd, conv_w, conv_b, w_gate_a, b_gate_a, w_gate_x, b_gate_x,
                    lru_lambda, w_out_odd, w_ffn_in, w_ffn_out, norm_final)
```

```python
import functools
from typing import NamedTuple

import jax
import jax.numpy as jnp
import numpy as np
from jax import lax
from jax.experimental import pallas as pl
from jax.experimental.pallas import tpu as pltpu

F32 = jnp.float32
BF16 = jnp.bfloat16

D_MODEL = 1024
DEPTH = 4
EPS = 1e-6

HEAD_DIM = 64
A_Q_HEADS = 8
A_KV_HEADS = 2
A_GROUPS = A_Q_HEADS // A_KV_HEADS
A_WIDTH = A_Q_HEADS * HEAD_DIM
ATT_BLOCK = 128
ROPE_BASE = 10000.0
GRID_W = 64
NEG_INF = -1e30
PAST_LEN = 256

B_HEADS = 4
B_DK = 64
B_DV = 128
B_QK = B_HEADS * B_DK
B_WIDTH = B_HEADS * B_DV
GLA_RANK = 16
GLA_NORMALIZER = 16.0
GLA_CHUNK = 64

D_RNN = D_MODEL
LRU_BLOCK_W = 256
LRU_BLOCKS = D_RNN // LRU_BLOCK_W
CONV_W = 4
LRU_C = 8.0
LOG2_E = 1.4426950408889634

D_FF = 2816

LANES = 128
SUBLANES = 8

ZA_W = A_WIDTH + 2 * 2 * LANES
ZB_LR = B_QK + B_QK + B_WIDTH + B_WIDTH
ZB_W = ZB_LR + LANES
P_A = A_WIDTH + 2 * LANES
P_MAIN = P_A + ZB_LR

TM = 512
TS = 256
FF_CHUNK = 256

VMEM_LIMIT = 56 * 1024 * 1024


class Cfg(NamedTuple):
    nc: int
    tc: int
    nl: int
    tl: int

    @property
    def n_ctx(self):
        return self.nc * self.tc

    @property
    def n_lat(self):
        return self.nl * self.tl

    @property
    def n_tok(self):
        return self.n_ctx + self.n_lat


def _cparams(n_axes):
    return pltpu.CompilerParams(dimension_semantics=("arbitrary",) * n_axes,
                                vmem_limit_bytes=VMEM_LIMIT)


def _const_spec(shape):
    nd = len(shape)
    return pl.BlockSpec(shape, lambda *_: (0,) * nd)


def _silu(x):
    return x * jax.nn.sigmoid(x)


def _group_of_tile(cfg, i, tile):
    nct = cfg.n_ctx // tile
    per_lat = cfg.tl // tile
    return jnp.where(i < nct, 0, 1 + jnp.maximum(i - nct, 0) // per_lat)


def _modnorm(x, nw, shift, scale):
    ms = jnp.mean(x * x, axis=-1, keepdims=True)
    y = x * lax.rsqrt(ms + EPS) * nw
    return y * (1.0 + scale) + shift


def _mod_rows(mod_ref, g, first):
    return tuple(mod_ref[first + j, pl.ds(g, 1), :] for j in range(3))


def _ada_kernel(c_ref, w_ref, b_ref, o_ref):
    s = _silu(c_ref[...]).astype(BF16)
    o_ref[...] = jnp.dot(s, w_ref[...].astype(BF16), preferred_element_type=F32) + b_ref[...]


def _ada(cvec, w_ada, b_ada):
    b4 = b_ada.reshape(DEPTH, 6, 1, D_MODEL)
    return pl.pallas_call(
        _ada_kernel,
        grid=(DEPTH, 6),
        in_specs=[
            _const_spec((SUBLANES, D_MODEL)),
            pl.BlockSpec((None, D_MODEL, D_MODEL), lambda l, j: (l, 0, j)),
            pl.BlockSpec((None, None, 1, D_MODEL), lambda l, j: (l, j, 0, 0)),
        ],
        out_specs=pl.BlockSpec((None, None, SUBLANES, D_MODEL), lambda l, j: (l, j, 0, 0)),
        out_shape=jax.ShapeDtypeStruct((DEPTH, 6, SUBLANES, D_MODEL), F32),
        compiler_params=_cparams(2),
        name="ada",
    )(cvec, w_ada, b4)


def _mod_spec(l):
    return pl.BlockSpec((None, 6, SUBLANES, D_MODEL), lambda *_: (l, 0, 0, 0))


def _ffn_tile(x, nw, shift, scale, gate, win_ref, wout_ref):
    h = _modnorm(x, nw, shift, scale).astype(BF16)
    acc = None
    for c in range(D_FF // FF_CHUNK):
        lo, hi = c * FF_CHUNK, (c + 1) * FF_CHUNK
        g = jnp.dot(h, win_ref[:, lo:hi].astype(BF16), preferred_element_type=F32)
        u = jnp.dot(h, win_ref[:, D_FF + lo:D_FF + hi].astype(BF16), preferred_element_type=F32)
        a = (_silu(g) * u).astype(BF16)
        part = jnp.dot(a, wout_ref[lo:hi, :].astype(BF16), preferred_element_type=F32)
        acc = part if acc is None else acc + part
    return x + gate * acc


def _x_args(cfg, x, tile=TM):
    if not isinstance(x, tuple):
        return [x], [pl.BlockSpec((tile, D_MODEL), lambda i: (i, 0))]
    nct = cfg.n_ctx // tile
    return list(x), [pl.BlockSpec((tile, D_MODEL), lambda i: (jnp.minimum(i, nct - 1), 0)),
                     pl.BlockSpec((tile, D_MODEL), lambda i: (jnp.maximum(i - nct, 0), 0))]


def _load_x(cfg, x_refs, i, tile=TM):
    if len(x_refs) == 1:
        return x_refs[0][...]
    return jnp.where(i < cfg.n_ctx // tile, x_refs[0][...], x_refs[1][...])


def _to_scan_order(x):
    nb, d = x.shape[0] // TS, x.shape[1]
    return jnp.swapaxes(x.reshape(nb, SUBLANES, TS // SUBLANES, d), 1, 2).reshape(nb * TS, d)


def _from_scan_order(x):
    nb, d = x.shape[0] // TS, x.shape[1]
    return jnp.swapaxes(x.reshape(nb, TS // SUBLANES, SUBLANES, d), 1, 2).reshape(nb * TS, d)


def _ffn_kernel(*refs, cfg, n_x, pre, final):
    x_refs, refs = refs[:n_x], refs[n_x:]
    i = pl.program_id(0)
    g = _group_of_tile(cfg, i, TM)
    x = _load_x(cfg, x_refs, i)
    if pre == "even":
        (oac_ref, oal_ref, obc_ref, obl_ref, wmix_ref), refs = refs[:5], refs[5:]
    else:
        d_ref, refs = refs[0], refs[1:]
    mod_ref, nw_ref, win_ref, wout_ref, nf_ref = refs[:5]
    outs = refs[5:]
    if pre == "even":
        is_ctx_tile = i < cfg.n_ctx // TM
        o = jnp.concatenate([jnp.where(is_ctx_tile, oac_ref[...], oal_ref[...]),
                             jnp.where(is_ctx_tile, obc_ref[...], obl_ref[...])], axis=1)
        x = x + mod_ref[2, pl.ds(g, 1), :] * jnp.dot(o, wmix_ref[...], preferred_element_type=F32)
    else:
        x = x + _from_scan_order(d_ref[...])
    shift, scale, gate = _mod_rows(mod_ref, g, 3)
    y = _ffn_tile(x, nw_ref[...], shift, scale, gate, win_ref, wout_ref)
    if not final:
        outs[0][...] = y
        return
    ms = jnp.mean(y * y, axis=-1, keepdims=True)
    y = y * lax.rsqrt(ms + EPS) * nf_ref[...]
    is_ctx = i < cfg.n_ctx // TM

    @pl.when(is_ctx)
    def _():
        outs[0][...] = y

    @pl.when(jnp.logical_not(is_ctx))
    def _():
        outs[1][...] = y


def _ffn(cfg, l, x, pre_args, mods, nw, win, wout, nf, final):
    n = cfg.n_tok
    nct = cfg.n_ctx // TM
    tok = lambda w: pl.BlockSpec((TM, w), lambda i: (i, 0))
    ctx = lambda w: pl.BlockSpec((TM, w), lambda i: (jnp.minimum(i, nct - 1), 0))
    lat = lambda w: pl.BlockSpec((TM, w), lambda i: (jnp.maximum(i - nct, 0), 0))
    x_ops, x_specs = _x_args(cfg, x)
    if l % 2 == 0:
        pre = "even"
        pre_specs = [ctx(A_WIDTH), lat(A_WIDTH), ctx(B_WIDTH), lat(B_WIDTH),
                     pl.BlockSpec((None, A_WIDTH + B_WIDTH, D_MODEL), lambda i: (l // 2, 0, 0))]
    else:
        pre, pre_specs = "odd", [tok(D_MODEL)]
    if final:
        out_specs = [pl.BlockSpec((TM, D_MODEL), lambda i: (jnp.minimum(i, nct - 1), 0)),
                     pl.BlockSpec((TM, D_MODEL), lambda i: (jnp.maximum(i - nct, 0), 0))]
        out_shape = [jax.ShapeDtypeStruct((cfg.n_ctx, D_MODEL), F32),
                     jax.ShapeDtypeStruct((cfg.n_lat, D_MODEL), F32)]
    else:
        out_specs, out_shape = tok(D_MODEL), jax.ShapeDtypeStruct((n, D_MODEL), F32)
    return pl.pallas_call(
        functools.partial(_ffn_kernel, cfg=cfg, n_x=len(x_ops), pre=pre, final=final),
        grid=(n // TM,),
        in_specs=x_specs + pre_specs + [
            _mod_spec(l), _const_spec((1, D_MODEL)),
            pl.BlockSpec((None, D_MODEL, 2 * D_FF), lambda i: (l, 0, 0), pipeline_mode=pl.Buffered(1)),
            pl.BlockSpec((None, D_FF, D_MODEL), lambda i: (l, 0, 0), pipeline_mode=pl.Buffered(1)),
            _const_spec((1, D_MODEL))],
        out_specs=out_specs,
        out_shape=out_shape,
        compiler_params=_cparams(1),
        name=f"ffn{l}",
    )(*x_ops, *pre_args, mods, nw, win, wout, nf)


def _rope(x, tab_ref):
    cos, sin_a, sin_b = tab_ref[:, 0:LANES], tab_ref[:, LANES:2 * LANES], tab_ref[:, 2 * LANES:3 * LANES]
    parts = []
    for j in range(x.shape[1] // LANES):
        xs = x[:, j * LANES:(j + 1) * LANES]
        parts.append(xs * cos + pltpu.roll(xs, LANES - 16, 1) * sin_a + pltpu.roll(xs, 16, 1) * sin_b)
    return jnp.concatenate(parts, axis=1)


TM_EVEN_IN = 1024


def _even_in_kernel(*refs, cfg, n_x):
    x_refs, (mod_ref, nw_ref, w_ref, wlr_ref, tab_ref, za_ref, zb_ref, k_ref, v_ref) = refs[:n_x], refs[n_x:]
    i = pl.program_id(0)
    g = _group_of_tile(cfg, i, TM_EVEN_IN)
    shift, scale, _ = _mod_rows(mod_ref, g, 0)
    h = _modnorm(_load_x(cfg, x_refs, i, TM_EVEN_IN), nw_ref[...], shift, scale).astype(BF16)
    z = jnp.dot(h, w_ref[...].astype(BF16), preferred_element_type=F32)
    q = z[:, 0:A_WIDTH] * (HEAD_DIM ** -0.5)
    k = z[:, A_WIDTH:A_WIDTH + LANES]
    v = z[:, A_WIDTH + LANES:P_A]
    lo = lax.broadcasted_iota(jnp.int32, (TM_EVEN_IN, LANES), 1) < HEAD_DIM

    def twice(a):
        swapped = pltpu.roll(a, HEAD_DIM, 1)
        return jnp.concatenate([jnp.where(lo, a, swapped), jnp.where(lo, swapped, a)], axis=1).astype(BF16)

    za_ref[:, A_WIDTH + 2 * LANES:ZA_W] = twice(v)
    zb_ref[:, 0:B_QK] = z[:, P_A:P_A + B_QK] * (B_DK ** -0.5)
    zb_ref[:, B_QK:ZB_LR] = z[:, P_A + B_QK:P_MAIN]
    zb_ref[:, ZB_LR:ZB_W] = jnp.dot(h, wlr_ref[...], preferred_element_type=F32)

    @pl.when(g == 0)
    def _():
        za_ref[:, 0:A_WIDTH] = q.astype(BF16)
        za_ref[:, A_WIDTH:A_WIDTH + 2 * LANES] = twice(k)
        k_ref[...] = k
        v_ref[...] = v

    @pl.when(g != 0)
    def _():
        za_ref[:, 0:A_WIDTH] = _rope(q, tab_ref).astype(BF16)
        za_ref[:, A_WIDTH:A_WIDTH + 2 * LANES] = twice(_rope(k, tab_ref))


def _even_in(cfg, l, x, mods, nw, w_all, w_lr, rope_tab):
    n = cfg.n_tok
    tile = TM_EVEN_IN
    assert cfg.n_ctx % tile == 0 and cfg.tl % tile == 0
    nct = cfg.n_ctx // tile
    per_lat = cfg.tl // tile
    tok = lambda w: pl.BlockSpec((tile, w), lambda i: (i, 0))
    x_ops, x_specs = _x_args(cfg, x, tile)
    ctx_kv = pl.BlockSpec((tile, LANES), lambda i: (jnp.minimum(i, nct - 1), 0))
    return pl.pallas_call(
        functools.partial(_even_in_kernel, cfg=cfg, n_x=len(x_ops)),
        grid=(n // tile,),
        in_specs=x_specs + [_mod_spec(l), _const_spec((1, D_MODEL)),
                            pl.BlockSpec((None, D_MODEL, P_MAIN), lambda i: (l // 2, 0, 0),
                                         pipeline_mode=pl.Buffered(1)),
                            _const_spec((D_MODEL, LANES)),
                            pl.BlockSpec((tile, 3 * LANES), lambda i: (jnp.maximum(i - nct, 0) % per_lat, 0))],
        out_specs=[tok(ZA_W), tok(ZB_W), ctx_kv, ctx_kv],
        out_shape=[jax.ShapeDtypeStruct((n, ZA_W), BF16),
                   jax.ShapeDtypeStruct((n, ZB_W), F32),
                   jax.ShapeDtypeStruct((cfg.n_ctx, LANES), F32),
                   jax.ShapeDtypeStruct((cfg.n_ctx, LANES), F32)],
        compiler_params=_cparams(1),
        name=f"even_in{l}",
    )(*x_ops, mods, nw, w_all, w_lr, rope_tab)


def _sink_attention(units, sink_ref):
    scores = []
    for q, k_all, _, mask, kvh in units:
        tq = q.shape[0]
        lo = lax.broadcasted_iota(jnp.int32, (tq, LANES), 1) < HEAD_DIM
        qs = []
        for gq in range(A_GROUPS):
            h = kvh * A_GROUPS + gq
            pair = q[:, (h // 2) * LANES:(h // 2 + 1) * LANES]
            keep = lo if h % 2 == 0 else jnp.logical_not(lo)
            qs.append(jnp.where(keep, pair, jnp.zeros_like(pair)))
        qst = jnp.concatenate(qs, axis=0)
        s = lax.dot_general(qst, k_all, (((1,), (1,)), ((), ())), preferred_element_type=F32)
        scores.append(s if mask is None else jnp.where(mask, s, NEG_INF))
    probs = []
    for (q, _, _, _, kvh), s in zip(units, scores):
        tq = q.shape[0]
        rows = lax.broadcasted_iota(jnp.int32, (A_GROUPS * tq, 1), 0) // tq
        sink = jnp.zeros((A_GROUPS * tq, 1), F32)
        for gq in range(A_GROUPS):
            sink = jnp.where(rows == gq, sink_ref[kvh * A_GROUPS + gq], sink)
        m = jnp.maximum(jnp.max(s, axis=-1, keepdims=True), sink)
        p = jnp.exp(s - m)
        den = jnp.sum(p, axis=-1, keepdims=True) + jnp.exp(sink - m)
        probs.append((p.astype(BF16), den))
    outs = []
    for (q, _, v_all, _, _), (p, den) in zip(units, probs):
        tq = q.shape[0]
        lo = lax.broadcasted_iota(jnp.int32, (tq, LANES), 1) < HEAD_DIM
        o = jnp.dot(p, v_all, preferred_element_type=F32) / den
        pairs = [jnp.where(lo, o[(2 * pr) * tq:(2 * pr + 1) * tq], o[(2 * pr + 1) * tq:(2 * pr + 2) * tq])
                 for pr in range(2)]
        outs.append(jnp.concatenate(pairs, axis=1).astype(BF16))
    return outs


def _attn_ctx_kernel(sink_ref, q_ref, k_ref, v_ref, o_ref):
    q = q_ref[...]
    units = [(q, k_ref[:, kvh * LANES:(kvh + 1) * LANES], v_ref[:, kvh * LANES:(kvh + 1) * LANES], None, kvh)
             for kvh in range(A_KV_HEADS)]
    for kvh, o in enumerate(_sink_attention(units, sink_ref)):
        o_ref[:, 2 * kvh * LANES:2 * (kvh + 1) * LANES] = o


def _attn_ctx(cfg, l, za, sink):
    tc = cfg.tc
    return pl.pallas_call(
        _attn_ctx_kernel,
        grid=(cfg.nc,),
        in_specs=[pl.BlockSpec(memory_space=pltpu.SMEM),
                  pl.BlockSpec((tc, A_WIDTH), lambda b: (b, 0)),
                  pl.BlockSpec((tc, 2 * LANES), lambda b: (b, A_WIDTH // (2 * LANES))),
                  pl.BlockSpec((tc, 2 * LANES), lambda b: (b, A_WIDTH // (2 * LANES) + 1))],
        out_specs=pl.BlockSpec((tc, A_WIDTH), lambda b: (b, 0)),
        out_shape=jax.ShapeDtypeStruct((cfg.n_ctx, A_WIDTH), BF16),
        compiler_params=_cparams(1),
        name=f"attn_ctx{l}",
    )(sink, za, za, za)


ATT_STEP = 4 * ATT_BLOCK
ATT_WIN = 3 * ATT_BLOCK


def _attn_lat_kernel(sink_ref, q_ref, k_ref, v_ref, ck_ref, cv_ref, o_ref):
    n = pl.program_id(1)
    tl = k_ref.shape[0]
    shape = (A_GROUPS * ATT_BLOCK, PAST_LEN + ATT_WIN)
    r = lax.broadcasted_iota(jnp.int32, shape, 0) % ATT_BLOCK
    c = lax.broadcasted_iota(jnp.int32, shape, 1) - PAST_LEN
    units = []
    for j in range(ATT_STEP // ATT_BLOCK):
        q0 = (n * (ATT_STEP // ATT_BLOCK) + j) * ATT_BLOCK
        start = pl.multiple_of(jnp.clip(q0 - ATT_BLOCK, 0, tl - ATT_WIN), ATT_BLOCK)
        dist = (q0 - start) + r - c
        mask = jnp.logical_or(c < 0, jnp.abs(dist) <= ATT_BLOCK)
        q = q_ref[j * ATT_BLOCK:(j + 1) * ATT_BLOCK, :]
        for kvh in range(A_KV_HEADS):
            sl = slice(kvh * LANES, (kvh + 1) * LANES)
            k_all = jnp.concatenate([ck_ref[:, sl], k_ref[pl.ds(start, ATT_WIN), sl]], axis=0)
            v_all = jnp.concatenate([cv_ref[:, sl], v_ref[pl.ds(start, ATT_WIN), sl]], axis=0)
            units.append((q, k_all, v_all, mask, kvh))
    for u, o in enumerate(_sink_attention(units, sink_ref)):
        j, kvh = divmod(u, A_KV_HEADS)
        o_ref[j * ATT_BLOCK:(j + 1) * ATT_BLOCK, 2 * kvh * LANES:2 * (kvh + 1) * LANES] = o


def _attn_lat(cfg, l, za, cache_k2, cache_v2, sink):
    assert cfg.n_ctx % cfg.tl == 0 and cfg.tl % ATT_STEP == 0 and cfg.tl >= ATT_WIN
    steps = cfg.tl // ATT_STEP
    base = cfg.n_ctx // ATT_STEP
    kcol = A_WIDTH // (2 * LANES)
    seq = lambda col: pl.BlockSpec((cfg.tl, 2 * LANES), lambda b, n: (cfg.n_ctx // cfg.tl + b, col))
    cache = pl.BlockSpec((None, PAST_LEN, 2 * LANES), lambda b, n: (b, 0, 0))
    return pl.pallas_call(
        _attn_lat_kernel,
        grid=(cfg.nl, steps),
        in_specs=[pl.BlockSpec(memory_space=pltpu.SMEM),
                  pl.BlockSpec((ATT_STEP, A_WIDTH), lambda b, n: (base + b * steps + n, 0)),
                  seq(kcol), seq(kcol + 1), cache, cache],
        out_specs=pl.BlockSpec((ATT_STEP, A_WIDTH), lambda b, n: (b * steps + n, 0)),
        out_shape=jax.ShapeDtypeStruct((cfg.n_lat, A_WIDTH), BF16),
        compiler_params=_cparams(2),
        name=f"attn_lat{l}",
    )(sink, za, za, za, cache_k2, cache_v2)


def _gla_chunk_prepare(q, k, v, la, reverse):
    cn = GLA_CHUNK
    row = lax.broadcasted_iota(jnp.int32, (cn, B_QK), 0)
    b = la
    s = 1
    while s < cn:
        if reverse:
            b = b + jnp.where(row < cn - s, pltpu.roll(b, cn - s, 0), 0.0)
        else:
            b = b + jnp.where(row >= s, pltpu.roll(b, s, 0), 0.0)
        s *= 2
    b_last = b[0:1, :] if reverse else b[cn - 1:cn, :]
    q_e = q * jnp.exp(b)
    k_e = (k * jnp.exp(-b)).astype(BF16)
    k_s = (k * jnp.exp(b_last - b)).astype(BF16)
    vb = v.astype(BF16)
    row_t = row[:, 0:LANES]
    lane_t = lax.broadcasted_iota(jnp.int32, (cn, LANES), 1)
    lo = lane_t < B_DK
    col = lane_t % cn
    tri = (col >= row_t) if reverse else (col <= row_t)
    q_tiles, att_tiles = [], []
    for t in range(B_HEADS // 2):
        sl = slice(t * LANES, (t + 1) * LANES)
        k_t = k_e[:, sl]
        zero = jnp.zeros_like(k_t)
        kbd = jnp.concatenate([jnp.where(lo, k_t, zero), jnp.where(lo, zero, k_t)], axis=0)
        q_t = q_e[:, sl]
        att = lax.dot_general(q_t.astype(BF16), kbd, (((1,), (1,)), ((), ())), preferred_element_type=F32)
        q_tiles.append(q_t)
        att_tiles.append(jnp.where(tri, att, 0.0))
    upd = lax.dot_general(k_s, vb, (((0,), (0,)), ((), ())), preferred_element_type=F32)
    own = jnp.concatenate([upd[h * B_DK:(h + 1) * B_DK, h * B_DV:(h + 1) * B_DV] for h in range(B_HEADS)], axis=0)
    decay = jnp.broadcast_to(jnp.exp(b_last), (LANES, B_QK)).T
    return q_tiles, att_tiles, vb, own, decay


def _gla_chunk_output(q_tiles, att_tiles, vb, st):
    lo = lax.broadcasted_iota(jnp.int32, (GLA_CHUNK, LANES), 1) < B_DK
    st_b = st.astype(BF16)
    outs = []
    for h in range(B_HEADS):
        att, q_t = att_tiles[h // 2], q_tiles[h // 2]
        x = (jnp.where(lo, att, pltpu.roll(q_t, B_DK, 1)) if h % 2 == 0
             else jnp.where(lo, pltpu.roll(att, B_DK, 1), q_t))
        rhs = jnp.concatenate([vb[:, h * B_DV:(h + 1) * B_DV], st_b[h * B_DK:(h + 1) * B_DK, :]], axis=0)
        outs.append(jnp.dot(x.astype(BF16), rhs, preferred_element_type=F32))
    return jnp.concatenate(outs, axis=1)


def _gla_log_decay(lr_pad, wal_ref, bal_ref, direction):
    z = jnp.dot(lr_pad.astype(BF16), wal_ref[:, direction * B_QK:(direction + 1) * B_QK],
                preferred_element_type=F32) + bal_ref[:, direction * B_QK:(direction + 1) * B_QK]
    return jax.nn.log_sigmoid(z) / GLA_NORMALIZER


def _gla_blocks(jobs, wal_ref, bal_ref):
    prepared = []
    for qkv_ref, lr_ref, _, direction in jobs:
        nchunk = qkv_ref.shape[0] // GLA_CHUNK
        la_all = _gla_log_decay(lr_ref[...], wal_ref, bal_ref, direction)
        chunks = {}
        for ci in range(nchunk):
            rs = slice(ci * GLA_CHUNK, (ci + 1) * GLA_CHUNK)
            chunks[ci] = _gla_chunk_prepare(qkv_ref[rs, 0:B_QK], qkv_ref[rs, B_QK:2 * B_QK],
                                            qkv_ref[rs, 2 * B_QK:2 * B_QK + B_WIDTH], la_all[rs, :],
                                            bool(direction))
        prepared.append(chunks)
    results = []
    for (qkv_ref, _, st, direction), chunks in zip(jobs, prepared):
        nchunk = len(chunks)
        states = {}
        for ci in (range(nchunk - 1, -1, -1) if direction else range(nchunk)):
            states[ci] = st
            _, _, _, own, decay = chunks[ci]
            st = st * decay + own
        outs = [_gla_chunk_output(*chunks[ci][:3], states[ci]) for ci in range(nchunk)]
        results.append((jnp.concatenate(outs, axis=0), st))
    return results


def _stack_state(s_ref):
    return s_ref[...].reshape(B_QK, B_DV)


def _unstack_state(st, s_ref):
    s_ref[...] = st.reshape(B_HEADS, B_DK, B_DV)


def _gla_finish(o, r, gain):
    parts = []
    for h in range(B_HEADS):
        oh = o[:, h * B_DV:(h + 1) * B_DV]
        parts.append(oh * lax.rsqrt(jnp.mean(oh * oh, axis=-1, keepdims=True) + EPS))
    return (jnp.concatenate(parts, axis=1) * gain * _silu(r)).astype(BF16)


def _gla_ctx_kernel(qkv_ref, r_ref, lr_ref, wal_ref, bal_ref, gain_ref, ob_ref, snew_ref):
    zero = jnp.zeros((B_QK, B_DV), F32)
    (o_f, st_f), (o_b, st_b) = _gla_blocks([(qkv_ref, lr_ref, zero, 0), (qkv_ref, lr_ref, zero, 1)],
                                           wal_ref, bal_ref)
    _unstack_state(st_f, snew_ref.at[0])
    _unstack_state(st_b, snew_ref.at[1])
    ob_ref[...] = _gla_finish(o_f + o_b, r_ref[...], gain_ref[...])


def _gla_ctx(cfg, l, zb, wal, bal, gain):
    tc = cfg.tc
    return pl.pallas_call(
        _gla_ctx_kernel,
        grid=(cfg.nc,),
        in_specs=[pl.BlockSpec((tc, 2 * B_QK + B_WIDTH), lambda b: (b, 0)),
                  pl.BlockSpec((tc, B_WIDTH), lambda b: (b, (2 * B_QK + B_WIDTH) // B_WIDTH)),
                  pl.BlockSpec((tc, LANES), lambda b: (b, ZB_LR // LANES)),
                  _const_spec((LANES, 2 * B_QK)), _const_spec((1, 2 * B_QK)), _const_spec((1, B_WIDTH))],
        out_specs=[pl.BlockSpec((tc, B_WIDTH), lambda b: (b, 0)),
                   pl.BlockSpec((None, 2, B_HEADS, B_DK, B_DV), lambda b: (b, 0, 0, 0, 0))],
        out_shape=[jax.ShapeDtypeStruct((cfg.n_ctx, B_WIDTH), BF16),
                   jax.ShapeDtypeStruct((cfg.nc, 2, B_HEADS, B_DK, B_DV), F32)],
        compiler_params=_cparams(1),
        name=f"gla_ctx{l}",
    )(zb, zb, zb, wal, bal, gain)


GLA_STEP = 512


def _gla_lat_kernel(qkvf_ref, lrf_ref, rf_ref, qkvb_ref, lrb_ref, rb_ref, wal_ref, bal_ref, gain_ref,
                    s0_ref, ob_ref, stf_ref, stb_ref, of_ref, obk_ref):
    n = pl.program_id(1)
    nb = pl.num_programs(1)

    @pl.when(n == 0)
    def _():
        stf_ref[...] = _stack_state(s0_ref.at[0])
        stb_ref[...] = _stack_state(s0_ref.at[1])
    (o_f, st_f), (o_b, st_b) = _gla_blocks([(qkvf_ref, lrf_ref, stf_ref[...], 0),
                                            (qkvb_ref, lrb_ref, stb_ref[...], 1)], wal_ref, bal_ref)
    stf_ref[...] = st_f
    stb_ref[...] = st_b
    rows_f = pl.ds(pl.multiple_of(n * GLA_STEP, GLA_STEP), GLA_STEP)
    rows_b = pl.ds(pl.multiple_of((nb - 1 - n) * GLA_STEP, GLA_STEP), GLA_STEP)
    of_ref[rows_f, :] = o_f
    obk_ref[rows_b, :] = o_b

    @pl.when(2 * n >= nb)
    def _():
        ob_ref[rows_f, :] = _gla_finish(o_f + obk_ref[rows_f, :], rf_ref[...], gain_ref[...])
        ob_ref[rows_b, :] = _gla_finish(of_ref[rows_b, :] + o_b, rb_ref[...], gain_ref[...])


def _gla_lat(cfg, l, zb, wal, bal, gain, s0):
    nbl = cfg.tl // GLA_STEP
    base = cfg.n_ctx // GLA_STEP
    assert nbl % 2 == 0 and cfg.tl % GLA_STEP == 0 and cfg.n_ctx % GLA_STEP == 0

    def blk(width, col, rev):
        def imap(b, n):
            return (base + b * nbl + (nbl - 1 - n if rev else n), col)
        return pl.BlockSpec((GLA_STEP, width), imap)

    qkv_w = 2 * B_QK + B_WIDTH
    per_dir = lambda rev: [blk(qkv_w, 0, rev), blk(LANES, ZB_LR // LANES, rev), blk(B_WIDTH, qkv_w // B_WIDTH, rev)]
    return pl.pallas_call(
        _gla_lat_kernel,
        grid=(cfg.nl, nbl),
        in_specs=per_dir(False) + per_dir(True) + [
            _const_spec((LANES, 2 * B_QK)), _const_spec((1, 2 * B_QK)), _const_spec((1, B_WIDTH)),
            pl.BlockSpec((None, 2, B_HEADS, B_DK, B_DV), lambda b, n: (b, 0, 0, 0, 0))],
        out_specs=pl.BlockSpec((cfg.tl, B_WIDTH), lambda b, n: (b, 0)),
        out_shape=jax.ShapeDtypeStruct((cfg.n_lat, B_WIDTH), BF16),
        scratch_shapes=[pltpu.VMEM((B_QK, B_DV), F32), pltpu.VMEM((B_QK, B_DV), F32),
                        pltpu.VMEM((cfg.tl, B_WIDTH), F32), pltpu.VMEM((cfg.tl, B_WIDTH), F32)],
        compiler_params=_cparams(2),
        name=f"gla_lat{l}",
    )(zb, zb, zb, zb, zb, zb, wal, bal, gain, s0)


LRU_CHUNK = TS // SUBLANES


TM_ODD_IN = 1024


def _odd_in_kernel(x_ref, mod_ref, nw_ref, w_ref, o_ref, *, cfg):
    g = _group_of_tile(cfg, pl.program_id(0), TM_ODD_IN)
    shift, scale, _ = _mod_rows(mod_ref, g, 0)
    h = _modnorm(_to_scan_order(x_ref[...]), nw_ref[...], shift, scale).astype(BF16)
    z = jnp.dot(h, w_ref[...].astype(BF16), preferred_element_type=F32)
    o_ref[:, 0:D_RNN] = jax.nn.gelu(z[:, 0:D_RNN])
    o_ref[:, D_RNN:2 * D_RNN] = z[:, D_RNN:2 * D_RNN]


def _odd_in(cfg, l, x, mods, nw, w):
    n = cfg.n_tok
    return pl.pallas_call(
        functools.partial(_odd_in_kernel, cfg=cfg),
        grid=(n // TM_ODD_IN,),
        in_specs=[pl.BlockSpec((TM_ODD_IN, D_MODEL), lambda i: (i, 0)), _mod_spec(l), _const_spec((1, D_MODEL)),
                  pl.BlockSpec((None, D_MODEL, 2 * D_RNN), lambda i: (l // 2, 0, 0),
                               pipeline_mode=pl.Buffered(1))],
        out_specs=pl.BlockSpec((TM_ODD_IN, 2 * D_RNN), lambda i: (i, 0)),
        out_shape=jax.ShapeDtypeStruct((n, 2 * D_RNN), F32),
        compiler_params=_cparams(1),
        name=f"odd_in{l}",
    )(x, mods, nw, w)


def _block_position(cfg, ib):
    ncb = cfg.n_ctx // TS
    bpc = cfg.tc // TS
    bpl = cfg.tl // TS
    is_ctx = ib < ncb
    jl = jnp.maximum(ib - ncb, 0)
    pos = jnp.where(is_ctx, ib % bpc, jl % bpl)
    per = jnp.where(is_ctx, bpc, bpl)
    return is_ctx, jl // bpl, pos == 0, pos == per - 1


def _conv_centred(u, prev16, next8, first, last, cw_ref, cb_ref):
    sub = lax.broadcasted_iota(jnp.int32, (SUBLANES, D_RNN), 0)
    lastrow = TS - SUBLANES

    def from_prev_chunk(own, other):
        return jnp.where(sub == 0, jnp.where(first, 0.0, pltpu.roll(other, 1, 0)), pltpu.roll(own, 1, 0))

    s30 = from_prev_chunk(u[lastrow - SUBLANES:lastrow], prev16[0:SUBLANES])
    s31 = from_prev_chunk(u[lastrow:TS], prev16[SUBLANES:2 * SUBLANES])
    n0 = jnp.where(sub == SUBLANES - 1, jnp.where(last, 0.0, pltpu.roll(next8, SUBLANES - 1, 0)),
                   pltpu.roll(u[0:SUBLANES], SUBLANES - 1, 0))
    m2 = jnp.concatenate([s30, s31, u[0:lastrow - SUBLANES]], axis=0)
    m1 = jnp.concatenate([s31, u[0:lastrow]], axis=0)
    p1 = jnp.concatenate([u[SUBLANES:TS], n0], axis=0)
    return m2 * cw_ref[0:1, :] + m1 * cw_ref[1:2, :] + u * cw_ref[2:3, :] + p1 * cw_ref[3:4, :] + cb_ref[...]


def _lru_terms(uh, wg_ref, bg_ref, lam_ref):
    ucb = uh.astype(BF16)
    rs, is_ = [], []
    for nblk in range(LRU_BLOCKS):
        z = jnp.dot(ucb[:, nblk * LRU_BLOCK_W:(nblk + 1) * LRU_BLOCK_W], wg_ref[nblk],
                    preferred_element_type=F32)
        rs.append(z[:, 0:LRU_BLOCK_W])
        is_.append(z[:, LRU_BLOCK_W:2 * LRU_BLOCK_W])
    tr = jnp.tanh(jnp.concatenate(rs, axis=1) + bg_ref[0:1, :])
    ti = jnp.tanh(jnp.concatenate(is_, axis=1) + bg_ref[1:2, :])
    k2 = (-0.5 * LRU_C * LOG2_E) * jax.nn.softplus(-lam_ref[...])
    a = jnp.exp2(k2 + k2 * tr)
    w = 1.0 - a * a
    root = jnp.where(w > 0.0, w * lax.rsqrt(w), 0.0)
    v = root * ((1.0 + ti) * uh)
    return a, v


def _lru_scan(a, v, carry, reverse):
    order = range(LRU_CHUNK - 1, -1, -1) if reverse else range(LRU_CHUNK)
    h = jnp.zeros((SUBLANES, D_RNN), F32)
    p = jnp.ones((SUBLANES, D_RNN), F32)
    hs, ps = [None] * LRU_CHUNK, [None] * LRU_CHUNK
    for j in order:
        aj = a[j * SUBLANES:(j + 1) * SUBLANES]
        h = aj * h + v[j * SUBLANES:(j + 1) * SUBLANES]
        p = aj * p
        hs[j], ps[j] = h, p
    entry = [None] * SUBLANES
    for c in (range(SUBLANES - 1, -1, -1) if reverse else range(SUBLANES)):
        entry[c] = carry
        carry = h[c:c + 1] + p[c:c + 1] * carry
    hm = jnp.concatenate(entry, axis=0)
    return jnp.concatenate([hs[j] + ps[j] * hm for j in range(LRU_CHUNK)], axis=0), carry


LRU_SUB = 2
LRU_STEP = LRU_SUB * TS


def _lru_scans(terms, h0_ref, carry_ref, hs_ref, cfg, tile, direction):
    carry = carry_ref[...]
    ys = [None] * LRU_SUB
    for sb in (range(LRU_SUB - 1, -1, -1) if direction else range(LRU_SUB)):
        is_ctx, jseq, first, last = _block_position(cfg, tile * LRU_SUB + sb)
        start = last if direction else first
        carry = jnp.where(start, jnp.where(is_ctx, 0.0, h0_ref[jseq]), carry)
        ys[sb], carry = _lru_scan(*terms[sb], carry, bool(direction))
        hs_ref[sb] = carry
    carry_ref[...] = carry
    return ys


def _odd_fwd_kernel(u_ref, up_ref, un_ref, cw_ref, cb_ref, wg_ref, bg_ref, lam_ref, h0_ref,
                    yf_ref, uc_ref, hs_ref, carry_ref, *, cfg):
    tile = pl.program_id(0)
    terms = []
    for sb in range(LRU_SUB):
        _, _, first, last = _block_position(cfg, tile * LRU_SUB + sb)
        rows = slice(sb * TS, (sb + 1) * TS)
        prev16 = up_ref[...] if sb == 0 else u_ref[sb * TS - 2 * SUBLANES:sb * TS, :]
        next8 = un_ref[...] if sb == LRU_SUB - 1 else u_ref[(sb + 1) * TS:(sb + 1) * TS + SUBLANES, :]
        uc = _conv_centred(u_ref[rows, :], prev16, next8, first, last, cw_ref, cb_ref)
        uc_ref[rows, :] = uc
        terms.append(_lru_terms(uc, wg_ref, bg_ref, lam_ref))
    for sb, y in enumerate(_lru_scans(terms, h0_ref, carry_ref, hs_ref, cfg, tile, 0)):
        yf_ref[sb * TS:(sb + 1) * TS, :] = y


def _odd_bwd_kernel(g_ref, uc_ref, yf_ref, mod_ref, wg_ref, bg_ref, lam_ref, h0_ref, wout_ref,
                    o_ref, hs_ref, carry_ref, *, cfg):
    tile = pl.num_programs(0) - 1 - pl.program_id(0)
    terms = [_lru_terms(uc_ref[sb * TS:(sb + 1) * TS, :], wg_ref, bg_ref, lam_ref) for sb in range(LRU_SUB)]
    y = jnp.concatenate(_lru_scans(terms, h0_ref, carry_ref, hs_ref, cfg, tile, 1), axis=0)
    mix = (g_ref[...] * (yf_ref[...] + y)).astype(BF16)
    grp = _group_of_tile(cfg, tile, LRU_STEP)
    gate = mod_ref[2, pl.ds(grp, 1), :]
    o_ref[...] = gate * jnp.dot(mix, wout_ref[...], preferred_element_type=F32)


def _odd_scans(cfg, l, gu, mods, cw, cb, wg, bg, lam, h0, w_out):
    n = cfg.n_tok
    assert n % LRU_STEP == 0 and cfg.n_ctx % LRU_STEP == 0 and cfg.tl % LRU_STEP == 0
    nblk = n // TS
    nstep = n // LRU_STEP
    r8 = LRU_STEP // SUBLANES
    last8 = n // SUBLANES - 1

    def specs(rev):
        it = (lambda i: nstep - 1 - i) if rev else (lambda i: i)
        cur = lambda col: pl.BlockSpec((LRU_STEP, D_RNN), lambda i: (it(i), col))
        cur_in = cur
        prev = pl.BlockSpec((2 * SUBLANES, D_RNN), lambda i: (jnp.maximum(it(i) * (r8 // 2) - 1, 0), 1))
        nxt = pl.BlockSpec((SUBLANES, D_RNN), lambda i: (jnp.minimum((it(i) + 1) * r8, last8), 1))
        hs = pl.BlockSpec((LRU_SUB, 1, D_RNN), lambda i: (it(i), 0, 0))
        return cur, cur_in, prev, nxt, hs

    def weights(direction):
        return [pl.BlockSpec((None, LRU_BLOCKS, LRU_BLOCK_W, 2 * LRU_BLOCK_W), lambda i: (direction, 0, 0, 0)),
                pl.BlockSpec((None, 2, D_RNN), lambda i: (direction, 0, 0)),
                pl.BlockSpec((None, 1, D_RNN), lambda i: (direction, 0, 0)),
                pl.BlockSpec((None, cfg.nl, 1, D_RNN), lambda i: (direction, 0, 0, 0))]

    cur, cur_in, prev, nxt, hs = specs(False)
    tok_shape = jax.ShapeDtypeStruct((n, D_RNN), F32)
    y_f, uc, hs_f = pl.pallas_call(
        functools.partial(_odd_fwd_kernel, cfg=cfg),
        grid=(nstep,),
        in_specs=[cur_in(1), prev, nxt, _const_spec((CONV_W, D_RNN)), _const_spec((1, D_RNN))] + weights(0),
        out_specs=[cur(0), cur(0), hs],
        out_shape=[tok_shape, tok_shape, jax.ShapeDtypeStruct((nblk, 1, D_RNN), F32)],
        scratch_shapes=[pltpu.VMEM((1, D_RNN), F32)],
        compiler_params=_cparams(1),
        name=f"lru_fwd{l}",
    )(gu, gu, gu, cw, cb, wg, bg, lam, h0)
    cur, cur_in, _, _, hs = specs(True)
    upd, hs_b = pl.pallas_call(
        functools.partial(_odd_bwd_kernel, cfg=cfg),
        grid=(nstep,),
        in_specs=[cur_in(0), cur_in(0), cur_in(0), _mod_spec(l)] + weights(1) + [_const_spec((D_RNN, D_MODEL))],
        out_specs=[cur(0), hs],
        out_shape=[tok_shape, jax.ShapeDtypeStruct((nblk, 1, D_RNN), F32)],
        scratch_shapes=[pltpu.VMEM((1, D_RNN), F32)],
        compiler_params=_cparams(1),
        name=f"lru_bwd{l}",
    )(gu, uc, y_f, mods, wg, bg, lam, h0, w_out)
    return upd, hs_f, hs_b


def _rope_table(tl):
    pos = np.arange(tl)
    nf = HEAD_DIM // 4
    inv = ROPE_BASE ** (-np.arange(nf, dtype=np.float32) / nf)
    ar = (pos // GRID_W).astype(np.float32)[:, None] * inv
    ac = (pos % GRID_W).astype(np.float32)[:, None] * inv
    ang = np.concatenate([ar, ar, ac, ac] * 2, axis=-1)
    lo = (np.arange(LANES) % (HEAD_DIM // 2)) < nf
    cos, sin = np.cos(ang), np.sin(ang)
    tab = np.concatenate([cos, np.where(lo, -sin, 0.0), np.where(lo, 0.0, sin)], axis=-1)
    return jnp.asarray(tab, F32)


def _dup_heads(w):
    lead = w.shape[:-1]
    w = w.reshape(lead + (A_KV_HEADS, 1, HEAD_DIM))
    return jnp.broadcast_to(w, lead + (A_KV_HEADS, 2, HEAD_DIM)).reshape(lead + (2 * LANES,))


def _pad_low_rank(w):
    pad = jnp.zeros((w.shape[0], LANES - 2 * GLA_RANK), w.dtype)
    return jnp.concatenate([w[:, P_MAIN:], pad], axis=1).astype(BF16)


def _pack_alpha(w_alpha, b_alpha):
    w = jnp.zeros((LANES, 2 * B_QK), F32)
    w = w.at[0:GLA_RANK, 0:B_QK].set(w_alpha[0]).at[GLA_RANK:2 * GLA_RANK, B_QK:].set(w_alpha[1])
    return w.astype(BF16), b_alpha.reshape(1, 2 * B_QK)


def _forward(cfg, x_prompt, x_sample, c, cache_k, cache_v, state_gla, state_lru, c_ctx,
             w_ada, b_ada, norm_mix, norm_ffn, w_in_even, attn_sink, w_alpha, b_alpha, gla_gain,
             w_out_even, w_in_odd, conv_w, conv_b, w_gate_a, b_gate_a, w_gate_x, b_gate_x,
             lru_lambda, w_out_odd, w_ffn_in, w_ffn_out, norm_final):
    assert cfg.tc == TS and cfg.tl % TM == 0 and cfg.n_ctx % TM == 0 and cfg.nl + 1 <= SUBLANES
    x = (x_prompt.reshape(cfg.n_ctx, D_MODEL), x_sample.reshape(cfg.n_lat, D_MODEL))
    cvec = jnp.concatenate([c_ctx[None, :], c, jnp.zeros((SUBLANES - 1 - cfg.nl, D_MODEL), F32)], axis=0)
    mods = _ada(cvec, w_ada, b_ada)
    rope_tab = _rope_table(cfg.tl)
    nf = norm_final.reshape(1, D_MODEL)
    w_out_even_bf = w_out_even.astype(BF16)
    new_k, new_v, new_gla, new_lru = [], [], [], []
    for l in range(DEPTH):
        nw = norm_mix[l].reshape(1, D_MODEL)
        if l % 2 == 0:
            e = l // 2
            za, zb, k_ctx, v_ctx = _even_in(cfg, l, x, mods, nw, w_in_even, _pad_low_rank(w_in_even[e]), rope_tab)
            new_k.append(k_ctx.reshape(cfg.nc, cfg.tc, A_KV_HEADS, HEAD_DIM))
            new_v.append(v_ctx.reshape(cfg.nc, cfg.tc, A_KV_HEADS, HEAD_DIM))
            ck2 = _dup_heads(cache_k[:, e].reshape(cfg.nl, PAST_LEN, A_KV_HEADS * HEAD_DIM)).astype(BF16)
            cv2 = _dup_heads(cache_v[:, e].reshape(cfg.nl, PAST_LEN, A_KV_HEADS * HEAD_DIM)).astype(BF16)
            oa_ctx = _attn_ctx(cfg, l, za, attn_sink[e])
            oa_lat = _attn_lat(cfg, l, za, ck2, cv2, attn_sink[e])
            wal, bal = _pack_alpha(w_alpha[e], b_alpha[e])
            gain = gla_gain[e].reshape(1, B_WIDTH)
            ob_ctx, s_new = _gla_ctx(cfg, l, zb, wal, bal, gain)
            ob_lat = _gla_lat(cfg, l, zb, wal, bal, gain, state_gla[:, e])
            new_gla.append(s_new)
            pre_args = (oa_ctx, oa_lat, ob_ctx, ob_lat, w_out_even_bf)
        else:
            o = l // 2
            gu = _odd_in(cfg, l, x, mods, nw, w_in_odd)
            wg = jnp.concatenate([w_gate_a[o], w_gate_x[o]], axis=-1).astype(BF16)
            bg = 0.5 * jnp.stack([b_gate_a[o], b_gate_x[o]], axis=1)
            lam = lru_lambda[o].reshape(2, 1, D_RNN)
            h0 = jnp.transpose(state_lru[:, o], (1, 0, 2)).reshape(2, cfg.nl, 1, D_RNN)
            upd, hs_f, hs_b = _odd_scans(cfg, l, gu, mods, 0.5 * conv_w[o], 0.5 * conv_b[o].reshape(1, D_RNN),
                                         wg, bg, lam, h0, w_out_odd[o].astype(BF16))
            new_lru.append(jnp.stack([hs_f[:cfg.nc, 0], hs_b[:cfg.nc, 0]], axis=1))
            pre_args = (upd,)
        x = _ffn(cfg, l, x, pre_args, mods, norm_ffn[l].reshape(1, D_MODEL), w_ffn_in, w_ffn_out, nf,
                 final=(l == DEPTH - 1))
    y_prompt = x[0].reshape(cfg.nc, cfg.tc, D_MODEL)
    y_sample = x[1].reshape(cfg.nl, cfg.tl, D_MODEL)
    return (y_prompt, y_sample, jnp.stack(new_k, axis=1), jnp.stack(new_v, axis=1),
            jnp.stack(new_gla, axis=1), jnp.stack(new_lru, axis=1))


def kernel(x_prompt, x_sample, c, cache_k, cache_v, state_gla, state_lru, c_ctx, w_ada, b_ada, norm_mix, norm_ffn, w_in_even, attn_sink, w_alpha, b_alpha, gla_gain, w_out_even, w_in_odd, conv_w, conv_b, w_gate_a, b_gate_a, w_gate_x, b_gate_x, lru_lambda, w_out_odd, w_ffn_in, w_ffn_out, norm_final):
    cfg = Cfg(nc=x_prompt.shape[0], tc=x_prompt.shape[1], nl=x_sample.shape[0], tl=x_sample.shape[1])
    return _forward(cfg, x_prompt, x_sample, c, cache_k, cache_v, state_gla, state_lru, c_ctx,
                    w_ada, b_ada, norm_mix, norm_ffn, w_in_even, attn_sink, w_alpha, b_alpha, gla_gain,
                    w_out_even, w_in_odd, conv_w, conv_b, w_gate_a, b_gate_a, w_gate_x, b_gate_x,
                    lru_lambda, w_out_odd, w_ffn_in, w_ffn_out, norm_final)
```

```python
import functools
from typing import NamedTuple

import jax
import jax.numpy as jnp
import numpy as np
from jax import lax
from jax.experimental import pallas as pl
from jax.experimental.pallas import tpu as pltpu

F32 = jnp.float32
BF16 = jnp.bfloat16

D_MODEL = 1024
DEPTH = 4
EPS = 1e-6

HEAD_DIM = 64
A_Q_HEADS = 8
A_KV_HEADS = 2
A_GROUPS = A_Q_HEADS // A_KV_HEADS
A_WIDTH = A_Q_HEADS * HEAD_DIM
ATT_BLOCK = 128
ROPE_BASE = 10000.0
GRID_W = 64
NEG_INF = -1e30
PAST_LEN = 256

B_HEADS = 4
B_DK = 64
B_DV = 128
B_QK = B_HEADS * B_DK
B_WIDTH = B_HEADS * B_DV
GLA_RANK = 16
GLA_NORMALIZER = 16.0
GLA_CHUNK = 64

D_RNN = D_MODEL
LRU_BLOCK_W = 256
LRU_BLOCKS = D_RNN // LRU_BLOCK_W
CONV_W = 4
LRU_C = 8.0
LOG2_E = 1.4426950408889634

D_FF = 2816

LANES = 128
SUBLANES = 8

ZA_W = A_WIDTH + 2 * 2 * LANES
ZB_LR = B_QK + B_QK + B_WIDTH + B_WIDTH
ZB_W = ZB_LR + LANES
P_A = A_WIDTH + 2 * LANES
P_MAIN = P_A + ZB_LR

TM = 512
TS = 256
FF_CHUNK = 256

VMEM_LIMIT = 56 * 1024 * 1024


class Cfg(NamedTuple):
    nc: int
    tc: int
    nl: int
    tl: int

    @property
    def n_ctx(self):
        return self.nc * self.tc

    @property
    def n_lat(self):
        return self.nl * self.tl

    @property
    def n_tok(self):
        return self.n_ctx + self.n_lat


def _cparams(n_axes):
    return pltpu.CompilerParams(dimension_semantics=("arbitrary",) * n_axes,
                                vmem_limit_bytes=VMEM_LIMIT)


def _const_spec(shape):
    nd = len(shape)
    return pl.BlockSpec(shape, lambda *_: (0,) * nd)


def _silu(x):
    return x * jax.nn.sigmoid(x)


def _group_of_tile(cfg, i, tile):
    nct = cfg.n_ctx // tile
    per_lat = cfg.tl // tile
    return jnp.where(i < nct, 0, 1 + jnp.maximum(i - nct, 0) // per_lat)


def _modnorm(x, nw, shift, scale):
    ms = jnp.mean(x * x, axis=-1, keepdims=True)
    y = x * lax.rsqrt(ms + EPS) * nw
    return y * (1.0 + scale) + shift


def _mod_rows(mod_ref, g, first):
    return tuple(mod_ref[first + j, pl.ds(g, 1), :] for j in range(3))


def _ada_kernel(c_ref, w_ref, b_ref, o_ref):
    s = _silu(c_ref[...]).astype(BF16)
    o_ref[...] = jnp.dot(s, w_ref[...].astype(BF16), preferred_element_type=F32) + b_ref[...]


def _ada(cvec, w_ada, b_ada):
    b4 = b_ada.reshape(DEPTH, 6, 1, D_MODEL)
    return pl.pallas_call(
        _ada_kernel,
        grid=(DEPTH, 6),
        in_specs=[
            _const_spec((SUBLANES, D_MODEL)),
            pl.BlockSpec((None, D_MODEL, D_MODEL), lambda l, j: (l, 0, j)),
            pl.BlockSpec((None, None, 1, D_MODEL), lambda l, j: (l, j, 0, 0)),
        ],
        out_specs=pl.BlockSpec((None, None, SUBLANES, D_MODEL), lambda l, j: (l, j, 0, 0)),
        out_shape=jax.ShapeDtypeStruct((DEPTH, 6, SUBLANES, D_MODEL), F32),
        compiler_params=_cparams(2),
        name="ada",
    )(cvec, w_ada, b4)


def _mod_spec(l):
    return pl.BlockSpec((None, 6, SUBLANES, D_MODEL), lambda *_: (l, 0, 0, 0))


def _ffn_tile(x, nw, shift, scale, gate, win_ref, wout_ref):
    h = _modnorm(x, nw, shift, scale).astype(BF16)
    acc = None
    for c in range(D_FF // FF_CHUNK):
        lo, hi = c * FF_CHUNK, (c + 1) * FF_CHUNK
        g = jnp.dot(h, win_ref[:, lo:hi].astype(BF16), preferred_element_type=F32)
        u = jnp.dot(h, win_ref[:, D_FF + lo:D_FF + hi].astype(BF16), preferred_element_type=F32)
        a = (_silu(g) * u).astype(BF16)
        part = jnp.dot(a, wout_ref[lo:hi, :].astype(BF16), preferred_element_type=F32)
        acc = part if acc is None else acc + part
    return x + gate * acc


def _x_args(cfg, x, tile=TM):
    if not isinstance(x, tuple):
        return [x], [pl.BlockSpec((tile, D_MODEL), lambda i: (i, 0))]
    nct = cfg.n_ctx // tile
    return list(x), [pl.BlockSpec((tile, D_MODEL), lambda i: (jnp.minimum(i, nct - 1), 0)),
                     pl.BlockSpec((tile, D_MODEL), lambda i: (jnp.maximum(i - nct, 0), 0))]


def _load_x(cfg, x_refs, i, tile=TM):
    if len(x_refs) == 1:
        return x_refs[0][...]
    return jnp.where(i < cfg.n_ctx // tile, x_refs[0][...], x_refs[1][...])


def _to_scan_order(x):
    nb, d = x.shape[0] // TS, x.shape[1]
    return jnp.swapaxes(x.reshape(nb, SUBLANES, TS // SUBLANES, d), 1, 2).reshape(nb * TS, d)


def _from_scan_order(x):
    nb, d = x.shape[0] // TS, x.shape[1]
    return jnp.swapaxes(x.reshape(nb, TS // SUBLANES, SUBLANES, d), 1, 2).reshape(nb * TS, d)


def _ffn_kernel(*refs, cfg, n_x, pre, final):
    x_refs, refs = refs[:n_x], refs[n_x:]
    i = pl.program_id(0)
    g = _group_of_tile(cfg, i, TM)
    x = _load_x(cfg, x_refs, i)
    if pre == "even":
        (oac_ref, oal_ref, obc_ref, obl_ref, wmix_ref), refs = refs[:5], refs[5:]
    else:
        d_ref, refs = refs[0], refs[1:]
    mod_ref, nw_ref, win_ref, wout_ref, nf_ref = refs[:5]
    outs = refs[5:]
    if pre == "even":
        is_ctx_tile = i < cfg.n_ctx // TM
        o = jnp.concatenate([jnp.where(is_ctx_tile, oac_ref[...], oal_ref[...]),
                             jnp.where(is_ctx_tile, obc_ref[...], obl_ref[...])], axis=1)
        x = x + mod_ref[2, pl.ds(g, 1), :] * jnp.dot(o, wmix_ref[...], preferred_element_type=F32)
    else:
        x = x + _from_scan_order(d_ref[...])
    shift, scale, gate = _mod_rows(mod_ref, g, 3)
    y = _ffn_tile(x, nw_ref[...], shift, scale, gate, win_ref, wout_ref)
    if not final:
        outs[0][...] = y
        return
    ms = jnp.mean(y * y, axis=-1, keepdims=True)
    y = y * lax.rsqrt(ms + EPS) * nf_ref[...]
    is_ctx = i < cfg.n_ctx // TM

    @pl.when(is_ctx)
    def _():
        outs[0][...] = y

    @pl.when(jnp.logical_not(is_ctx))
    def _():
        outs[1][...] = y


def _ffn(cfg, l, x, pre_args, mods, nw, win, wout, nf, final):
    n = cfg.n_tok
    nct = cfg.n_ctx // TM
    tok = lambda w: pl.BlockSpec((TM, w), lambda i: (i, 0))
    ctx = lambda w: pl.BlockSpec((TM, w), lambda i: (jnp.minimum(i, nct - 1), 0))
    lat = lambda w: pl.BlockSpec((TM, w), lambda i: (jnp.maximum(i - nct, 0), 0))
    x_ops, x_specs = _x_args(cfg, x)
    if l % 2 == 0:
        pre = "even"
        pre_specs = [ctx(A_WIDTH), lat(A_WIDTH), ctx(B_WIDTH), lat(B_WIDTH),
                     pl.BlockSpec((None, A_WIDTH + B_WIDTH, D_MODEL), lambda i: (l // 2, 0, 0))]
    else:
        pre, pre_specs = "odd", [tok(D_MODEL)]
    if final:
        out_specs = [pl.BlockSpec((TM, D_MODEL), lambda i: (jnp.minimum(i, nct - 1), 0)),
                     pl.BlockSpec((TM, D_MODEL), lambda i: (jnp.maximum(i - nct, 0), 0))]
        out_shape = [jax.ShapeDtypeStruct((cfg.n_ctx, D_MODEL), F32),
                     jax.ShapeDtypeStruct((cfg.n_lat, D_MODEL), F32)]
    else:
        out_specs, out_shape = tok(D_MODEL), jax.ShapeDtypeStruct((n, D_MODEL), F32)
    return pl.pallas_call(
        functools.partial(_ffn_kernel, cfg=cfg, n_x=len(x_ops), pre=pre, final=final),
        grid=(n // TM,),
        in_specs=x_specs + pre_specs + [
            _mod_spec(l), _const_spec((1, D_MODEL)),
            pl.BlockSpec((None, D_MODEL, 2 * D_FF), lambda i: (l, 0, 0), pipeline_mode=pl.Buffered(1)),
            pl.BlockSpec((None, D_FF, D_MODEL), lambda i: (l, 0, 0), pipeline_mode=pl.Buffered(1)),
            _const_spec((1, D_MODEL))],
        out_specs=out_specs,
        out_shape=out_shape,
        compiler_params=_cparams(1),
        name=f"ffn{l}",
    )(*x_ops, *pre_args, mods, nw, win, wout, nf)


def _rope(x, tab_ref):
    cos, sin_a, sin_b = tab_ref[:, 0:LANES], tab_ref[:, LANES:2 * LANES], tab_ref[:, 2 * LANES:3 * LANES]
    parts = []
    for j in range(x.shape[1] // LANES):
        xs = x[:, j * LANES:(j + 1) * LANES]
        parts.append(xs * cos + pltpu.roll(xs, LANES - 16, 1) * sin_a + pltpu.roll(xs, 16, 1) * sin_b)
    return jnp.concatenate(parts, axis=1)


TM_EVEN_IN = 1024


def _even_in_kernel(*refs, cfg, n_x):
    x_refs, (mod_ref, nw_ref, w_ref, wlr_ref, tab_ref, za_ref, zb_ref, k_ref, v_ref) = refs[:n_x], refs[n_x:]
    i = pl.program_id(0)
    g = _group_of_tile(cfg, i, TM_EVEN_IN)
    shift, scale, _ = _mod_rows(mod_ref, g, 0)
    h = _modnorm(_load_x(cfg, x_refs, i, TM_EVEN_IN), nw_ref[...], shift, scale).astype(BF16)
    z = jnp.dot(h, w_ref[...].astype(BF16), preferred_element_type=F32)
    q = z[:, 0:A_WIDTH] * (LOG2_E * HEAD_DIM ** -0.5)
    k = z[:, A_WIDTH:A_WIDTH + LANES]
    v = z[:, A_WIDTH + LANES:P_A]
    lo = lax.broadcasted_iota(jnp.int32, (TM_EVEN_IN, LANES), 1) < HEAD_DIM

    def twice(a):
        swapped = pltpu.roll(a, HEAD_DIM, 1)
        return jnp.concatenate([jnp.where(lo, a, swapped), jnp.where(lo, swapped, a)], axis=1).astype(BF16)

    za_ref[:, A_WIDTH + 2 * LANES:ZA_W] = twice(v)
    zb_ref[:, 0:B_QK] = z[:, P_A:P_A + B_QK] * (B_DK ** -0.5)
    zb_ref[:, B_QK:ZB_LR] = z[:, P_A + B_QK:P_MAIN]
    zb_ref[:, ZB_LR:ZB_W] = jnp.dot(h, wlr_ref[...], preferred_element_type=F32)

    @pl.when(g == 0)
    def _():
        za_ref[:, 0:A_WIDTH] = q.astype(BF16)
        za_ref[:, A_WIDTH:A_WIDTH + 2 * LANES] = twice(k)
        k_ref[...] = k
        v_ref[...] = v

    @pl.when(g != 0)
    def _():
        za_ref[:, 0:A_WIDTH] = _rope(q, tab_ref).astype(BF16)
        za_ref[:, A_WIDTH:A_WIDTH + 2 * LANES] = twice(_rope(k, tab_ref))


def _even_in(cfg, l, x, mods, nw, w_all, w_lr, rope_tab):
    n = cfg.n_tok
    tile = TM_EVEN_IN
    assert cfg.n_ctx % tile == 0 and cfg.tl % tile == 0
    nct = cfg.n_ctx // tile
    per_lat = cfg.tl // tile
    tok = lambda w: pl.BlockSpec((tile, w), lambda i: (i, 0))
    x_ops, x_specs = _x_args(cfg, x, tile)
    ctx_kv = pl.BlockSpec((tile, LANES), lambda i: (jnp.minimum(i, nct - 1), 0))
    return pl.pallas_call(
        functools.partial(_even_in_kernel, cfg=cfg, n_x=len(x_ops)),
        grid=(n // tile,),
        in_specs=x_specs + [_mod_spec(l), _const_spec((1, D_MODEL)),
                            pl.BlockSpec((None, D_MODEL, P_MAIN), lambda i: (l // 2, 0, 0),
                                         pipeline_mode=pl.Buffered(1)),
                            _const_spec((D_MODEL, LANES)),
                            pl.BlockSpec((tile, 3 * LANES), lambda i: (jnp.maximum(i - nct, 0) % per_lat, 0))],
        out_specs=[tok(ZA_W), tok(ZB_W), ctx_kv, ctx_kv],
        out_shape=[jax.ShapeDtypeStruct((n, ZA_W), BF16),
                   jax.ShapeDtypeStruct((n, ZB_W), F32),
                   jax.ShapeDtypeStruct((cfg.n_ctx, LANES), F32),
                   jax.ShapeDtypeStruct((cfg.n_ctx, LANES), F32)],
        compiler_params=_cparams(1),
        name=f"even_in{l}",
    )(*x_ops, mods, nw, w_all, w_lr, rope_tab)


def _sink_attention(units, sink_ref):
    scores = []
    for q, k_all, _, mask, kvh in units:
        tq = q.shape[0]
        lo = lax.broadcasted_iota(jnp.int32, (tq, LANES), 1) < HEAD_DIM
        qs = []
        for gq in range(A_GROUPS):
            h = kvh * A_GROUPS + gq
            pair = q[:, (h // 2) * LANES:(h // 2 + 1) * LANES]
            keep = lo if h % 2 == 0 else jnp.logical_not(lo)
            qs.append(jnp.where(keep, pair, jnp.zeros_like(pair)))
        qst = jnp.concatenate(qs, axis=0)
        s = lax.dot_general(qst, k_all, (((1,), (1,)), ((), ())), preferred_element_type=F32)
        if mask is not None:
            nfree = s.shape[1] - mask.shape[1]
            s = jnp.concatenate([s[:, :nfree], jnp.where(mask, s[:, nfree:], NEG_INF)], axis=1)
        scores.append(s)
    probs = []
    for (q, _, _, _, kvh), s in zip(units, scores):
        tq = q.shape[0]
        rows = lax.broadcasted_iota(jnp.int32, (A_GROUPS * tq, 1), 0) // tq
        sink = jnp.zeros((A_GROUPS * tq, 1), F32)
        for gq in range(A_GROUPS):
            sink = jnp.where(rows == gq, sink_ref[kvh * A_GROUPS + gq] * LOG2_E, sink)
        m = jnp.maximum(jnp.max(s, axis=-1, keepdims=True), sink)
        p = jnp.exp2(s - m)
        den = jnp.sum(p, axis=-1, keepdims=True) + jnp.exp2(sink - m)
        probs.append((p.astype(BF16), den))
    outs = []
    for (q, _, v_all, _, _), (p, den) in zip(units, probs):
        tq = q.shape[0]
        lo = lax.broadcasted_iota(jnp.int32, (tq, LANES), 1) < HEAD_DIM
        o = jnp.dot(p, v_all, preferred_element_type=F32) / den
        pairs = [jnp.where(lo, o[(2 * pr) * tq:(2 * pr + 1) * tq], o[(2 * pr + 1) * tq:(2 * pr + 2) * tq])
                 for pr in range(2)]
        outs.append(jnp.concatenate(pairs, axis=1).astype(BF16))
    return outs


def _attn_ctx_kernel(sink_ref, q_ref, k_ref, v_ref, o_ref):
    q = q_ref[...]
    units = [(q, k_ref[:, kvh * LANES:(kvh + 1) * LANES], v_ref[:, kvh * LANES:(kvh + 1) * LANES], None, kvh)
             for kvh in range(A_KV_HEADS)]
    for kvh, o in enumerate(_sink_attention(units, sink_ref)):
        o_ref[:, 2 * kvh * LANES:2 * (kvh + 1) * LANES] = o


def _attn_ctx(cfg, l, za, sink):
    tc = cfg.tc
    return pl.pallas_call(
        _attn_ctx_kernel,
        grid=(cfg.nc,),
        in_specs=[pl.BlockSpec(memory_space=pltpu.SMEM),
                  pl.BlockSpec((tc, A_WIDTH), lambda b: (b, 0)),
                  pl.BlockSpec((tc, 2 * LANES), lambda b: (b, A_WIDTH // (2 * LANES))),
                  pl.BlockSpec((tc, 2 * LANES), lambda b: (b, A_WIDTH // (2 * LANES) + 1))],
        out_specs=pl.BlockSpec((tc, A_WIDTH), lambda b: (b, 0)),
        out_shape=jax.ShapeDtypeStruct((cfg.n_ctx, A_WIDTH), BF16),
        compiler_params=_cparams(1),
        name=f"attn_ctx{l}",
    )(sink, za, za, za)


ATT_STEP = 4 * ATT_BLOCK
ATT_WIN = 3 * ATT_BLOCK


def _attn_lat_kernel(sink_ref, q_ref, k_ref, v_ref, ck_ref, cv_ref, o_ref):
    n = pl.program_id(1)
    tl = k_ref.shape[0]
    shape = (A_GROUPS * ATT_BLOCK, ATT_WIN)
    rc = (lax.broadcasted_iota(jnp.int32, shape, 0) % ATT_BLOCK - lax.broadcasted_iota(jnp.int32, shape, 1)
          + ATT_BLOCK)
    units = []
    for j in range(ATT_STEP // ATT_BLOCK):
        q0 = (n * (ATT_STEP // ATT_BLOCK) + j) * ATT_BLOCK
        start = pl.multiple_of(jnp.clip(q0 - ATT_BLOCK, 0, tl - ATT_WIN), ATT_BLOCK)
        mask = lax.bitcast_convert_type(rc + (q0 - start), jnp.uint32) <= jnp.uint32(2 * ATT_BLOCK)
        q = q_ref[j * ATT_BLOCK:(j + 1) * ATT_BLOCK, :]
        for kvh in range(A_KV_HEADS):
            sl = slice(kvh * LANES, (kvh + 1) * LANES)
            k_all = jnp.concatenate([ck_ref[:, sl], k_ref[pl.ds(start, ATT_WIN), sl]], axis=0)
            v_all = jnp.concatenate([cv_ref[:, sl], v_ref[pl.ds(start, ATT_WIN), sl]], axis=0)
            units.append((q, k_all, v_all, mask, kvh))
    for u, o in enumerate(_sink_attention(units, sink_ref)):
        j, kvh = divmod(u, A_KV_HEADS)
        o_ref[j * ATT_BLOCK:(j + 1) * ATT_BLOCK, 2 * kvh * LANES:2 * (kvh + 1) * LANES] = o


def _attn_lat(cfg, l, za, cache_k2, cache_v2, sink):
    assert cfg.n_ctx % cfg.tl == 0 and cfg.tl % ATT_STEP == 0 and cfg.tl >= ATT_WIN
    steps = cfg.tl // ATT_STEP
    base = cfg.n_ctx // ATT_STEP
    kcol = A_WIDTH // (2 * LANES)
    seq = lambda col: pl.BlockSpec((cfg.tl, 2 * LANES), lambda b, n: (cfg.n_ctx // cfg.tl + b, col))
    cache = pl.BlockSpec((None, PAST_LEN, 2 * LANES), lambda b, n: (b, 0, 0))
    return pl.pallas_call(
        _attn_lat_kernel,
        grid=(cfg.nl, steps),
        in_specs=[pl.BlockSpec(memory_space=pltpu.SMEM),
                  pl.BlockSpec((ATT_STEP, A_WIDTH), lambda b, n: (base + b * steps + n, 0)),
                  seq(kcol), seq(kcol + 1), cache, cache],
        out_specs=pl.BlockSpec((ATT_STEP, A_WIDTH), lambda b, n: (b * steps + n, 0)),
        out_shape=jax.ShapeDtypeStruct((cfg.n_lat, A_WIDTH), BF16),
        compiler_params=_cparams(2),
        name=f"attn_lat{l}",
    )(sink, za, za, za, cache_k2, cache_v2)


def _gla_chunk_prepare(q, k, v, la, reverse):
    cn = GLA_CHUNK
    row = lax.broadcasted_iota(jnp.int32, (cn, B_QK), 0)
    b = la
    s = 1
    while s < cn:
        if reverse:
            b = b + jnp.where(row < cn - s, pltpu.roll(b, cn - s, 0), 0.0)
        else:
            b = b + jnp.where(row >= s, pltpu.roll(b, s, 0), 0.0)
        s *= 2
    b_last = b[0:1, :] if reverse else b[cn - 1:cn, :]
    q_e = q * jnp.exp(b)
    k_e = (k * jnp.exp(-b)).astype(BF16)
    k_s = (k * jnp.exp(b_last - b)).astype(BF16)
    vb = v.astype(BF16)
    row_t = row[:, 0:LANES]
    lane_t = lax.broadcasted_iota(jnp.int32, (cn, LANES), 1)
    lo = lane_t < B_DK
    col = lane_t % cn
    tri = (col >= row_t) if reverse else (col <= row_t)
    q_tiles, att_tiles = [], []
    for t in range(B_HEADS // 2):
        sl = slice(t * LANES, (t + 1) * LANES)
        k_t = k_e[:, sl]
        zero = jnp.zeros_like(k_t)
        kbd = jnp.concatenate([jnp.where(lo, k_t, zero), jnp.where(lo, zero, k_t)], axis=0)
        q_t = q_e[:, sl]
        att = lax.dot_general(q_t.astype(BF16), kbd, (((1,), (1,)), ((), ())), preferred_element_type=F32)
        q_tiles.append(q_t)
        att_tiles.append(jnp.where(tri, att, 0.0))
    upd = lax.dot_general(k_s, vb, (((0,), (0,)), ((), ())), preferred_element_type=F32)
    own = jnp.concatenate([upd[h * B_DK:(h + 1) * B_DK, h * B_DV:(h + 1) * B_DV] for h in range(B_HEADS)], axis=0)
    decay = jnp.broadcast_to(jnp.exp(b_last), (LANES, B_QK)).T
    return q_tiles, att_tiles, vb, own, decay


def _gla_chunk_output(q_tiles, att_tiles, vb, st):
    lo = lax.broadcasted_iota(jnp.int32, (GLA_CHUNK, LANES), 1) < B_DK
    st_b = st.astype(BF16)
    outs = []
    for h in range(B_HEADS):
        att, q_t = att_tiles[h // 2], q_tiles[h // 2]
        x = (jnp.where(lo, att, pltpu.roll(q_t, B_DK, 1)) if h % 2 == 0
             else jnp.where(lo, pltpu.roll(att, B_DK, 1), q_t))
        rhs = jnp.concatenate([vb[:, h * B_DV:(h + 1) * B_DV], st_b[h * B_DK:(h + 1) * B_DK, :]], axis=0)
        outs.append(jnp.dot(x.astype(BF16), rhs, preferred_element_type=F32))
    return jnp.concatenate(outs, axis=1)


def _gla_log_decay(lr_pad, wal_ref, bal_ref, direction):
    z = jnp.dot(lr_pad.astype(BF16), wal_ref[:, direction * B_QK:(direction + 1) * B_QK],
                preferred_element_type=F32) + bal_ref[:, direction * B_QK:(direction + 1) * B_QK]
    return jax.nn.log_sigmoid(z) / GLA_NORMALIZER


def _gla_blocks(jobs, wal_ref, bal_ref):
    prepared = []
    for qkv_ref, lr_ref, _, direction in jobs:
        nchunk = qkv_ref.shape[0] // GLA_CHUNK
        la_all = _gla_log_decay(lr_ref[...], wal_ref, bal_ref, direction)
        chunks = {}
        for ci in range(nchunk):
            rs = slice(ci * GLA_CHUNK, (ci + 1) * GLA_CHUNK)
            chunks[ci] = _gla_chunk_prepare(qkv_ref[rs, 0:B_QK], qkv_ref[rs, B_QK:2 * B_QK],
                                            qkv_ref[rs, 2 * B_QK:2 * B_QK + B_WIDTH], la_all[rs, :],
                                            bool(direction))
        prepared.append(chunks)
    results = []
    for (qkv_ref, _, st, direction), chunks in zip(jobs, prepared):
        nchunk = len(chunks)
        states = {}
        for ci in (range(nchunk - 1, -1, -1) if direction else range(nchunk)):
            states[ci] = st
            _, _, _, own, decay = chunks[ci]
            st = st * decay + own
        outs = [_gla_chunk_output(*chunks[ci][:3], states[ci]) for ci in range(nchunk)]
        results.append((jnp.concatenate(outs, axis=0), st))
    return results


def _stack_state(s_ref):
    return s_ref[...].reshape(B_QK, B_DV)


def _unstack_state(st, s_ref):
    s_ref[...] = st.reshape(B_HEADS, B_DK, B_DV)


def _gla_finish(o, r, gain):
    parts = []
    for h in range(B_HEADS):
        oh = o[:, h * B_DV:(h + 1) * B_DV]
        parts.append(oh * lax.rsqrt(jnp.mean(oh * oh, axis=-1, keepdims=True) + EPS))
    return (jnp.concatenate(parts, axis=1) * gain * _silu(r)).astype(BF16)


def _gla_ctx_kernel(qkv_ref, r_ref, lr_ref, wal_ref, bal_ref, gain_ref, ob_ref, snew_ref):
    zero = jnp.zeros((B_QK, B_DV), F32)
    (o_f, st_f), (o_b, st_b) = _gla_blocks([(qkv_ref, lr_ref, zero, 0), (qkv_ref, lr_ref, zero, 1)],
                                           wal_ref, bal_ref)
    _unstack_state(st_f, snew_ref.at[0])
    _unstack_state(st_b, snew_ref.at[1])
    ob_ref[...] = _gla_finish(o_f + o_b, r_ref[...], gain_ref[...])


def _gla_ctx(cfg, l, zb, wal, bal, gain):
    tc = cfg.tc
    return pl.pallas_call(
        _gla_ctx_kernel,
        grid=(cfg.nc,),
        in_specs=[pl.BlockSpec((tc, 2 * B_QK + B_WIDTH), lambda b: (b, 0)),
                  pl.BlockSpec((tc, B_WIDTH), lambda b: (b, (2 * B_QK + B_WIDTH) // B_WIDTH)),
                  pl.BlockSpec((tc, LANES), lambda b: (b, ZB_LR // LANES)),
                  _const_spec((LANES, 2 * B_QK)), _const_spec((1, 2 * B_QK)), _const_spec((1, B_WIDTH))],
        out_specs=[pl.BlockSpec((tc, B_WIDTH), lambda b: (b, 0)),
                   pl.BlockSpec((None, 2, B_HEADS, B_DK, B_DV), lambda b: (b, 0, 0, 0, 0))],
        out_shape=[jax.ShapeDtypeStruct((cfg.n_ctx, B_WIDTH), BF16),
                   jax.ShapeDtypeStruct((cfg.nc, 2, B_HEADS, B_DK, B_DV), F32)],
        compiler_params=_cparams(1),
        name=f"gla_ctx{l}",
    )(zb, zb, zb, wal, bal, gain)


GLA_STEP = 512


def _gla_lat_kernel(qkvf_ref, lrf_ref, rf_ref, qkvb_ref, lrb_ref, rb_ref, wal_ref, bal_ref, gain_ref,
                    s0_ref, ob_ref, stf_ref, stb_ref, of_ref, obk_ref):
    n = pl.program_id(1)
    nb = pl.num_programs(1)

    @pl.when(n == 0)
    def _():
        stf_ref[...] = _stack_state(s0_ref.at[0])
        stb_ref[...] = _stack_state(s0_ref.at[1])
    (o_f, st_f), (o_b, st_b) = _gla_blocks([(qkvf_ref, lrf_ref, stf_ref[...], 0),
                                            (qkvb_ref, lrb_ref, stb_ref[...], 1)], wal_ref, bal_ref)
    stf_ref[...] = st_f
    stb_ref[...] = st_b
    rows_f = pl.ds(pl.multiple_of(n * GLA_STEP, GLA_STEP), GLA_STEP)
    rows_b = pl.ds(pl.multiple_of((nb - 1 - n) * GLA_STEP, GLA_STEP), GLA_STEP)
    of_ref[rows_f, :] = o_f
    obk_ref[rows_b, :] = o_b

    @pl.when(2 * n >= nb)
    def _():
        ob_ref[rows_f, :] = _gla_finish(o_f + obk_ref[rows_f, :], rf_ref[...], gain_ref[...])
        ob_ref[rows_b, :] = _gla_finish(of_ref[rows_b, :] + o_b, rb_ref[...], gain_ref[...])


def _gla_lat(cfg, l, zb, wal, bal, gain, s0):
    nbl = cfg.tl // GLA_STEP
    base = cfg.n_ctx // GLA_STEP
    assert nbl % 2 == 0 and cfg.tl % GLA_STEP == 0 and cfg.n_ctx % GLA_STEP == 0

    def blk(width, col, rev):
        def imap(b, n):
            return (base + b * nbl + (nbl - 1 - n if rev else n), col)
        return pl.BlockSpec((GLA_STEP, width), imap)

    qkv_w = 2 * B_QK + B_WIDTH
    per_dir = lambda rev: [blk(qkv_w, 0, rev), blk(LANES, ZB_LR // LANES, rev), blk(B_WIDTH, qkv_w // B_WIDTH, rev)]
    return pl.pallas_call(
        _gla_lat_kernel,
        grid=(cfg.nl, nbl),
        in_specs=per_dir(False) + per_dir(True) + [
            _const_spec((LANES, 2 * B_QK)), _const_spec((1, 2 * B_QK)), _const_spec((1, B_WIDTH)),
            pl.BlockSpec((None, 2, B_HEADS, B_DK, B_DV), lambda b, n: (b, 0, 0, 0, 0))],
        out_specs=pl.BlockSpec((cfg.tl, B_WIDTH), lambda b, n: (b, 0)),
        out_shape=jax.ShapeDtypeStruct((cfg.n_lat, B_WIDTH), BF16),
        scratch_shapes=[pltpu.VMEM((B_QK, B_DV), F32), pltpu.VMEM((B_QK, B_DV), F32),
                        pltpu.VMEM((cfg.tl, B_WIDTH), F32), pltpu.VMEM((cfg.tl, B_WIDTH), F32)],
        compiler_params=_cparams(2),
        name=f"gla_lat{l}",
    )(zb, zb, zb, zb, zb, zb, wal, bal, gain, s0)


LRU_CHUNK = TS // SUBLANES


TM_ODD_IN = 1024


def _odd_in_kernel(x_ref, mod_ref, nw_ref, w_ref, g_ref, u_ref, *, cfg):
    g = _group_of_tile(cfg, pl.program_id(0), TM_ODD_IN)
    shift, scale, _ = _mod_rows(mod_ref, g, 0)
    h = _modnorm(_to_scan_order(x_ref[...]), nw_ref[...], shift, scale).astype(BF16)
    z = jnp.dot(h, w_ref[...].astype(BF16), preferred_element_type=F32)
    g_ref[...] = jax.nn.gelu(z[:, 0:D_RNN]).astype(BF16)
    u_ref[...] = z[:, D_RNN:2 * D_RNN]


def _odd_in(cfg, l, x, mods, nw, w):
    n = cfg.n_tok
    return pl.pallas_call(
        functools.partial(_odd_in_kernel, cfg=cfg),
        grid=(n // TM_ODD_IN,),
        in_specs=[pl.BlockSpec((TM_ODD_IN, D_MODEL), lambda i: (i, 0)), _mod_spec(l), _const_spec((1, D_MODEL)),
                  pl.BlockSpec((None, D_MODEL, 2 * D_RNN), lambda i: (l // 2, 0, 0),
                               pipeline_mode=pl.Buffered(1))],
        out_specs=[pl.BlockSpec((TM_ODD_IN, D_RNN), lambda i: (i, 0))] * 2,
        out_shape=[jax.ShapeDtypeStruct((n, D_RNN), BF16), jax.ShapeDtypeStruct((n, D_RNN), F32)],
        compiler_params=_cparams(1),
        name=f"odd_in{l}",
    )(x, mods, nw, w)


def _block_position(cfg, ib):
    ncb = cfg.n_ctx // TS
    bpc = cfg.tc // TS
    bpl = cfg.tl // TS
    is_ctx = ib < ncb
    jl = jnp.maximum(ib - ncb, 0)
    pos = jnp.where(is_ctx, ib % bpc, jl % bpl)
    per = jnp.where(is_ctx, bpc, bpl)
    return is_ctx, jl // bpl, pos == 0, pos == per - 1


def _conv_centred(u, prev16, next8, first, last, cw_ref, cb_ref):
    sub = lax.broadcasted_iota(jnp.int32, (SUBLANES, D_RNN), 0)
    lastrow = TS - SUBLANES

    def from_prev_chunk(own, other):
        return jnp.where(sub == 0, jnp.where(first, 0.0, pltpu.roll(other, 1, 0)), pltpu.roll(own, 1, 0))

    s30 = from_prev_chunk(u[lastrow - SUBLANES:lastrow], prev16[0:SUBLANES])
    s31 = from_prev_chunk(u[lastrow:TS], prev16[SUBLANES:2 * SUBLANES])
    n0 = jnp.where(sub == SUBLANES - 1, jnp.where(last, 0.0, pltpu.roll(next8, SUBLANES - 1, 0)),
                   pltpu.roll(u[0:SUBLANES], SUBLANES - 1, 0))
    m2 = jnp.concatenate([s30, s31, u[0:lastrow - SUBLANES]], axis=0)
    m1 = jnp.concatenate([s31, u[0:lastrow]], axis=0)
    p1 = jnp.concatenate([u[SUBLANES:TS], n0], axis=0)
    return m2 * cw_ref[0:1, :] + m1 * cw_ref[1:2, :] + u * cw_ref[2:3, :] + p1 * cw_ref[3:4, :] + cb_ref[...]


def _lru_terms(uh, wg_ref, bg_ref, lam_ref):
    ucb = uh.astype(BF16)
    rs, is_ = [], []
    for nblk in range(LRU_BLOCKS):
        z = jnp.dot(ucb[:, nblk * LRU_BLOCK_W:(nblk + 1) * LRU_BLOCK_W], wg_ref[nblk],
                    preferred_element_type=F32)
        rs.append(z[:, 0:LRU_BLOCK_W])
        is_.append(z[:, LRU_BLOCK_W:2 * LRU_BLOCK_W])
    tr = jnp.tanh(jnp.concatenate(rs, axis=1) + bg_ref[0:1, :])
    ti = jnp.tanh(jnp.concatenate(is_, axis=1) + bg_ref[1:2, :])
    k2 = (-0.5 * LRU_C * LOG2_E) * jax.nn.softplus(-lam_ref[...])
    a = jnp.exp2(k2 + k2 * tr)
    w = 1.0 - a * a
    root = jnp.where(w > 0.0, w * lax.rsqrt(w), 0.0)
    v = root * ((1.0 + ti) * uh)
    return a, v


def _lru_scan(a, v, carry, reverse):
    order = range(LRU_CHUNK - 1, -1, -1) if reverse else range(LRU_CHUNK)
    h = jnp.zeros((SUBLANES, D_RNN), F32)
    p = jnp.ones((SUBLANES, D_RNN), F32)
    hs, ps = [None] * LRU_CHUNK, [None] * LRU_CHUNK
    for j in order:
        aj = a[j * SUBLANES:(j + 1) * SUBLANES]
        h = aj * h + v[j * SUBLANES:(j + 1) * SUBLANES]
        p = aj * p
        hs[j], ps[j] = h, p
    entry = [None] * SUBLANES
    for c in (range(SUBLANES - 1, -1, -1) if reverse else range(SUBLANES)):
        entry[c] = carry
        carry = h[c:c + 1] + p[c:c + 1] * carry
    hm = jnp.concatenate(entry, axis=0)
    return jnp.concatenate([hs[j] + ps[j] * hm for j in range(LRU_CHUNK)], axis=0), carry


LRU_SUB = 2
LRU_STEP = LRU_SUB * TS


def _lru_scans(terms, h0_ref, carry_ref, hs_ref, cfg, tile, direction):
    carry = carry_ref[...]
    ys = [None] * LRU_SUB
    for sb in (range(LRU_SUB - 1, -1, -1) if direction else range(LRU_SUB)):
        is_ctx, jseq, first, last = _block_position(cfg, tile * LRU_SUB + sb)
        start = last if direction else first
        carry = jnp.where(start, jnp.where(is_ctx, 0.0, h0_ref[jseq]), carry)
        ys[sb], carry = _lru_scan(*terms[sb], carry, bool(direction))
        hs_ref[sb] = carry
    carry_ref[...] = carry
    return ys


def _odd_fwd_kernel(u_ref, up_ref, un_ref, cw_ref, cb_ref, wg_ref, bg_ref, lam_ref, h0_ref,
                    yf_ref, uc_ref, hs_ref, carry_ref, *, cfg):
    tile = pl.program_id(0)
    terms = []
    for sb in range(LRU_SUB):
        _, _, first, last = _block_position(cfg, tile * LRU_SUB + sb)
        rows = slice(sb * TS, (sb + 1) * TS)
        prev16 = up_ref[...] if sb == 0 else u_ref[sb * TS - 2 * SUBLANES:sb * TS, :]
        next8 = un_ref[...] if sb == LRU_SUB - 1 else u_ref[(sb + 1) * TS:(sb + 1) * TS + SUBLANES, :]
        uc = _conv_centred(u_ref[rows, :], prev16, next8, first, last, cw_ref, cb_ref)
        uc_ref[rows, :] = uc
        terms.append(_lru_terms(uc, wg_ref, bg_ref, lam_ref))
    for sb, y in enumerate(_lru_scans(terms, h0_ref, carry_ref, hs_ref, cfg, tile, 0)):
        yf_ref[sb * TS:(sb + 1) * TS, :] = y.astype(BF16)


def _odd_bwd_kernel(g_ref, uc_ref, yf_ref, mod_ref, wg_ref, bg_ref, lam_ref, h0_ref, wout_ref,
                    o_ref, hs_ref, carry_ref, *, cfg):
    tile = pl.num_programs(0) - 1 - pl.program_id(0)
    terms = [_lru_terms(uc_ref[sb * TS:(sb + 1) * TS, :], wg_ref, bg_ref, lam_ref) for sb in range(LRU_SUB)]
    y = jnp.concatenate(_lru_scans(terms, h0_ref, carry_ref, hs_ref, cfg, tile, 1), axis=0)
    mix = (g_ref[...].astype(F32) * (yf_ref[...].astype(F32) + y)).astype(BF16)
    grp = _group_of_tile(cfg, tile, LRU_STEP)
    gate = mod_ref[2, pl.ds(grp, 1), :]
    o_ref[...] = gate * jnp.dot(mix, wout_ref[...], preferred_element_type=F32)


def _odd_scans(cfg, l, g, u, mods, cw, cb, wg, bg, lam, h0, w_out):
    n = cfg.n_tok
    assert n % LRU_STEP == 0 and cfg.n_ctx % LRU_STEP == 0 and cfg.tl % LRU_STEP == 0
    nblk = n // TS
    nstep = n // LRU_STEP
    r8 = LRU_STEP // SUBLANES
    last8 = n // SUBLANES - 1

    def specs(rev):
        it = (lambda i: nstep - 1 - i) if rev else (lambda i: i)
        cur = pl.BlockSpec((LRU_STEP, D_RNN), lambda i: (it(i), 0))
        prev = pl.BlockSpec((2 * SUBLANES, D_RNN), lambda i: (jnp.maximum(it(i) * (r8 // 2) - 1, 0), 0))
        nxt = pl.BlockSpec((SUBLANES, D_RNN), lambda i: (jnp.minimum((it(i) + 1) * r8, last8), 0))
        hs = pl.BlockSpec((LRU_SUB, 1, D_RNN), lambda i: (it(i), 0, 0))
        return cur, prev, nxt, hs

    def weights(direction):
        return [pl.BlockSpec((None, LRU_BLOCKS, LRU_BLOCK_W, 2 * LRU_BLOCK_W), lambda i: (direction, 0, 0, 0)),
                pl.BlockSpec((None, 2, D_RNN), lambda i: (direction, 0, 0)),
                pl.BlockSpec((None, 1, D_RNN), lambda i: (direction, 0, 0)),
                pl.BlockSpec((None, cfg.nl, 1, D_RNN), lambda i: (direction, 0, 0, 0))]

    cur, prev, nxt, hs = specs(False)
    tok_shape = jax.ShapeDtypeStruct((n, D_RNN), F32)
    y_f, uc, hs_f = pl.pallas_call(
        functools.partial(_odd_fwd_kernel, cfg=cfg),
        grid=(nstep,),
        in_specs=[cur, prev, nxt, _const_spec((CONV_W, D_RNN)), _const_spec((1, D_RNN))] + weights(0),
        out_specs=[cur, cur, hs],
        out_shape=[jax.ShapeDtypeStruct((n, D_RNN), BF16), tok_shape, jax.ShapeDtypeStruct((nblk, 1, D_RNN), F32)],
        scratch_shapes=[pltpu.VMEM((1, D_RNN), F32)],
        compiler_params=_cparams(1),
        name=f"lru_fwd{l}",
    )(u, u, u, cw, cb, wg, bg, lam, h0)
    cur, _, _, hs = specs(True)
    upd, hs_b = pl.pallas_call(
        functools.partial(_odd_bwd_kernel, cfg=cfg),
        grid=(nstep,),
        in_specs=[cur, cur, cur, _mod_spec(l)] + weights(1) + [_const_spec((D_RNN, D_MODEL))],
        out_specs=[cur, hs],
        out_shape=[tok_shape, jax.ShapeDtypeStruct((nblk, 1, D_RNN), F32)],
        scratch_shapes=[pltpu.VMEM((1, D_RNN), F32)],
        compiler_params=_cparams(1),
        name=f"lru_bwd{l}",
    )(g, uc, y_f, mods, wg, bg, lam, h0, w_out)
    return upd, hs_f, hs_b


def _rope_table(tl):
    pos = np.arange(tl)
    nf = HEAD_DIM // 4
    inv = ROPE_BASE ** (-np.arange(nf, dtype=np.float32) / nf)
    ar = (pos // GRID_W).astype(np.float32)[:, None] * inv
    ac = (pos % GRID_W).astype(np.float32)[:, None] * inv
    ang = np.concatenate([ar, ar, ac, ac] * 2, axis=-1)
    lo = (np.arange(LANES) % (HEAD_DIM // 2)) < nf
    cos, sin = np.cos(ang), np.sin(ang)
    tab = np.concatenate([cos, np.where(lo, -sin, 0.0), np.where(lo, 0.0, sin)], axis=-1)
    return jnp.asarray(tab, F32)


def _dup_heads(w):
    lead = w.shape[:-1]
    w = w.reshape(lead + (A_KV_HEADS, 1, HEAD_DIM))
    return jnp.broadcast_to(w, lead + (A_KV_HEADS, 2, HEAD_DIM)).reshape(lead + (2 * LANES,))


def _pad_low_rank(w):
    pad = jnp.zeros((w.shape[0], LANES - 2 * GLA_RANK), w.dtype)
    return jnp.concatenate([w[:, P_MAIN:], pad], axis=1).astype(BF16)


def _pack_alpha(w_alpha, b_alpha):
    w = jnp.zeros((LANES, 2 * B_QK), F32)
    w = w.at[0:GLA_RANK, 0:B_QK].set(w_alpha[0]).at[GLA_RANK:2 * GLA_RANK, B_QK:].set(w_alpha[1])
    return w.astype(BF16), b_alpha.reshape(1, 2 * B_QK)


def _forward(cfg, x_prompt, x_sample, c, cache_k, cache_v, state_gla, state_lru, c_ctx,
             w_ada, b_ada, norm_mix, norm_ffn, w_in_even, attn_sink, w_alpha, b_alpha, gla_gain,
             w_out_even, w_in_odd, conv_w, conv_b, w_gate_a, b_gate_a, w_gate_x, b_gate_x,
             lru_lambda, w_out_odd, w_ffn_in, w_ffn_out, norm_final):
    assert cfg.tc == TS and cfg.tl % TM == 0 and cfg.n_ctx % TM == 0 and cfg.nl + 1 <= SUBLANES
    x = (x_prompt.reshape(cfg.n_ctx, D_MODEL), x_sample.reshape(cfg.n_lat, D_MODEL))
    cvec = jnp.concatenate([c_ctx[None, :], c, jnp.zeros((SUBLANES - 1 - cfg.nl, D_MODEL), F32)], axis=0)
    mods = _ada(cvec, w_ada, b_ada)
    rope_tab = _rope_table(cfg.tl)
    nf = norm_final.reshape(1, D_MODEL)
    w_out_even_bf = w_out_even.astype(BF16)
    new_k, new_v, new_gla, new_lru = [], [], [], []
    for l in range(DEPTH):
        nw = norm_mix[l].reshape(1, D_MODEL)
        if l % 2 == 0:
            e = l // 2
            za, zb, k_ctx, v_ctx = _even_in(cfg, l, x, mods, nw, w_in_even, _pad_low_rank(w_in_even[e]), rope_tab)
            new_k.append(k_ctx.reshape(cfg.nc, cfg.tc, A_KV_HEADS, HEAD_DIM))
            new_v.append(v_ctx.reshape(cfg.nc, cfg.tc, A_KV_HEADS, HEAD_DIM))
            ck2 = _dup_heads(cache_k[:, e].reshape(cfg.nl, PAST_LEN, A_KV_HEADS * HEAD_DIM)).astype(BF16)
            cv2 = _dup_heads(cache_v[:, e].reshape(cfg.nl, PAST_LEN, A_KV_HEADS * HEAD_DIM)).astype(BF16)
            oa_ctx = _attn_ctx(cfg, l, za, attn_sink[e])
            oa_lat = _attn_lat(cfg, l, za, ck2, cv2, attn_sink[e])
            wal, bal = _pack_alpha(w_alpha[e], b_alpha[e])
            gain = gla_gain[e].reshape(1, B_WIDTH)
            ob_ctx, s_new = _gla_ctx(cfg, l, zb, wal, bal, gain)
            ob_lat = _gla_lat(cfg, l, zb, wal, bal, gain, state_gla[:, e])
            new_gla.append(s_new)
            pre_args = (oa_ctx, oa_lat, ob_ctx, ob_lat, w_out_even_bf)
        else:
            o = l // 2
            g_act, u = _odd_in(cfg, l, x, mods, nw, w_in_odd)
            wg = jnp.concatenate([w_gate_a[o], w_gate_x[o]], axis=-1).astype(BF16)
            bg = 0.5 * jnp.stack([b_gate_a[o], b_gate_x[o]], axis=1)
            lam = lru_lambda[o].reshape(2, 1, D_RNN)
            h0 = jnp.transpose(state_lru[:, o], (1, 0, 2)).reshape(2, cfg.nl, 1, D_RNN)
            upd, hs_f, hs_b = _odd_scans(cfg, l, g_act, u, mods, 0.5 * conv_w[o], 0.5 * conv_b[o].reshape(1, D_RNN),
                                         wg, bg, lam, h0, w_out_odd[o].astype(BF16))
            new_lru.append(jnp.stack([hs_f[:cfg.nc, 0], hs_b[:cfg.nc, 0]], axis=1))
            pre_args = (upd,)
        x = _ffn(cfg, l, x, pre_args, mods, norm_ffn[l].reshape(1, D_MODEL), w_ffn_in, w_ffn_out, nf,
                 final=(l == DEPTH - 1))
    y_prompt = x[0].reshape(cfg.nc, cfg.tc, D_MODEL)
    y_sample = x[1].reshape(cfg.nl, cfg.tl, D_MODEL)
    return (y_prompt, y_sample, jnp.stack(new_k, axis=1), jnp.stack(new_v, axis=1),
            jnp.stack(new_gla, axis=1), jnp.stack(new_lru, axis=1))


def kernel(x_prompt, x_sample, c, cache_k, cache_v, state_gla, state_lru, c_ctx, w_ada, b_ada, norm_mix, norm_ffn, w_in_even, attn_sink, w_alpha, b_alpha, gla_gain, w_out_even, w_in_odd, conv_w, conv_b, w_gate_a, b_gate_a, w_gate_x, b_gate_x, lru_lambda, w_out_odd, w_ffn_in, w_ffn_out, norm_final):
    cfg = Cfg(nc=x_prompt.shape[0], tc=x_prompt.shape[1], nl=x_sample.shape[0], tl=x_sample.shape[1])
    return _forward(cfg, x_prompt, x_sample, c, cache_k, cache_v, state_gla, state_lru, c_ctx,
                    w_ada, b_ada, norm_mix, norm_ffn, w_in_even, attn_sink, w_alpha, b_alpha, gla_gain,
                    w_out_even, w_in_odd, conv_w, conv_b, w_gate_a, b_gate_a, w_gate_x, b_gate_x,
                    lru_lambda, w_out_odd, w_ffn_in, w_ffn_out, norm_final)
```

```python
import functools
from typing import NamedTuple

import jax
import jax.numpy as jnp
import numpy as np
from jax import lax
from jax.experimental import pallas as pl
from jax.experimental.pallas import tpu as pltpu

F32 = jnp.float32
BF16 = jnp.bfloat16

D_MODEL = 1024
DEPTH = 4
EPS = 1e-6

HEAD_DIM = 64
A_Q_HEADS = 8
A_KV_HEADS = 2
A_GROUPS = A_Q_HEADS // A_KV_HEADS
A_WIDTH = A_Q_HEADS * HEAD_DIM
ATT_BLOCK = 128
ROPE_BASE = 10000.0
GRID_W = 64
NEG_INF = -1e30
PAST_LEN = 256

B_HEADS = 4
B_DK = 64
B_DV = 128
B_QK = B_HEADS * B_DK
B_WIDTH = B_HEADS * B_DV
GLA_RANK = 16
GLA_NORMALIZER = 16.0
GLA_CHUNK = 64

D_RNN = D_MODEL
LRU_BLOCK_W = 256
LRU_BLOCKS = D_RNN // LRU_BLOCK_W
CONV_W = 4
LRU_C = 8.0
LOG2_E = 1.4426950408889634

D_FF = 2816

LANES = 128
SUBLANES = 8

ZA_W = A_WIDTH + 2 * 2 * LANES
ZB_LR = B_QK + B_QK + B_WIDTH + B_WIDTH
ZB_W = ZB_LR + LANES
P_A = A_WIDTH + 2 * LANES
P_MAIN = P_A + ZB_LR

TM = 512
TS = 256
FF_CHUNK = 256

VMEM_LIMIT = 56 * 1024 * 1024


class Cfg(NamedTuple):
    nc: int
    tc: int
    nl: int
    tl: int

    @property
    def n_ctx(self):
        return self.nc * self.tc

    @property
    def n_lat(self):
        return self.nl * self.tl

    @property
    def n_tok(self):
        return self.n_ctx + self.n_lat


def _cparams(n_axes):
    return pltpu.CompilerParams(dimension_semantics=("arbitrary",) * n_axes,
                                vmem_limit_bytes=VMEM_LIMIT)


def _const_spec(shape):
    nd = len(shape)
    return pl.BlockSpec(shape, lambda *_: (0,) * nd)


def _silu(x):
    return x * jax.nn.sigmoid(x)


def _group_of_tile(cfg, i, tile):
    nct = cfg.n_ctx // tile
    per_lat = cfg.tl // tile
    return jnp.where(i < nct, 0, 1 + jnp.maximum(i - nct, 0) // per_lat)


def _modnorm(x, nw, shift, scale):
    ms = jnp.mean(x * x, axis=-1, keepdims=True)
    y = x * lax.rsqrt(ms + EPS) * nw
    return y * (1.0 + scale) + shift


def _mod_rows(mod_ref, g, first):
    return tuple(mod_ref[first + j, pl.ds(g, 1), :] for j in range(3))


def _ada_kernel(c_ref, w_ref, b_ref, o_ref):
    s = _silu(c_ref[...]).astype(BF16)
    o_ref[...] = jnp.dot(s, w_ref[...].astype(BF16), preferred_element_type=F32) + b_ref[...]


def _ada(cvec, w_ada, b_ada):
    b4 = b_ada.reshape(DEPTH, 6, 1, D_MODEL)
    return pl.pallas_call(
        _ada_kernel,
        grid=(DEPTH, 6),
        in_specs=[
            _const_spec((SUBLANES, D_MODEL)),
            pl.BlockSpec((None, D_MODEL, D_MODEL), lambda l, j: (l, 0, j)),
            pl.BlockSpec((None, None, 1, D_MODEL), lambda l, j: (l, j, 0, 0)),
        ],
        out_specs=pl.BlockSpec((None, None, SUBLANES, D_MODEL), lambda l, j: (l, j, 0, 0)),
        out_shape=jax.ShapeDtypeStruct((DEPTH, 6, SUBLANES, D_MODEL), F32),
        compiler_params=_cparams(2),
        name="ada",
    )(cvec, w_ada, b4)


def _mod_spec(l):
    return pl.BlockSpec((None, 6, SUBLANES, D_MODEL), lambda *_: (l, 0, 0, 0))


def _ffn_tile(x, nw, shift, scale, gate, win_ref, wout_ref):
    h = _modnorm(x, nw, shift, scale).astype(BF16)
    acc = None
    for c in range(D_FF // FF_CHUNK):
        lo, hi = c * FF_CHUNK, (c + 1) * FF_CHUNK
        g = jnp.dot(h, win_ref[:, lo:hi].astype(BF16), preferred_element_type=F32)
        u = jnp.dot(h, win_ref[:, D_FF + lo:D_FF + hi].astype(BF16), preferred_element_type=F32)
        a = (_silu(g) * u).astype(BF16)
        part = jnp.dot(a, wout_ref[lo:hi, :].astype(BF16), preferred_element_type=F32)
        acc = part if acc is None else acc + part
    return x + gate * acc


def _x_args(cfg, x, tile=TM):
    if not isinstance(x, tuple):
        return [x], [pl.BlockSpec((tile, D_MODEL), lambda i: (i, 0))]
    nct = cfg.n_ctx // tile
    return list(x), [pl.BlockSpec((tile, D_MODEL), lambda i: (jnp.minimum(i, nct - 1), 0)),
                     pl.BlockSpec((tile, D_MODEL), lambda i: (jnp.maximum(i - nct, 0), 0))]


def _load_x(cfg, x_refs, i, tile=TM):
    if len(x_refs) == 1:
        return x_refs[0][...]
    return jnp.where(i < cfg.n_ctx // tile, x_refs[0][...], x_refs[1][...])


def _to_scan_order(x):
    nb, d = x.shape[0] // TS, x.shape[1]
    return jnp.swapaxes(x.reshape(nb, SUBLANES, TS // SUBLANES, d), 1, 2).reshape(nb * TS, d)


def _from_scan_order(x):
    nb, d = x.shape[0] // TS, x.shape[1]
    return jnp.swapaxes(x.reshape(nb, TS // SUBLANES, SUBLANES, d), 1, 2).reshape(nb * TS, d)


def _ffn_kernel(*refs, cfg, n_x, pre, final):
    x_refs, refs = refs[:n_x], refs[n_x:]
    i = pl.program_id(0)
    g = _group_of_tile(cfg, i, TM)
    x = _load_x(cfg, x_refs, i)
    if pre == "even":
        (oac_ref, oal_ref, obc_ref, obl_ref, wmix_ref), refs = refs[:5], refs[5:]
    else:
        d_ref, refs = refs[0], refs[1:]
    mod_ref, nw_ref, win_ref, wout_ref, nf_ref = refs[:5]
    outs = refs[5:]
    if pre == "even":
        is_ctx_tile = i < cfg.n_ctx // TM
        o = jnp.concatenate([jnp.where(is_ctx_tile, oac_ref[...], oal_ref[...]),
                             jnp.where(is_ctx_tile, obc_ref[...], obl_ref[...])], axis=1)
        x = x + mod_ref[2, pl.ds(g, 1), :] * jnp.dot(o, wmix_ref[...], preferred_element_type=F32)
    else:
        x = x + _from_scan_order(d_ref[...])
    shift, scale, gate = _mod_rows(mod_ref, g, 3)
    y = _ffn_tile(x, nw_ref[...], shift, scale, gate, win_ref, wout_ref)
    if not final:
        outs[0][...] = y
        return
    ms = jnp.mean(y * y, axis=-1, keepdims=True)
    y = y * lax.rsqrt(ms + EPS) * nf_ref[...]
    is_ctx = i < cfg.n_ctx // TM

    @pl.when(is_ctx)
    def _():
        outs[0][...] = y

    @pl.when(jnp.logical_not(is_ctx))
    def _():
        outs[1][...] = y


def _ffn(cfg, l, x, pre_args, mods, nw, win, wout, nf, final):
    n = cfg.n_tok
    nct = cfg.n_ctx // TM
    tok = lambda w: pl.BlockSpec((TM, w), lambda i: (i, 0))
    ctx = lambda w: pl.BlockSpec((TM, w), lambda i: (jnp.minimum(i, nct - 1), 0))
    lat = lambda w: pl.BlockSpec((TM, w), lambda i: (jnp.maximum(i - nct, 0), 0))
    x_ops, x_specs = _x_args(cfg, x)
    if l % 2 == 0:
        pre = "even"
        pre_specs = [ctx(A_WIDTH), lat(A_WIDTH), ctx(B_WIDTH), lat(B_WIDTH),
                     pl.BlockSpec((None, A_WIDTH + B_WIDTH, D_MODEL), lambda i: (l // 2, 0, 0))]
    else:
        pre, pre_specs = "odd", [tok(D_MODEL)]
    if final:
        out_specs = [pl.BlockSpec((TM, D_MODEL), lambda i: (jnp.minimum(i, nct - 1), 0)),
                     pl.BlockSpec((TM, D_MODEL), lambda i: (jnp.maximum(i - nct, 0), 0))]
        out_shape = [jax.ShapeDtypeStruct((cfg.n_ctx, D_MODEL), F32),
                     jax.ShapeDtypeStruct((cfg.n_lat, D_MODEL), F32)]
    else:
        out_specs, out_shape = tok(D_MODEL), jax.ShapeDtypeStruct((n, D_MODEL), F32)
    return pl.pallas_call(
        functools.partial(_ffn_kernel, cfg=cfg, n_x=len(x_ops), pre=pre, final=final),
        grid=(n // TM,),
        in_specs=x_specs + pre_specs + [
            _mod_spec(l), _const_spec((1, D_MODEL)),
            pl.BlockSpec((None, D_MODEL, 2 * D_FF), lambda i: (l, 0, 0), pipeline_mode=pl.Buffered(1)),
            pl.BlockSpec((None, D_FF, D_MODEL), lambda i: (l, 0, 0), pipeline_mode=pl.Buffered(1)),
            _const_spec((1, D_MODEL))],
        out_specs=out_specs,
        out_shape=out_shape,
        compiler_params=_cparams(1),
        name=f"ffn{l}",
    )(*x_ops, *pre_args, mods, nw, win, wout, nf)


TM_EVEN_IN = 1024


def _even_in_kernel(*refs, cfg, n_x):
    x_refs, (mod_ref, nw_ref, w_ref, wlr_ref, tab_ref, za_ref, zb_ref, k_ref, v_ref) = refs[:n_x], refs[n_x:]
    i = pl.program_id(0)
    g = _group_of_tile(cfg, i, TM_EVEN_IN)
    shift, scale, _ = _mod_rows(mod_ref, g, 0)
    h = _modnorm(_load_x(cfg, x_refs, i, TM_EVEN_IN), nw_ref[...], shift, scale).astype(BF16)
    def proj(lo_col, hi_col):
        return jnp.dot(h, w_ref[:, lo_col:hi_col].astype(BF16), preferred_element_type=F32)

    is_lat = g != 0
    cos = jnp.where(is_lat, tab_ref[:, 0:LANES], 1.0)
    sin_a = jnp.where(is_lat, tab_ref[:, LANES:2 * LANES], 0.0)
    sin_b = jnp.where(is_lat, tab_ref[:, 2 * LANES:3 * LANES], 0.0)

    def rope(a):
        parts = []
        for j in range(a.shape[1] // LANES):
            t = a[:, j * LANES:(j + 1) * LANES]
            parts.append(t * cos + pltpu.roll(t, LANES - 16, 1) * sin_a + pltpu.roll(t, 16, 1) * sin_b)
        return jnp.concatenate(parts, axis=1)

    lo = lax.broadcasted_iota(jnp.int32, (TM_EVEN_IN, LANES), 1) < HEAD_DIM

    def twice(a):
        swapped = pltpu.roll(a, HEAD_DIM, 1)
        return jnp.concatenate([jnp.where(lo, a, swapped), jnp.where(lo, swapped, a)], axis=1).astype(BF16)

    q = proj(0, A_WIDTH) * (LOG2_E * HEAD_DIM ** -0.5)
    za_ref[:, 0:A_WIDTH] = rope(q).astype(BF16)
    kv = proj(A_WIDTH, P_A)
    k, v = kv[:, 0:LANES], kv[:, LANES:2 * LANES]
    za_ref[:, A_WIDTH:A_WIDTH + 2 * LANES] = twice(rope(k))
    za_ref[:, A_WIDTH + 2 * LANES:ZA_W] = twice(v)
    zb_ref[:, 0:B_QK] = proj(P_A, P_A + B_QK) * (B_DK ** -0.5)
    for c0 in range(B_QK, ZB_LR, 2 * LANES):
        zb_ref[:, c0:c0 + 2 * LANES] = proj(P_A + c0, P_A + c0 + 2 * LANES)
    zb_ref[:, ZB_LR:ZB_W] = jnp.dot(h, wlr_ref[...], preferred_element_type=F32)

    @pl.when(g == 0)
    def _():
        k_ref[...] = k
        v_ref[...] = v


def _even_in(cfg, l, x, mods, nw, w_all, w_lr, rope_tab):
    n = cfg.n_tok
    tile = TM_EVEN_IN
    assert cfg.n_ctx % tile == 0 and cfg.tl % tile == 0
    nct = cfg.n_ctx // tile
    per_lat = cfg.tl // tile
    tok = lambda w: pl.BlockSpec((tile, w), lambda i: (i, 0))
    x_ops, x_specs = _x_args(cfg, x, tile)
    ctx_kv = pl.BlockSpec((tile, LANES), lambda i: (jnp.minimum(i, nct - 1), 0))
    return pl.pallas_call(
        functools.partial(_even_in_kernel, cfg=cfg, n_x=len(x_ops)),
        grid=(n // tile,),
        in_specs=x_specs + [_mod_spec(l), _const_spec((1, D_MODEL)),
                            pl.BlockSpec((None, D_MODEL, P_MAIN), lambda i: (l // 2, 0, 0),
                                         pipeline_mode=pl.Buffered(1)),
                            _const_spec((D_MODEL, LANES)),
                            pl.BlockSpec((tile, 3 * LANES), lambda i: (jnp.maximum(i - nct, 0) % per_lat, 0))],
        out_specs=[tok(ZA_W), tok(ZB_W), ctx_kv, ctx_kv],
        out_shape=[jax.ShapeDtypeStruct((n, ZA_W), BF16),
                   jax.ShapeDtypeStruct((n, ZB_W), F32),
                   jax.ShapeDtypeStruct((cfg.n_ctx, LANES), F32),
                   jax.ShapeDtypeStruct((cfg.n_ctx, LANES), F32)],
        compiler_params=_cparams(1),
        name=f"even_in{l}",
    )(*x_ops, mods, nw, w_all, w_lr, rope_tab)


def _sink_attention(units, sink_ref):
    scores = []
    for q, k_all, _, mask, kvh in units:
        tq = q.shape[0]
        lo = lax.broadcasted_iota(jnp.int32, (tq, LANES), 1) < HEAD_DIM
        qs = []
        for gq in range(A_GROUPS):
            h = kvh * A_GROUPS + gq
            pair = q[:, (h // 2) * LANES:(h // 2 + 1) * LANES]
            keep = lo if h % 2 == 0 else jnp.logical_not(lo)
            qs.append(jnp.where(keep, pair, jnp.zeros_like(pair)))
        qst = jnp.concatenate(qs, axis=0)
        s = lax.dot_general(qst, k_all, (((1,), (1,)), ((), ())), preferred_element_type=F32)
        if mask is not None:
            nfree = s.shape[1] - mask.shape[1]
            s = jnp.concatenate([s[:, :nfree], jnp.where(mask, s[:, nfree:], NEG_INF)], axis=1)
        scores.append(s)
    probs = []
    for (q, _, _, _, kvh), s in zip(units, scores):
        tq = q.shape[0]
        rows = lax.broadcasted_iota(jnp.int32, (A_GROUPS * tq, 1), 0) // tq
        sink = jnp.zeros((A_GROUPS * tq, 1), F32)
        for gq in range(A_GROUPS):
            sink = jnp.where(rows == gq, sink_ref[kvh * A_GROUPS + gq] * LOG2_E, sink)
        m = jnp.maximum(jnp.max(s, axis=-1, keepdims=True), sink)
        p = jnp.exp2(s - m)
        den = jnp.sum(p, axis=-1, keepdims=True) + jnp.exp2(sink - m)
        probs.append((p.astype(BF16), den))
    outs = []
    for (q, _, v_all, _, _), (p, den) in zip(units, probs):
        tq = q.shape[0]
        lo = lax.broadcasted_iota(jnp.int32, (tq, LANES), 1) < HEAD_DIM
        o = jnp.dot(p, v_all, preferred_element_type=F32) / den
        pairs = [jnp.where(lo, o[(2 * pr) * tq:(2 * pr + 1) * tq], o[(2 * pr + 1) * tq:(2 * pr + 2) * tq])
                 for pr in range(2)]
        outs.append(jnp.concatenate(pairs, axis=1).astype(BF16))
    return outs


def _attn_ctx_kernel(sink_ref, q_ref, k_ref, v_ref, o_ref):
    q = q_ref[...]
    units = [(q, k_ref[:, kvh * LANES:(kvh + 1) * LANES], v_ref[:, kvh * LANES:(kvh + 1) * LANES], None, kvh)
             for kvh in range(A_KV_HEADS)]
    for kvh, o in enumerate(_sink_attention(units, sink_ref)):
        o_ref[:, 2 * kvh * LANES:2 * (kvh + 1) * LANES] = o


def _attn_ctx(cfg, l, za, sink):
    tc = cfg.tc
    return pl.pallas_call(
        _attn_ctx_kernel,
        grid=(cfg.nc,),
        in_specs=[pl.BlockSpec(memory_space=pltpu.SMEM),
                  pl.BlockSpec((tc, A_WIDTH), lambda b: (b, 0)),
                  pl.BlockSpec((tc, 2 * LANES), lambda b: (b, A_WIDTH // (2 * LANES))),
                  pl.BlockSpec((tc, 2 * LANES), lambda b: (b, A_WIDTH // (2 * LANES) + 1))],
        out_specs=pl.BlockSpec((tc, A_WIDTH), lambda b: (b, 0)),
        out_shape=jax.ShapeDtypeStruct((cfg.n_ctx, A_WIDTH), BF16),
        compiler_params=_cparams(1),
        name=f"attn_ctx{l}",
    )(sink, za, za, za)


ATT_STEP = 4 * ATT_BLOCK
ATT_WIN = 3 * ATT_BLOCK


def _attn_lat_kernel(sink_ref, q_ref, k_ref, v_ref, ck_ref, cv_ref, o_ref):
    n = pl.program_id(1)
    tl = k_ref.shape[0]
    shape = (A_GROUPS * ATT_BLOCK, ATT_WIN)
    rc = (lax.broadcasted_iota(jnp.int32, shape, 0) % ATT_BLOCK - lax.broadcasted_iota(jnp.int32, shape, 1)
          + ATT_BLOCK)
    units = []
    for j in range(ATT_STEP // ATT_BLOCK):
        q0 = (n * (ATT_STEP // ATT_BLOCK) + j) * ATT_BLOCK
        start = pl.multiple_of(jnp.clip(q0 - ATT_BLOCK, 0, tl - ATT_WIN), ATT_BLOCK)
        mask = lax.bitcast_convert_type(rc + (q0 - start), jnp.uint32) <= jnp.uint32(2 * ATT_BLOCK)
        q = q_ref[j * ATT_BLOCK:(j + 1) * ATT_BLOCK, :]
        for kvh in range(A_KV_HEADS):
            sl = slice(kvh * LANES, (kvh + 1) * LANES)
            k_all = jnp.concatenate([ck_ref[:, sl], k_ref[pl.ds(start, ATT_WIN), sl]], axis=0)
            v_all = jnp.concatenate([cv_ref[:, sl], v_ref[pl.ds(start, ATT_WIN), sl]], axis=0)
            units.append((q, k_all, v_all, mask, kvh))
    for u, o in enumerate(_sink_attention(units, sink_ref)):
        j, kvh = divmod(u, A_KV_HEADS)
        o_ref[j * ATT_BLOCK:(j + 1) * ATT_BLOCK, 2 * kvh * LANES:2 * (kvh + 1) * LANES] = o


def _attn_lat(cfg, l, za, cache_k2, cache_v2, sink):
    assert cfg.n_ctx % cfg.tl == 0 and cfg.tl % ATT_STEP == 0 and cfg.tl >= ATT_WIN
    steps = cfg.tl // ATT_STEP
    base = cfg.n_ctx // ATT_STEP
    kcol = A_WIDTH // (2 * LANES)
    seq = lambda col: pl.BlockSpec((cfg.tl, 2 * LANES), lambda b, n: (cfg.n_ctx // cfg.tl + b, col))
    cache = pl.BlockSpec((None, PAST_LEN, 2 * LANES), lambda b, n: (b, 0, 0))
    return pl.pallas_call(
        _attn_lat_kernel,
        grid=(cfg.nl, steps),
        in_specs=[pl.BlockSpec(memory_space=pltpu.SMEM),
                  pl.BlockSpec((ATT_STEP, A_WIDTH), lambda b, n: (base + b * steps + n, 0)),
                  seq(kcol), seq(kcol + 1), cache, cache],
        out_specs=pl.BlockSpec((ATT_STEP, A_WIDTH), lambda b, n: (b * steps + n, 0)),
        out_shape=jax.ShapeDtypeStruct((cfg.n_lat, A_WIDTH), BF16),
        compiler_params=_cparams(2),
        name=f"attn_lat{l}",
    )(sink, za, za, za, cache_k2, cache_v2)


def _gla_chunk_prepare(q, k, v, la, reverse):
    cn = GLA_CHUNK
    row = lax.broadcasted_iota(jnp.int32, (cn, B_QK), 0)
    b = la
    s = 1
    while s < cn:
        if reverse:
            b = b + jnp.where(row < cn - s, pltpu.roll(b, cn - s, 0), 0.0)
        else:
            b = b + jnp.where(row >= s, pltpu.roll(b, s, 0), 0.0)
        s *= 2
    b_last = b[0:1, :] if reverse else b[cn - 1:cn, :]
    q_e = q * jnp.exp(b)
    k_e = (k * jnp.exp(-b)).astype(BF16)
    k_s = (k * jnp.exp(b_last - b)).astype(BF16)
    vb = v.astype(BF16)
    row_t = row[:, 0:LANES]
    lane_t = lax.broadcasted_iota(jnp.int32, (cn, LANES), 1)
    lo = lane_t < B_DK
    col = lane_t % cn
    tri = (col >= row_t) if reverse else (col <= row_t)
    q_tiles, att_tiles = [], []
    for t in range(B_HEADS // 2):
        sl = slice(t * LANES, (t + 1) * LANES)
        k_t = k_e[:, sl]
        zero = jnp.zeros_like(k_t)
        kbd = jnp.concatenate([jnp.where(lo, k_t, zero), jnp.where(lo, zero, k_t)], axis=0)
        q_t = q_e[:, sl]
        att = lax.dot_general(q_t.astype(BF16), kbd, (((1,), (1,)), ((), ())), preferred_element_type=F32)
        q_tiles.append(q_t)
        att_tiles.append(jnp.where(tri, att, 0.0))
    upd = lax.dot_general(k_s, vb, (((0,), (0,)), ((), ())), preferred_element_type=F32)
    own = jnp.concatenate([upd[h * B_DK:(h + 1) * B_DK, h * B_DV:(h + 1) * B_DV] for h in range(B_HEADS)], axis=0)
    decay = jnp.broadcast_to(jnp.exp(b_last), (LANES, B_QK)).T
    return q_tiles, att_tiles, vb, own, decay


def _gla_chunk_output(q_tiles, att_tiles, vb, st):
    lo = lax.broadcasted_iota(jnp.int32, (GLA_CHUNK, LANES), 1) < B_DK
    st_b = st.astype(BF16)
    outs = []
    for h in range(B_HEADS):
        att, q_t = att_tiles[h // 2], q_tiles[h // 2]
        x = (jnp.where(lo, att, pltpu.roll(q_t, B_DK, 1)) if h % 2 == 0
             else jnp.where(lo, pltpu.roll(att, B_DK, 1), q_t))
        rhs = jnp.concatenate([vb[:, h * B_DV:(h + 1) * B_DV], st_b[h * B_DK:(h + 1) * B_DK, :]], axis=0)
        outs.append(jnp.dot(x.astype(BF16), rhs, preferred_element_type=F32))
    return jnp.concatenate(outs, axis=1)


def _gla_log_decay(lr_pad, wal_ref, bal_ref, direction):
    z = jnp.dot(lr_pad.astype(BF16), wal_ref[:, direction * B_QK:(direction + 1) * B_QK],
                preferred_element_type=F32) + bal_ref[:, direction * B_QK:(direction + 1) * B_QK]
    return jax.nn.log_sigmoid(z) / GLA_NORMALIZER


def _gla_blocks(jobs, wal_ref, bal_ref):
    prepared = []
    for qkv_ref, lr_ref, _, direction in jobs:
        nchunk = qkv_ref.shape[0] // GLA_CHUNK
        la_all = _gla_log_decay(lr_ref[...], wal_ref, bal_ref, direction)
        chunks = {}
        for ci in range(nchunk):
            rs = slice(ci * GLA_CHUNK, (ci + 1) * GLA_CHUNK)
            chunks[ci] = _gla_chunk_prepare(qkv_ref[rs, 0:B_QK], qkv_ref[rs, B_QK:2 * B_QK],
                                            qkv_ref[rs, 2 * B_QK:2 * B_QK + B_WIDTH], la_all[rs, :],
                                            bool(direction))
        prepared.append(chunks)
    results = []
    for (qkv_ref, _, st, direction), chunks in zip(jobs, prepared):
        nchunk = len(chunks)
        states = {}
        for ci in (range(nchunk - 1, -1, -1) if direction else range(nchunk)):
            states[ci] = st
            _, _, _, own, decay = chunks[ci]
            st = st * decay + own
        outs = [_gla_chunk_output(*chunks[ci][:3], states[ci]) for ci in range(nchunk)]
        results.append((jnp.concatenate(outs, axis=0), st))
    return results


def _stack_state(s_ref):
    return s_ref[...].reshape(B_QK, B_DV)


def _unstack_state(st, s_ref):
    s_ref[...] = st.reshape(B_HEADS, B_DK, B_DV)


def _gla_finish(o, r, gain):
    parts = []
    for h in range(B_HEADS):
        oh = o[:, h * B_DV:(h + 1) * B_DV]
        parts.append(oh * lax.rsqrt(jnp.mean(oh * oh, axis=-1, keepdims=True) + EPS))
    return (jnp.concatenate(parts, axis=1) * gain * _silu(r)).astype(BF16)


def _gla_ctx_kernel(qkv_ref, r_ref, lr_ref, wal_ref, bal_ref, gain_ref, ob_ref, snew_ref):
    zero = jnp.zeros((B_QK, B_DV), F32)
    (o_f, st_f), (o_b, st_b) = _gla_blocks([(qkv_ref, lr_ref, zero, 0), (qkv_ref, lr_ref, zero, 1)],
                                           wal_ref, bal_ref)
    _unstack_state(st_f, snew_ref.at[0])
    _unstack_state(st_b, snew_ref.at[1])
    ob_ref[...] = _gla_finish(o_f + o_b, r_ref[...], gain_ref[...])


def _gla_ctx(cfg, l, zb, wal, bal, gain):
    tc = cfg.tc
    return pl.pallas_call(
        _gla_ctx_kernel,
        grid=(cfg.nc,),
        in_specs=[pl.BlockSpec((tc, 2 * B_QK + B_WIDTH), lambda b: (b, 0)),
                  pl.BlockSpec((tc, B_WIDTH), lambda b: (b, (2 * B_QK + B_WIDTH) // B_WIDTH)),
                  pl.BlockSpec((tc, LANES), lambda b: (b, ZB_LR // LANES)),
                  _const_spec((LANES, 2 * B_QK)), _const_spec((1, 2 * B_QK)), _const_spec((1, B_WIDTH))],
        out_specs=[pl.BlockSpec((tc, B_WIDTH), lambda b: (b, 0)),
                   pl.BlockSpec((None, 2, B_HEADS, B_DK, B_DV), lambda b: (b, 0, 0, 0, 0))],
        out_shape=[jax.ShapeDtypeStruct((cfg.n_ctx, B_WIDTH), BF16),
                   jax.ShapeDtypeStruct((cfg.nc, 2, B_HEADS, B_DK, B_DV), F32)],
        compiler_params=_cparams(1),
        name=f"gla_ctx{l}",
    )(zb, zb, zb, wal, bal, gain)


GLA_STEP = 512


def _gla_lat_kernel(qkvf_ref, lrf_ref, rf_ref, qkvb_ref, lrb_ref, rb_ref, wal_ref, bal_ref, gain_ref,
                    s0_ref, ob_ref, stf_ref, stb_ref, of_ref, obk_ref):
    n = pl.program_id(1)
    nb = pl.num_programs(1)

    @pl.when(n == 0)
    def _():
        stf_ref[...] = _stack_state(s0_ref.at[0])
        stb_ref[...] = _stack_state(s0_ref.at[1])
    (o_f, st_f), (o_b, st_b) = _gla_blocks([(qkvf_ref, lrf_ref, stf_ref[...], 0),
                                            (qkvb_ref, lrb_ref, stb_ref[...], 1)], wal_ref, bal_ref)
    stf_ref[...] = st_f
    stb_ref[...] = st_b
    rows_f = pl.ds(pl.multiple_of(n * GLA_STEP, GLA_STEP), GLA_STEP)
    rows_b = pl.ds(pl.multiple_of((nb - 1 - n) * GLA_STEP, GLA_STEP), GLA_STEP)
    of_ref[rows_f, :] = o_f
    obk_ref[rows_b, :] = o_b

    @pl.when(2 * n >= nb)
    def _():
        ob_ref[rows_f, :] = _gla_finish(o_f + obk_ref[rows_f, :], rf_ref[...], gain_ref[...])
        ob_ref[rows_b, :] = _gla_finish(of_ref[rows_b, :] + o_b, rb_ref[...], gain_ref[...])


def _gla_lat(cfg, l, zb, wal, bal, gain, s0):
    nbl = cfg.tl // GLA_STEP
    base = cfg.n_ctx // GLA_STEP
    assert nbl % 2 == 0 and cfg.tl % GLA_STEP == 0 and cfg.n_ctx % GLA_STEP == 0

    def blk(width, col, rev):
        def imap(b, n):
            return (base + b * nbl + (nbl - 1 - n if rev else n), col)
        return pl.BlockSpec((GLA_STEP, width), imap)

    qkv_w = 2 * B_QK + B_WIDTH
    per_dir = lambda rev: [blk(qkv_w, 0, rev), blk(LANES, ZB_LR // LANES, rev), blk(B_WIDTH, qkv_w // B_WIDTH, rev)]
    return pl.pallas_call(
        _gla_lat_kernel,
        grid=(cfg.nl, nbl),
        in_specs=per_dir(False) + per_dir(True) + [
            _const_spec((LANES, 2 * B_QK)), _const_spec((1, 2 * B_QK)), _const_spec((1, B_WIDTH)),
            pl.BlockSpec((None, 2, B_HEADS, B_DK, B_DV), lambda b, n: (b, 0, 0, 0, 0))],
        out_specs=pl.BlockSpec((cfg.tl, B_WIDTH), lambda b, n: (b, 0)),
        out_shape=jax.ShapeDtypeStruct((cfg.n_lat, B_WIDTH), BF16),
        scratch_shapes=[pltpu.VMEM((B_QK, B_DV), F32), pltpu.VMEM((B_QK, B_DV), F32),
                        pltpu.VMEM((cfg.tl, B_WIDTH), F32), pltpu.VMEM((cfg.tl, B_WIDTH), F32)],
        compiler_params=_cparams(2),
        name=f"gla_lat{l}",
    )(zb, zb, zb, zb, zb, zb, wal, bal, gain, s0)


LRU_CHUNK = TS // SUBLANES


TM_ODD_IN = 1024
ODD_IN_COLS = 256


def _odd_in_kernel(x_ref, mod_ref, nw_ref, w_ref, g_ref, u_ref, *, cfg):
    g = _group_of_tile(cfg, pl.program_id(0), TM_ODD_IN)
    shift, scale, _ = _mod_rows(mod_ref, g, 0)
    h = _modnorm(_to_scan_order(x_ref[...]), nw_ref[...], shift, scale).astype(BF16)
    for c0 in range(0, D_RNN, ODD_IN_COLS):
        z = jnp.dot(h, w_ref[:, c0:c0 + ODD_IN_COLS].astype(BF16), preferred_element_type=F32)
        g_ref[:, c0:c0 + ODD_IN_COLS] = jax.nn.gelu(z).astype(BF16)
    for c0 in range(0, D_RNN, ODD_IN_COLS):
        u_ref[:, c0:c0 + ODD_IN_COLS] = jnp.dot(h, w_ref[:, D_RNN + c0:D_RNN + c0 + ODD_IN_COLS].astype(BF16),
                                                preferred_element_type=F32)


def _odd_in(cfg, l, x, mods, nw, w):
    n = cfg.n_tok
    return pl.pallas_call(
        functools.partial(_odd_in_kernel, cfg=cfg),
        grid=(n // TM_ODD_IN,),
        in_specs=[pl.BlockSpec((TM_ODD_IN, D_MODEL), lambda i: (i, 0)), _mod_spec(l), _const_spec((1, D_MODEL)),
                  pl.BlockSpec((None, D_MODEL, 2 * D_RNN), lambda i: (l // 2, 0, 0),
                               pipeline_mode=pl.Buffered(1))],
        out_specs=[pl.BlockSpec((TM_ODD_IN, D_RNN), lambda i: (i, 0))] * 2,
        out_shape=[jax.ShapeDtypeStruct((n, D_RNN), BF16), jax.ShapeDtypeStruct((n, D_RNN), F32)],
        compiler_params=_cparams(1),
        name=f"odd_in{l}",
    )(x, mods, nw, w)


def _block_position(cfg, ib):
    ncb = cfg.n_ctx // TS
    bpc = cfg.tc // TS
    bpl = cfg.tl // TS
    is_ctx = ib < ncb
    jl = jnp.maximum(ib - ncb, 0)
    pos = jnp.where(is_ctx, ib % bpc, jl % bpl)
    per = jnp.where(is_ctx, bpc, bpl)
    return is_ctx, jl // bpl, pos == 0, pos == per - 1


def _conv_centred(u, prev16, next8, first, last, cw_ref, cb_ref):
    sub = lax.broadcasted_iota(jnp.int32, (SUBLANES, D_RNN), 0)
    lastrow = TS - SUBLANES

    def from_prev_chunk(own, other):
        return jnp.where(sub == 0, jnp.where(first, 0.0, pltpu.roll(other, 1, 0)), pltpu.roll(own, 1, 0))

    s30 = from_prev_chunk(u[lastrow - SUBLANES:lastrow], prev16[0:SUBLANES])
    s31 = from_prev_chunk(u[lastrow:TS], prev16[SUBLANES:2 * SUBLANES])
    n0 = jnp.where(sub == SUBLANES - 1, jnp.where(last, 0.0, pltpu.roll(next8, SUBLANES - 1, 0)),
                   pltpu.roll(u[0:SUBLANES], SUBLANES - 1, 0))
    m2 = jnp.concatenate([s30, s31, u[0:lastrow - SUBLANES]], axis=0)
    m1 = jnp.concatenate([s31, u[0:lastrow]], axis=0)
    p1 = jnp.concatenate([u[SUBLANES:TS], n0], axis=0)
    return m2 * cw_ref[0:1, :] + m1 * cw_ref[1:2, :] + u * cw_ref[2:3, :] + p1 * cw_ref[3:4, :] + cb_ref[...]


def _lru_terms(uh, wg_ref, bg_ref, lam_ref):
    ucb = uh.astype(BF16)
    rs, is_ = [], []
    for nblk in range(LRU_BLOCKS):
        z = jnp.dot(ucb[:, nblk * LRU_BLOCK_W:(nblk + 1) * LRU_BLOCK_W], wg_ref[nblk],
                    preferred_element_type=F32)
        rs.append(z[:, 0:LRU_BLOCK_W])
        is_.append(z[:, LRU_BLOCK_W:2 * LRU_BLOCK_W])
    tr = jnp.tanh(jnp.concatenate(rs, axis=1) + bg_ref[0:1, :])
    ti = jnp.tanh(jnp.concatenate(is_, axis=1) + bg_ref[1:2, :])
    k2 = (-0.5 * LRU_C * LOG2_E) * jax.nn.softplus(-lam_ref[...])
    a = jnp.exp2(k2 + k2 * tr)
    w = 1.0 - a * a
    root = jnp.where(w > 0.0, w * lax.rsqrt(w), 0.0)
    v = root * ((1.0 + ti) * uh)
    return a, v


def _lru_scan(a, v, carry, reverse):
    order = range(LRU_CHUNK - 1, -1, -1) if reverse else range(LRU_CHUNK)
    h = jnp.zeros((SUBLANES, D_RNN), F32)
    p = jnp.ones((SUBLANES, D_RNN), F32)
    hs, ps = [None] * LRU_CHUNK, [None] * LRU_CHUNK
    for j in order:
        aj = a[j * SUBLANES:(j + 1) * SUBLANES]
        h = aj * h + v[j * SUBLANES:(j + 1) * SUBLANES]
        p = aj * p
        hs[j], ps[j] = h, p
    entry = [None] * SUBLANES
    for c in (range(SUBLANES - 1, -1, -1) if reverse else range(SUBLANES)):
        entry[c] = carry
        carry = h[c:c + 1] + p[c:c + 1] * carry
    hm = jnp.concatenate(entry, axis=0)
    return jnp.concatenate([hs[j] + ps[j] * hm for j in range(LRU_CHUNK)], axis=0), carry


LRU_SUB = 2
LRU_STEP = LRU_SUB * TS


def _lru_scans(terms, h0_ref, carry_ref, hs_ref, cfg, tile, direction):
    carry = carry_ref[...]
    ys = [None] * LRU_SUB
    for sb in (range(LRU_SUB - 1, -1, -1) if direction else range(LRU_SUB)):
        is_ctx, jseq, first, last = _block_position(cfg, tile * LRU_SUB + sb)
        start = last if direction else first
        carry = jnp.where(start, jnp.where(is_ctx, 0.0, h0_ref[jseq]), carry)
        ys[sb], carry = _lru_scan(*terms[sb], carry, bool(direction))
        hs_ref[sb] = carry
    carry_ref[...] = carry
    return ys


def _odd_fwd_kernel(u_ref, up_ref, un_ref, cw_ref, cb_ref, wg_ref, bg_ref, lam_ref, h0_ref,
                    yf_ref, uc_ref, hs_ref, carry_ref, *, cfg):
    tile = pl.program_id(0)
    terms = []
    for sb in range(LRU_SUB):
        _, _, first, last = _block_position(cfg, tile * LRU_SUB + sb)
        rows = slice(sb * TS, (sb + 1) * TS)
        prev16 = up_ref[...] if sb == 0 else u_ref[sb * TS - 2 * SUBLANES:sb * TS, :]
        next8 = un_ref[...] if sb == LRU_SUB - 1 else u_ref[(sb + 1) * TS:(sb + 1) * TS + SUBLANES, :]
        uc = _conv_centred(u_ref[rows, :], prev16, next8, first, last, cw_ref, cb_ref)
        uc_ref[rows, :] = uc
        terms.append(_lru_terms(uc, wg_ref, bg_ref, lam_ref))
    for sb, y in enumerate(_lru_scans(terms, h0_ref, carry_ref, hs_ref, cfg, tile, 0)):
        yf_ref[sb * TS:(sb + 1) * TS, :] = y.astype(BF16)


def _odd_bwd_kernel(g_ref, uc_ref, yf_ref, mod_ref, wg_ref, bg_ref, lam_ref, h0_ref, wout_ref,
                    o_ref, hs_ref, carry_ref, *, cfg):
    tile = pl.num_programs(0) - 1 - pl.program_id(0)
    terms = [_lru_terms(uc_ref[sb * TS:(sb + 1) * TS, :], wg_ref, bg_ref, lam_ref) for sb in range(LRU_SUB)]
    y = jnp.concatenate(_lru_scans(terms, h0_ref, carry_ref, hs_ref, cfg, tile, 1), axis=0)
    mix = (g_ref[...].astype(F32) * (yf_ref[...].astype(F32) + y)).astype(BF16)
    grp = _group_of_tile(cfg, tile, LRU_STEP)
    gate = mod_ref[2, pl.ds(grp, 1), :]
    o_ref[...] = gate * jnp.dot(mix, wout_ref[...], preferred_element_type=F32)


def _odd_scans(cfg, l, g, u, mods, cw, cb, wg, bg, lam, h0, w_out):
    n = cfg.n_tok
    assert n % LRU_STEP == 0 and cfg.n_ctx % LRU_STEP == 0 and cfg.tl % LRU_STEP == 0
    nblk = n // TS
    nstep = n // LRU_STEP
    r8 = LRU_STEP // SUBLANES
    last8 = n // SUBLANES - 1

    def specs(rev):
        it = (lambda i: nstep - 1 - i) if rev else (lambda i: i)
        cur = pl.BlockSpec((LRU_STEP, D_RNN), lambda i: (it(i), 0))
        prev = pl.BlockSpec((2 * SUBLANES, D_RNN), lambda i: (jnp.maximum(it(i) * (r8 // 2) - 1, 0), 0))
        nxt = pl.BlockSpec((SUBLANES, D_RNN), lambda i: (jnp.minimum((it(i) + 1) * r8, last8), 0))
        hs = pl.BlockSpec((LRU_SUB, 1, D_RNN), lambda i: (it(i), 0, 0))
        return cur, prev, nxt, hs

    def weights(direction):
        return [pl.BlockSpec((None, LRU_BLOCKS, LRU_BLOCK_W, 2 * LRU_BLOCK_W), lambda i: (direction, 0, 0, 0)),
                pl.BlockSpec((None, 2, D_RNN), lambda i: (direction, 0, 0)),
                pl.BlockSpec((None, 1, D_RNN), lambda i: (direction, 0, 0)),
                pl.BlockSpec((None, cfg.nl, 1, D_RNN), lambda i: (direction, 0, 0, 0))]

    cur, prev, nxt, hs = specs(False)
    tok_shape = jax.ShapeDtypeStruct((n, D_RNN), F32)
    y_f, uc, hs_f = pl.pallas_call(
        functools.partial(_odd_fwd_kernel, cfg=cfg),
        grid=(nstep,),
        in_specs=[cur, prev, nxt, _const_spec((CONV_W, D_RNN)), _const_spec((1, D_RNN))] + weights(0),
        out_specs=[cur, cur, hs],
        out_shape=[jax.ShapeDtypeStruct((n, D_RNN), BF16), tok_shape, jax.ShapeDtypeStruct((nblk, 1, D_RNN), F32)],
        scratch_shapes=[pltpu.VMEM((1, D_RNN), F32)],
        compiler_params=_cparams(1),
        name=f"lru_fwd{l}",
    )(u, u, u, cw, cb, wg, bg, lam, h0)
    cur, _, _, hs = specs(True)
    upd, hs_b = pl.pallas_call(
        functools.partial(_odd_bwd_kernel, cfg=cfg),
        grid=(nstep,),
        in_specs=[cur, cur, cur, _mod_spec(l)] + weights(1) + [_const_spec((D_RNN, D_MODEL))],
        out_specs=[cur, hs],
        out_shape=[tok_shape, jax.ShapeDtypeStruct((nblk, 1, D_RNN), F32)],
        scratch_shapes=[pltpu.VMEM((1, D_RNN), F32)],
        compiler_params=_cparams(1),
        name=f"lru_bwd{l}",
    )(g, uc, y_f, mods, wg, bg, lam, h0, w_out)
    return upd, hs_f, hs_b


def _rope_table(tl):
    pos = np.arange(tl)
    nf = HEAD_DIM // 4
    inv = ROPE_BASE ** (-np.arange(nf, dtype=np.float32) / nf)
    ar = (pos // GRID_W).astype(np.float32)[:, None] * inv
    ac = (pos % GRID_W).astype(np.float32)[:, None] * inv
    ang = np.concatenate([ar, ar, ac, ac] * 2, axis=-1)
    lo = (np.arange(LANES) % (HEAD_DIM // 2)) < nf
    cos, sin = np.cos(ang), np.sin(ang)
    tab = np.concatenate([cos, np.where(lo, -sin, 0.0), np.where(lo, 0.0, sin)], axis=-1)
    return jnp.asarray(tab, F32)


def _dup_heads(w):
    lead = w.shape[:-1]
    w = w.reshape(lead + (A_KV_HEADS, 1, HEAD_DIM))
    return jnp.broadcast_to(w, lead + (A_KV_HEADS, 2, HEAD_DIM)).reshape(lead + (2 * LANES,))


def _pad_low_rank(w):
    pad = jnp.zeros((w.shape[0], LANES - 2 * GLA_RANK), w.dtype)
    return jnp.concatenate([w[:, P_MAIN:], pad], axis=1).astype(BF16)


def _pack_alpha(w_alpha, b_alpha):
    w = jnp.zeros((LANES, 2 * B_QK), F32)
    w = w.at[0:GLA_RANK, 0:B_QK].set(w_alpha[0]).at[GLA_RANK:2 * GLA_RANK, B_QK:].set(w_alpha[1])
    return w.astype(BF16), b_alpha.reshape(1, 2 * B_QK)


def _forward(cfg, x_prompt, x_sample, c, cache_k, cache_v, state_gla, state_lru, c_ctx,
             w_ada, b_ada, norm_mix, norm_ffn, w_in_even, attn_sink, w_alpha, b_alpha, gla_gain,
             w_out_even, w_in_odd, conv_w, conv_b, w_gate_a, b_gate_a, w_gate_x, b_gate_x,
             lru_lambda, w_out_odd, w_ffn_in, w_ffn_out, norm_final):
    assert cfg.tc == TS and cfg.tl % TM == 0 and cfg.n_ctx % TM == 0 and cfg.nl + 1 <= SUBLANES
    x = (x_prompt.reshape(cfg.n_ctx, D_MODEL), x_sample.reshape(cfg.n_lat, D_MODEL))
    cvec = jnp.concatenate([c_ctx[None, :], c, jnp.zeros((SUBLANES - 1 - cfg.nl, D_MODEL), F32)], axis=0)
    mods = _ada(cvec, w_ada, b_ada)
    rope_tab = _rope_table(cfg.tl)
    nf = norm_final.reshape(1, D_MODEL)
    w_out_even_bf = w_out_even.astype(BF16)
    new_k, new_v, new_gla, new_lru = [], [], [], []
    for l in range(DEPTH):
        nw = norm_mix[l].reshape(1, D_MODEL)
        if l % 2 == 0:
            e = l // 2
            za, zb, k_ctx, v_ctx = _even_in(cfg, l, x, mods, nw, w_in_even, _pad_low_rank(w_in_even[e]), rope_tab)
            new_k.append(k_ctx.reshape(cfg.nc, cfg.tc, A_KV_HEADS, HEAD_DIM))
            new_v.append(v_ctx.reshape(cfg.nc, cfg.tc, A_KV_HEADS, HEAD_DIM))
            ck2 = _dup_heads(cache_k[:, e].reshape(cfg.nl, PAST_LEN, A_KV_HEADS * HEAD_DIM)).astype(BF16)
            cv2 = _dup_heads(cache_v[:, e].reshape(cfg.nl, PAST_LEN, A_KV_HEADS * HEAD_DIM)).astype(BF16)
            oa_ctx = _attn_ctx(cfg, l, za, attn_sink[e])
            oa_lat = _attn_lat(cfg, l, za, ck2, cv2, attn_sink[e])
            wal, bal = _pack_alpha(w_alpha[e], b_alpha[e])
            gain = gla_gain[e].reshape(1, B_WIDTH)
            ob_ctx, s_new = _gla_ctx(cfg, l, zb, wal, bal, gain)
            ob_lat = _gla_lat(cfg, l, zb, wal, bal, gain, state_gla[:, e])
            new_gla.append(s_new)
            pre_args = (oa_ctx, oa_lat, ob_ctx, ob_lat, w_out_even_bf)
        else:
            o = l // 2
            g_act, u = _odd_in(cfg, l, x, mods, nw, w_in_odd)
            wg = jnp.concatenate([w_gate_a[o], w_gate_x[o]], axis=-1).astype(BF16)
            bg = 0.5 * jnp.stack([b_gate_a[o], b_gate_x[o]], axis=1)
            lam = lru_lambda[o].reshape(2, 1, D_RNN)
            h0 = jnp.transpose(state_lru[:, o], (1, 0, 2)).reshape(2, cfg.nl, 1, D_RNN)
            upd, hs_f, hs_b = _odd_scans(cfg, l, g_act, u, mods, 0.5 * conv_w[o], 0.5 * conv_b[o].reshape(1, D_RNN),
                                         wg, bg, lam, h0, w_out_odd[o].astype(BF16))
            new_lru.append(jnp.stack([hs_f[:cfg.nc, 0], hs_b[:cfg.nc, 0]], axis=1))
            pre_args = (upd,)
        x = _ffn(cfg, l, x, pre_args, mods, norm_ffn[l].reshape(1, D_MODEL), w_ffn_in, w_ffn_out, nf,
                 final=(l == DEPTH - 1))
    y_prompt = x[0].reshape(cfg.nc, cfg.tc, D_MODEL)
    y_sample = x[1].reshape(cfg.nl, cfg.tl, D_MODEL)
    return (y_prompt, y_sample, jnp.stack(new_k, axis=1), jnp.stack(new_v, axis=1),
            jnp.stack(new_gla, axis=1), jnp.stack(new_lru, axis=1))


def kernel(x_prompt, x_sample, c, cache_k, cache_v, state_gla, state_lru, c_ctx, w_ada, b_ada, norm_mix, norm_ffn, w_in_even, attn_sink, w_alpha, b_alpha, gla_gain, w_out_even, w_in_odd, conv_w, conv_b, w_gate_a, b_gate_a, w_gate_x, b_gate_x, lru_lambda, w_out_odd, w_ffn_in, w_ffn_out, norm_final):
    cfg = Cfg(nc=x_prompt.shape[0], tc=x_prompt.shape[1], nl=x_sample.shape[0], tl=x_sample.shape[1])
    return _forward(cfg, x_prompt, x_sample, c, cache_k, cache_v, state_gla, state_lru, c_ctx,
                    w_ada, b_ada, norm_mix, norm_ffn, w_in_even, attn_sink, w_alpha, b_alpha, gla_gain,
                    w_out_even, w_in_odd, conv_w, conv_b, w_gate_a, b_gate_a, w_gate_x, b_gate_x,
                    lru_lambda, w_out_odd, w_ffn_in, w_ffn_out, norm_final)
```

```python
import functools
from typing import NamedTuple

import jax
import jax.numpy as jnp
import numpy as np
from jax import lax
from jax.experimental import pallas as pl
from jax.experimental.pallas import tpu as pltpu

F32 = jnp.float32
BF16 = jnp.bfloat16

D_MODEL = 1024
DEPTH = 4
EPS = 1e-6

HEAD_DIM = 64
A_Q_HEADS = 8
A_KV_HEADS = 2
A_GROUPS = A_Q_HEADS // A_KV_HEADS
A_WIDTH = A_Q_HEADS * HEAD_DIM
ATT_BLOCK = 128
ROPE_BASE = 10000.0
GRID_W = 64
NEG_INF = -1e30
PAST_LEN = 256

B_HEADS = 4
B_DK = 64
B_DV = 128
B_QK = B_HEADS * B_DK
B_WIDTH = B_HEADS * B_DV
GLA_RANK = 16
GLA_NORMALIZER = 16.0
GLA_CHUNK = 64

D_RNN = D_MODEL
LRU_BLOCK_W = 256
LRU_BLOCKS = D_RNN // LRU_BLOCK_W
CONV_W = 4
LRU_C = 8.0
LOG2_E = 1.4426950408889634

D_FF = 2816

LANES = 128
SUBLANES = 8

ZA_W = A_WIDTH + 2 * 2 * LANES
ZB_LR = B_QK + B_QK + B_WIDTH + B_WIDTH
ZB_W = ZB_LR + LANES
P_A = A_WIDTH + 2 * LANES
P_MAIN = P_A + ZB_LR

TM = 512
TS = 256
FF_CHUNK = 256

VMEM_LIMIT = 56 * 1024 * 1024


class Cfg(NamedTuple):
    nc: int
    tc: int
    nl: int
    tl: int

    @property
    def n_ctx(self):
        return self.nc * self.tc

    @property
    def n_lat(self):
        return self.nl * self.tl

    @property
    def n_tok(self):
        return self.n_ctx + self.n_lat


def _cparams(n_axes):
    return pltpu.CompilerParams(dimension_semantics=("arbitrary",) * n_axes,
                                vmem_limit_bytes=VMEM_LIMIT)


def _const_spec(shape):
    nd = len(shape)
    return pl.BlockSpec(shape, lambda *_: (0,) * nd)


def _silu(x):
    hx = 0.5 * x
    return hx + hx * jnp.tanh(hx)


def _group_of_tile(cfg, i, tile):
    nct = cfg.n_ctx // tile
    per_lat = cfg.tl // tile
    return jnp.where(i < nct, 0, 1 + jnp.maximum(i - nct, 0) // per_lat)


def _modnorm(x, nw, shift, scale):
    ms = jnp.mean(x * x, axis=-1, keepdims=True)
    y = x * lax.rsqrt(ms + EPS) * nw
    return y * (1.0 + scale) + shift


def _mod_rows(mod_ref, g, first):
    return tuple(mod_ref[first + j, pl.ds(g, 1), :] for j in range(3))


def _ada_kernel(c_ref, w_ref, b_ref, o_ref):
    s = _silu(c_ref[...]).astype(BF16)
    o_ref[...] = jnp.dot(s, w_ref[...].astype(BF16), preferred_element_type=F32) + b_ref[...]


def _ada(cvec, w_ada, b_ada):
    b4 = b_ada.reshape(DEPTH, 6, 1, D_MODEL)
    return pl.pallas_call(
        _ada_kernel,
        grid=(DEPTH, 6),
        in_specs=[
            _const_spec((SUBLANES, D_MODEL)),
            pl.BlockSpec((None, D_MODEL, D_MODEL), lambda l, j: (l, 0, j)),
            pl.BlockSpec((None, None, 1, D_MODEL), lambda l, j: (l, j, 0, 0)),
        ],
        out_specs=pl.BlockSpec((None, None, SUBLANES, D_MODEL), lambda l, j: (l, j, 0, 0)),
        out_shape=jax.ShapeDtypeStruct((DEPTH, 6, SUBLANES, D_MODEL), F32),
        compiler_params=_cparams(2),
        name="ada",
    )(cvec, w_ada, b4)


def _mod_spec(l):
    return pl.BlockSpec((None, 6, SUBLANES, D_MODEL), lambda *_: (l, 0, 0, 0))


def _ffn_tile(x, nw, shift, scale, gate, win_ref, wout_ref):
    h = _modnorm(x, nw, shift, scale).astype(BF16)
    acc = None
    for c in range(D_FF // FF_CHUNK):
        lo, hi = c * FF_CHUNK, (c + 1) * FF_CHUNK
        g = jnp.dot(h, win_ref[:, lo:hi].astype(BF16), preferred_element_type=F32)
        u = jnp.dot(h, win_ref[:, D_FF + lo:D_FF + hi].astype(BF16), preferred_element_type=F32)
        a = (_silu(g) * u).astype(BF16)
        part = jnp.dot(a, wout_ref[lo:hi, :].astype(BF16), preferred_element_type=F32)
        acc = part if acc is None else acc + part
    return x + gate * acc


def _x_args(cfg, x, tile=TM):
    if not isinstance(x, tuple):
        return [x], [pl.BlockSpec((tile, D_MODEL), lambda i: (i, 0))]
    nct = cfg.n_ctx // tile
    return list(x), [pl.BlockSpec((tile, D_MODEL), lambda i: (jnp.minimum(i, nct - 1), 0)),
                     pl.BlockSpec((tile, D_MODEL), lambda i: (jnp.maximum(i - nct, 0), 0))]


def _load_x(cfg, x_refs, i, tile=TM):
    if len(x_refs) == 1:
        return x_refs[0][...]
    return jnp.where(i < cfg.n_ctx // tile, x_refs[0][...], x_refs[1][...])


def _to_scan_order(x):
    nb, d = x.shape[0] // TS, x.shape[1]
    return jnp.swapaxes(x.reshape(nb, SUBLANES, TS // SUBLANES, d), 1, 2).reshape(nb * TS, d)


def _from_scan_order(x):
    nb, d = x.shape[0] // TS, x.shape[1]
    return jnp.swapaxes(x.reshape(nb, TS // SUBLANES, SUBLANES, d), 1, 2).reshape(nb * TS, d)


def _ffn_kernel(*refs, cfg, n_x, pre, final):
    x_refs, refs = refs[:n_x], refs[n_x:]
    i = pl.program_id(0)
    g = _group_of_tile(cfg, i, TM)
    x = _load_x(cfg, x_refs, i)
    if pre == "even":
        (oac_ref, oal_ref, obc_ref, obl_ref, wmix_ref), refs = refs[:5], refs[5:]
    else:
        d_ref, refs = refs[0], refs[1:]
    mod_ref, nw_ref, win_ref, wout_ref, nf_ref = refs[:5]
    outs = refs[5:]
    if pre == "even":
        is_ctx_tile = i < cfg.n_ctx // TM
        o = jnp.concatenate([jnp.where(is_ctx_tile, oac_ref[...], oal_ref[...]),
                             jnp.where(is_ctx_tile, obc_ref[...], obl_ref[...])], axis=1)
        x = x + mod_ref[2, pl.ds(g, 1), :] * jnp.dot(o, wmix_ref[...], preferred_element_type=F32)
    else:
        x = x + _from_scan_order(d_ref[...])
    shift, scale, gate = _mod_rows(mod_ref, g, 3)
    y = _ffn_tile(x, nw_ref[...], shift, scale, gate, win_ref, wout_ref)
    if not final:
        outs[0][...] = y
        return
    ms = jnp.mean(y * y, axis=-1, keepdims=True)
    y = y * lax.rsqrt(ms + EPS) * nf_ref[...]
    is_ctx = i < cfg.n_ctx // TM

    @pl.when(is_ctx)
    def _():
        outs[0][...] = y

    @pl.when(jnp.logical_not(is_ctx))
    def _():
        outs[1][...] = y


def _ffn(cfg, l, x, pre_args, mods, nw, win, wout, nf, final):
    n = cfg.n_tok
    nct = cfg.n_ctx // TM
    tok = lambda w: pl.BlockSpec((TM, w), lambda i: (i, 0))
    ctx = lambda w: pl.BlockSpec((TM, w), lambda i: (jnp.minimum(i, nct - 1), 0))
    lat = lambda w: pl.BlockSpec((TM, w), lambda i: (jnp.maximum(i - nct, 0), 0))
    x_ops, x_specs = _x_args(cfg, x)
    if l % 2 == 0:
        pre = "even"
        pre_specs = [ctx(A_WIDTH), lat(A_WIDTH), ctx(B_WIDTH), lat(B_WIDTH),
                     pl.BlockSpec((None, A_WIDTH + B_WIDTH, D_MODEL), lambda i: (l // 2, 0, 0))]
    else:
        pre, pre_specs = "odd", [tok(D_MODEL)]
    if final:
        out_specs = [pl.BlockSpec((TM, D_MODEL), lambda i: (jnp.minimum(i, nct - 1), 0)),
                     pl.BlockSpec((TM, D_MODEL), lambda i: (jnp.maximum(i - nct, 0), 0))]
        out_shape = [jax.ShapeDtypeStruct((cfg.n_ctx, D_MODEL), F32),
                     jax.ShapeDtypeStruct((cfg.n_lat, D_MODEL), F32)]
    else:
        out_specs, out_shape = tok(D_MODEL), jax.ShapeDtypeStruct((n, D_MODEL), F32)
    return pl.pallas_call(
        functools.partial(_ffn_kernel, cfg=cfg, n_x=len(x_ops), pre=pre, final=final),
        grid=(n // TM,),
        in_specs=x_specs + pre_specs + [
            _mod_spec(l), _const_spec((1, D_MODEL)),
            pl.BlockSpec((None, D_MODEL, 2 * D_FF), lambda i: (l, 0, 0), pipeline_mode=pl.Buffered(1)),
            pl.BlockSpec((None, D_FF, D_MODEL), lambda i: (l, 0, 0), pipeline_mode=pl.Buffered(1)),
            _const_spec((1, D_MODEL))],
        out_specs=out_specs,
        out_shape=out_shape,
        compiler_params=_cparams(1),
        name=f"ffn{l}",
    )(*x_ops, *pre_args, mods, nw, win, wout, nf)


TM_EVEN_IN = 1024


def _even_in_kernel(*refs, cfg, n_x):
    x_refs, (mod_ref, nw_ref, w_ref, wlr_ref, tab_ref, za_ref, zb_ref, k_ref, v_ref) = refs[:n_x], refs[n_x:]
    i = pl.program_id(0)
    g = _group_of_tile(cfg, i, TM_EVEN_IN)
    shift, scale, _ = _mod_rows(mod_ref, g, 0)
    h = _modnorm(_load_x(cfg, x_refs, i, TM_EVEN_IN), nw_ref[...], shift, scale).astype(BF16)
    def proj(lo_col, hi_col):
        return jnp.dot(h, w_ref[:, lo_col:hi_col].astype(BF16), preferred_element_type=F32)

    is_lat = g != 0
    cos = jnp.where(is_lat, tab_ref[:, 0:LANES], 1.0)
    sin_a = jnp.where(is_lat, tab_ref[:, LANES:2 * LANES], 0.0)
    sin_b = jnp.where(is_lat, tab_ref[:, 2 * LANES:3 * LANES], 0.0)

    def rope(a):
        parts = []
        for j in range(a.shape[1] // LANES):
            t = a[:, j * LANES:(j + 1) * LANES]
            parts.append(t * cos + pltpu.roll(t, LANES - 16, 1) * sin_a + pltpu.roll(t, 16, 1) * sin_b)
        return jnp.concatenate(parts, axis=1)

    lo = lax.broadcasted_iota(jnp.int32, (TM_EVEN_IN, LANES), 1) < HEAD_DIM

    def twice(a):
        swapped = pltpu.roll(a, HEAD_DIM, 1)
        return jnp.concatenate([jnp.where(lo, a, swapped), jnp.where(lo, swapped, a)], axis=1).astype(BF16)

    q = proj(0, A_WIDTH) * (LOG2_E * HEAD_DIM ** -0.5)
    za_ref[:, 0:A_WIDTH] = rope(q).astype(BF16)
    kv = proj(A_WIDTH, P_A)
    k, v = kv[:, 0:LANES], kv[:, LANES:2 * LANES]
    za_ref[:, A_WIDTH:A_WIDTH + 2 * LANES] = twice(rope(k))
    za_ref[:, A_WIDTH + 2 * LANES:ZA_W] = twice(v)
    zb_ref[:, 0:B_QK] = proj(P_A, P_A + B_QK) * (B_DK ** -0.5)
    for c0 in range(B_QK, ZB_LR, 2 * LANES):
        zb_ref[:, c0:c0 + 2 * LANES] = proj(P_A + c0, P_A + c0 + 2 * LANES)
    zb_ref[:, ZB_LR:ZB_W] = jnp.dot(h, wlr_ref[...], preferred_element_type=F32)

    @pl.when(g == 0)
    def _():
        k_ref[...] = k
        v_ref[...] = v


def _even_in(cfg, l, x, mods, nw, w_all, w_lr, rope_tab):
    n = cfg.n_tok
    tile = TM_EVEN_IN
    assert cfg.n_ctx % tile == 0 and cfg.tl % tile == 0
    nct = cfg.n_ctx // tile
    per_lat = cfg.tl // tile
    tok = lambda w: pl.BlockSpec((tile, w), lambda i: (i, 0))
    x_ops, x_specs = _x_args(cfg, x, tile)
    ctx_kv = pl.BlockSpec((tile, LANES), lambda i: (jnp.minimum(i, nct - 1), 0))
    return pl.pallas_call(
        functools.partial(_even_in_kernel, cfg=cfg, n_x=len(x_ops)),
        grid=(n // tile,),
        in_specs=x_specs + [_mod_spec(l), _const_spec((1, D_MODEL)),
                            pl.BlockSpec((None, D_MODEL, P_MAIN), lambda i: (l // 2, 0, 0),
                                         pipeline_mode=pl.Buffered(1)),
                            _const_spec((D_MODEL, LANES)),
                            pl.BlockSpec((tile, 3 * LANES), lambda i: (jnp.maximum(i - nct, 0) % per_lat, 0))],
        out_specs=[tok(ZA_W), tok(ZB_W), ctx_kv, ctx_kv],
        out_shape=[jax.ShapeDtypeStruct((n, ZA_W), BF16),
                   jax.ShapeDtypeStruct((n, ZB_W), F32),
                   jax.ShapeDtypeStruct((cfg.n_ctx, LANES), F32),
                   jax.ShapeDtypeStruct((cfg.n_ctx, LANES), F32)],
        compiler_params=_cparams(1),
        name=f"even_in{l}",
    )(*x_ops, mods, nw, w_all, w_lr, rope_tab)


def _sink_attention(units, sink_ref):
    scores = []
    for q, k_all, _, mask, kvh in units:
        tq = q.shape[0]
        lo = lax.broadcasted_iota(jnp.int32, (tq, LANES), 1) < HEAD_DIM
        qs = []
        for gq in range(A_GROUPS):
            h = kvh * A_GROUPS + gq
            pair = q[:, (h // 2) * LANES:(h // 2 + 1) * LANES]
            keep = lo if h % 2 == 0 else jnp.logical_not(lo)
            qs.append(jnp.where(keep, pair, jnp.zeros_like(pair)))
        qst = jnp.concatenate(qs, axis=0)
        s = lax.dot_general(qst, k_all, (((1,), (1,)), ((), ())), preferred_element_type=F32)
        if mask is not None:
            nfree = s.shape[1] - mask.shape[1]
            s = jnp.concatenate([s[:, :nfree], jnp.where(mask, s[:, nfree:], NEG_INF)], axis=1)
        scores.append(s)
    probs = []
    for (q, _, _, _, kvh), s in zip(units, scores):
        tq = q.shape[0]
        rows = lax.broadcasted_iota(jnp.int32, (A_GROUPS * tq, 1), 0) // tq
        sink = jnp.zeros((A_GROUPS * tq, 1), F32)
        for gq in range(A_GROUPS):
            sink = jnp.where(rows == gq, sink_ref[kvh * A_GROUPS + gq] * LOG2_E, sink)
        m = jnp.maximum(jnp.max(s, axis=-1, keepdims=True), sink)
        p = jnp.exp2(s - m)
        den = jnp.sum(p, axis=-1, keepdims=True) + jnp.exp2(sink - m)
        probs.append((p.astype(BF16), den))
    outs = []
    for (q, _, v_all, _, _), (p, den) in zip(units, probs):
        tq = q.shape[0]
        lo = lax.broadcasted_iota(jnp.int32, (tq, LANES), 1) < HEAD_DIM
        o = jnp.dot(p, v_all, preferred_element_type=F32) / den
        pairs = [jnp.where(lo, o[(2 * pr) * tq:(2 * pr + 1) * tq], o[(2 * pr + 1) * tq:(2 * pr + 2) * tq])
                 for pr in range(2)]
        outs.append(jnp.concatenate(pairs, axis=1).astype(BF16))
    return outs


def _attn_ctx_kernel(sink_ref, q_ref, k_ref, v_ref, o_ref):
    q = q_ref[...]
    units = [(q, k_ref[:, kvh * LANES:(kvh + 1) * LANES], v_ref[:, kvh * LANES:(kvh + 1) * LANES], None, kvh)
             for kvh in range(A_KV_HEADS)]
    for kvh, o in enumerate(_sink_attention(units, sink_ref)):
        o_ref[:, 2 * kvh * LANES:2 * (kvh + 1) * LANES] = o


def _attn_ctx(cfg, l, za, sink):
    tc = cfg.tc
    return pl.pallas_call(
        _attn_ctx_kernel,
        grid=(cfg.nc,),
        in_specs=[pl.BlockSpec(memory_space=pltpu.SMEM),
                  pl.BlockSpec((tc, A_WIDTH), lambda b: (b, 0)),
                  pl.BlockSpec((tc, 2 * LANES), lambda b: (b, A_WIDTH // (2 * LANES))),
                  pl.BlockSpec((tc, 2 * LANES), lambda b: (b, A_WIDTH // (2 * LANES) + 1))],
        out_specs=pl.BlockSpec((tc, A_WIDTH), lambda b: (b, 0)),
        out_shape=jax.ShapeDtypeStruct((cfg.n_ctx, A_WIDTH), BF16),
        compiler_params=_cparams(1),
        name=f"attn_ctx{l}",
    )(sink, za, za, za)


ATT_STEP = 4 * ATT_BLOCK
ATT_WIN = 3 * ATT_BLOCK


def _attn_lat_kernel(sink_ref, q_ref, k_ref, v_ref, ck_ref, cv_ref, o_ref):
    n = pl.program_id(1)
    tl = k_ref.shape[0]
    shape = (A_GROUPS * ATT_BLOCK, ATT_WIN)
    rc = (lax.broadcasted_iota(jnp.int32, shape, 0) % ATT_BLOCK - lax.broadcasted_iota(jnp.int32, shape, 1)
          + ATT_BLOCK)
    units = []
    for j in range(ATT_STEP // ATT_BLOCK):
        q0 = (n * (ATT_STEP // ATT_BLOCK) + j) * ATT_BLOCK
        start = pl.multiple_of(jnp.clip(q0 - ATT_BLOCK, 0, tl - ATT_WIN), ATT_BLOCK)
        mask = lax.bitcast_convert_type(rc + (q0 - start), jnp.uint32) <= jnp.uint32(2 * ATT_BLOCK)
        q = q_ref[j * ATT_BLOCK:(j + 1) * ATT_BLOCK, :]
        for kvh in range(A_KV_HEADS):
            sl = slice(kvh * LANES, (kvh + 1) * LANES)
            k_all = jnp.concatenate([ck_ref[:, sl], k_ref[pl.ds(start, ATT_WIN), sl]], axis=0)
            v_all = jnp.concatenate([cv_ref[:, sl], v_ref[pl.ds(start, ATT_WIN), sl]], axis=0)
            units.append((q, k_all, v_all, mask, kvh))
    for u, o in enumerate(_sink_attention(units, sink_ref)):
        j, kvh = divmod(u, A_KV_HEADS)
        o_ref[j * ATT_BLOCK:(j + 1) * ATT_BLOCK, 2 * kvh * LANES:2 * (kvh + 1) * LANES] = o


def _attn_lat(cfg, l, za, cache_k2, cache_v2, sink):
    assert cfg.n_ctx % cfg.tl == 0 and cfg.tl % ATT_STEP == 0 and cfg.tl >= ATT_WIN
    steps = cfg.tl // ATT_STEP
    base = cfg.n_ctx // ATT_STEP
    kcol = A_WIDTH // (2 * LANES)
    seq = lambda col: pl.BlockSpec((cfg.tl, 2 * LANES), lambda b, n: (cfg.n_ctx // cfg.tl + b, col))
    cache = pl.BlockSpec((None, PAST_LEN, 2 * LANES), lambda b, n: (b, 0, 0))
    return pl.pallas_call(
        _attn_lat_kernel,
        grid=(cfg.nl, steps),
        in_specs=[pl.BlockSpec(memory_space=pltpu.SMEM),
                  pl.BlockSpec((ATT_STEP, A_WIDTH), lambda b, n: (base + b * steps + n, 0)),
                  seq(kcol), seq(kcol + 1), cache, cache],
        out_specs=pl.BlockSpec((ATT_STEP, A_WIDTH), lambda b, n: (b * steps + n, 0)),
        out_shape=jax.ShapeDtypeStruct((cfg.n_lat, A_WIDTH), BF16),
        compiler_params=_cparams(2),
        name=f"attn_lat{l}",
    )(sink, za, za, za, cache_k2, cache_v2)


def _gla_chunk_cumsum(la, reverse):
    cn = GLA_CHUNK
    row = lax.broadcasted_iota(jnp.int32, (cn, B_QK), 0)
    b = la
    s = 1
    while s < cn:
        if reverse:
            b = b + jnp.where(row < cn - s, pltpu.roll(b, cn - s, 0), 0.0)
        else:
            b = b + jnp.where(row >= s, pltpu.roll(b, s, 0), 0.0)
        s *= 2
    return b


def _gla_chunk_prepare(q, k, v, b, reverse):
    cn = GLA_CHUNK
    row = lax.broadcasted_iota(jnp.int32, (cn, B_QK), 0)
    b_last = b[0:1, :] if reverse else b[cn - 1:cn, :]
    q_e = q * jnp.exp(b)
    k_e = (k * jnp.exp(-b)).astype(BF16)
    k_s = (k * jnp.exp(b_last - b)).astype(BF16)
    vb = v.astype(BF16)
    row_t = row[:, 0:LANES]
    lane_t = lax.broadcasted_iota(jnp.int32, (cn, LANES), 1)
    lo = lane_t < B_DK
    col = lane_t % cn
    tri = (col >= row_t) if reverse else (col <= row_t)
    q_tiles, att_tiles = [], []
    for t in range(B_HEADS // 2):
        sl = slice(t * LANES, (t + 1) * LANES)
        k_t = k_e[:, sl]
        zero = jnp.zeros_like(k_t)
        kbd = jnp.concatenate([jnp.where(lo, k_t, zero), jnp.where(lo, zero, k_t)], axis=0)
        q_t = q_e[:, sl]
        att = lax.dot_general(q_t.astype(BF16), kbd, (((1,), (1,)), ((), ())), preferred_element_type=F32)
        q_tiles.append(q_t)
        att_tiles.append(jnp.where(tri, att, 0.0))
    blocks = []
    for t in range(B_HEADS // 2):
        upd = lax.dot_general(k_s[:, t * LANES:(t + 1) * LANES], vb[:, 2 * t * B_DV:2 * (t + 1) * B_DV],
                              (((0,), (0,)), ((), ())), preferred_element_type=F32)
        blocks += [upd[g * B_DK:(g + 1) * B_DK, g * B_DV:(g + 1) * B_DV] for g in range(2)]
    own = jnp.concatenate(blocks, axis=0)
    decay = jnp.broadcast_to(jnp.exp(b_last), (LANES, B_QK)).T
    return q_tiles, att_tiles, vb, own, decay


def _gla_chunk_output(q_tiles, att_tiles, vb, st):
    lo = lax.broadcasted_iota(jnp.int32, (GLA_CHUNK, LANES), 1) < B_DK
    st_b = st.astype(BF16)
    outs = []
    for h in range(B_HEADS):
        att, q_t = att_tiles[h // 2], q_tiles[h // 2]
        x = (jnp.where(lo, att, pltpu.roll(q_t, B_DK, 1)) if h % 2 == 0
             else jnp.where(lo, pltpu.roll(att, B_DK, 1), q_t))
        rhs = jnp.concatenate([vb[:, h * B_DV:(h + 1) * B_DV], st_b[h * B_DK:(h + 1) * B_DK, :]], axis=0)
        outs.append(jnp.dot(x.astype(BF16), rhs, preferred_element_type=F32))
    return jnp.concatenate(outs, axis=1)


def _gla_log_decay(lr_pad, wal_ref, bal_ref, direction):
    z = jnp.dot(lr_pad.astype(BF16), wal_ref[:, direction * B_QK:(direction + 1) * B_QK],
                preferred_element_type=F32) + bal_ref[:, direction * B_QK:(direction + 1) * B_QK]
    return jax.nn.log_sigmoid(z) / GLA_NORMALIZER


def _gla_blocks(jobs, wal_ref, bal_ref):
    prepared = []
    for qkv_ref, lr_ref, _, direction in jobs:
        nchunk = qkv_ref.shape[0] // GLA_CHUNK
        la_all = _gla_log_decay(lr_ref[...], wal_ref, bal_ref, direction)
        b_all = jnp.concatenate(
            [_gla_chunk_cumsum(la_all[r0:r0 + GLA_CHUNK, :], bool(direction))
             for r0 in range(0, qkv_ref.shape[0], GLA_CHUNK)], axis=0)
        chunks = {}
        for ci in range(nchunk):
            rs = slice(ci * GLA_CHUNK, (ci + 1) * GLA_CHUNK)
            chunks[ci] = _gla_chunk_prepare(qkv_ref[rs, 0:B_QK], qkv_ref[rs, B_QK:2 * B_QK],
                                            qkv_ref[rs, 2 * B_QK:2 * B_QK + B_WIDTH], b_all[rs, :],
                                            bool(direction))
        prepared.append(chunks)
    results = []
    for (qkv_ref, _, st, direction), chunks in zip(jobs, prepared):
        nchunk = len(chunks)
        states = {}
        for ci in (range(nchunk - 1, -1, -1) if direction else range(nchunk)):
            states[ci] = st
            _, _, _, own, decay = chunks[ci]
            st = st * decay + own
        outs = [_gla_chunk_output(*chunks[ci][:3], states[ci]) for ci in range(nchunk)]
        results.append((jnp.concatenate(outs, axis=0), st))
    return results


def _stack_state(s_ref):
    return s_ref[...].reshape(B_QK, B_DV)


def _unstack_state(st, s_ref):
    s_ref[...] = st.reshape(B_HEADS, B_DK, B_DV)


def _gla_finish(o, r, gain):
    parts = []
    for h in range(B_HEADS):
        oh = o[:, h * B_DV:(h + 1) * B_DV]
        parts.append(oh * lax.rsqrt(jnp.mean(oh * oh, axis=-1, keepdims=True) + EPS))
    return (jnp.concatenate(parts, axis=1) * gain * _silu(r)).astype(BF16)


def _gla_ctx_kernel(qkv_ref, r_ref, lr_ref, wal_ref, bal_ref, gain_ref, ob_ref, snew_ref):
    zero = jnp.zeros((B_QK, B_DV), F32)
    (o_f, st_f), (o_b, st_b) = _gla_blocks([(qkv_ref, lr_ref, zero, 0), (qkv_ref, lr_ref, zero, 1)],
                                           wal_ref, bal_ref)
    _unstack_state(st_f, snew_ref.at[0])
    _unstack_state(st_b, snew_ref.at[1])
    ob_ref[...] = _gla_finish(o_f + o_b, r_ref[...], gain_ref[...])


def _gla_ctx(cfg, l, zb, wal, bal, gain):
    tc = cfg.tc
    return pl.pallas_call(
        _gla_ctx_kernel,
        grid=(cfg.nc,),
        in_specs=[pl.BlockSpec((tc, 2 * B_QK + B_WIDTH), lambda b: (b, 0)),
                  pl.BlockSpec((tc, B_WIDTH), lambda b: (b, (2 * B_QK + B_WIDTH) // B_WIDTH)),
                  pl.BlockSpec((tc, LANES), lambda b: (b, ZB_LR // LANES)),
                  _const_spec((LANES, 2 * B_QK)), _const_spec((1, 2 * B_QK)), _const_spec((1, B_WIDTH))],
        out_specs=[pl.BlockSpec((tc, B_WIDTH), lambda b: (b, 0)),
                   pl.BlockSpec((None, 2, B_HEADS, B_DK, B_DV), lambda b: (b, 0, 0, 0, 0))],
        out_shape=[jax.ShapeDtypeStruct((cfg.n_ctx, B_WIDTH), BF16),
                   jax.ShapeDtypeStruct((cfg.nc, 2, B_HEADS, B_DK, B_DV), F32)],
        compiler_params=_cparams(1),
        name=f"gla_ctx{l}",
    )(zb, zb, zb, wal, bal, gain)


GLA_STEP = 512


def _gla_lat_kernel(qkvf_ref, lrf_ref, rf_ref, qkvb_ref, lrb_ref, rb_ref, wal_ref, bal_ref, gain_ref,
                    s0_ref, ob_ref, stf_ref, stb_ref, of_ref, obk_ref):
    n = pl.program_id(1)
    nb = pl.num_programs(1)

    @pl.when(n == 0)
    def _():
        stf_ref[...] = _stack_state(s0_ref.at[0])
        stb_ref[...] = _stack_state(s0_ref.at[1])
    (o_f, st_f), (o_b, st_b) = _gla_blocks([(qkvf_ref, lrf_ref, stf_ref[...], 0),
                                            (qkvb_ref, lrb_ref, stb_ref[...], 1)], wal_ref, bal_ref)
    stf_ref[...] = st_f
    stb_ref[...] = st_b
    rows_f = pl.ds(pl.multiple_of(n * GLA_STEP, GLA_STEP), GLA_STEP)
    rows_b = pl.ds(pl.multiple_of((nb - 1 - n) * GLA_STEP, GLA_STEP), GLA_STEP)
    of_ref[rows_f, :] = o_f
    obk_ref[rows_b, :] = o_b

    @pl.when(2 * n >= nb)
    def _():
        ob_ref[rows_f, :] = _gla_finish(o_f + obk_ref[rows_f, :], rf_ref[...], gain_ref[...])
        ob_ref[rows_b, :] = _gla_finish(of_ref[rows_b, :] + o_b, rb_ref[...], gain_ref[...])


def _gla_lat(cfg, l, zb, wal, bal, gain, s0):
    nbl = cfg.tl // GLA_STEP
    base = cfg.n_ctx // GLA_STEP
    assert nbl % 2 == 0 and cfg.tl % GLA_STEP == 0 and cfg.n_ctx % GLA_STEP == 0

    def blk(width, col, rev):
        def imap(b, n):
            return (base + b * nbl + (nbl - 1 - n if rev else n), col)
        return pl.BlockSpec((GLA_STEP, width), imap)

    qkv_w = 2 * B_QK + B_WIDTH
    per_dir = lambda rev: [blk(qkv_w, 0, rev), blk(LANES, ZB_LR // LANES, rev), blk(B_WIDTH, qkv_w // B_WIDTH, rev)]
    return pl.pallas_call(
        _gla_lat_kernel,
        grid=(cfg.nl, nbl),
        in_specs=per_dir(False) + per_dir(True) + [
            _const_spec((LANES, 2 * B_QK)), _const_spec((1, 2 * B_QK)), _const_spec((1, B_WIDTH)),
            pl.BlockSpec((None, 2, B_HEADS, B_DK, B_DV), lambda b, n: (b, 0, 0, 0, 0))],
        out_specs=pl.BlockSpec((cfg.tl, B_WIDTH), lambda b, n: (b, 0)),
        out_shape=jax.ShapeDtypeStruct((cfg.n_lat, B_WIDTH), BF16),
        scratch_shapes=[pltpu.VMEM((B_QK, B_DV), F32), pltpu.VMEM((B_QK, B_DV), F32),
                        pltpu.VMEM((cfg.tl, B_WIDTH), F32), pltpu.VMEM((cfg.tl, B_WIDTH), F32)],
        compiler_params=_cparams(2),
        name=f"gla_lat{l}",
    )(zb, zb, zb, zb, zb, zb, wal, bal, gain, s0)


LRU_CHUNK = TS // SUBLANES


TM_ODD_IN = 1024
ODD_IN_COLS = 256


def _odd_in_kernel(x_ref, mod_ref, nw_ref, w_ref, g_ref, u_ref, *, cfg):
    g = _group_of_tile(cfg, pl.program_id(0), TM_ODD_IN)
    shift, scale, _ = _mod_rows(mod_ref, g, 0)
    h = _modnorm(_to_scan_order(x_ref[...]), nw_ref[...], shift, scale).astype(BF16)
    for c0 in range(0, D_RNN, ODD_IN_COLS):
        z = jnp.dot(h, w_ref[:, c0:c0 + ODD_IN_COLS].astype(BF16), preferred_element_type=F32)
        g_ref[:, c0:c0 + ODD_IN_COLS] = jax.nn.gelu(z).astype(BF16)
    for c0 in range(0, D_RNN, ODD_IN_COLS):
        u_ref[:, c0:c0 + ODD_IN_COLS] = jnp.dot(h, w_ref[:, D_RNN + c0:D_RNN + c0 + ODD_IN_COLS].astype(BF16),
                                                preferred_element_type=F32)


def _odd_in(cfg, l, x, mods, nw, w):
    n = cfg.n_tok
    return pl.pallas_call(
        functools.partial(_odd_in_kernel, cfg=cfg),
        grid=(n // TM_ODD_IN,),
        in_specs=[pl.BlockSpec((TM_ODD_IN, D_MODEL), lambda i: (i, 0)), _mod_spec(l), _const_spec((1, D_MODEL)),
                  pl.BlockSpec((None, D_MODEL, 2 * D_RNN), lambda i: (l // 2, 0, 0),
                               pipeline_mode=pl.Buffered(1))],
        out_specs=[pl.BlockSpec((TM_ODD_IN, D_RNN), lambda i: (i, 0))] * 2,
        out_shape=[jax.ShapeDtypeStruct((n, D_RNN), BF16), jax.ShapeDtypeStruct((n, D_RNN), F32)],
        compiler_params=_cparams(1),
        name=f"odd_in{l}",
    )(x, mods, nw, w)


def _block_position(cfg, ib):
    ncb = cfg.n_ctx // TS
    bpc = cfg.tc // TS
    bpl = cfg.tl // TS
    is_ctx = ib < ncb
    jl = jnp.maximum(ib - ncb, 0)
    pos = jnp.where(is_ctx, ib % bpc, jl % bpl)
    per = jnp.where(is_ctx, bpc, bpl)
    return is_ctx, jl // bpl, pos == 0, pos == per - 1


def _conv_centred(u, prev16, next8, first, last, cw_ref, cb_ref):
    sub = lax.broadcasted_iota(jnp.int32, (SUBLANES, D_RNN), 0)
    lastrow = TS - SUBLANES

    def from_prev_chunk(own, other):
        return jnp.where(sub == 0, jnp.where(first, 0.0, pltpu.roll(other, 1, 0)), pltpu.roll(own, 1, 0))

    s30 = from_prev_chunk(u[lastrow - SUBLANES:lastrow], prev16[0:SUBLANES])
    s31 = from_prev_chunk(u[lastrow:TS], prev16[SUBLANES:2 * SUBLANES])
    n0 = jnp.where(sub == SUBLANES - 1, jnp.where(last, 0.0, pltpu.roll(next8, SUBLANES - 1, 0)),
                   pltpu.roll(u[0:SUBLANES], SUBLANES - 1, 0))
    m2 = jnp.concatenate([s30, s31, u[0:lastrow - SUBLANES]], axis=0)
    m1 = jnp.concatenate([s31, u[0:lastrow]], axis=0)
    p1 = jnp.concatenate([u[SUBLANES:TS], n0], axis=0)
    return m2 * cw_ref[0:1, :] + m1 * cw_ref[1:2, :] + u * cw_ref[2:3, :] + p1 * cw_ref[3:4, :] + cb_ref[...]


def _lru_terms(uh, wg_ref, bg_ref, lam_ref):
    ucb = uh.astype(BF16)
    rs, is_ = [], []
    for nblk in range(LRU_BLOCKS):
        z = jnp.dot(ucb[:, nblk * LRU_BLOCK_W:(nblk + 1) * LRU_BLOCK_W], wg_ref[nblk],
                    preferred_element_type=F32)
        rs.append(z[:, 0:LRU_BLOCK_W])
        is_.append(z[:, LRU_BLOCK_W:2 * LRU_BLOCK_W])
    tr = jnp.tanh(jnp.concatenate(rs, axis=1) + bg_ref[0:1, :])
    ti = jnp.tanh(jnp.concatenate(is_, axis=1) + bg_ref[1:2, :])
    k2 = (-0.5 * LRU_C * LOG2_E) * jax.nn.softplus(-lam_ref[...])
    a = jnp.exp2(k2 + k2 * tr)
    w = 1.0 - a * a
    root = jnp.where(w > 0.0, w * lax.rsqrt(w), 0.0)
    v = root * ((1.0 + ti) * uh)
    return a, v


def _lru_scan(a, v, carry, reverse):
    order = range(LRU_CHUNK - 1, -1, -1) if reverse else range(LRU_CHUNK)
    h = jnp.zeros((SUBLANES, D_RNN), F32)
    p = jnp.ones((SUBLANES, D_RNN), F32)
    hs, ps = [None] * LRU_CHUNK, [None] * LRU_CHUNK
    for j in order:
        aj = a[j * SUBLANES:(j + 1) * SUBLANES]
        h = aj * h + v[j * SUBLANES:(j + 1) * SUBLANES]
        p = aj * p
        hs[j], ps[j] = h, p
    entry = [None] * SUBLANES
    for c in (range(SUBLANES - 1, -1, -1) if reverse else range(SUBLANES)):
        entry[c] = carry
        carry = h[c:c + 1] + p[c:c + 1] * carry
    hm = jnp.concatenate(entry, axis=0)
    return jnp.concatenate([hs[j] + ps[j] * hm for j in range(LRU_CHUNK)], axis=0), carry


LRU_SUB = 4
LRU_STEP = LRU_SUB * TS


def _lru_scans(terms, h0_ref, carry_ref, hs_ref, cfg, tile, direction):
    carry = carry_ref[...]
    ys = [None] * LRU_SUB
    for sb in (range(LRU_SUB - 1, -1, -1) if direction else range(LRU_SUB)):
        is_ctx, jseq, first, last = _block_position(cfg, tile * LRU_SUB + sb)
        start = last if direction else first
        carry = jnp.where(start, jnp.where(is_ctx, 0.0, h0_ref[jseq]), carry)
        ys[sb], carry = _lru_scan(*terms[sb], carry, bool(direction))
        hs_ref[sb] = carry
    carry_ref[...] = carry
    return ys


def _odd_fwd_kernel(u_ref, up_ref, un_ref, cw_ref, cb_ref, wg_ref, bg_ref, lam_ref, h0_ref,
                    yf_ref, uc_ref, hs_ref, carry_ref, *, cfg):
    tile = pl.program_id(0)
    terms = []
    for sb in range(LRU_SUB):
        _, _, first, last = _block_position(cfg, tile * LRU_SUB + sb)
        rows = slice(sb * TS, (sb + 1) * TS)
        prev16 = up_ref[...] if sb == 0 else u_ref[sb * TS - 2 * SUBLANES:sb * TS, :]
        next8 = un_ref[...] if sb == LRU_SUB - 1 else u_ref[(sb + 1) * TS:(sb + 1) * TS + SUBLANES, :]
        uc = _conv_centred(u_ref[rows, :], prev16, next8, first, last, cw_ref, cb_ref)
        uc_ref[rows, :] = uc
        terms.append(_lru_terms(uc, wg_ref, bg_ref, lam_ref))
    for sb, y in enumerate(_lru_scans(terms, h0_ref, carry_ref, hs_ref, cfg, tile, 0)):
        yf_ref[sb * TS:(sb + 1) * TS, :] = y.astype(BF16)


def _odd_bwd_kernel(g_ref, uc_ref, yf_ref, mod_ref, wg_ref, bg_ref, lam_ref, h0_ref, wout_ref,
                    o_ref, hs_ref, carry_ref, *, cfg):
    tile = pl.num_programs(0) - 1 - pl.program_id(0)
    terms = [_lru_terms(uc_ref[sb * TS:(sb + 1) * TS, :], wg_ref, bg_ref, lam_ref) for sb in range(LRU_SUB)]
    y = jnp.concatenate(_lru_scans(terms, h0_ref, carry_ref, hs_ref, cfg, tile, 1), axis=0)
    mix = (g_ref[...].astype(F32) * (yf_ref[...].astype(F32) + y)).astype(BF16)
    grp = _group_of_tile(cfg, tile, LRU_STEP)
    gate = mod_ref[2, pl.ds(grp, 1), :]
    o_ref[...] = gate * jnp.dot(mix, wout_ref[...], preferred_element_type=F32)


def _odd_scans(cfg, l, g, u, mods, cw, cb, wg, bg, lam, h0, w_out):
    n = cfg.n_tok
    assert n % LRU_STEP == 0 and cfg.n_ctx % LRU_STEP == 0 and cfg.tl % LRU_STEP == 0
    nblk = n // TS
    nstep = n // LRU_STEP
    r8 = LRU_STEP // SUBLANES
    last8 = n // SUBLANES - 1

    def specs(rev):
        it = (lambda i: nstep - 1 - i) if rev else (lambda i: i)
        cur = pl.BlockSpec((LRU_STEP, D_RNN), lambda i: (it(i), 0))
        prev = pl.BlockSpec((2 * SUBLANES, D_RNN), lambda i: (jnp.maximum(it(i) * (r8 // 2) - 1, 0), 0))
        nxt = pl.BlockSpec((SUBLANES, D_RNN), lambda i: (jnp.minimum((it(i) + 1) * r8, last8), 0))
        hs = pl.BlockSpec((LRU_SUB, 1, D_RNN), lambda i: (it(i), 0, 0))
        return cur, prev, nxt, hs

    def weights(direction):
        return [pl.BlockSpec((None, LRU_BLOCKS, LRU_BLOCK_W, 2 * LRU_BLOCK_W), lambda i: (direction, 0, 0, 0)),
                pl.BlockSpec((None, 2, D_RNN), lambda i: (direction, 0, 0)),
                pl.BlockSpec((None, 1, D_RNN), lambda i: (direction, 0, 0)),
                pl.BlockSpec((None, cfg.nl, 1, D_RNN), lambda i: (direction, 0, 0, 0))]

    cur, prev, nxt, hs = specs(False)
    tok_shape = jax.ShapeDtypeStruct((n, D_RNN), F32)
    y_f, uc, hs_f = pl.pallas_call(
        functools.partial(_odd_fwd_kernel, cfg=cfg),
        grid=(nstep,),
        in_specs=[cur, prev, nxt, _const_spec((CONV_W, D_RNN)), _const_spec((1, D_RNN))] + weights(0),
        out_specs=[cur, cur, hs],
        out_shape=[jax.ShapeDtypeStruct((n, D_RNN), BF16), tok_shape, jax.ShapeDtypeStruct((nblk, 1, D_RNN), F32)],
        scratch_shapes=[pltpu.VMEM((1, D_RNN), F32)],
        compiler_params=_cparams(1),
        name=f"lru_fwd{l}",
    )(u, u, u, cw, cb, wg, bg, lam, h0)
    cur, _, _, hs = specs(True)
    upd, hs_b = pl.pallas_call(
        functools.partial(_odd_bwd_kernel, cfg=cfg),
        grid=(nstep,),
        in_specs=[cur, cur, cur, _mod_spec(l)] + weights(1) + [_const_spec((D_RNN, D_MODEL))],
        out_specs=[cur, hs],
        out_shape=[tok_shape, jax.ShapeDtypeStruct((nblk, 1, D_RNN), F32)],
        scratch_shapes=[pltpu.VMEM((1, D_RNN), F32)],
        compiler_params=_cparams(1),
        name=f"lru_bwd{l}",
    )(g, uc, y_f, mods, wg, bg, lam, h0, w_out)
    return upd, hs_f, hs_b


def _rope_table(tl):
    pos = np.arange(tl)
    nf = HEAD_DIM // 4
    inv = ROPE_BASE ** (-np.arange(nf, dtype=np.float32) / nf)
    ar = (pos // GRID_W).astype(np.float32)[:, None] * inv
    ac = (pos % GRID_W).astype(np.float32)[:, None] * inv
    ang = np.concatenate([ar, ar, ac, ac] * 2, axis=-1)
    lo = (np.arange(LANES) % (HEAD_DIM // 2)) < nf
    cos, sin = np.cos(ang), np.sin(ang)
    tab = np.concatenate([cos, np.where(lo, -sin, 0.0), np.where(lo, 0.0, sin)], axis=-1)
    return jnp.asarray(tab, F32)


def _dup_heads(w):
    lead = w.shape[:-1]
    w = w.reshape(lead + (A_KV_HEADS, 1, HEAD_DIM))
    return jnp.broadcast_to(w, lead + (A_KV_HEADS, 2, HEAD_DIM)).reshape(lead + (2 * LANES,))


def _pad_low_rank(w):
    pad = jnp.zeros((w.shape[0], LANES - 2 * GLA_RANK), w.dtype)
    return jnp.concatenate([w[:, P_MAIN:], pad], axis=1).astype(BF16)


def _pack_alpha(w_alpha, b_alpha):
    w = jnp.zeros((LANES, 2 * B_QK), F32)
    w = w.at[0:GLA_RANK, 0:B_QK].set(w_alpha[0]).at[GLA_RANK:2 * GLA_RANK, B_QK:].set(w_alpha[1])
    return w.astype(BF16), b_alpha.reshape(1, 2 * B_QK)


def _forward(cfg, x_prompt, x_sample, c, cache_k, cache_v, state_gla, state_lru, c_ctx,
             w_ada, b_ada, norm_mix, norm_ffn, w_in_even, attn_sink, w_alpha, b_alpha, gla_gain,
             w_out_even, w_in_odd, conv_w, conv_b, w_gate_a, b_gate_a, w_gate_x, b_gate_x,
             lru_lambda, w_out_odd, w_ffn_in, w_ffn_out, norm_final):
    assert cfg.tc == TS and cfg.tl % TM == 0 and cfg.n_ctx % TM == 0 and cfg.nl + 1 <= SUBLANES
    x = (x_prompt.reshape(cfg.n_ctx, D_MODEL), x_sample.reshape(cfg.n_lat, D_MODEL))
    cvec = jnp.concatenate([c_ctx[None, :], c, jnp.zeros((SUBLANES - 1 - cfg.nl, D_MODEL), F32)], axis=0)
    mods = _ada(cvec, w_ada, b_ada)
    rope_tab = _rope_table(cfg.tl)
    nf = norm_final.reshape(1, D_MODEL)
    w_out_even_bf = w_out_even.astype(BF16)
    new_k, new_v, new_gla, new_lru = [], [], [], []
    for l in range(DEPTH):
        nw = norm_mix[l].reshape(1, D_MODEL)
        if l % 2 == 0:
            e = l // 2
            za, zb, k_ctx, v_ctx = _even_in(cfg, l, x, mods, nw, w_in_even, _pad_low_rank(w_in_even[e]), rope_tab)
            new_k.append(k_ctx.reshape(cfg.nc, cfg.tc, A_KV_HEADS, HEAD_DIM))
            new_v.append(v_ctx.reshape(cfg.nc, cfg.tc, A_KV_HEADS, HEAD_DIM))
            ck2 = _dup_heads(cache_k[:, e].reshape(cfg.nl, PAST_LEN, A_KV_HEADS * HEAD_DIM)).astype(BF16)
            cv2 = _dup_heads(cache_v[:, e].reshape(cfg.nl, PAST_LEN, A_KV_HEADS * HEAD_DIM)).astype(BF16)
            oa_ctx = _attn_ctx(cfg, l, za, attn_sink[e])
            oa_lat = _attn_lat(cfg, l, za, ck2, cv2, attn_sink[e])
            wal, bal = _pack_alpha(w_alpha[e], b_alpha[e])
            gain = gla_gain[e].reshape(1, B_WIDTH)
            ob_ctx, s_new = _gla_ctx(cfg, l, zb, wal, bal, gain)
            ob_lat = _gla_lat(cfg, l, zb, wal, bal, gain, state_gla[:, e])
            new_gla.append(s_new)
            pre_args = (oa_ctx, oa_lat, ob_ctx, ob_lat, w_out_even_bf)
        else:
            o = l // 2
            g_act, u = _odd_in(cfg, l, x, mods, nw, w_in_odd)
            wg = jnp.concatenate([w_gate_a[o], w_gate_x[o]], axis=-1).astype(BF16)
            bg = 0.5 * jnp.stack([b_gate_a[o], b_gate_x[o]], axis=1)
            lam = lru_lambda[o].reshape(2, 1, D_RNN)
            h0 = jnp.transpose(state_lru[:, o], (1, 0, 2)).reshape(2, cfg.nl, 1, D_RNN)
            upd, hs_f, hs_b = _odd_scans(cfg, l, g_act, u, mods, 0.5 * conv_w[o], 0.5 * conv_b[o].reshape(1, D_RNN),
                                         wg, bg, lam, h0, w_out_odd[o].astype(BF16))
            new_lru.append(jnp.stack([hs_f[:cfg.nc, 0], hs_b[:cfg.nc, 0]], axis=1))
            pre_args = (upd,)
        x = _ffn(cfg, l, x, pre_args, mods, norm_ffn[l].reshape(1, D_MODEL), w_ffn_in, w_ffn_out, nf,
                 final=(l == DEPTH - 1))
    y_prompt = x[0].reshape(cfg.nc, cfg.tc, D_MODEL)
    y_sample = x[1].reshape(cfg.nl, cfg.tl, D_MODEL)
    return (y_prompt, y_sample, jnp.stack(new_k, axis=1), jnp.stack(new_v, axis=1),
            jnp.stack(new_gla, axis=1), jnp.stack(new_lru, axis=1))


def kernel(x_prompt, x_sample, c, cache_k, cache_v, state_gla, state_lru, c_ctx, w_ada, b_ada, norm_mix, norm_ffn, w_in_even, attn_sink, w_alpha, b_alpha, gla_gain, w_out_even, w_in_odd, conv_w, conv_b, w_gate_a, b_gate_a, w_gate_x, b_gate_x, lru_lambda, w_out_odd, w_ffn_in, w_ffn_out, norm_final):
    cfg = Cfg(nc=x_prompt.shape[0], tc=x_prompt.shape[1], nl=x_sample.shape[0], tl=x_sample.shape[1])
    return _forward(cfg, x_prompt, x_sample, c, cache_k, cache_v, state_gla, state_lru, c_ctx,
                    w_ada, b_ada, norm_mix, norm_ffn, w_in_even, attn_sink, w_alpha, b_alpha, gla_gain,
                    w_out_even, w_in_odd, conv_w, conv_b, w_gate_a, b_gate_a, w_gate_x, b_gate_x,
                    lru_lambda, w_out_odd, w_ffn_in, w_ffn_out, norm_final)
```

```python
import functools
from typing import NamedTuple

import jax
import jax.numpy as jnp
import numpy as np
from jax import lax
from jax.experimental import pallas as pl
from jax.experimental.pallas import tpu as pltpu

F32 = jnp.float32
BF16 = jnp.bfloat16

D_MODEL = 1024
DEPTH = 4
EPS = 1e-6

HEAD_DIM = 64
A_Q_HEADS = 8
A_KV_HEADS = 2
A_GROUPS = A_Q_HEADS // A_KV_HEADS
A_WIDTH = A_Q_HEADS * HEAD_DIM
ATT_BLOCK = 128
ROPE_BASE = 10000.0
GRID_W = 64
NEG_INF = -1e30
PAST_LEN = 256

B_HEADS = 4
B_DK = 64
B_DV = 128
B_QK = B_HEADS * B_DK
B_WIDTH = B_HEADS * B_DV
GLA_RANK = 16
GLA_NORMALIZER = 16.0
GLA_CHUNK = 64

D_RNN = D_MODEL
LRU_BLOCK_W = 256
LRU_BLOCKS = D_RNN // LRU_BLOCK_W
CONV_W = 4
LRU_C = 8.0
LOG2_E = 1.4426950408889634

D_FF = 2816

LANES = 128
SUBLANES = 8

ZA_W = A_WIDTH + 2 * 2 * LANES
ZB_LR = B_QK + B_QK + B_WIDTH + B_WIDTH
ZB_W = ZB_LR + LANES
P_A = A_WIDTH + 2 * LANES
P_MAIN = P_A + ZB_LR

TM = 512
TS = 256
FF_CHUNK = 256

VMEM_LIMIT = 56 * 1024 * 1024


class Cfg(NamedTuple):
    nc: int
    tc: int
    nl: int
    tl: int

    @property
    def n_ctx(self):
        return self.nc * self.tc

    @property
    def n_lat(self):
        return self.nl * self.tl

    @property
    def n_tok(self):
        return self.n_ctx + self.n_lat


def _cparams(n_axes):
    return pltpu.CompilerParams(dimension_semantics=("arbitrary",) * n_axes,
                                vmem_limit_bytes=VMEM_LIMIT)


def _const_spec(shape):
    nd = len(shape)
    return pl.BlockSpec(shape, lambda *_: (0,) * nd)


def _silu(x):
    hx = 0.5 * x
    return hx + hx * jnp.tanh(hx)


def _group_of_tile(cfg, i, tile):
    nct = cfg.n_ctx // tile
    per_lat = cfg.tl // tile
    return jnp.where(i < nct, 0, 1 + jnp.maximum(i - nct, 0) // per_lat)


def _modnorm(x, nw, shift, scale):
    ms = jnp.mean(x * x, axis=-1, keepdims=True)
    y = x * lax.rsqrt(ms + EPS) * nw
    return y * (1.0 + scale) + shift


def _mod_rows(mod_ref, g, first):
    return tuple(mod_ref[first + j, pl.ds(g, 1), :] for j in range(3))


ADA_CHUNKS = 3


def _ada_kernel(c_ref, w_ref, b_ref, o_ref):
    s = _silu(c_ref[...]).astype(BF16)
    for j in range(ADA_CHUNKS):
        o_ref[j] = jnp.dot(s, w_ref[:, j * D_MODEL:(j + 1) * D_MODEL].astype(BF16),
                           preferred_element_type=F32) + b_ref[j]


def _ada(cvec, w_ada, b_ada):
    b4 = b_ada.reshape(DEPTH, 6, 1, D_MODEL)
    return pl.pallas_call(
        _ada_kernel,
        grid=(DEPTH, 6 // ADA_CHUNKS),
        in_specs=[
            _const_spec((SUBLANES, D_MODEL)),
            pl.BlockSpec((None, D_MODEL, ADA_CHUNKS * D_MODEL), lambda l, j: (l, 0, j)),
            pl.BlockSpec((None, ADA_CHUNKS, 1, D_MODEL), lambda l, j: (l, j, 0, 0)),
        ],
        out_specs=pl.BlockSpec((None, ADA_CHUNKS, SUBLANES, D_MODEL), lambda l, j: (l, j, 0, 0)),
        out_shape=jax.ShapeDtypeStruct((DEPTH, 6, SUBLANES, D_MODEL), F32),
        compiler_params=_cparams(2),
        name="ada",
    )(cvec, w_ada, b4)


def _mod_spec(l):
    return pl.BlockSpec((None, 6, SUBLANES, D_MODEL), lambda *_: (l, 0, 0, 0))


def _ffn_tile(x, nw, shift, scale, gate, win_ref, wout_ref):
    h = _modnorm(x, nw, shift, scale).astype(BF16)
    acc = None
    for c in range(D_FF // FF_CHUNK):
        lo, hi = c * FF_CHUNK, (c + 1) * FF_CHUNK
        g = jnp.dot(h, win_ref[:, lo:hi].astype(BF16), preferred_element_type=F32)
        u = jnp.dot(h, win_ref[:, D_FF + lo:D_FF + hi].astype(BF16), preferred_element_type=F32)
        a = (_silu(g) * u).astype(BF16)
        part = jnp.dot(a, wout_ref[lo:hi, :].astype(BF16), preferred_element_type=F32)
        acc = part if acc is None else acc + part
    return x + gate * acc


def _x_args(cfg, x, tile=TM):
    if not isinstance(x, tuple):
        return [x], [pl.BlockSpec((tile, D_MODEL), lambda i: (i, 0))]
    nct = cfg.n_ctx // tile
    return list(x), [pl.BlockSpec((tile, D_MODEL), lambda i: (jnp.minimum(i, nct - 1), 0)),
                     pl.BlockSpec((tile, D_MODEL), lambda i: (jnp.maximum(i - nct, 0), 0))]


def _load_x(cfg, x_refs, i, tile=TM):
    if len(x_refs) == 1:
        return x_refs[0][...]
    return jnp.where(i < cfg.n_ctx // tile, x_refs[0][...], x_refs[1][...])


def _to_scan_order(x):
    nb, d = x.shape[0] // TS, x.shape[1]
    return jnp.swapaxes(x.reshape(nb, SUBLANES, TS // SUBLANES, d), 1, 2).reshape(nb * TS, d)


def _from_scan_order(x):
    nb, d = x.shape[0] // TS, x.shape[1]
    return jnp.swapaxes(x.reshape(nb, TS // SUBLANES, SUBLANES, d), 1, 2).reshape(nb * TS, d)


def _ffn_kernel(*refs, cfg, n_x, pre, final):
    x_refs, refs = refs[:n_x], refs[n_x:]
    i = pl.program_id(0)
    g = _group_of_tile(cfg, i, TM)
    x = _load_x(cfg, x_refs, i)
    if pre == "even":
        (oac_ref, oal_ref, obc_ref, obl_ref, wmix_ref), refs = refs[:5], refs[5:]
    else:
        d_ref, refs = refs[0], refs[1:]
    mod_ref, nw_ref, win_ref, wout_ref, nf_ref = refs[:5]
    outs = refs[5:]
    if pre == "even":
        is_ctx_tile = i < cfg.n_ctx // TM
        o = jnp.concatenate([jnp.where(is_ctx_tile, oac_ref[...], oal_ref[...]),
                             jnp.where(is_ctx_tile, obc_ref[...], obl_ref[...])], axis=1)
        x = x + mod_ref[2, pl.ds(g, 1), :] * jnp.dot(o, wmix_ref[...], preferred_element_type=F32)
    else:
        x = x + _from_scan_order(d_ref[...])
    shift, scale, gate = _mod_rows(mod_ref, g, 3)
    y = _ffn_tile(x, nw_ref[...], shift, scale, gate, win_ref, wout_ref)
    if not final:
        outs[0][...] = y
        return
    ms = jnp.mean(y * y, axis=-1, keepdims=True)
    y = y * lax.rsqrt(ms + EPS) * nf_ref[...]
    is_ctx = i < cfg.n_ctx // TM

    @pl.when(is_ctx)
    def _():
        outs[0][...] = y

    @pl.when(jnp.logical_not(is_ctx))
    def _():
        outs[1][...] = y


def _ffn(cfg, l, x, pre_args, mods, nw, win, wout, nf, final):
    n = cfg.n_tok
    nct = cfg.n_ctx // TM
    tok = lambda w: pl.BlockSpec((TM, w), lambda i: (i, 0))
    ctx = lambda w: pl.BlockSpec((TM, w), lambda i: (jnp.minimum(i, nct - 1), 0))
    lat = lambda w: pl.BlockSpec((TM, w), lambda i: (jnp.maximum(i - nct, 0), 0))
    x_ops, x_specs = _x_args(cfg, x)
    if l % 2 == 0:
        pre = "even"
        pre_specs = [ctx(A_WIDTH), lat(A_WIDTH), ctx(B_WIDTH), lat(B_WIDTH),
                     pl.BlockSpec((None, A_WIDTH + B_WIDTH, D_MODEL), lambda i: (l // 2, 0, 0))]
    else:
        pre, pre_specs = "odd", [tok(D_MODEL)]
    if final:
        out_specs = [pl.BlockSpec((TM, D_MODEL), lambda i: (jnp.minimum(i, nct - 1), 0)),
                     pl.BlockSpec((TM, D_MODEL), lambda i: (jnp.maximum(i - nct, 0), 0))]
        out_shape = [jax.ShapeDtypeStruct((cfg.n_ctx, D_MODEL), F32),
                     jax.ShapeDtypeStruct((cfg.n_lat, D_MODEL), F32)]
    else:
        out_specs, out_shape = tok(D_MODEL), jax.ShapeDtypeStruct((n, D_MODEL), F32)
    return pl.pallas_call(
        functools.partial(_ffn_kernel, cfg=cfg, n_x=len(x_ops), pre=pre, final=final),
        grid=(n // TM,),
        in_specs=x_specs + pre_specs + [
            _mod_spec(l), _const_spec((1, D_MODEL)),
            pl.BlockSpec((None, D_MODEL, 2 * D_FF), lambda i: (l, 0, 0), pipeline_mode=pl.Buffered(1)),
            pl.BlockSpec((None, D_FF, D_MODEL), lambda i: (l, 0, 0), pipeline_mode=pl.Buffered(1)),
            _const_spec((1, D_MODEL))],
        out_specs=out_specs,
        out_shape=out_shape,
        compiler_params=_cparams(1),
        name=f"ffn{l}",
    )(*x_ops, *pre_args, mods, nw, win, wout, nf)


TM_EVEN_IN = 1024


def _even_in_kernel(*refs, cfg, n_x):
    x_refs, (mod_ref, nw_ref, w_ref, wlr_ref, tab_ref, za_ref, zb_ref, k_ref, v_ref) = refs[:n_x], refs[n_x:]
    i = pl.program_id(0)
    g = _group_of_tile(cfg, i, TM_EVEN_IN)
    shift, scale, _ = _mod_rows(mod_ref, g, 0)
    h = _modnorm(_load_x(cfg, x_refs, i, TM_EVEN_IN), nw_ref[...], shift, scale).astype(BF16)
    def proj(lo_col, hi_col):
        return jnp.dot(h, w_ref[:, lo_col:hi_col].astype(BF16), preferred_element_type=F32)

    is_lat = g != 0
    cos = jnp.where(is_lat, tab_ref[:, 0:LANES], 1.0)
    sin_a = jnp.where(is_lat, tab_ref[:, LANES:2 * LANES], 0.0)
    sin_b = jnp.where(is_lat, tab_ref[:, 2 * LANES:3 * LANES], 0.0)

    def rope(a):
        parts = []
        for j in range(a.shape[1] // LANES):
            t = a[:, j * LANES:(j + 1) * LANES]
            parts.append(t * cos + pltpu.roll(t, LANES - 16, 1) * sin_a + pltpu.roll(t, 16, 1) * sin_b)
        return jnp.concatenate(parts, axis=1)

    lo = lax.broadcasted_iota(jnp.int32, (TM_EVEN_IN, LANES), 1) < HEAD_DIM

    def twice(a):
        swapped = pltpu.roll(a, HEAD_DIM, 1)
        return jnp.concatenate([jnp.where(lo, a, swapped), jnp.where(lo, swapped, a)], axis=1).astype(BF16)

    q = proj(0, A_WIDTH) * (LOG2_E * HEAD_DIM ** -0.5)
    za_ref[:, 0:A_WIDTH] = rope(q).astype(BF16)
    kv = proj(A_WIDTH, P_A)
    k, v = kv[:, 0:LANES], kv[:, LANES:2 * LANES]
    za_ref[:, A_WIDTH:A_WIDTH + 2 * LANES] = twice(rope(k))
    za_ref[:, A_WIDTH + 2 * LANES:ZA_W] = twice(v)
    zb_ref[:, 0:B_QK] = proj(P_A, P_A + B_QK) * (B_DK ** -0.5)
    for c0 in range(B_QK, ZB_LR, 2 * LANES):
        zb_ref[:, c0:c0 + 2 * LANES] = proj(P_A + c0, P_A + c0 + 2 * LANES)
    zb_ref[:, ZB_LR:ZB_W] = jnp.dot(h, wlr_ref[...], preferred_element_type=F32)

    @pl.when(g == 0)
    def _():
        k_ref[...] = k
        v_ref[...] = v


def _even_in(cfg, l, x, mods, nw, w_all, w_lr, rope_tab):
    n = cfg.n_tok
    tile = TM_EVEN_IN
    assert cfg.n_ctx % tile == 0 and cfg.tl % tile == 0
    nct = cfg.n_ctx // tile
    per_lat = cfg.tl // tile
    tok = lambda w: pl.BlockSpec((tile, w), lambda i: (i, 0))
    x_ops, x_specs = _x_args(cfg, x, tile)
    ctx_kv = pl.BlockSpec((tile, LANES), lambda i: (jnp.minimum(i, nct - 1), 0))
    return pl.pallas_call(
        functools.partial(_even_in_kernel, cfg=cfg, n_x=len(x_ops)),
        grid=(n // tile,),
        in_specs=x_specs + [_mod_spec(l), _const_spec((1, D_MODEL)),
                            pl.BlockSpec((None, D_MODEL, P_MAIN), lambda i: (l // 2, 0, 0),
                                         pipeline_mode=pl.Buffered(1)),
                            pl.BlockSpec((None, D_MODEL, LANES), lambda i: (l // 2, 0, 0)),
                            pl.BlockSpec((tile, 3 * LANES), lambda i: (jnp.maximum(i - nct, 0) % per_lat, 0))],
        out_specs=[tok(ZA_W), tok(ZB_W), ctx_kv, ctx_kv],
        out_shape=[jax.ShapeDtypeStruct((n, ZA_W), BF16),
                   jax.ShapeDtypeStruct((n, ZB_W), F32),
                   jax.ShapeDtypeStruct((cfg.n_ctx, LANES), F32),
                   jax.ShapeDtypeStruct((cfg.n_ctx, LANES), F32)],
        compiler_params=_cparams(1),
        name=f"even_in{l}",
    )(*x_ops, mods, nw, w_all, w_lr, rope_tab)


def _sink_attention(units, sink_ref):
    scores = []
    for q, k_all, _, mask, kvh in units:
        tq = q.shape[0]
        lo = lax.broadcasted_iota(jnp.int32, (tq, LANES), 1) < HEAD_DIM
        qs = []
        for gq in range(A_GROUPS):
            h = kvh * A_GROUPS + gq
            pair = q[:, (h // 2) * LANES:(h // 2 + 1) * LANES]
            keep = lo if h % 2 == 0 else jnp.logical_not(lo)
            qs.append(jnp.where(keep, pair, jnp.zeros_like(pair)))
        qst = jnp.concatenate(qs, axis=0)
        s = lax.dot_general(qst, k_all, (((1,), (1,)), ((), ())), preferred_element_type=F32)
        if mask is not None:
            nfree = s.shape[1] - mask.shape[1]
            s = jnp.concatenate([s[:, :nfree], jnp.where(mask, s[:, nfree:], NEG_INF)], axis=1)
        scores.append(s)
    probs = []
    for (q, _, _, _, kvh), s in zip(units, scores):
        tq = q.shape[0]
        rows = lax.broadcasted_iota(jnp.int32, (A_GROUPS * tq, 1), 0) // tq
        sink = jnp.zeros((A_GROUPS * tq, 1), F32)
        for gq in range(A_GROUPS):
            sink = jnp.where(rows == gq, sink_ref[kvh * A_GROUPS + gq] * LOG2_E, sink)
        m = jnp.maximum(jnp.max(s, axis=-1, keepdims=True), sink)
        p = jnp.exp2(s - m)
        den = jnp.sum(p, axis=-1, keepdims=True) + jnp.exp2(sink - m)
        probs.append((p.astype(BF16), den))
    outs = []
    for (q, _, v_all, _, _), (p, den) in zip(units, probs):
        tq = q.shape[0]
        lo = lax.broadcasted_iota(jnp.int32, (tq, LANES), 1) < HEAD_DIM
        o = jnp.dot(p, v_all, preferred_element_type=F32) / den
        pairs = [jnp.where(lo, o[(2 * pr) * tq:(2 * pr + 1) * tq], o[(2 * pr + 1) * tq:(2 * pr + 2) * tq])
                 for pr in range(2)]
        outs.append(jnp.concatenate(pairs, axis=1).astype(BF16))
    return outs


def _attn_ctx_kernel(sink_ref, q_ref, k_ref, v_ref, o_ref):
    q = q_ref[...]
    units = [(q, k_ref[:, kvh * LANES:(kvh + 1) * LANES], v_ref[:, kvh * LANES:(kvh + 1) * LANES], None, kvh)
             for kvh in range(A_KV_HEADS)]
    for kvh, o in enumerate(_sink_attention(units, sink_ref)):
        o_ref[:, 2 * kvh * LANES:2 * (kvh + 1) * LANES] = o


def _attn_ctx(cfg, l, za, sink):
    tc = cfg.tc
    return pl.pallas_call(
        _attn_ctx_kernel,
        grid=(cfg.nc,),
        in_specs=[pl.BlockSpec(memory_space=pltpu.SMEM),
                  pl.BlockSpec((tc, A_WIDTH), lambda b: (b, 0)),
                  pl.BlockSpec((tc, 2 * LANES), lambda b: (b, A_WIDTH // (2 * LANES))),
                  pl.BlockSpec((tc, 2 * LANES), lambda b: (b, A_WIDTH // (2 * LANES) + 1))],
        out_specs=pl.BlockSpec((tc, A_WIDTH), lambda b: (b, 0)),
        out_shape=jax.ShapeDtypeStruct((cfg.n_ctx, A_WIDTH), BF16),
        compiler_params=_cparams(1),
        name=f"attn_ctx{l}",
    )(sink, za, za, za)


ATT_STEP = 4 * ATT_BLOCK
ATT_WIN = 3 * ATT_BLOCK


def _attn_lat_kernel(sink_ref, q_ref, k_ref, v_ref, ck_ref, cv_ref, o_ref):
    n = pl.program_id(1)
    tl = k_ref.shape[0]
    shape = (A_GROUPS * ATT_BLOCK, ATT_WIN)
    rc = (lax.broadcasted_iota(jnp.int32, shape, 0) % ATT_BLOCK - lax.broadcasted_iota(jnp.int32, shape, 1)
          + ATT_BLOCK)
    units = []
    for j in range(ATT_STEP // ATT_BLOCK):
        q0 = (n * (ATT_STEP // ATT_BLOCK) + j) * ATT_BLOCK
        start = pl.multiple_of(jnp.clip(q0 - ATT_BLOCK, 0, tl - ATT_WIN), ATT_BLOCK)
        mask = lax.bitcast_convert_type(rc + (q0 - start), jnp.uint32) <= jnp.uint32(2 * ATT_BLOCK)
        q = q_ref[j * ATT_BLOCK:(j + 1) * ATT_BLOCK, :]
        for kvh in range(A_KV_HEADS):
            sl = slice(kvh * LANES, (kvh + 1) * LANES)
            k_all = jnp.concatenate([ck_ref[:, sl], k_ref[pl.ds(start, ATT_WIN), sl]], axis=0)
            v_all = jnp.concatenate([cv_ref[:, sl], v_ref[pl.ds(start, ATT_WIN), sl]], axis=0)
            units.append((q, k_all, v_all, mask, kvh))
    for u, o in enumerate(_sink_attention(units, sink_ref)):
        j, kvh = divmod(u, A_KV_HEADS)
        o_ref[j * ATT_BLOCK:(j + 1) * ATT_BLOCK, 2 * kvh * LANES:2 * (kvh + 1) * LANES] = o


def _attn_lat(cfg, l, za, cache_k2, cache_v2, sink):
    assert cfg.n_ctx % cfg.tl == 0 and cfg.tl % ATT_STEP == 0 and cfg.tl >= ATT_WIN
    steps = cfg.tl // ATT_STEP
    base = cfg.n_ctx // ATT_STEP
    kcol = A_WIDTH // (2 * LANES)
    seq = lambda col: pl.BlockSpec((cfg.tl, 2 * LANES), lambda b, n: (cfg.n_ctx // cfg.tl + b, col))
    cache = pl.BlockSpec((None, PAST_LEN, 2 * LANES), lambda b, n: (b, 0, 0))
    return pl.pallas_call(
        _attn_lat_kernel,
        grid=(cfg.nl, steps),
        in_specs=[pl.BlockSpec(memory_space=pltpu.SMEM),
                  pl.BlockSpec((ATT_STEP, A_WIDTH), lambda b, n: (base + b * steps + n, 0)),
                  seq(kcol), seq(kcol + 1), cache, cache],
        out_specs=pl.BlockSpec((ATT_STEP, A_WIDTH), lambda b, n: (b * steps + n, 0)),
        out_shape=jax.ShapeDtypeStruct((cfg.n_lat, A_WIDTH), BF16),
        compiler_params=_cparams(2),
        name=f"attn_lat{l}",
    )(sink, za, za, za, cache_k2, cache_v2)


def _gla_chunk_cumsum(la, reverse):
    cn = GLA_CHUNK
    row = lax.broadcasted_iota(jnp.int32, (cn, B_QK), 0)
    b = la
    s = 1
    while s < cn:
        if reverse:
            b = b + jnp.where(row < cn - s, pltpu.roll(b, cn - s, 0), 0.0)
        else:
            b = b + jnp.where(row >= s, pltpu.roll(b, s, 0), 0.0)
        s *= 2
    return b


def _gla_chunk_prepare(q, k, v, b, reverse):
    cn = GLA_CHUNK
    row = lax.broadcasted_iota(jnp.int32, (cn, B_QK), 0)
    b_last = b[0:1, :] if reverse else b[cn - 1:cn, :]
    q_e = q * jnp.exp(b)
    k_e = (k * jnp.exp(-b)).astype(BF16)
    k_s = (k * jnp.exp(b_last - b)).astype(BF16)
    vb = v.astype(BF16)
    row_t = row[:, 0:LANES]
    lane_t = lax.broadcasted_iota(jnp.int32, (cn, LANES), 1)
    lo = lane_t < B_DK
    col = lane_t % cn
    tri = (col >= row_t) if reverse else (col <= row_t)
    q_tiles, att_tiles = [], []
    for t in range(B_HEADS // 2):
        sl = slice(t * LANES, (t + 1) * LANES)
        k_t = k_e[:, sl]
        zero = jnp.zeros_like(k_t)
        kbd = jnp.concatenate([jnp.where(lo, k_t, zero), jnp.where(lo, zero, k_t)], axis=0)
        q_t = q_e[:, sl]
        att = lax.dot_general(q_t.astype(BF16), kbd, (((1,), (1,)), ((), ())), preferred_element_type=F32)
        q_tiles.append(q_t)
        att_tiles.append(jnp.where(tri, att, 0.0))
    blocks = []
    for t in range(B_HEADS // 2):
        upd = lax.dot_general(k_s[:, t * LANES:(t + 1) * LANES], vb[:, 2 * t * B_DV:2 * (t + 1) * B_DV],
                              (((0,), (0,)), ((), ())), preferred_element_type=F32)
        blocks += [upd[g * B_DK:(g + 1) * B_DK, g * B_DV:(g + 1) * B_DV] for g in range(2)]
    own = jnp.concatenate(blocks, axis=0)
    decay = jnp.broadcast_to(jnp.exp(b_last), (LANES, B_QK)).T
    return q_tiles, att_tiles, vb, own, decay


def _gla_chunk_output(q_tiles, att_tiles, vb, st):
    lo = lax.broadcasted_iota(jnp.int32, (GLA_CHUNK, LANES), 1) < B_DK
    st_b = st.astype(BF16)
    outs = []
    for h in range(B_HEADS):
        att, q_t = att_tiles[h // 2], q_tiles[h // 2]
        x = (jnp.where(lo, att, pltpu.roll(q_t, B_DK, 1)) if h % 2 == 0
             else jnp.where(lo, pltpu.roll(att, B_DK, 1), q_t))
        rhs = jnp.concatenate([vb[:, h * B_DV:(h + 1) * B_DV], st_b[h * B_DK:(h + 1) * B_DK, :]], axis=0)
        outs.append(jnp.dot(x.astype(BF16), rhs, preferred_element_type=F32))
    return jnp.concatenate(outs, axis=1)


def _gla_log_decay(lr_pad, wal_ref, bal_ref, direction):
    z = jnp.dot(lr_pad.astype(BF16), wal_ref[:, direction * B_QK:(direction + 1) * B_QK],
                preferred_element_type=F32) + bal_ref[:, direction * B_QK:(direction + 1) * B_QK]
    return jax.nn.log_sigmoid(z) / GLA_NORMALIZER


def _gla_blocks(jobs, wal_ref, bal_ref):
    prepared = []
    for qkv_ref, lr_ref, _, direction in jobs:
        nchunk = qkv_ref.shape[0] // GLA_CHUNK
        la_all = _gla_log_decay(lr_ref[...], wal_ref, bal_ref, direction)
        b_all = jnp.concatenate(
            [_gla_chunk_cumsum(la_all[r0:r0 + GLA_CHUNK, :], bool(direction))
             for r0 in range(0, qkv_ref.shape[0], GLA_CHUNK)], axis=0)
        chunks = {}
        for ci in range(nchunk):
            rs = slice(ci * GLA_CHUNK, (ci + 1) * GLA_CHUNK)
            chunks[ci] = _gla_chunk_prepare(qkv_ref[rs, 0:B_QK], qkv_ref[rs, B_QK:2 * B_QK],
                                            qkv_ref[rs, 2 * B_QK:2 * B_QK + B_WIDTH], b_all[rs, :],
                                            bool(direction))
        prepared.append(chunks)
    results = []
    for (qkv_ref, _, st, direction), chunks in zip(jobs, prepared):
        nchunk = len(chunks)
        states = {}
        for ci in (range(nchunk - 1, -1, -1) if direction else range(nchunk)):
            states[ci] = st
            _, _, _, own, decay = chunks[ci]
            st = st * decay + own
        outs = [_gla_chunk_output(*chunks[ci][:3], states[ci]) for ci in range(nchunk)]
        results.append((jnp.concatenate(outs, axis=0), st))
    return results


def _stack_state(s_ref):
    return s_ref[...].reshape(B_QK, B_DV)


def _unstack_state(st, s_ref):
    s_ref[...] = st.reshape(B_HEADS, B_DK, B_DV)


def _gla_finish(o, r, gain):
    parts = []
    for h in range(B_HEADS):
        oh = o[:, h * B_DV:(h + 1) * B_DV]
        parts.append(oh * lax.rsqrt(jnp.mean(oh * oh, axis=-1, keepdims=True) + EPS))
    return (jnp.concatenate(parts, axis=1) * gain * _silu(r)).astype(BF16)


def _gla_ctx_kernel(qkv_ref, r_ref, lr_ref, wal_ref, bal_ref, gain_ref, ob_ref, snew_ref):
    zero = jnp.zeros((B_QK, B_DV), F32)
    (o_f, st_f), (o_b, st_b) = _gla_blocks([(qkv_ref, lr_ref, zero, 0), (qkv_ref, lr_ref, zero, 1)],
                                           wal_ref, bal_ref)
    _unstack_state(st_f, snew_ref.at[0])
    _unstack_state(st_b, snew_ref.at[1])
    ob_ref[...] = _gla_finish(o_f + o_b, r_ref[...], gain_ref[...])


def _gla_ctx(cfg, l, zb, wal, bal, gain):
    tc = cfg.tc
    return pl.pallas_call(
        _gla_ctx_kernel,
        grid=(cfg.nc,),
        in_specs=[pl.BlockSpec((tc, 2 * B_QK + B_WIDTH), lambda b: (b, 0)),
                  pl.BlockSpec((tc, B_WIDTH), lambda b: (b, (2 * B_QK + B_WIDTH) // B_WIDTH)),
                  pl.BlockSpec((tc, LANES), lambda b: (b, ZB_LR // LANES)),
                  _const_spec((LANES, 2 * B_QK)), _const_spec((1, 2 * B_QK)), _const_spec((1, B_WIDTH))],
        out_specs=[pl.BlockSpec((tc, B_WIDTH), lambda b: (b, 0)),
                   pl.BlockSpec((None, 2, B_HEADS, B_DK, B_DV), lambda b: (b, 0, 0, 0, 0))],
        out_shape=[jax.ShapeDtypeStruct((cfg.n_ctx, B_WIDTH), BF16),
                   jax.ShapeDtypeStruct((cfg.nc, 2, B_HEADS, B_DK, B_DV), F32)],
        compiler_params=_cparams(1),
        name=f"gla_ctx{l}",
    )(zb, zb, zb, wal, bal, gain)


GLA_STEP = 512


def _gla_lat_kernel(qkvf_ref, lrf_ref, rf_ref, qkvb_ref, lrb_ref, rb_ref, wal_ref, bal_ref, gain_ref,
                    s0_ref, ob_ref, stf_ref, stb_ref, of_ref, obk_ref):
    n = pl.program_id(1)
    nb = pl.num_programs(1)

    @pl.when(n == 0)
    def _():
        stf_ref[...] = _stack_state(s0_ref.at[0])
        stb_ref[...] = _stack_state(s0_ref.at[1])
    (o_f, st_f), (o_b, st_b) = _gla_blocks([(qkvf_ref, lrf_ref, stf_ref[...], 0),
                                            (qkvb_ref, lrb_ref, stb_ref[...], 1)], wal_ref, bal_ref)
    stf_ref[...] = st_f
    stb_ref[...] = st_b
    rows_f = pl.ds(pl.multiple_of(n * GLA_STEP, GLA_STEP), GLA_STEP)
    rows_b = pl.ds(pl.multiple_of((nb - 1 - n) * GLA_STEP, GLA_STEP), GLA_STEP)
    of_ref[rows_f, :] = o_f
    obk_ref[rows_b, :] = o_b

    @pl.when(2 * n >= nb)
    def _():
        ob_ref[rows_f, :] = _gla_finish(o_f + obk_ref[rows_f, :], rf_ref[...], gain_ref[...])
        ob_ref[rows_b, :] = _gla_finish(of_ref[rows_b, :] + o_b, rb_ref[...], gain_ref[...])


def _gla_lat(cfg, l, zb, wal, bal, gain, s0):
    nbl = cfg.tl // GLA_STEP
    base = cfg.n_ctx // GLA_STEP
    assert nbl % 2 == 0 and cfg.tl % GLA_STEP == 0 and cfg.n_ctx % GLA_STEP == 0

    def blk(width, col, rev):
        def imap(b, n):
            return (base + b * nbl + (nbl - 1 - n if rev else n), col)
        return pl.BlockSpec((GLA_STEP, width), imap)

    qkv_w = 2 * B_QK + B_WIDTH
    per_dir = lambda rev: [blk(qkv_w, 0, rev), blk(LANES, ZB_LR // LANES, rev), blk(B_WIDTH, qkv_w // B_WIDTH, rev)]
    return pl.pallas_call(
        _gla_lat_kernel,
        grid=(cfg.nl, nbl),
        in_specs=per_dir(False) + per_dir(True) + [
            _const_spec((LANES, 2 * B_QK)), _const_spec((1, 2 * B_QK)), _const_spec((1, B_WIDTH)),
            pl.BlockSpec((None, 2, B_HEADS, B_DK, B_DV), lambda b, n: (b, 0, 0, 0, 0))],
        out_specs=pl.BlockSpec((cfg.tl, B_WIDTH), lambda b, n: (b, 0)),
        out_shape=jax.ShapeDtypeStruct((cfg.n_lat, B_WIDTH), BF16),
        scratch_shapes=[pltpu.VMEM((B_QK, B_DV), F32), pltpu.VMEM((B_QK, B_DV), F32),
                        pltpu.VMEM((cfg.tl, B_WIDTH), F32), pltpu.VMEM((cfg.tl, B_WIDTH), F32)],
        compiler_params=_cparams(2),
        name=f"gla_lat{l}",
    )(zb, zb, zb, zb, zb, zb, wal, bal, gain, s0)


LRU_CHUNK = TS // SUBLANES


TM_ODD_IN = 1024
ODD_IN_COLS = 256


def _odd_in_kernel(x_ref, mod_ref, nw_ref, w_ref, g_ref, u_ref, *, cfg):
    g = _group_of_tile(cfg, pl.program_id(0), TM_ODD_IN)
    shift, scale, _ = _mod_rows(mod_ref, g, 0)
    h = _modnorm(_to_scan_order(x_ref[...]), nw_ref[...], shift, scale).astype(BF16)
    for c0 in range(0, D_RNN, ODD_IN_COLS):
        z = jnp.dot(h, w_ref[:, c0:c0 + ODD_IN_COLS].astype(BF16), preferred_element_type=F32)
        g_ref[:, c0:c0 + ODD_IN_COLS] = jax.nn.gelu(z).astype(BF16)
    for c0 in range(0, D_RNN, ODD_IN_COLS):
        u_ref[:, c0:c0 + ODD_IN_COLS] = jnp.dot(h, w_ref[:, D_RNN + c0:D_RNN + c0 + ODD_IN_COLS].astype(BF16),
                                                preferred_element_type=F32)


def _odd_in(cfg, l, x, mods, nw, w):
    n = cfg.n_tok
    return pl.pallas_call(
        functools.partial(_odd_in_kernel, cfg=cfg),
        grid=(n // TM_ODD_IN,),
        in_specs=[pl.BlockSpec((TM_ODD_IN, D_MODEL), lambda i: (i, 0)), _mod_spec(l), _const_spec((1, D_MODEL)),
                  pl.BlockSpec((None, D_MODEL, 2 * D_RNN), lambda i: (l // 2, 0, 0),
                               pipeline_mode=pl.Buffered(1))],
        out_specs=[pl.BlockSpec((TM_ODD_IN, D_RNN), lambda i: (i, 0))] * 2,
        out_shape=[jax.ShapeDtypeStruct((n, D_RNN), BF16), jax.ShapeDtypeStruct((n, D_RNN), F32)],
        compiler_params=_cparams(1),
        name=f"odd_in{l}",
    )(x, mods, nw, w)


def _block_position(cfg, ib):
    ncb = cfg.n_ctx // TS
    bpc = cfg.tc // TS
    bpl = cfg.tl // TS
    is_ctx = ib < ncb
    jl = jnp.maximum(ib - ncb, 0)
    pos = jnp.where(is_ctx, ib % bpc, jl % bpl)
    per = jnp.where(is_ctx, bpc, bpl)
    return is_ctx, jl // bpl, pos == 0, pos == per - 1


def _conv_centred(u, prev16, next8, first, last, cw_ref, cb_ref):
    sub = lax.broadcasted_iota(jnp.int32, (SUBLANES, D_RNN), 0)
    lastrow = TS - SUBLANES

    def from_prev_chunk(own, other):
        return jnp.where(sub == 0, jnp.where(first, 0.0, pltpu.roll(other, 1, 0)), pltpu.roll(own, 1, 0))

    s30 = from_prev_chunk(u[lastrow - SUBLANES:lastrow], prev16[0:SUBLANES])
    s31 = from_prev_chunk(u[lastrow:TS], prev16[SUBLANES:2 * SUBLANES])
    n0 = jnp.where(sub == SUBLANES - 1, jnp.where(last, 0.0, pltpu.roll(next8, SUBLANES - 1, 0)),
                   pltpu.roll(u[0:SUBLANES], SUBLANES - 1, 0))
    m2 = jnp.concatenate([s30, s31, u[0:lastrow - SUBLANES]], axis=0)
    m1 = jnp.concatenate([s31, u[0:lastrow]], axis=0)
    p1 = jnp.concatenate([u[SUBLANES:TS], n0], axis=0)
    return m2 * cw_ref[0:1, :] + m1 * cw_ref[1:2, :] + u * cw_ref[2:3, :] + p1 * cw_ref[3:4, :] + cb_ref[...]


def _lru_terms(uh, wg_ref, bg_ref, lam_ref):
    ucb = uh.astype(BF16)
    rs, is_ = [], []
    for nblk in range(LRU_BLOCKS):
        z = jnp.dot(ucb[:, nblk * LRU_BLOCK_W:(nblk + 1) * LRU_BLOCK_W], wg_ref[nblk],
                    preferred_element_type=F32)
        rs.append(z[:, 0:LRU_BLOCK_W])
        is_.append(z[:, LRU_BLOCK_W:2 * LRU_BLOCK_W])
    tr = jnp.tanh(jnp.concatenate(rs, axis=1) + bg_ref[0:1, :])
    ti = jnp.tanh(jnp.concatenate(is_, axis=1) + bg_ref[1:2, :])
    k2 = (-0.5 * LRU_C * LOG2_E) * jax.nn.softplus(-lam_ref[...])
    a = jnp.exp2(k2 + k2 * tr)
    w = 1.0 - a * a
    root = jnp.where(w > 0.0, w * lax.rsqrt(w), 0.0)
    v = root * ((1.0 + ti) * uh)
    return a, v


def _lru_scan(a, v, carry, reverse):
    order = range(LRU_CHUNK - 1, -1, -1) if reverse else range(LRU_CHUNK)
    h = jnp.zeros((SUBLANES, D_RNN), F32)
    p = jnp.ones((SUBLANES, D_RNN), F32)
    hs, ps = [None] * LRU_CHUNK, [None] * LRU_CHUNK
    for j in order:
        aj = a[j * SUBLANES:(j + 1) * SUBLANES]
        h = aj * h + v[j * SUBLANES:(j + 1) * SUBLANES]
        p = aj * p
        hs[j], ps[j] = h, p
    entry = [None] * SUBLANES
    for c in (range(SUBLANES - 1, -1, -1) if reverse else range(SUBLANES)):
        entry[c] = carry
        carry = h[c:c + 1] + p[c:c + 1] * carry
    hm = jnp.concatenate(entry, axis=0)
    return jnp.concatenate([hs[j] + ps[j] * hm for j in range(LRU_CHUNK)], axis=0), carry


LRU_SUB = 4
LRU_STEP = LRU_SUB * TS


def _lru_scans(terms, h0_ref, carry_ref, hs_ref, cfg, tile, direction):
    carry = carry_ref[...]
    ys = [None] * LRU_SUB
    for sb in (range(LRU_SUB - 1, -1, -1) if direction else range(LRU_SUB)):
        is_ctx, jseq, first, last = _block_position(cfg, tile * LRU_SUB + sb)
        start = last if direction else first
        carry = jnp.where(start, jnp.where(is_ctx, 0.0, h0_ref[jseq]), carry)
        ys[sb], carry = _lru_scan(*terms[sb], carry, bool(direction))
        hs_ref[sb] = carry
    carry_ref[...] = carry
    return ys


def _odd_fwd_kernel(u_ref, up_ref, un_ref, cw_ref, cb_ref, wg_ref, bg_ref, lam_ref, h0_ref,
                    yf_ref, uc_ref, hs_ref, carry_ref, *, cfg):
    tile = pl.program_id(0)
    terms = []
    for sb in range(LRU_SUB):
        _, _, first, last = _block_position(cfg, tile * LRU_SUB + sb)
        rows = slice(sb * TS, (sb + 1) * TS)
        prev16 = up_ref[...] if sb == 0 else u_ref[sb * TS - 2 * SUBLANES:sb * TS, :]
        next8 = un_ref[...] if sb == LRU_SUB - 1 else u_ref[(sb + 1) * TS:(sb + 1) * TS + SUBLANES, :]
        uc = _conv_centred(u_ref[rows, :], prev16, next8, first, last, cw_ref, cb_ref)
        uc_ref[rows, :] = uc
        terms.append(_lru_terms(uc, wg_ref, bg_ref, lam_ref))
    for sb, y in enumerate(_lru_scans(terms, h0_ref, carry_ref, hs_ref, cfg, tile, 0)):
        yf_ref[sb * TS:(sb + 1) * TS, :] = y.astype(BF16)


def _odd_bwd_kernel(g_ref, uc_ref, yf_ref, mod_ref, wg_ref, bg_ref, lam_ref, h0_ref, wout_ref,
                    o_ref, hs_ref, carry_ref, *, cfg):
    tile = pl.num_programs(0) - 1 - pl.program_id(0)
    terms = [_lru_terms(uc_ref[sb * TS:(sb + 1) * TS, :], wg_ref, bg_ref, lam_ref) for sb in range(LRU_SUB)]
    y = jnp.concatenate(_lru_scans(terms, h0_ref, carry_ref, hs_ref, cfg, tile, 1), axis=0)
    mix = (g_ref[...].astype(F32) * (yf_ref[...].astype(F32) + y)).astype(BF16)
    grp = _group_of_tile(cfg, tile, LRU_STEP)
    gate = mod_ref[2, pl.ds(grp, 1), :]
    o_ref[...] = gate * jnp.dot(mix, wout_ref[...], preferred_element_type=F32)


def _odd_scans(cfg, l, g, u, mods, cw, cb, wg, bg, lam, h0, w_out):
    n = cfg.n_tok
    assert n % LRU_STEP == 0 and cfg.n_ctx % LRU_STEP == 0 and cfg.tl % LRU_STEP == 0
    nblk = n // TS
    nstep = n // LRU_STEP
    r8 = LRU_STEP // SUBLANES
    last8 = n // SUBLANES - 1

    def specs(rev):
        it = (lambda i: nstep - 1 - i) if rev else (lambda i: i)
        cur = pl.BlockSpec((LRU_STEP, D_RNN), lambda i: (it(i), 0))
        prev = pl.BlockSpec((2 * SUBLANES, D_RNN), lambda i: (jnp.maximum(it(i) * (r8 // 2) - 1, 0), 0))
        nxt = pl.BlockSpec((SUBLANES, D_RNN), lambda i: (jnp.minimum((it(i) + 1) * r8, last8), 0))
        hs = pl.BlockSpec((LRU_SUB, 1, D_RNN), lambda i: (it(i), 0, 0))
        return cur, prev, nxt, hs

    def weights(direction):
        return [pl.BlockSpec((None, LRU_BLOCKS, LRU_BLOCK_W, 2 * LRU_BLOCK_W), lambda i: (direction, 0, 0, 0)),
                pl.BlockSpec((None, 2, D_RNN), lambda i: (direction, 0, 0)),
                pl.BlockSpec((None, 1, D_RNN), lambda i: (direction, 0, 0)),
                pl.BlockSpec((None, cfg.nl, 1, D_RNN), lambda i: (direction, 0, 0, 0))]

    cur, prev, nxt, hs = specs(False)
    tok_shape = jax.ShapeDtypeStruct((n, D_RNN), F32)
    y_f, uc, hs_f = pl.pallas_call(
        functools.partial(_odd_fwd_kernel, cfg=cfg),
        grid=(nstep,),
        in_specs=[cur, prev, nxt, _const_spec((CONV_W, D_RNN)), _const_spec((1, D_RNN))] + weights(0),
        out_specs=[cur, cur, hs],
        out_shape=[jax.ShapeDtypeStruct((n, D_RNN), BF16), tok_shape, jax.ShapeDtypeStruct((nblk, 1, D_RNN), F32)],
        scratch_shapes=[pltpu.VMEM((1, D_RNN), F32)],
        compiler_params=_cparams(1),
        name=f"lru_fwd{l}",
    )(u, u, u, cw, cb, wg, bg, lam, h0)
    cur, _, _, hs = specs(True)
    upd, hs_b = pl.pallas_call(
        functools.partial(_odd_bwd_kernel, cfg=cfg),
        grid=(nstep,),
        in_specs=[cur, cur, cur, _mod_spec(l)] + weights(1) + [_const_spec((D_RNN, D_MODEL))],
        out_specs=[cur, hs],
        out_shape=[tok_shape, jax.ShapeDtypeStruct((nblk, 1, D_RNN), F32)],
        scratch_shapes=[pltpu.VMEM((1, D_RNN), F32)],
        compiler_params=_cparams(1),
        name=f"lru_bwd{l}",
    )(g, uc, y_f, mods, wg, bg, lam, h0, w_out)
    return upd, hs_f, hs_b


def _rope_table(tl):
    pos = np.arange(tl)
    nf = HEAD_DIM // 4
    inv = ROPE_BASE ** (-np.arange(nf, dtype=np.float32) / nf)
    ar = (pos // GRID_W).astype(np.float32)[:, None] * inv
    ac = (pos % GRID_W).astype(np.float32)[:, None] * inv
    ang = np.concatenate([ar, ar, ac, ac] * 2, axis=-1)
    lo = (np.arange(LANES) % (HEAD_DIM // 2)) < nf
    cos, sin = np.cos(ang), np.sin(ang)
    tab = np.concatenate([cos, np.where(lo, -sin, 0.0), np.where(lo, 0.0, sin)], axis=-1)
    return jnp.asarray(tab, F32)


def _dup_heads(w):
    lead = w.shape[:-1]
    w = w.reshape(lead + (A_KV_HEADS, 1, HEAD_DIM))
    return jnp.broadcast_to(w, lead + (A_KV_HEADS, 2, HEAD_DIM)).reshape(lead + (2 * LANES,))


def _split_even_in(w):
    low = lax.slice_in_dim(w, P_MAIN, w.shape[2], axis=2)
    low = jnp.pad(low, ((0, 0), (0, 0), (0, LANES - 2 * GLA_RANK))).astype(BF16)
    return lax.slice_in_dim(w, 0, P_MAIN, axis=2), low


def _pack_alpha(w_alpha, b_alpha):
    w = jnp.zeros((LANES, 2 * B_QK), F32)
    w = w.at[0:GLA_RANK, 0:B_QK].set(w_alpha[0]).at[GLA_RANK:2 * GLA_RANK, B_QK:].set(w_alpha[1])
    return w.astype(BF16), b_alpha.reshape(1, 2 * B_QK)


def _forward(cfg, x_prompt, x_sample, c, cache_k, cache_v, state_gla, state_lru, c_ctx,
             w_ada, b_ada, norm_mix, norm_ffn, w_in_even, attn_sink, w_alpha, b_alpha, gla_gain,
             w_out_even, w_in_odd, conv_w, conv_b, w_gate_a, b_gate_a, w_gate_x, b_gate_x,
             lru_lambda, w_out_odd, w_ffn_in, w_ffn_out, norm_final):
    assert cfg.tc == TS and cfg.tl % TM == 0 and cfg.n_ctx % TM == 0 and cfg.nl + 1 <= SUBLANES
    x = (x_prompt.reshape(cfg.n_ctx, D_MODEL), x_sample.reshape(cfg.n_lat, D_MODEL))
    cvec = jnp.concatenate([c_ctx[None, :], c, jnp.zeros((SUBLANES - 1 - cfg.nl, D_MODEL), F32)], axis=0)
    mods = _ada(cvec, w_ada, b_ada)
    rope_tab = _rope_table(cfg.tl)
    nf = norm_final.reshape(1, D_MODEL)
    w_out_even_bf = w_out_even.astype(BF16)
    w_even_main, w_even_low = _split_even_in(w_in_even)
    new_k, new_v, new_gla, new_lru = [], [], [], []
    for l in range(DEPTH):
        nw = norm_mix[l].reshape(1, D_MODEL)
        if l % 2 == 0:
            e = l // 2
            za, zb, k_ctx, v_ctx = _even_in(cfg, l, x, mods, nw, w_even_main, w_even_low, rope_tab)
            new_k.append(k_ctx.reshape(cfg.nc, cfg.tc, A_KV_HEADS, HEAD_DIM))
            new_v.append(v_ctx.reshape(cfg.nc, cfg.tc, A_KV_HEADS, HEAD_DIM))
            ck2 = _dup_heads(cache_k[:, e].reshape(cfg.nl, PAST_LEN, A_KV_HEADS * HEAD_DIM)).astype(BF16)
            cv2 = _dup_heads(cache_v[:, e].reshape(cfg.nl, PAST_LEN, A_KV_HEADS * HEAD_DIM)).astype(BF16)
            oa_ctx = _attn_ctx(cfg, l, za, attn_sink[e])
            oa_lat = _attn_lat(cfg, l, za, ck2, cv2, attn_sink[e])
            wal, bal = _pack_alpha(w_alpha[e], b_alpha[e])
            gain = gla_gain[e].reshape(1, B_WIDTH)
            ob_ctx, s_new = _gla_ctx(cfg, l, zb, wal, bal, gain)
            ob_lat = _gla_lat(cfg, l, zb, wal, bal, gain, state_gla[:, e])
            new_gla.append(s_new)
            pre_args = (oa_ctx, oa_lat, ob_ctx, ob_lat, w_out_even_bf)
        else:
            o = l // 2
            g_act, u = _odd_in(cfg, l, x, mods, nw, w_in_odd)
            wg = jnp.concatenate([w_gate_a[o], w_gate_x[o]], axis=-1).astype(BF16)
            bg = 0.5 * jnp.stack([b_gate_a[o], b_gate_x[o]], axis=1)
            lam = lru_lambda[o].reshape(2, 1, D_RNN)
            h0 = jnp.transpose(state_lru[:, o], (1, 0, 2)).reshape(2, cfg.nl, 1, D_RNN)
            upd, hs_f, hs_b = _odd_scans(cfg, l, g_act, u, mods, 0.5 * conv_w[o], 0.5 * conv_b[o].reshape(1, D_RNN),
                                         wg, bg, lam, h0, w_out_odd[o].astype(BF16))
            new_lru.append(jnp.stack([hs_f[:cfg.nc, 0], hs_b[:cfg.nc, 0]], axis=1))
            pre_args = (upd,)
        x = _ffn(cfg, l, x, pre_args, mods, norm_ffn[l].reshape(1, D_MODEL), w_ffn_in, w_ffn_out, nf,
                 final=(l == DEPTH - 1))
    y_prompt = x[0].reshape(cfg.nc, cfg.tc, D_MODEL)
    y_sample = x[1].reshape(cfg.nl, cfg.tl, D_MODEL)
    return (y_prompt, y_sample, jnp.stack(new_k, axis=1), jnp.stack(new_v, axis=1),
            jnp.stack(new_gla, axis=1), jnp.stack(new_lru, axis=1))


def kernel(x_prompt, x_sample, c, cache_k, cache_v, state_gla, state_lru, c_ctx, w_ada, b_ada, norm_mix, norm_ffn, w_in_even, attn_sink, w_alpha, b_alpha, gla_gain, w_out_even, w_in_odd, conv_w, conv_b, w_gate_a, b_gate_a, w_gate_x, b_gate_x, lru_lambda, w_out_odd, w_ffn_in, w_ffn_out, norm_final):
    cfg = Cfg(nc=x_prompt.shape[0], tc=x_prompt.shape[1], nl=x_sample.shape[0], tl=x_sample.shape[1])
    return _forward(cfg, x_prompt, x_sample, c, cache_k, cache_v, state_gla, state_lru, c_ctx,
                    w_ada, b_ada, norm_mix, norm_ffn, w_in_even, attn_sink, w_alpha, b_alpha, gla_gain,
                    w_out_even, w_in_odd, conv_w, conv_b, w_gate_a, b_gate_a, w_gate_x, b_gate_x,
                    lru_lambda, w_out_odd, w_ffn_in, w_ffn_out, norm_final)
```

```python
import functools
from typing import NamedTuple

import jax
import jax.numpy as jnp
import numpy as np
from jax import lax
from jax.experimental import pallas as pl
from jax.experimental.pallas import tpu as pltpu

F32 = jnp.float32
BF16 = jnp.bfloat16

D_MODEL = 1024
DEPTH = 4
EPS = 1e-6

HEAD_DIM = 64
A_Q_HEADS = 8
A_KV_HEADS = 2
A_GROUPS = A_Q_HEADS // A_KV_HEADS
A_WIDTH = A_Q_HEADS * HEAD_DIM
ATT_BLOCK = 128
ROPE_BASE = 10000.0
GRID_W = 64
NEG_INF = -1e30
PAST_LEN = 256

B_HEADS = 4
B_DK = 64
B_DV = 128
B_QK = B_HEADS * B_DK
B_WIDTH = B_HEADS * B_DV
GLA_RANK = 16
GLA_NORMALIZER = 16.0
GLA_CHUNK = 64

D_RNN = D_MODEL
LRU_BLOCK_W = 256
LRU_BLOCKS = D_RNN // LRU_BLOCK_W
CONV_W = 4
LRU_C = 8.0
LOG2_E = 1.4426950408889634

D_FF = 2816

LANES = 128
SUBLANES = 8

ZA_W = A_WIDTH + 2 * 2 * LANES
ZB_LR = B_QK + B_QK + B_WIDTH + B_WIDTH
ZB_W = ZB_LR + LANES
P_A = A_WIDTH + 2 * LANES
P_MAIN = P_A + ZB_LR

TM = 512
TS = 256
FF_CHUNK = 256

VMEM_LIMIT = 56 * 1024 * 1024


class Cfg(NamedTuple):
    nc: int
    tc: int
    nl: int
    tl: int

    @property
    def n_ctx(self):
        return self.nc * self.tc

    @property
    def n_lat(self):
        return self.nl * self.tl

    @property
    def n_tok(self):
        return self.n_ctx + self.n_lat


def _cparams(n_axes):
    return pltpu.CompilerParams(dimension_semantics=("arbitrary",) * n_axes,
                                vmem_limit_bytes=VMEM_LIMIT)


def _const_spec(shape):
    nd = len(shape)
    return pl.BlockSpec(shape, lambda *_: (0,) * nd)


def _silu(x):
    hx = 0.5 * x
    return hx + hx * jnp.tanh(hx)


def _group_of_tile(cfg, i, tile):
    nct = cfg.n_ctx // tile
    per_lat = cfg.tl // tile
    return jnp.where(i < nct, 0, 1 + jnp.maximum(i - nct, 0) // per_lat)


def _modnorm(x, nw, shift, scale):
    ms = jnp.mean(x * x, axis=-1, keepdims=True)
    y = x * lax.rsqrt(ms + EPS) * nw
    return y * (1.0 + scale) + shift


def _mod_rows(mod_ref, g, first):
    return tuple(mod_ref[first + j, pl.ds(g, 1), :] for j in range(3))


ADA_CHUNKS = 3


def _ada_kernel(c_ref, w_ref, b_ref, o_ref):
    s = _silu(c_ref[...]).astype(BF16)
    for j in range(ADA_CHUNKS):
        o_ref[j] = jnp.dot(s, w_ref[:, j * D_MODEL:(j + 1) * D_MODEL].astype(BF16),
                           preferred_element_type=F32) + b_ref[j]


def _ada(cvec, w_ada, b_ada):
    b4 = b_ada.reshape(DEPTH, 6, 1, D_MODEL)
    return pl.pallas_call(
        _ada_kernel,
        grid=(DEPTH, 6 // ADA_CHUNKS),
        in_specs=[
            _const_spec((SUBLANES, D_MODEL)),
            pl.BlockSpec((None, D_MODEL, ADA_CHUNKS * D_MODEL), lambda l, j: (l, 0, j)),
            pl.BlockSpec((None, ADA_CHUNKS, 1, D_MODEL), lambda l, j: (l, j, 0, 0)),
        ],
        out_specs=pl.BlockSpec((None, ADA_CHUNKS, SUBLANES, D_MODEL), lambda l, j: (l, j, 0, 0)),
        out_shape=jax.ShapeDtypeStruct((DEPTH, 6, SUBLANES, D_MODEL), F32),
        compiler_params=_cparams(2),
        name="ada",
    )(cvec, w_ada, b4)


def _mod_spec(l):
    return pl.BlockSpec((None, 6, SUBLANES, D_MODEL), lambda *_: (l, 0, 0, 0))


def _ffn_tile(x, nw, shift, scale, gate, win_ref, wout_ref):
    h = _modnorm(x, nw, shift, scale).astype(BF16)
    acc = None
    for c in range(D_FF // FF_CHUNK):
        lo, hi = c * FF_CHUNK, (c + 1) * FF_CHUNK
        g = jnp.dot(h, win_ref[:, lo:hi].astype(BF16), preferred_element_type=F32)
        u = jnp.dot(h, win_ref[:, D_FF + lo:D_FF + hi].astype(BF16), preferred_element_type=F32)
        a = (_silu(g) * u).astype(BF16)
        part = jnp.dot(a, wout_ref[lo:hi, :].astype(BF16), preferred_element_type=F32)
        acc = part if acc is None else acc + part
    return x + gate * acc


def _x_args(cfg, x, tile=TM):
    if not isinstance(x, tuple):
        return [x], [pl.BlockSpec((tile, D_MODEL), lambda i: (i, 0))]
    nct = cfg.n_ctx // tile
    return list(x), [pl.BlockSpec((tile, D_MODEL), lambda i: (jnp.minimum(i, nct - 1), 0)),
                     pl.BlockSpec((tile, D_MODEL), lambda i: (jnp.maximum(i - nct, 0), 0))]


def _load_x(cfg, x_refs, i, tile=TM):
    if len(x_refs) == 1:
        return x_refs[0][...]
    return jnp.where(i < cfg.n_ctx // tile, x_refs[0][...], x_refs[1][...])


def _to_scan_order(x):
    nb, d = x.shape[0] // TS, x.shape[1]
    return jnp.swapaxes(x.reshape(nb, SUBLANES, TS // SUBLANES, d), 1, 2).reshape(nb * TS, d)


def _from_scan_order(x):
    nb, d = x.shape[0] // TS, x.shape[1]
    return jnp.swapaxes(x.reshape(nb, TS // SUBLANES, SUBLANES, d), 1, 2).reshape(nb * TS, d)


def _ffn_kernel(*refs, cfg, n_x, pre, final):
    x_refs, refs = refs[:n_x], refs[n_x:]
    i = pl.program_id(0)
    g = _group_of_tile(cfg, i, TM)
    x = _load_x(cfg, x_refs, i)
    if pre == "even":
        (oac_ref, oal_ref, obc_ref, obl_ref, wmix_ref), refs = refs[:5], refs[5:]
    else:
        d_ref, refs = refs[0], refs[1:]
    mod_ref, nw_ref, win_ref, wout_ref, nf_ref = refs[:5]
    outs = refs[5:]
    if pre == "even":
        is_ctx_tile = i < cfg.n_ctx // TM
        o = jnp.concatenate([jnp.where(is_ctx_tile, oac_ref[...], oal_ref[...]),
                             jnp.where(is_ctx_tile, obc_ref[...], obl_ref[...])], axis=1)
        x = x + mod_ref[2, pl.ds(g, 1), :] * jnp.dot(o, wmix_ref[...], preferred_element_type=F32)
    else:
        x = x + _from_scan_order(d_ref[...])
    shift, scale, gate = _mod_rows(mod_ref, g, 3)
    y = _ffn_tile(x, nw_ref[...], shift, scale, gate, win_ref, wout_ref)
    if not final:
        outs[0][...] = y
        return
    ms = jnp.mean(y * y, axis=-1, keepdims=True)
    y = y * lax.rsqrt(ms + EPS) * nf_ref[...]
    is_ctx = i < cfg.n_ctx // TM

    @pl.when(is_ctx)
    def _():
        outs[0][...] = y

    @pl.when(jnp.logical_not(is_ctx))
    def _():
        outs[1][...] = y


def _ffn(cfg, l, x, pre_args, mods, nw, win, wout, nf, final):
    n = cfg.n_tok
    nct = cfg.n_ctx // TM
    tok = lambda w: pl.BlockSpec((TM, w), lambda i: (i, 0))
    ctx = lambda w: pl.BlockSpec((TM, w), lambda i: (jnp.minimum(i, nct - 1), 0))
    lat = lambda w: pl.BlockSpec((TM, w), lambda i: (jnp.maximum(i - nct, 0), 0))
    x_ops, x_specs = _x_args(cfg, x)
    if l % 2 == 0:
        pre = "even"
        pre_specs = [ctx(A_WIDTH), lat(A_WIDTH), ctx(B_WIDTH), lat(B_WIDTH),
                     pl.BlockSpec((None, A_WIDTH + B_WIDTH, D_MODEL), lambda i: (l // 2, 0, 0))]
    else:
        pre, pre_specs = "odd", [tok(D_MODEL)]
    if final:
        out_specs = [pl.BlockSpec((TM, D_MODEL), lambda i: (jnp.minimum(i, nct - 1), 0)),
                     pl.BlockSpec((TM, D_MODEL), lambda i: (jnp.maximum(i - nct, 0), 0))]
        out_shape = [jax.ShapeDtypeStruct((cfg.n_ctx, D_MODEL), F32),
                     jax.ShapeDtypeStruct((cfg.n_lat, D_MODEL), F32)]
    else:
        out_specs, out_shape = tok(D_MODEL), jax.ShapeDtypeStruct((n, D_MODEL), F32)
    return pl.pallas_call(
        functools.partial(_ffn_kernel, cfg=cfg, n_x=len(x_ops), pre=pre, final=final),
        grid=(n // TM,),
        in_specs=x_specs + pre_specs + [
            _mod_spec(l), _const_spec((1, D_MODEL)),
            pl.BlockSpec((None, D_MODEL, 2 * D_FF), lambda i: (l, 0, 0), pipeline_mode=pl.Buffered(1)),
            pl.BlockSpec((None, D_FF, D_MODEL), lambda i: (l, 0, 0), pipeline_mode=pl.Buffered(1)),
            _const_spec((1, D_MODEL))],
        out_specs=out_specs,
        out_shape=out_shape,
        compiler_params=_cparams(1),
        name=f"ffn{l}",
    )(*x_ops, *pre_args, mods, nw, win, wout, nf)


TM_EVEN_IN = 1024


def _even_in_kernel(*refs, cfg, n_x):
    x_refs, (mod_ref, nw_ref, w_ref, wlr_ref, tab_ref, za_ref, zb_ref, k_ref, v_ref) = refs[:n_x], refs[n_x:]
    i = pl.program_id(0)
    g = _group_of_tile(cfg, i, TM_EVEN_IN)
    shift, scale, _ = _mod_rows(mod_ref, g, 0)
    h = _modnorm(_load_x(cfg, x_refs, i, TM_EVEN_IN), nw_ref[...], shift, scale).astype(BF16)
    def proj(lo_col, hi_col):
        return jnp.dot(h, w_ref[:, lo_col:hi_col].astype(BF16), preferred_element_type=F32)

    is_lat = g != 0
    cos = jnp.where(is_lat, tab_ref[:, 0:LANES], 1.0)
    sin_a = jnp.where(is_lat, tab_ref[:, LANES:2 * LANES], 0.0)
    sin_b = jnp.where(is_lat, tab_ref[:, 2 * LANES:3 * LANES], 0.0)

    def rope(a):
        parts = []
        for j in range(a.shape[1] // LANES):
            t = a[:, j * LANES:(j + 1) * LANES]
            parts.append(t * cos + pltpu.roll(t, LANES - 16, 1) * sin_a + pltpu.roll(t, 16, 1) * sin_b)
        return jnp.concatenate(parts, axis=1)

    lo = lax.broadcasted_iota(jnp.int32, (TM_EVEN_IN, LANES), 1) < HEAD_DIM

    def twice(a):
        swapped = pltpu.roll(a, HEAD_DIM, 1)
        return jnp.concatenate([jnp.where(lo, a, swapped), jnp.where(lo, swapped, a)], axis=1).astype(BF16)

    q = proj(0, A_WIDTH) * (LOG2_E * HEAD_DIM ** -0.5)
    za_ref[:, 0:A_WIDTH] = rope(q).astype(BF16)
    kv = proj(A_WIDTH, P_A)
    k, v = kv[:, 0:LANES], kv[:, LANES:2 * LANES]
    za_ref[:, A_WIDTH:A_WIDTH + 2 * LANES] = twice(rope(k))
    za_ref[:, A_WIDTH + 2 * LANES:ZA_W] = twice(v)
    zb_ref[:, 0:B_QK] = proj(P_A, P_A + B_QK) * (B_DK ** -0.5)
    for c0 in range(B_QK, ZB_LR, 2 * LANES):
        zb_ref[:, c0:c0 + 2 * LANES] = proj(P_A + c0, P_A + c0 + 2 * LANES)
    zb_ref[:, ZB_LR:ZB_W] = jnp.dot(h, wlr_ref[...], preferred_element_type=F32)

    @pl.when(g == 0)
    def _():
        k_ref[...] = k
        v_ref[...] = v


def _even_in(cfg, l, x, mods, nw, w_all, w_lr, rope_tab):
    n = cfg.n_tok
    tile = TM_EVEN_IN
    assert cfg.n_ctx % tile == 0 and cfg.tl % tile == 0
    nct = cfg.n_ctx // tile
    per_lat = cfg.tl // tile
    tok = lambda w: pl.BlockSpec((tile, w), lambda i: (i, 0))
    x_ops, x_specs = _x_args(cfg, x, tile)
    ctx_kv = pl.BlockSpec((tile, LANES), lambda i: (jnp.minimum(i, nct - 1), 0))
    return pl.pallas_call(
        functools.partial(_even_in_kernel, cfg=cfg, n_x=len(x_ops)),
        grid=(n // tile,),
        in_specs=x_specs + [_mod_spec(l), _const_spec((1, D_MODEL)),
                            pl.BlockSpec((None, D_MODEL, P_MAIN), lambda i: (l // 2, 0, 0),
                                         pipeline_mode=pl.Buffered(1)),
                            pl.BlockSpec((None, D_MODEL, LANES), lambda i: (l // 2, 0, 0)),
                            pl.BlockSpec((tile, 3 * LANES), lambda i: (jnp.maximum(i - nct, 0) % per_lat, 0))],
        out_specs=[tok(ZA_W), tok(ZB_W), ctx_kv, ctx_kv],
        out_shape=[jax.ShapeDtypeStruct((n, ZA_W), BF16),
                   jax.ShapeDtypeStruct((n, ZB_W), F32),
                   jax.ShapeDtypeStruct((cfg.n_ctx, LANES), F32),
                   jax.ShapeDtypeStruct((cfg.n_ctx, LANES), F32)],
        compiler_params=_cparams(1),
        name=f"even_in{l}",
    )(*x_ops, mods, nw, w_all, w_lr, rope_tab)


def _sink_attention(units, sink_ref):
    scores = []
    for q, k_all, _, mask, kvh in units:
        tq = q.shape[0]
        lo = lax.broadcasted_iota(jnp.int32, (tq, LANES), 1) < HEAD_DIM
        qs = []
        for gq in range(A_GROUPS):
            h = kvh * A_GROUPS + gq
            pair = q[:, (h // 2) * LANES:(h // 2 + 1) * LANES]
            keep = lo if h % 2 == 0 else jnp.logical_not(lo)
            qs.append(jnp.where(keep, pair, jnp.zeros_like(pair)))
        qst = jnp.concatenate(qs, axis=0)
        s = lax.dot_general(qst, k_all, (((1,), (1,)), ((), ())), preferred_element_type=F32)
        if mask is not None:
            nfree = s.shape[1] - mask.shape[1]
            s = jnp.concatenate([s[:, :nfree], jnp.where(mask, s[:, nfree:], NEG_INF)], axis=1)
        scores.append(s)
    probs = []
    for (q, _, _, _, kvh), s in zip(units, scores):
        tq = q.shape[0]
        rows = lax.broadcasted_iota(jnp.int32, (A_GROUPS * tq, 1), 0) // tq
        sink = jnp.zeros((A_GROUPS * tq, 1), F32)
        for gq in range(A_GROUPS):
            sink = jnp.where(rows == gq, sink_ref[kvh * A_GROUPS + gq] * LOG2_E, sink)
        m = jnp.maximum(jnp.max(s, axis=-1, keepdims=True), sink)
        p = jnp.exp2(s - m)
        den = jnp.sum(p, axis=-1, keepdims=True) + jnp.exp2(sink - m)
        probs.append((p.astype(BF16), den))
    outs = []
    for (q, _, v_all, _, _), (p, den) in zip(units, probs):
        tq = q.shape[0]
        lo = lax.broadcasted_iota(jnp.int32, (tq, LANES), 1) < HEAD_DIM
        o = jnp.dot(p, v_all, preferred_element_type=F32) / den
        pairs = [jnp.where(lo, o[(2 * pr) * tq:(2 * pr + 1) * tq], o[(2 * pr + 1) * tq:(2 * pr + 2) * tq])
                 for pr in range(2)]
        outs.append(jnp.concatenate(pairs, axis=1).astype(BF16))
    return outs


def _attn_ctx_kernel(sink_ref, q_ref, k_ref, v_ref, o_ref):
    q = q_ref[...]
    units = [(q, k_ref[:, kvh * LANES:(kvh + 1) * LANES], v_ref[:, kvh * LANES:(kvh + 1) * LANES], None, kvh)
             for kvh in range(A_KV_HEADS)]
    for kvh, o in enumerate(_sink_attention(units, sink_ref)):
        o_ref[:, 2 * kvh * LANES:2 * (kvh + 1) * LANES] = o


def _attn_ctx(cfg, l, za, sink):
    tc = cfg.tc
    return pl.pallas_call(
        _attn_ctx_kernel,
        grid=(cfg.nc,),
        in_specs=[pl.BlockSpec(memory_space=pltpu.SMEM),
                  pl.BlockSpec((tc, A_WIDTH), lambda b: (b, 0)),
                  pl.BlockSpec((tc, 2 * LANES), lambda b: (b, A_WIDTH // (2 * LANES))),
                  pl.BlockSpec((tc, 2 * LANES), lambda b: (b, A_WIDTH // (2 * LANES) + 1))],
        out_specs=pl.BlockSpec((tc, A_WIDTH), lambda b: (b, 0)),
        out_shape=jax.ShapeDtypeStruct((cfg.n_ctx, A_WIDTH), BF16),
        compiler_params=_cparams(1),
        name=f"attn_ctx{l}",
    )(sink, za, za, za)


ATT_STEP = 4 * ATT_BLOCK
ATT_WIN = 3 * ATT_BLOCK


def _attn_lat_kernel(sink_ref, q_ref, k_ref, v_ref, ck_ref, cv_ref, o_ref):
    n = pl.program_id(1)
    tl = k_ref.shape[0]
    shape = (A_GROUPS * ATT_BLOCK, ATT_WIN)
    rc = (lax.broadcasted_iota(jnp.int32, shape, 0) % ATT_BLOCK - lax.broadcasted_iota(jnp.int32, shape, 1)
          + ATT_BLOCK)
    units = []
    for j in range(ATT_STEP // ATT_BLOCK):
        q0 = (n * (ATT_STEP // ATT_BLOCK) + j) * ATT_BLOCK
        start = pl.multiple_of(jnp.clip(q0 - ATT_BLOCK, 0, tl - ATT_WIN), ATT_BLOCK)
        mask = lax.bitcast_convert_type(rc + (q0 - start), jnp.uint32) <= jnp.uint32(2 * ATT_BLOCK)
        q = q_ref[j * ATT_BLOCK:(j + 1) * ATT_BLOCK, :]
        for kvh in range(A_KV_HEADS):
            sl = slice(kvh * LANES, (kvh + 1) * LANES)
            k_all = jnp.concatenate([ck_ref[:, sl], k_ref[pl.ds(start, ATT_WIN), sl]], axis=0)
            v_all = jnp.concatenate([cv_ref[:, sl], v_ref[pl.ds(start, ATT_WIN), sl]], axis=0)
            units.append((q, k_all, v_all, mask, kvh))
    for u, o in enumerate(_sink_attention(units, sink_ref)):
        j, kvh = divmod(u, A_KV_HEADS)
        o_ref[j * ATT_BLOCK:(j + 1) * ATT_BLOCK, 2 * kvh * LANES:2 * (kvh + 1) * LANES] = o


def _attn_lat(cfg, l, za, cache_k2, cache_v2, sink):
    assert cfg.n_ctx % cfg.tl == 0 and cfg.tl % ATT_STEP == 0 and cfg.tl >= ATT_WIN
    steps = cfg.tl // ATT_STEP
    base = cfg.n_ctx // ATT_STEP
    kcol = A_WIDTH // (2 * LANES)
    seq = lambda col: pl.BlockSpec((cfg.tl, 2 * LANES), lambda b, n: (cfg.n_ctx // cfg.tl + b, col))
    cache = pl.BlockSpec((None, PAST_LEN, 2 * LANES), lambda b, n: (b, 0, 0))
    return pl.pallas_call(
        _attn_lat_kernel,
        grid=(cfg.nl, steps),
        in_specs=[pl.BlockSpec(memory_space=pltpu.SMEM),
                  pl.BlockSpec((ATT_STEP, A_WIDTH), lambda b, n: (base + b * steps + n, 0)),
                  seq(kcol), seq(kcol + 1), cache, cache],
        out_specs=pl.BlockSpec((ATT_STEP, A_WIDTH), lambda b, n: (b * steps + n, 0)),
        out_shape=jax.ShapeDtypeStruct((cfg.n_lat, A_WIDTH), BF16),
        compiler_params=_cparams(2),
        name=f"attn_lat{l}",
    )(sink, za, za, za, cache_k2, cache_v2)


def _gla_chunk_cumsum(la, reverse):
    cn = GLA_CHUNK
    row = lax.broadcasted_iota(jnp.int32, (cn, B_QK), 0)
    b = la
    s = 1
    while s < cn:
        if reverse:
            b = b + jnp.where(row < cn - s, pltpu.roll(b, cn - s, 0), 0.0)
        else:
            b = b + jnp.where(row >= s, pltpu.roll(b, s, 0), 0.0)
        s *= 2
    return b


def _gla_chunk_prepare(q, k, v, b, reverse):
    cn = GLA_CHUNK
    row = lax.broadcasted_iota(jnp.int32, (cn, B_QK), 0)
    b_last = b[0:1, :] if reverse else b[cn - 1:cn, :]
    q_e = q * jnp.exp(b)
    k_e = (k * jnp.exp(-b)).astype(BF16)
    k_s = (k * jnp.exp(b_last - b)).astype(BF16)
    vb = v.astype(BF16)
    row_t = row[:, 0:LANES]
    lane_t = lax.broadcasted_iota(jnp.int32, (cn, LANES), 1)
    lo = lane_t < B_DK
    col = lane_t % cn
    tri = (col >= row_t) if reverse else (col <= row_t)
    q_tiles, att_tiles = [], []
    for t in range(B_HEADS // 2):
        sl = slice(t * LANES, (t + 1) * LANES)
        k_t = k_e[:, sl]
        zero = jnp.zeros_like(k_t)
        kbd = jnp.concatenate([jnp.where(lo, k_t, zero), jnp.where(lo, zero, k_t)], axis=0)
        q_t = q_e[:, sl]
        att = lax.dot_general(q_t.astype(BF16), kbd, (((1,), (1,)), ((), ())), preferred_element_type=F32)
        q_tiles.append(q_t)
        att_tiles.append(jnp.where(tri, att, 0.0))
    blocks = []
    for t in range(B_HEADS // 2):
        upd = lax.dot_general(k_s[:, t * LANES:(t + 1) * LANES], vb[:, 2 * t * B_DV:2 * (t + 1) * B_DV],
                              (((0,), (0,)), ((), ())), preferred_element_type=F32)
        blocks += [upd[g * B_DK:(g + 1) * B_DK, g * B_DV:(g + 1) * B_DV] for g in range(2)]
    own = jnp.concatenate(blocks, axis=0)
    decay = jnp.broadcast_to(jnp.exp(b_last), (LANES, B_QK)).T
    return q_tiles, att_tiles, vb, own, decay


def _gla_chunk_output(q_tiles, att_tiles, vb, st):
    lo = lax.broadcasted_iota(jnp.int32, (GLA_CHUNK, LANES), 1) < B_DK
    st_b = st.astype(BF16)
    outs = []
    for h in range(B_HEADS):
        att, q_t = att_tiles[h // 2], q_tiles[h // 2]
        x = (jnp.where(lo, att, pltpu.roll(q_t, B_DK, 1)) if h % 2 == 0
             else jnp.where(lo, pltpu.roll(att, B_DK, 1), q_t))
        rhs = jnp.concatenate([vb[:, h * B_DV:(h + 1) * B_DV], st_b[h * B_DK:(h + 1) * B_DK, :]], axis=0)
        outs.append(jnp.dot(x.astype(BF16), rhs, preferred_element_type=F32))
    return jnp.concatenate(outs, axis=1)


def _gla_log_decay(lr_pad, wal_ref, bal_ref, direction):
    z = jnp.dot(lr_pad.astype(BF16), wal_ref[:, direction * B_QK:(direction + 1) * B_QK],
                preferred_element_type=F32) + bal_ref[:, direction * B_QK:(direction + 1) * B_QK]
    return jax.nn.log_sigmoid(z) / GLA_NORMALIZER


def _gla_blocks(jobs, wal_ref, bal_ref):
    prepared = []
    for qkv_ref, lr_ref, _, direction in jobs:
        nchunk = qkv_ref.shape[0] // GLA_CHUNK
        la_all = _gla_log_decay(lr_ref[...], wal_ref, bal_ref, direction)
        b_all = jnp.concatenate(
            [_gla_chunk_cumsum(la_all[r0:r0 + GLA_CHUNK, :], bool(direction))
             for r0 in range(0, qkv_ref.shape[0], GLA_CHUNK)], axis=0)
        chunks = {}
        for ci in range(nchunk):
            rs = slice(ci * GLA_CHUNK, (ci + 1) * GLA_CHUNK)
            chunks[ci] = _gla_chunk_prepare(qkv_ref[rs, 0:B_QK], qkv_ref[rs, B_QK:2 * B_QK],
                                            qkv_ref[rs, 2 * B_QK:2 * B_QK + B_WIDTH], b_all[rs, :],
                                            bool(direction))
        prepared.append(chunks)
    results = []
    for (qkv_ref, _, st, direction), chunks in zip(jobs, prepared):
        nchunk = len(chunks)
        states = {}
        for ci in (range(nchunk - 1, -1, -1) if direction else range(nchunk)):
            states[ci] = st
            _, _, _, own, decay = chunks[ci]
            st = st * decay + own
        outs = [_gla_chunk_output(*chunks[ci][:3], states[ci]) for ci in range(nchunk)]
        results.append((jnp.concatenate(outs, axis=0), st))
    return results


def _stack_state(s_ref):
    return s_ref[...].reshape(B_QK, B_DV)


def _unstack_state(st, s_ref):
    s_ref[...] = st.reshape(B_HEADS, B_DK, B_DV)


def _gla_finish(o, r, gain):
    parts = []
    for h in range(B_HEADS):
        oh = o[:, h * B_DV:(h + 1) * B_DV]
        parts.append(oh * lax.rsqrt(jnp.mean(oh * oh, axis=-1, keepdims=True) + EPS))
    return (jnp.concatenate(parts, axis=1) * gain * _silu(r)).astype(BF16)


def _gla_ctx_kernel(qkv_ref, r_ref, lr_ref, wal_ref, bal_ref, gain_ref, ob_ref, snew_ref):
    zero = jnp.zeros((B_QK, B_DV), F32)
    (o_f, st_f), (o_b, st_b) = _gla_blocks([(qkv_ref, lr_ref, zero, 0), (qkv_ref, lr_ref, zero, 1)],
                                           wal_ref, bal_ref)
    _unstack_state(st_f, snew_ref.at[0])
    _unstack_state(st_b, snew_ref.at[1])
    ob_ref[...] = _gla_finish(o_f + o_b, r_ref[...], gain_ref[...])


def _gla_ctx(cfg, l, zb, wal, bal, gain):
    tc = cfg.tc
    return pl.pallas_call(
        _gla_ctx_kernel,
        grid=(cfg.nc,),
        in_specs=[pl.BlockSpec((tc, 2 * B_QK + B_WIDTH), lambda b: (b, 0)),
                  pl.BlockSpec((tc, B_WIDTH), lambda b: (b, (2 * B_QK + B_WIDTH) // B_WIDTH)),
                  pl.BlockSpec((tc, LANES), lambda b: (b, ZB_LR // LANES)),
                  _const_spec((LANES, 2 * B_QK)), _const_spec((1, 2 * B_QK)), _const_spec((1, B_WIDTH))],
        out_specs=[pl.BlockSpec((tc, B_WIDTH), lambda b: (b, 0)),
                   pl.BlockSpec((None, 2, B_HEADS, B_DK, B_DV), lambda b: (b, 0, 0, 0, 0))],
        out_shape=[jax.ShapeDtypeStruct((cfg.n_ctx, B_WIDTH), BF16),
                   jax.ShapeDtypeStruct((cfg.nc, 2, B_HEADS, B_DK, B_DV), F32)],
        compiler_params=_cparams(1),
        name=f"gla_ctx{l}",
    )(zb, zb, zb, wal, bal, gain)


GLA_STEP = 512


def _gla_lat_kernel(qkvf_ref, lrf_ref, rf_ref, qkvb_ref, lrb_ref, rb_ref, wal_ref, bal_ref, gain_ref,
                    s0_ref, ob_ref, stf_ref, stb_ref, of_ref, obk_ref):
    n = pl.program_id(1)
    nb = pl.num_programs(1)

    @pl.when(n == 0)
    def _():
        stf_ref[...] = _stack_state(s0_ref.at[0])
        stb_ref[...] = _stack_state(s0_ref.at[1])
    (o_f, st_f), (o_b, st_b) = _gla_blocks([(qkvf_ref, lrf_ref, stf_ref[...], 0),
                                            (qkvb_ref, lrb_ref, stb_ref[...], 1)], wal_ref, bal_ref)
    stf_ref[...] = st_f
    stb_ref[...] = st_b
    rows_f = pl.ds(pl.multiple_of(n * GLA_STEP, GLA_STEP), GLA_STEP)
    rows_b = pl.ds(pl.multiple_of((nb - 1 - n) * GLA_STEP, GLA_STEP), GLA_STEP)
    of_ref[rows_f, :] = o_f
    obk_ref[rows_b, :] = o_b

    @pl.when(2 * n >= nb)
    def _():
        ob_ref[rows_f, :] = _gla_finish(o_f + obk_ref[rows_f, :], rf_ref[...], gain_ref[...])
        ob_ref[rows_b, :] = _gla_finish(of_ref[rows_b, :] + o_b, rb_ref[...], gain_ref[...])


def _gla_lat(cfg, l, zb, wal, bal, gain, s0):
    nbl = cfg.tl // GLA_STEP
    base = cfg.n_ctx // GLA_STEP
    assert nbl % 2 == 0 and cfg.tl % GLA_STEP == 0 and cfg.n_ctx % GLA_STEP == 0

    def blk(width, col, rev):
        def imap(b, n):
            return (base + b * nbl + (nbl - 1 - n if rev else n), col)
        return pl.BlockSpec((GLA_STEP, width), imap)

    qkv_w = 2 * B_QK + B_WIDTH
    per_dir = lambda rev: [blk(qkv_w, 0, rev), blk(LANES, ZB_LR // LANES, rev), blk(B_WIDTH, qkv_w // B_WIDTH, rev)]
    return pl.pallas_call(
        _gla_lat_kernel,
        grid=(cfg.nl, nbl),
        in_specs=per_dir(False) + per_dir(True) + [
            _const_spec((LANES, 2 * B_QK)), _const_spec((1, 2 * B_QK)), _const_spec((1, B_WIDTH)),
            pl.BlockSpec((None, 2, B_HEADS, B_DK, B_DV), lambda b, n: (b, 0, 0, 0, 0))],
        out_specs=pl.BlockSpec((cfg.tl, B_WIDTH), lambda b, n: (b, 0)),
        out_shape=jax.ShapeDtypeStruct((cfg.n_lat, B_WIDTH), BF16),
        scratch_shapes=[pltpu.VMEM((B_QK, B_DV), F32), pltpu.VMEM((B_QK, B_DV), F32),
                        pltpu.VMEM((cfg.tl, B_WIDTH), F32), pltpu.VMEM((cfg.tl, B_WIDTH), F32)],
        compiler_params=_cparams(2),
        name=f"gla_lat{l}",
    )(zb, zb, zb, zb, zb, zb, wal, bal, gain, s0)


LRU_CHUNK = TS // SUBLANES


TM_ODD_IN = 1024
ODD_IN_COLS = 256


def _odd_in_kernel(x_ref, mod_ref, nw_ref, w_ref, cw_ref, cb_ref, g_ref, uc_ref, ufirst_ref, tail_ref, *, cfg):
    tile = pl.program_id(0)

    @pl.when(tile == 0)
    def _():
        tail_ref[...] = jnp.zeros_like(tail_ref)
    g = _group_of_tile(cfg, tile, TM_ODD_IN)
    shift, scale, _ = _mod_rows(mod_ref, g, 0)
    h = _modnorm(_to_scan_order(x_ref[...]), nw_ref[...], shift, scale).astype(BF16)
    nsub = TM_ODD_IN // TS
    for c0 in range(0, D_RNN, ODD_IN_COLS):
        z = jnp.dot(h, w_ref[:, c0:c0 + ODD_IN_COLS].astype(BF16), preferred_element_type=F32)
        g_ref[:, c0:c0 + ODD_IN_COLS] = jax.nn.gelu(z).astype(BF16)
    for c0 in range(0, D_RNN, ODD_IN_COLS):
        cols = slice(c0, c0 + ODD_IN_COLS)
        u = jnp.dot(h, w_ref[:, D_RNN + c0:D_RNN + c0 + ODD_IN_COLS].astype(BF16), preferred_element_type=F32)
        cw, cb = cw_ref[:, cols], cb_ref[:, cols]
        for sb in range(nsub):
            _, _, first, last = _block_position(cfg, tile * nsub + sb)
            prev16 = tail_ref[:, cols] if sb == 0 else u[sb * TS - 2 * SUBLANES:sb * TS]
            if sb == nsub - 1:
                next8, last = jnp.zeros((SUBLANES, ODD_IN_COLS), F32), True
            else:
                next8 = u[(sb + 1) * TS:(sb + 1) * TS + SUBLANES]
            uc_ref[sb * TS:(sb + 1) * TS, cols] = _conv_centred(u[sb * TS:(sb + 1) * TS], prev16, next8,
                                                                first, last, cw, cb)
        ufirst_ref[:, cols] = u[0:1]
        tail_ref[:, cols] = u[TM_ODD_IN - 2 * SUBLANES:TM_ODD_IN]


def _odd_in(cfg, l, x, mods, nw, w, cw, cb):
    n = cfg.n_tok
    return pl.pallas_call(
        functools.partial(_odd_in_kernel, cfg=cfg),
        grid=(n // TM_ODD_IN,),
        in_specs=[pl.BlockSpec((TM_ODD_IN, D_MODEL), lambda i: (i, 0)), _mod_spec(l), _const_spec((1, D_MODEL)),
                  pl.BlockSpec((None, D_MODEL, 2 * D_RNN), lambda i: (l // 2, 0, 0),
                               pipeline_mode=pl.Buffered(1)),
                  _const_spec((CONV_W, D_RNN)), _const_spec((1, D_RNN))],
        out_specs=[pl.BlockSpec((TM_ODD_IN, D_RNN), lambda i: (i, 0))] * 2
                  + [pl.BlockSpec((None, 1, D_RNN), lambda i: (i, 0, 0))],
        out_shape=[jax.ShapeDtypeStruct((n, D_RNN), BF16), jax.ShapeDtypeStruct((n, D_RNN), F32),
                   jax.ShapeDtypeStruct((n // TM_ODD_IN, 1, D_RNN), F32)],
        scratch_shapes=[pltpu.VMEM((2 * SUBLANES, D_RNN), F32)],
        compiler_params=_cparams(1),
        name=f"odd_in{l}",
    )(x, mods, nw, w, cw, cb)


def _block_position(cfg, ib):
    ncb = cfg.n_ctx // TS
    bpc = cfg.tc // TS
    bpl = cfg.tl // TS
    is_ctx = ib < ncb
    jl = jnp.maximum(ib - ncb, 0)
    pos = jnp.where(is_ctx, ib % bpc, jl % bpl)
    per = jnp.where(is_ctx, bpc, bpl)
    return is_ctx, jl // bpl, pos == 0, pos == per - 1


def _conv_centred(u, prev16, next8, first, last, cw, cb):
    sub = lax.broadcasted_iota(jnp.int32, (SUBLANES, u.shape[1]), 0)
    lastrow = TS - SUBLANES

    def from_prev_chunk(own, other):
        return jnp.where(sub == 0, jnp.where(first, 0.0, pltpu.roll(other, 1, 0)), pltpu.roll(own, 1, 0))

    s30 = from_prev_chunk(u[lastrow - SUBLANES:lastrow], prev16[0:SUBLANES])
    s31 = from_prev_chunk(u[lastrow:TS], prev16[SUBLANES:2 * SUBLANES])
    n0 = jnp.where(sub == SUBLANES - 1, jnp.where(last, 0.0, pltpu.roll(next8, SUBLANES - 1, 0)),
                   pltpu.roll(u[0:SUBLANES], SUBLANES - 1, 0))
    m2 = jnp.concatenate([s30, s31, u[0:lastrow - SUBLANES]], axis=0)
    m1 = jnp.concatenate([s31, u[0:lastrow]], axis=0)
    p1 = jnp.concatenate([u[SUBLANES:TS], n0], axis=0)
    return m2 * cw[0:1, :] + m1 * cw[1:2, :] + u * cw[2:3, :] + p1 * cw[3:4, :] + cb


def _lru_terms(uh, wg_ref, bg_ref, lam_ref):
    ucb = uh.astype(BF16)
    rs, is_ = [], []
    for nblk in range(LRU_BLOCKS):
        z = jnp.dot(ucb[:, nblk * LRU_BLOCK_W:(nblk + 1) * LRU_BLOCK_W], wg_ref[nblk],
                    preferred_element_type=F32)
        rs.append(z[:, 0:LRU_BLOCK_W])
        is_.append(z[:, LRU_BLOCK_W:2 * LRU_BLOCK_W])
    tr = jnp.tanh(jnp.concatenate(rs, axis=1) + bg_ref[0:1, :])
    ti = jnp.tanh(jnp.concatenate(is_, axis=1) + bg_ref[1:2, :])
    k2 = (-0.5 * LRU_C * LOG2_E) * jax.nn.softplus(-lam_ref[...])
    a = jnp.exp2(k2 + k2 * tr)
    w = 1.0 - a * a
    root = jnp.where(w > 0.0, w * lax.rsqrt(w), 0.0)
    v = root * ((1.0 + ti) * uh)
    return a, v


def _lru_scan(a, v, carry, reverse):
    order = range(LRU_CHUNK - 1, -1, -1) if reverse else range(LRU_CHUNK)
    h = jnp.zeros((SUBLANES, D_RNN), F32)
    p = jnp.ones((SUBLANES, D_RNN), F32)
    hs, ps = [None] * LRU_CHUNK, [None] * LRU_CHUNK
    for j in order:
        aj = a[j * SUBLANES:(j + 1) * SUBLANES]
        h = aj * h + v[j * SUBLANES:(j + 1) * SUBLANES]
        p = aj * p
        hs[j], ps[j] = h, p
    entry = [None] * SUBLANES
    for c in (range(SUBLANES - 1, -1, -1) if reverse else range(SUBLANES)):
        entry[c] = carry
        carry = h[c:c + 1] + p[c:c + 1] * carry
    hm = jnp.concatenate(entry, axis=0)
    return jnp.concatenate([hs[j] + ps[j] * hm for j in range(LRU_CHUNK)], axis=0), carry


LRU_SUB = 4
LRU_STEP = LRU_SUB * TS


def _lru_scans(terms, h0_ref, carry_ref, hs_ref, cfg, tile, direction):
    carry = carry_ref[...]
    ys = [None] * LRU_SUB
    for sb in (range(LRU_SUB - 1, -1, -1) if direction else range(LRU_SUB)):
        is_ctx, jseq, first, last = _block_position(cfg, tile * LRU_SUB + sb)
        start = last if direction else first
        carry = jnp.where(start, jnp.where(is_ctx, 0.0, h0_ref[jseq]), carry)
        ys[sb], carry = _lru_scan(*terms[sb], carry, bool(direction))
        hs_ref[sb] = carry
    carry_ref[...] = carry
    return ys


def _lru_tile_terms(uc_ref, unext_ref, cw_ref, wg_ref, bg_ref, lam_ref, cfg, tile):
    _, _, _, last = _block_position(cfg, tile * LRU_SUB + LRU_SUB - 1)
    missing = jnp.where(last, 0.0, cw_ref[CONV_W - 1:CONV_W, :] * unext_ref[...])
    sub = lax.broadcasted_iota(jnp.int32, (SUBLANES, D_RNN), 0)
    terms = []
    for sb in range(LRU_SUB):
        uc = uc_ref[sb * TS:(sb + 1) * TS, :]
        if sb == LRU_SUB - 1:
            tail = uc[TS - SUBLANES:TS] + jnp.where(sub == SUBLANES - 1, missing, 0.0)
            uc = jnp.concatenate([uc[0:TS - SUBLANES], tail], axis=0)
        terms.append(_lru_terms(uc, wg_ref, bg_ref, lam_ref))
    return terms


def _odd_fwd_kernel(uc_ref, unext_ref, cw_ref, wg_ref, bg_ref, lam_ref, h0_ref,
                    yf_ref, hs_ref, carry_ref, *, cfg):
    tile = pl.program_id(0)
    terms = _lru_tile_terms(uc_ref, unext_ref, cw_ref, wg_ref, bg_ref, lam_ref, cfg, tile)
    for sb, y in enumerate(_lru_scans(terms, h0_ref, carry_ref, hs_ref, cfg, tile, 0)):
        yf_ref[sb * TS:(sb + 1) * TS, :] = y.astype(BF16)


def _odd_bwd_kernel(g_ref, uc_ref, unext_ref, yf_ref, mod_ref, cw_ref, wg_ref, bg_ref, lam_ref, h0_ref, wout_ref,
                    o_ref, hs_ref, carry_ref, *, cfg):
    tile = pl.num_programs(0) - 1 - pl.program_id(0)
    terms = _lru_tile_terms(uc_ref, unext_ref, cw_ref, wg_ref, bg_ref, lam_ref, cfg, tile)
    y = jnp.concatenate(_lru_scans(terms, h0_ref, carry_ref, hs_ref, cfg, tile, 1), axis=0)
    mix = (g_ref[...].astype(F32) * (yf_ref[...].astype(F32) + y)).astype(BF16)
    grp = _group_of_tile(cfg, tile, LRU_STEP)
    gate = mod_ref[2, pl.ds(grp, 1), :]
    o_ref[...] = gate * jnp.dot(mix, wout_ref[...], preferred_element_type=F32)


def _odd_scans(cfg, l, g, uc, ufirst, mods, cw, wg, bg, lam, h0, w_out):
    n = cfg.n_tok
    assert LRU_STEP == TM_ODD_IN and n % LRU_STEP == 0 and cfg.n_ctx % LRU_STEP == 0 and cfg.tl % LRU_STEP == 0
    nblk = n // TS
    nstep = n // LRU_STEP

    def specs(rev):
        it = (lambda i: nstep - 1 - i) if rev else (lambda i: i)
        cur = pl.BlockSpec((LRU_STEP, D_RNN), lambda i: (it(i), 0))
        unext = pl.BlockSpec((None, 1, D_RNN), lambda i: (jnp.minimum(it(i) + 1, nstep - 1), 0, 0))
        hs = pl.BlockSpec((LRU_SUB, 1, D_RNN), lambda i: (it(i), 0, 0))
        return cur, unext, hs

    def weights(direction):
        return [pl.BlockSpec((None, LRU_BLOCKS, LRU_BLOCK_W, 2 * LRU_BLOCK_W), lambda i: (direction, 0, 0, 0)),
                pl.BlockSpec((None, 2, D_RNN), lambda i: (direction, 0, 0)),
                pl.BlockSpec((None, 1, D_RNN), lambda i: (direction, 0, 0)),
                pl.BlockSpec((None, cfg.nl, 1, D_RNN), lambda i: (direction, 0, 0, 0))]

    cur, unext, hs = specs(False)
    conv_taps = _const_spec((CONV_W, D_RNN))
    y_f, hs_f = pl.pallas_call(
        functools.partial(_odd_fwd_kernel, cfg=cfg),
        grid=(nstep,),
        in_specs=[cur, unext, conv_taps] + weights(0),
        out_specs=[cur, hs],
        out_shape=[jax.ShapeDtypeStruct((n, D_RNN), BF16), jax.ShapeDtypeStruct((nblk, 1, D_RNN), F32)],
        scratch_shapes=[pltpu.VMEM((1, D_RNN), F32)],
        compiler_params=_cparams(1),
        name=f"lru_fwd{l}",
    )(uc, ufirst, cw, wg, bg, lam, h0)
    cur, unext, hs = specs(True)
    upd, hs_b = pl.pallas_call(
        functools.partial(_odd_bwd_kernel, cfg=cfg),
        grid=(nstep,),
        in_specs=[cur, cur, unext, cur, _mod_spec(l), conv_taps] + weights(1) + [_const_spec((D_RNN, D_MODEL))],
        out_specs=[cur, hs],
        out_shape=[jax.ShapeDtypeStruct((n, D_RNN), F32), jax.ShapeDtypeStruct((nblk, 1, D_RNN), F32)],
        scratch_shapes=[pltpu.VMEM((1, D_RNN), F32)],
        compiler_params=_cparams(1),
        name=f"lru_bwd{l}",
    )(g, uc, ufirst, y_f, mods, cw, wg, bg, lam, h0, w_out)
    return upd, hs_f, hs_b


def _rope_table(tl):
    pos = np.arange(tl)
    nf = HEAD_DIM // 4
    inv = ROPE_BASE ** (-np.arange(nf, dtype=np.float32) / nf)
    ar = (pos // GRID_W).astype(np.float32)[:, None] * inv
    ac = (pos % GRID_W).astype(np.float32)[:, None] * inv
    ang = np.concatenate([ar, ar, ac, ac] * 2, axis=-1)
    lo = (np.arange(LANES) % (HEAD_DIM // 2)) < nf
    cos, sin = np.cos(ang), np.sin(ang)
    tab = np.concatenate([cos, np.where(lo, -sin, 0.0), np.where(lo, 0.0, sin)], axis=-1)
    return jnp.asarray(tab, F32)


def _dup_heads(w):
    lead = w.shape[:-1]
    w = w.reshape(lead + (A_KV_HEADS, 1, HEAD_DIM))
    return jnp.broadcast_to(w, lead + (A_KV_HEADS, 2, HEAD_DIM)).reshape(lead + (2 * LANES,))


def _low_rank_columns(w):
    low = lax.slice_in_dim(w, P_MAIN, w.shape[2], axis=2)
    return jnp.pad(low, ((0, 0), (0, 0), (0, LANES - 2 * GLA_RANK))).astype(BF16)


def _pack_alpha(w_alpha, b_alpha):
    w = jnp.zeros((LANES, 2 * B_QK), F32)
    w = w.at[0:GLA_RANK, 0:B_QK].set(w_alpha[0]).at[GLA_RANK:2 * GLA_RANK, B_QK:].set(w_alpha[1])
    return w.astype(BF16), b_alpha.reshape(1, 2 * B_QK)


def _forward(cfg, x_prompt, x_sample, c, cache_k, cache_v, state_gla, state_lru, c_ctx,
             w_ada, b_ada, norm_mix, norm_ffn, w_in_even, attn_sink, w_alpha, b_alpha, gla_gain,
             w_out_even, w_in_odd, conv_w, conv_b, w_gate_a, b_gate_a, w_gate_x, b_gate_x,
             lru_lambda, w_out_odd, w_ffn_in, w_ffn_out, norm_final):
    assert cfg.tc == TS and cfg.tl % TM == 0 and cfg.n_ctx % TM == 0 and cfg.nl + 1 <= SUBLANES
    x = (x_prompt.reshape(cfg.n_ctx, D_MODEL), x_sample.reshape(cfg.n_lat, D_MODEL))
    cvec = jnp.concatenate([c_ctx[None, :], c, jnp.zeros((SUBLANES - 1 - cfg.nl, D_MODEL), F32)], axis=0)
    mods = _ada(cvec, w_ada, b_ada)
    rope_tab = _rope_table(cfg.tl)
    nf = norm_final.reshape(1, D_MODEL)
    w_out_even_bf = w_out_even.astype(BF16)
    w_even_main, w_even_low = w_in_even, _low_rank_columns(w_in_even)
    new_k, new_v, new_gla, new_lru = [], [], [], []
    for l in range(DEPTH):
        nw = norm_mix[l].reshape(1, D_MODEL)
        if l % 2 == 0:
            e = l // 2
            za, zb, k_ctx, v_ctx = _even_in(cfg, l, x, mods, nw, w_even_main, w_even_low, rope_tab)
            new_k.append(k_ctx.reshape(cfg.nc, cfg.tc, A_KV_HEADS, HEAD_DIM))
            new_v.append(v_ctx.reshape(cfg.nc, cfg.tc, A_KV_HEADS, HEAD_DIM))
            ck2 = _dup_heads(cache_k[:, e].reshape(cfg.nl, PAST_LEN, A_KV_HEADS * HEAD_DIM)).astype(BF16)
            cv2 = _dup_heads(cache_v[:, e].reshape(cfg.nl, PAST_LEN, A_KV_HEADS * HEAD_DIM)).astype(BF16)
            oa_ctx = _attn_ctx(cfg, l, za, attn_sink[e])
            oa_lat = _attn_lat(cfg, l, za, ck2, cv2, attn_sink[e])
            wal, bal = _pack_alpha(w_alpha[e], b_alpha[e])
            gain = gla_gain[e].reshape(1, B_WIDTH)
            ob_ctx, s_new = _gla_ctx(cfg, l, zb, wal, bal, gain)
            ob_lat = _gla_lat(cfg, l, zb, wal, bal, gain, state_gla[:, e])
            new_gla.append(s_new)
            pre_args = (oa_ctx, oa_lat, ob_ctx, ob_lat, w_out_even_bf)
        else:
            o = l // 2
            cw_half, cb_half = 0.5 * conv_w[o], 0.5 * conv_b[o].reshape(1, D_RNN)
            g_act, uc, ufirst = _odd_in(cfg, l, x, mods, nw, w_in_odd, cw_half, cb_half)
            wg = jnp.concatenate([w_gate_a[o], w_gate_x[o]], axis=-1).astype(BF16)
            bg = 0.5 * jnp.stack([b_gate_a[o], b_gate_x[o]], axis=1)
            lam = lru_lambda[o].reshape(2, 1, D_RNN)
            h0 = jnp.transpose(state_lru[:, o], (1, 0, 2)).reshape(2, cfg.nl, 1, D_RNN)
            upd, hs_f, hs_b = _odd_scans(cfg, l, g_act, uc, ufirst, mods, cw_half,
                                         wg, bg, lam, h0, w_out_odd[o].astype(BF16))
            new_lru.append(jnp.stack([hs_f[:cfg.nc, 0], hs_b[:cfg.nc, 0]], axis=1))
            pre_args = (upd,)
        x = _ffn(cfg, l, x, pre_args, mods, norm_ffn[l].reshape(1, D_MODEL), w_ffn_in, w_ffn_out, nf,
                 final=(l == DEPTH - 1))
    y_prompt = x[0].reshape(cfg.nc, cfg.tc, D_MODEL)
    y_sample = x[1].reshape(cfg.nl, cfg.tl, D_MODEL)
    return (y_prompt, y_sample, jnp.stack(new_k, axis=1), jnp.stack(new_v, axis=1),
            jnp.stack(new_gla, axis=1), jnp.stack(new_lru, axis=1))


def kernel(x_prompt, x_sample, c, cache_k, cache_v, state_gla, state_lru, c_ctx, w_ada, b_ada, norm_mix, norm_ffn, w_in_even, attn_sink, w_alpha, b_alpha, gla_gain, w_out_even, w_in_odd, conv_w, conv_b, w_gate_a, b_gate_a, w_gate_x, b_gate_x, lru_lambda, w_out_odd, w_ffn_in, w_ffn_out, norm_final):
    cfg = Cfg(nc=x_prompt.shape[0], tc=x_prompt.shape[1], nl=x_sample.shape[0], tl=x_sample.shape[1])
    return _forward(cfg, x_prompt, x_sample, c, cache_k, cache_v, state_gla, state_lru, c_ctx,
                    w_ada, b_ada, norm_mix, norm_ffn, w_in_even, attn_sink, w_alpha, b_alpha, gla_gain,
                    w_out_even, w_in_odd, conv_w, conv_b, w_gate_a, b_gate_a, w_gate_x, b_gate_x,
                    lru_lambda, w_out_odd, w_ffn_in, w_ffn_out, norm_final)
```

```python
import functools
from typing import NamedTuple

import jax
import jax.numpy as jnp
import numpy as np
from jax import lax
from jax.experimental import pallas as pl
from jax.experimental.pallas import tpu as pltpu

F32 = jnp.float32
BF16 = jnp.bfloat16

D_MODEL = 1024
DEPTH = 4
EPS = 1e-6

HEAD_DIM = 64
A_Q_HEADS = 8
A_KV_HEADS = 2
A_GROUPS = A_Q_HEADS // A_KV_HEADS
A_WIDTH = A_Q_HEADS * HEAD_DIM
ATT_BLOCK = 128
ROPE_BASE = 10000.0
GRID_W = 64
NEG_INF = -1e30
PAST_LEN = 256

B_HEADS = 4
B_DK = 64
B_DV = 128
B_QK = B_HEADS * B_DK
B_WIDTH = B_HEADS * B_DV
GLA_RANK = 16
GLA_NORMALIZER = 16.0
GLA_CHUNK = 64

D_RNN = D_MODEL
LRU_BLOCK_W = 256
LRU_BLOCKS = D_RNN // LRU_BLOCK_W
CONV_W = 4
LRU_C = 8.0
LOG2_E = 1.4426950408889634

D_FF = 2816

LANES = 128
SUBLANES = 8

ZA_W = A_WIDTH + 2 * 2 * LANES
ZB_LR = B_QK + B_QK + B_WIDTH + B_WIDTH
ZB_W = ZB_LR + LANES
P_A = A_WIDTH + 2 * LANES
P_MAIN = P_A + ZB_LR

MXU_COLS_V7X = 256
TM = 512
TS = 256
FF_CHUNK = MXU_COLS_V7X

VMEM_BYTES_V7X = 64 * 1024 * 1024
VMEM_LIMIT = VMEM_BYTES_V7X - 8 * 1024 * 1024


class Cfg(NamedTuple):
    nc: int
    tc: int
    nl: int
    tl: int

    @property
    def n_ctx(self):
        return self.nc * self.tc

    @property
    def n_lat(self):
        return self.nl * self.tl

    @property
    def n_tok(self):
        return self.n_ctx + self.n_lat


def _cparams(n_axes):
    return pltpu.CompilerParams(dimension_semantics=("arbitrary",) * n_axes,
                                vmem_limit_bytes=VMEM_LIMIT)


def _const_spec(shape):
    nd = len(shape)
    return pl.BlockSpec(shape, lambda *_: (0,) * nd)


def _silu(x):
    hx = 0.5 * x
    return hx + hx * jnp.tanh(hx)


def _group_of_tile(cfg, i, tile):
    nct = cfg.n_ctx // tile
    per_lat = cfg.tl // tile
    return jnp.where(i < nct, 0, 1 + jnp.maximum(i - nct, 0) // per_lat)


def _modnorm(x, nw, shift, scale):
    ms = jnp.mean(x * x, axis=-1, keepdims=True)
    y = x * lax.rsqrt(ms + EPS) * nw
    return y * (1.0 + scale) + shift


def _mod_rows(mod_ref, g, first):
    return tuple(mod_ref[first + j, pl.ds(g, 1), :] for j in range(3))


ADA_CHUNKS = 3


def _ada_kernel(c_ref, w_ref, b_ref, o_ref):
    s = _silu(c_ref[...]).astype(BF16)
    for j in range(ADA_CHUNKS):
        o_ref[j] = jnp.dot(s, w_ref[:, j * D_MODEL:(j + 1) * D_MODEL].astype(BF16),
                           preferred_element_type=F32) + b_ref[j]


def _ada(cvec, w_ada, b_ada):
    b4 = b_ada.reshape(DEPTH, 6, 1, D_MODEL)
    return pl.pallas_call(
        _ada_kernel,
        grid=(DEPTH, 6 // ADA_CHUNKS),
        in_specs=[
            _const_spec((SUBLANES, D_MODEL)),
            pl.BlockSpec((None, D_MODEL, ADA_CHUNKS * D_MODEL), lambda l, j: (l, 0, j)),
            pl.BlockSpec((None, ADA_CHUNKS, 1, D_MODEL), lambda l, j: (l, j, 0, 0)),
        ],
        out_specs=pl.BlockSpec((None, ADA_CHUNKS, SUBLANES, D_MODEL), lambda l, j: (l, j, 0, 0)),
        out_shape=jax.ShapeDtypeStruct((DEPTH, 6, SUBLANES, D_MODEL), F32),
        compiler_params=_cparams(2),
        name="ada",
    )(cvec, w_ada, b4)


def _mod_spec(l):
    return pl.BlockSpec((None, 6, SUBLANES, D_MODEL), lambda *_: (l, 0, 0, 0))


def _ffn_tile(x, nw, shift, scale, gate, win_ref, wout_ref):
    h = _modnorm(x, nw, shift, scale).astype(BF16)
    acc = None
    for c in range(D_FF // FF_CHUNK):
        lo, hi = c * FF_CHUNK, (c + 1) * FF_CHUNK
        g = jnp.dot(h, win_ref[:, lo:hi].astype(BF16), preferred_element_type=F32)
        u = jnp.dot(h, win_ref[:, D_FF + lo:D_FF + hi].astype(BF16), preferred_element_type=F32)
        a = (_silu(g) * u).astype(BF16)
        part = jnp.dot(a, wout_ref[lo:hi, :].astype(BF16), preferred_element_type=F32)
        acc = part if acc is None else acc + part
    return x + gate * acc


def _x_args(cfg, x, tile=TM):
    if not isinstance(x, tuple):
        return [x], [pl.BlockSpec((tile, D_MODEL), lambda i: (i, 0))]
    nct = cfg.n_ctx // tile
    return list(x), [pl.BlockSpec((tile, D_MODEL), lambda i: (jnp.minimum(i, nct - 1), 0)),
                     pl.BlockSpec((tile, D_MODEL), lambda i: (jnp.maximum(i - nct, 0), 0))]


def _load_x(cfg, x_refs, i, tile=TM):
    if len(x_refs) == 1:
        return x_refs[0][...]
    return jnp.where(i < cfg.n_ctx // tile, x_refs[0][...], x_refs[1][...])


def _to_scan_order(x):
    nb, d = x.shape[0] // TS, x.shape[1]
    return jnp.swapaxes(x.reshape(nb, SUBLANES, TS // SUBLANES, d), 1, 2).reshape(nb * TS, d)


def _from_scan_order(x):
    nb, d = x.shape[0] // TS, x.shape[1]
    return jnp.swapaxes(x.reshape(nb, TS // SUBLANES, SUBLANES, d), 1, 2).reshape(nb * TS, d)


def _ffn_kernel(*refs, cfg, n_x, pre, final):
    x_refs, refs = refs[:n_x], refs[n_x:]
    i = pl.program_id(0)
    g = _group_of_tile(cfg, i, TM)
    x = _load_x(cfg, x_refs, i)
    if pre == "even":
        (oac_ref, oal_ref, obc_ref, obl_ref, wmix_ref), refs = refs[:5], refs[5:]
    else:
        d_ref, refs = refs[0], refs[1:]
    mod_ref, nw_ref, win_ref, wout_ref, nf_ref = refs[:5]
    outs = refs[5:]
    if pre == "even":
        is_ctx_tile = i < cfg.n_ctx // TM
        o = jnp.concatenate([jnp.where(is_ctx_tile, oac_ref[...], oal_ref[...]),
                             jnp.where(is_ctx_tile, obc_ref[...], obl_ref[...])], axis=1)
        x = x + mod_ref[2, pl.ds(g, 1), :] * jnp.dot(o, wmix_ref[...], preferred_element_type=F32)
    else:
        x = x + _from_scan_order(d_ref[...])
    shift, scale, gate = _mod_rows(mod_ref, g, 3)
    y = _ffn_tile(x, nw_ref[...], shift, scale, gate, win_ref, wout_ref)
    if not final:
        outs[0][...] = y
        return
    ms = jnp.mean(y * y, axis=-1, keepdims=True)
    y = y * lax.rsqrt(ms + EPS) * nf_ref[...]
    is_ctx = i < cfg.n_ctx // TM

    @pl.when(is_ctx)
    def _():
        outs[0][...] = y

    @pl.when(jnp.logical_not(is_ctx))
    def _():
        outs[1][...] = y


def _ffn(cfg, l, x, pre_args, mods, nw, win, wout, nf, final):
    n = cfg.n_tok
    nct = cfg.n_ctx // TM
    tok = lambda w: pl.BlockSpec((TM, w), lambda i: (i, 0))
    ctx = lambda w: pl.BlockSpec((TM, w), lambda i: (jnp.minimum(i, nct - 1), 0))
    lat = lambda w: pl.BlockSpec((TM, w), lambda i: (jnp.maximum(i - nct, 0), 0))
    x_ops, x_specs = _x_args(cfg, x)
    if l % 2 == 0:
        pre = "even"
        pre_specs = [ctx(A_WIDTH), lat(A_WIDTH), ctx(B_WIDTH), lat(B_WIDTH),
                     pl.BlockSpec((None, A_WIDTH + B_WIDTH, D_MODEL), lambda i: (l // 2, 0, 0))]
    else:
        pre, pre_specs = "odd", [tok(D_MODEL)]
    if final:
        out_specs = [pl.BlockSpec((TM, D_MODEL), lambda i: (jnp.minimum(i, nct - 1), 0)),
                     pl.BlockSpec((TM, D_MODEL), lambda i: (jnp.maximum(i - nct, 0), 0))]
        out_shape = [jax.ShapeDtypeStruct((cfg.n_ctx, D_MODEL), F32),
                     jax.ShapeDtypeStruct((cfg.n_lat, D_MODEL), F32)]
    else:
        out_specs, out_shape = tok(D_MODEL), jax.ShapeDtypeStruct((n, D_MODEL), F32)
    return pl.pallas_call(
        functools.partial(_ffn_kernel, cfg=cfg, n_x=len(x_ops), pre=pre, final=final),
        grid=(n // TM,),
        in_specs=x_specs + pre_specs + [
            _mod_spec(l), _const_spec((1, D_MODEL)),
            pl.BlockSpec((None, D_MODEL, 2 * D_FF), lambda i: (l, 0, 0), pipeline_mode=pl.Buffered(1)),
            pl.BlockSpec((None, D_FF, D_MODEL), lambda i: (l, 0, 0), pipeline_mode=pl.Buffered(1)),
            _const_spec((1, D_MODEL))],
        out_specs=out_specs,
        out_shape=out_shape,
        compiler_params=_cparams(1),
        name=f"ffn{l}",
    )(*x_ops, *pre_args, mods, nw, win, wout, nf)


TM_EVEN_IN = 1024


def _even_in_kernel(*refs, cfg, n_x):
    x_refs, (mod_ref, nw_ref, w_ref, wlr_ref, tab_ref, za_ref, zb_ref, k_ref, v_ref) = refs[:n_x], refs[n_x:]
    i = pl.program_id(0)
    g = _group_of_tile(cfg, i, TM_EVEN_IN)
    shift, scale, _ = _mod_rows(mod_ref, g, 0)
    h = _modnorm(_load_x(cfg, x_refs, i, TM_EVEN_IN), nw_ref[...], shift, scale).astype(BF16)
    def proj(lo_col, hi_col):
        return jnp.dot(h, w_ref[:, lo_col:hi_col].astype(BF16), preferred_element_type=F32)

    is_lat = g != 0
    cos = jnp.where(is_lat, tab_ref[:, 0:LANES], 1.0)
    sin_a = jnp.where(is_lat, tab_ref[:, LANES:2 * LANES], 0.0)
    sin_b = jnp.where(is_lat, tab_ref[:, 2 * LANES:3 * LANES], 0.0)

    def rope(a):
        parts = []
        for j in range(a.shape[1] // LANES):
            t = a[:, j * LANES:(j + 1) * LANES]
            parts.append(t * cos + pltpu.roll(t, LANES - 16, 1) * sin_a + pltpu.roll(t, 16, 1) * sin_b)
        return jnp.concatenate(parts, axis=1)

    lo = lax.broadcasted_iota(jnp.int32, (TM_EVEN_IN, LANES), 1) < HEAD_DIM

    def twice(a):
        swapped = pltpu.roll(a, HEAD_DIM, 1)
        return jnp.concatenate([jnp.where(lo, a, swapped), jnp.where(lo, swapped, a)], axis=1).astype(BF16)

    q = proj(0, A_WIDTH) * (LOG2_E * HEAD_DIM ** -0.5)
    za_ref[:, 0:A_WIDTH] = rope(q).astype(BF16)
    kv = proj(A_WIDTH, P_A)
    k, v = kv[:, 0:LANES], kv[:, LANES:2 * LANES]
    za_ref[:, A_WIDTH:A_WIDTH + 2 * LANES] = twice(rope(k))
    za_ref[:, A_WIDTH + 2 * LANES:ZA_W] = twice(v)
    zb_ref[:, 0:B_QK] = proj(P_A, P_A + B_QK) * (B_DK ** -0.5)
    for c0 in range(B_QK, ZB_LR, 2 * LANES):
        zb_ref[:, c0:c0 + 2 * LANES] = proj(P_A + c0, P_A + c0 + 2 * LANES)
    zb_ref[:, ZB_LR:ZB_W] = jnp.dot(h, wlr_ref[...], preferred_element_type=F32)

    @pl.when(g == 0)
    def _():
        k_ref[...] = k
        v_ref[...] = v


def _even_in(cfg, l, x, mods, nw, w_all, w_lr, rope_tab):
    n = cfg.n_tok
    tile = TM_EVEN_IN
    assert cfg.n_ctx % tile == 0 and cfg.tl % tile == 0
    nct = cfg.n_ctx // tile
    per_lat = cfg.tl // tile
    tok = lambda w: pl.BlockSpec((tile, w), lambda i: (i, 0))
    x_ops, x_specs = _x_args(cfg, x, tile)
    ctx_kv = pl.BlockSpec((tile, LANES), lambda i: (jnp.minimum(i, nct - 1), 0))
    return pl.pallas_call(
        functools.partial(_even_in_kernel, cfg=cfg, n_x=len(x_ops)),
        grid=(n // tile,),
        in_specs=x_specs + [_mod_spec(l), _const_spec((1, D_MODEL)),
                            pl.BlockSpec((None, D_MODEL, P_MAIN), lambda i: (l // 2, 0, 0),
                                         pipeline_mode=pl.Buffered(1)),
                            pl.BlockSpec((None, D_MODEL, LANES), lambda i: (l // 2, 0, 0)),
                            pl.BlockSpec((tile, 3 * LANES), lambda i: (jnp.maximum(i - nct, 0) % per_lat, 0))],
        out_specs=[tok(ZA_W), tok(ZB_W), ctx_kv, ctx_kv],
        out_shape=[jax.ShapeDtypeStruct((n, ZA_W), BF16),
                   jax.ShapeDtypeStruct((n, ZB_W), F32),
                   jax.ShapeDtypeStruct((cfg.n_ctx, LANES), F32),
                   jax.ShapeDtypeStruct((cfg.n_ctx, LANES), F32)],
        compiler_params=_cparams(1),
        name=f"even_in{l}",
    )(*x_ops, mods, nw, w_all, w_lr, rope_tab)


def _sink_attention(units, sink_ref):
    scores = []
    for q, k_all, _, mask, kvh in units:
        tq = q.shape[0]
        lo = lax.broadcasted_iota(jnp.int32, (tq, LANES), 1) < HEAD_DIM
        qs = []
        for gq in range(A_GROUPS):
            h = kvh * A_GROUPS + gq
            pair = q[:, (h // 2) * LANES:(h // 2 + 1) * LANES]
            keep = lo if h % 2 == 0 else jnp.logical_not(lo)
            qs.append(jnp.where(keep, pair, jnp.zeros_like(pair)))
        qst = jnp.concatenate(qs, axis=0)
        s = lax.dot_general(qst, k_all, (((1,), (1,)), ((), ())), preferred_element_type=F32)
        if mask is not None:
            nfree = s.shape[1] - mask.shape[1]
            s = jnp.concatenate([s[:, :nfree], jnp.where(mask, s[:, nfree:], NEG_INF)], axis=1)
        scores.append(s)
    probs = []
    for (q, _, _, _, kvh), s in zip(units, scores):
        tq = q.shape[0]
        rows = lax.broadcasted_iota(jnp.int32, (A_GROUPS * tq, 1), 0) // tq
        sink = jnp.zeros((A_GROUPS * tq, 1), F32)
        for gq in range(A_GROUPS):
            sink = jnp.where(rows == gq, sink_ref[kvh * A_GROUPS + gq] * LOG2_E, sink)
        m = jnp.maximum(jnp.max(s, axis=-1, keepdims=True), sink)
        p = jnp.exp2(s - m)
        den = jnp.sum(p, axis=-1, keepdims=True) + jnp.exp2(sink - m)
        probs.append((p.astype(BF16), den))
    outs = []
    for (q, _, v_all, _, _), (p, den) in zip(units, probs):
        tq = q.shape[0]
        lo = lax.broadcasted_iota(jnp.int32, (tq, LANES), 1) < HEAD_DIM
        o = jnp.dot(p, v_all, preferred_element_type=F32) / den
        pairs = [jnp.where(lo, o[(2 * pr) * tq:(2 * pr + 1) * tq], o[(2 * pr + 1) * tq:(2 * pr + 2) * tq])
                 for pr in range(2)]
        outs.append(jnp.concatenate(pairs, axis=1).astype(BF16))
    return outs


def _attn_ctx_kernel(sink_ref, q_ref, k_ref, v_ref, o_ref):
    q = q_ref[...]
    units = [(q, k_ref[:, kvh * LANES:(kvh + 1) * LANES], v_ref[:, kvh * LANES:(kvh + 1) * LANES], None, kvh)
             for kvh in range(A_KV_HEADS)]
    for kvh, o in enumerate(_sink_attention(units, sink_ref)):
        o_ref[:, 2 * kvh * LANES:2 * (kvh + 1) * LANES] = o


def _attn_ctx(cfg, l, za, sink):
    tc = cfg.tc
    return pl.pallas_call(
        _attn_ctx_kernel,
        grid=(cfg.nc,),
        in_specs=[pl.BlockSpec(memory_space=pltpu.SMEM),
                  pl.BlockSpec((tc, A_WIDTH), lambda b: (b, 0)),
                  pl.BlockSpec((tc, 2 * LANES), lambda b: (b, A_WIDTH // (2 * LANES))),
                  pl.BlockSpec((tc, 2 * LANES), lambda b: (b, A_WIDTH // (2 * LANES) + 1))],
        out_specs=pl.BlockSpec((tc, A_WIDTH), lambda b: (b, 0)),
        out_shape=jax.ShapeDtypeStruct((cfg.n_ctx, A_WIDTH), BF16),
        compiler_params=_cparams(1),
        name=f"attn_ctx{l}",
    )(sink, za, za, za)


ATT_STEP = 4 * ATT_BLOCK
ATT_WIN = 3 * ATT_BLOCK


def _attn_lat_kernel(sink_ref, q_ref, k_ref, v_ref, ck_ref, cv_ref, o_ref):
    n = pl.program_id(1)
    tl = k_ref.shape[0]
    shape = (A_GROUPS * ATT_BLOCK, ATT_WIN)
    rc = (lax.broadcasted_iota(jnp.int32, shape, 0) % ATT_BLOCK - lax.broadcasted_iota(jnp.int32, shape, 1)
          + ATT_BLOCK)
    units = []
    for j in range(ATT_STEP // ATT_BLOCK):
        q0 = (n * (ATT_STEP // ATT_BLOCK) + j) * ATT_BLOCK
        start = pl.multiple_of(jnp.clip(q0 - ATT_BLOCK, 0, tl - ATT_WIN), ATT_BLOCK)
        mask = lax.bitcast_convert_type(rc + (q0 - start), jnp.uint32) <= jnp.uint32(2 * ATT_BLOCK)
        q = q_ref[j * ATT_BLOCK:(j + 1) * ATT_BLOCK, :]
        for kvh in range(A_KV_HEADS):
            sl = slice(kvh * LANES, (kvh + 1) * LANES)
            k_all = jnp.concatenate([ck_ref[:, sl], k_ref[pl.ds(start, ATT_WIN), sl]], axis=0)
            v_all = jnp.concatenate([cv_ref[:, sl], v_ref[pl.ds(start, ATT_WIN), sl]], axis=0)
            units.append((q, k_all, v_all, mask, kvh))
    for u, o in enumerate(_sink_attention(units, sink_ref)):
        j, kvh = divmod(u, A_KV_HEADS)
        o_ref[j * ATT_BLOCK:(j + 1) * ATT_BLOCK, 2 * kvh * LANES:2 * (kvh + 1) * LANES] = o


def _attn_lat(cfg, l, za, cache_k2, cache_v2, sink):
    assert cfg.n_ctx % cfg.tl == 0 and cfg.tl % ATT_STEP == 0 and cfg.tl >= ATT_WIN
    steps = cfg.tl // ATT_STEP
    base = cfg.n_ctx // ATT_STEP
    kcol = A_WIDTH // (2 * LANES)
    seq = lambda col: pl.BlockSpec((cfg.tl, 2 * LANES), lambda b, n: (cfg.n_ctx // cfg.tl + b, col))
    cache = pl.BlockSpec((None, PAST_LEN, 2 * LANES), lambda b, n: (b, 0, 0))
    return pl.pallas_call(
        _attn_lat_kernel,
        grid=(cfg.nl, steps),
        in_specs=[pl.BlockSpec(memory_space=pltpu.SMEM),
                  pl.BlockSpec((ATT_STEP, A_WIDTH), lambda b, n: (base + b * steps + n, 0)),
                  seq(kcol), seq(kcol + 1), cache, cache],
        out_specs=pl.BlockSpec((ATT_STEP, A_WIDTH), lambda b, n: (b * steps + n, 0)),
        out_shape=jax.ShapeDtypeStruct((cfg.n_lat, A_WIDTH), BF16),
        compiler_params=_cparams(2),
        name=f"attn_lat{l}",
    )(sink, za, za, za, cache_k2, cache_v2)


def _gla_chunk_cumsum(la, reverse):
    cn = GLA_CHUNK
    row = lax.broadcasted_iota(jnp.int32, (cn, B_QK), 0)
    b = la
    s = 1
    while s < cn:
        if reverse:
            b = b + jnp.where(row < cn - s, pltpu.roll(b, cn - s, 0), 0.0)
        else:
            b = b + jnp.where(row >= s, pltpu.roll(b, s, 0), 0.0)
        s *= 2
    return b


def _gla_chunk_prepare(q, k, v, b, reverse):
    cn = GLA_CHUNK
    row = lax.broadcasted_iota(jnp.int32, (cn, B_QK), 0)
    b_last = b[0:1, :] if reverse else b[cn - 1:cn, :]
    q_e = q * jnp.exp(b)
    k_e = (k * jnp.exp(-b)).astype(BF16)
    k_s = (k * jnp.exp(b_last - b)).astype(BF16)
    vb = v.astype(BF16)
    row_t = row[:, 0:LANES]
    lane_t = lax.broadcasted_iota(jnp.int32, (cn, LANES), 1)
    lo = lane_t < B_DK
    col = lane_t % cn
    tri = (col >= row_t) if reverse else (col <= row_t)
    q_tiles, att_tiles = [], []
    for t in range(B_HEADS // 2):
        sl = slice(t * LANES, (t + 1) * LANES)
        k_t = k_e[:, sl]
        zero = jnp.zeros_like(k_t)
        kbd = jnp.concatenate([jnp.where(lo, k_t, zero), jnp.where(lo, zero, k_t)], axis=0)
        q_t = q_e[:, sl]
        att = lax.dot_general(q_t.astype(BF16), kbd, (((1,), (1,)), ((), ())), preferred_element_type=F32)
        q_tiles.append(q_t)
        att_tiles.append(jnp.where(tri, att, 0.0))
    blocks = []
    for t in range(B_HEADS // 2):
        upd = lax.dot_general(k_s[:, t * LANES:(t + 1) * LANES], vb[:, 2 * t * B_DV:2 * (t + 1) * B_DV],
                              (((0,), (0,)), ((), ())), preferred_element_type=F32)
        blocks += [upd[g * B_DK:(g + 1) * B_DK, g * B_DV:(g + 1) * B_DV] for g in range(2)]
    own = jnp.concatenate(blocks, axis=0)
    decay = jnp.broadcast_to(jnp.exp(b_last), (LANES, B_QK)).T
    return q_tiles, att_tiles, vb, own, decay


def _gla_chunk_output(q_tiles, att_tiles, vb, st):
    lo = lax.broadcasted_iota(jnp.int32, (GLA_CHUNK, LANES), 1) < B_DK
    st_b = st.astype(BF16)
    outs = []
    for h in range(B_HEADS):
        att, q_t = att_tiles[h // 2], q_tiles[h // 2]
        x = (jnp.where(lo, att, pltpu.roll(q_t, B_DK, 1)) if h % 2 == 0
             else jnp.where(lo, pltpu.roll(att, B_DK, 1), q_t))
        rhs = jnp.concatenate([vb[:, h * B_DV:(h + 1) * B_DV], st_b[h * B_DK:(h + 1) * B_DK, :]], axis=0)
        outs.append(jnp.dot(x.astype(BF16), rhs, preferred_element_type=F32))
    return jnp.concatenate(outs, axis=1)


def _gla_log_decay(lr_pad, wal_ref, bal_ref, direction):
    z = jnp.dot(lr_pad.astype(BF16), wal_ref[:, direction * B_QK:(direction + 1) * B_QK],
                preferred_element_type=F32) + bal_ref[:, direction * B_QK:(direction + 1) * B_QK]
    return jax.nn.log_sigmoid(z) / GLA_NORMALIZER


def _gla_blocks(jobs, wal_ref, bal_ref):
    prepared = []
    for qkv_ref, lr_ref, _, direction in jobs:
        nchunk = qkv_ref.shape[0] // GLA_CHUNK
        la_all = _gla_log_decay(lr_ref[...], wal_ref, bal_ref, direction)
        b_all = jnp.concatenate(
            [_gla_chunk_cumsum(la_all[r0:r0 + GLA_CHUNK, :], bool(direction))
             for r0 in range(0, qkv_ref.shape[0], GLA_CHUNK)], axis=0)
        chunks = {}
        for ci in range(nchunk):
            rs = slice(ci * GLA_CHUNK, (ci + 1) * GLA_CHUNK)
            chunks[ci] = _gla_chunk_prepare(qkv_ref[rs, 0:B_QK], qkv_ref[rs, B_QK:2 * B_QK],
                                            qkv_ref[rs, 2 * B_QK:2 * B_QK + B_WIDTH], b_all[rs, :],
                                            bool(direction))
        prepared.append(chunks)
    results = []
    for (qkv_ref, _, st, direction), chunks in zip(jobs, prepared):
        nchunk = len(chunks)
        states = {}
        for ci in (range(nchunk - 1, -1, -1) if direction else range(nchunk)):
            states[ci] = st
            _, _, _, own, decay = chunks[ci]
            st = st * decay + own
        outs = [_gla_chunk_output(*chunks[ci][:3], states[ci]) for ci in range(nchunk)]
        results.append((jnp.concatenate(outs, axis=0), st))
    return results


def _stack_state(s_ref):
    return s_ref[...].reshape(B_QK, B_DV)


def _unstack_state(st, s_ref):
    s_ref[...] = st.reshape(B_HEADS, B_DK, B_DV)


def _gla_finish(o, r, gain):
    parts = []
    for h in range(B_HEADS):
        oh = o[:, h * B_DV:(h + 1) * B_DV]
        parts.append(oh * lax.rsqrt(jnp.mean(oh * oh, axis=-1, keepdims=True) + EPS))
    return (jnp.concatenate(parts, axis=1) * gain * _silu(r)).astype(BF16)


def _gla_ctx_kernel(qkv_ref, r_ref, lr_ref, wal_ref, bal_ref, gain_ref, ob_ref, snew_ref):
    zero = jnp.zeros((B_QK, B_DV), F32)
    (o_f, st_f), (o_b, st_b) = _gla_blocks([(qkv_ref, lr_ref, zero, 0), (qkv_ref, lr_ref, zero, 1)],
                                           wal_ref, bal_ref)
    _unstack_state(st_f, snew_ref.at[0])
    _unstack_state(st_b, snew_ref.at[1])
    ob_ref[...] = _gla_finish(o_f + o_b, r_ref[...], gain_ref[...])


def _gla_ctx(cfg, l, zb, wal, bal, gain):
    tc = cfg.tc
    return pl.pallas_call(
        _gla_ctx_kernel,
        grid=(cfg.nc,),
        in_specs=[pl.BlockSpec((tc, 2 * B_QK + B_WIDTH), lambda b: (b, 0)),
                  pl.BlockSpec((tc, B_WIDTH), lambda b: (b, (2 * B_QK + B_WIDTH) // B_WIDTH)),
                  pl.BlockSpec((tc, LANES), lambda b: (b, ZB_LR // LANES)),
                  _const_spec((LANES, 2 * B_QK)), _const_spec((1, 2 * B_QK)), _const_spec((1, B_WIDTH))],
        out_specs=[pl.BlockSpec((tc, B_WIDTH), lambda b: (b, 0)),
                   pl.BlockSpec((None, 2, B_HEADS, B_DK, B_DV), lambda b: (b, 0, 0, 0, 0))],
        out_shape=[jax.ShapeDtypeStruct((cfg.n_ctx, B_WIDTH), BF16),
                   jax.ShapeDtypeStruct((cfg.nc, 2, B_HEADS, B_DK, B_DV), F32)],
        compiler_params=_cparams(1),
        name=f"gla_ctx{l}",
    )(zb, zb, zb, wal, bal, gain)


GLA_STEP = 512


def _gla_lat_kernel(qkvf_ref, lrf_ref, rf_ref, qkvb_ref, lrb_ref, rb_ref, wal_ref, bal_ref, gain_ref,
                    s0_ref, ob_ref, stf_ref, stb_ref, of_ref, obk_ref):
    n = pl.program_id(1)
    nb = pl.num_programs(1)

    @pl.when(n == 0)
    def _():
        stf_ref[...] = _stack_state(s0_ref.at[0])
        stb_ref[...] = _stack_state(s0_ref.at[1])
    (o_f, st_f), (o_b, st_b) = _gla_blocks([(qkvf_ref, lrf_ref, stf_ref[...], 0),
                                            (qkvb_ref, lrb_ref, stb_ref[...], 1)], wal_ref, bal_ref)
    stf_ref[...] = st_f
    stb_ref[...] = st_b
    rows_f = pl.ds(pl.multiple_of(n * GLA_STEP, GLA_STEP), GLA_STEP)
    rows_b = pl.ds(pl.multiple_of((nb - 1 - n) * GLA_STEP, GLA_STEP), GLA_STEP)
    of_ref[rows_f, :] = o_f
    obk_ref[rows_b, :] = o_b

    @pl.when(2 * n >= nb)
    def _():
        ob_ref[rows_f, :] = _gla_finish(o_f + obk_ref[rows_f, :], rf_ref[...], gain_ref[...])
        ob_ref[rows_b, :] = _gla_finish(of_ref[rows_b, :] + o_b, rb_ref[...], gain_ref[...])


def _gla_lat(cfg, l, zb, wal, bal, gain, s0):
    nbl = cfg.tl // GLA_STEP
    base = cfg.n_ctx // GLA_STEP
    assert nbl % 2 == 0 and cfg.tl % GLA_STEP == 0 and cfg.n_ctx % GLA_STEP == 0

    def blk(width, col, rev):
        def imap(b, n):
            return (base + b * nbl + (nbl - 1 - n if rev else n), col)
        return pl.BlockSpec((GLA_STEP, width), imap)

    qkv_w = 2 * B_QK + B_WIDTH
    per_dir = lambda rev: [blk(qkv_w, 0, rev), blk(LANES, ZB_LR // LANES, rev), blk(B_WIDTH, qkv_w // B_WIDTH, rev)]
    return pl.pallas_call(
        _gla_lat_kernel,
        grid=(cfg.nl, nbl),
        in_specs=per_dir(False) + per_dir(True) + [
            _const_spec((LANES, 2 * B_QK)), _const_spec((1, 2 * B_QK)), _const_spec((1, B_WIDTH)),
            pl.BlockSpec((None, 2, B_HEADS, B_DK, B_DV), lambda b, n: (b, 0, 0, 0, 0))],
        out_specs=pl.BlockSpec((cfg.tl, B_WIDTH), lambda b, n: (b, 0)),
        out_shape=jax.ShapeDtypeStruct((cfg.n_lat, B_WIDTH), BF16),
        scratch_shapes=[pltpu.VMEM((B_QK, B_DV), F32), pltpu.VMEM((B_QK, B_DV), F32),
                        pltpu.VMEM((cfg.tl, B_WIDTH), F32), pltpu.VMEM((cfg.tl, B_WIDTH), F32)],
        compiler_params=_cparams(2),
        name=f"gla_lat{l}",
    )(zb, zb, zb, zb, zb, zb, wal, bal, gain, s0)


LRU_CHUNK = TS // SUBLANES


TM_ODD_IN = 1024
ODD_IN_COLS = 256


def _odd_in_kernel(x_ref, mod_ref, nw_ref, w_ref, cw_ref, cb_ref, g_ref, uc_ref, ufirst_ref, tail_ref, *, cfg):
    tile = pl.program_id(0)

    @pl.when(tile == 0)
    def _():
        tail_ref[...] = jnp.zeros_like(tail_ref)
    g = _group_of_tile(cfg, tile, TM_ODD_IN)
    shift, scale, _ = _mod_rows(mod_ref, g, 0)
    h = _modnorm(_to_scan_order(x_ref[...]), nw_ref[...], shift, scale).astype(BF16)
    nsub = TM_ODD_IN // TS
    for c0 in range(0, D_RNN, ODD_IN_COLS):
        z = jnp.dot(h, w_ref[:, c0:c0 + ODD_IN_COLS].astype(BF16), preferred_element_type=F32)
        g_ref[:, c0:c0 + ODD_IN_COLS] = jax.nn.gelu(z).astype(BF16)
    for c0 in range(0, D_RNN, ODD_IN_COLS):
        cols = slice(c0, c0 + ODD_IN_COLS)
        u = jnp.dot(h, w_ref[:, D_RNN + c0:D_RNN + c0 + ODD_IN_COLS].astype(BF16), preferred_element_type=F32)
        cw, cb = cw_ref[:, cols], cb_ref[:, cols]
        for sb in range(nsub):
            _, _, first, last = _block_position(cfg, tile * nsub + sb)
            prev16 = tail_ref[:, cols] if sb == 0 else u[sb * TS - 2 * SUBLANES:sb * TS]
            if sb == nsub - 1:
                next8, last = jnp.zeros((SUBLANES, ODD_IN_COLS), F32), True
            else:
                next8 = u[(sb + 1) * TS:(sb + 1) * TS + SUBLANES]
            uc_ref[sb * TS:(sb + 1) * TS, cols] = _conv_centred(u[sb * TS:(sb + 1) * TS], prev16, next8,
                                                                first, last, cw, cb)
        ufirst_ref[:, cols] = u[0:1]
        tail_ref[:, cols] = u[TM_ODD_IN - 2 * SUBLANES:TM_ODD_IN]


def _odd_in(cfg, l, x, mods, nw, w, cw, cb):
    n = cfg.n_tok
    return pl.pallas_call(
        functools.partial(_odd_in_kernel, cfg=cfg),
        grid=(n // TM_ODD_IN,),
        in_specs=[pl.BlockSpec((TM_ODD_IN, D_MODEL), lambda i: (i, 0)), _mod_spec(l), _const_spec((1, D_MODEL)),
                  pl.BlockSpec((None, D_MODEL, 2 * D_RNN), lambda i: (l // 2, 0, 0),
                               pipeline_mode=pl.Buffered(1)),
                  _const_spec((CONV_W, D_RNN)), _const_spec((1, D_RNN))],
        out_specs=[pl.BlockSpec((TM_ODD_IN, D_RNN), lambda i: (i, 0))] * 2
                  + [pl.BlockSpec((None, 1, D_RNN), lambda i: (i, 0, 0))],
        out_shape=[jax.ShapeDtypeStruct((n, D_RNN), BF16), jax.ShapeDtypeStruct((n, D_RNN), F32),
                   jax.ShapeDtypeStruct((n // TM_ODD_IN, 1, D_RNN), F32)],
        scratch_shapes=[pltpu.VMEM((2 * SUBLANES, D_RNN), F32)],
        compiler_params=_cparams(1),
        name=f"odd_in{l}",
    )(x, mods, nw, w, cw, cb)


def _block_position(cfg, ib):
    ncb = cfg.n_ctx // TS
    bpc = cfg.tc // TS
    bpl = cfg.tl // TS
    is_ctx = ib < ncb
    jl = jnp.maximum(ib - ncb, 0)
    pos = jnp.where(is_ctx, ib % bpc, jl % bpl)
    per = jnp.where(is_ctx, bpc, bpl)
    return is_ctx, jl // bpl, pos == 0, pos == per - 1


def _conv_centred(u, prev16, next8, first, last, cw, cb):
    sub = lax.broadcasted_iota(jnp.int32, (SUBLANES, u.shape[1]), 0)
    lastrow = TS - SUBLANES

    def from_prev_chunk(own, other):
        return jnp.where(sub == 0, jnp.where(first, 0.0, pltpu.roll(other, 1, 0)), pltpu.roll(own, 1, 0))

    s30 = from_prev_chunk(u[lastrow - SUBLANES:lastrow], prev16[0:SUBLANES])
    s31 = from_prev_chunk(u[lastrow:TS], prev16[SUBLANES:2 * SUBLANES])
    n0 = jnp.where(sub == SUBLANES - 1, jnp.where(last, 0.0, pltpu.roll(next8, SUBLANES - 1, 0)),
                   pltpu.roll(u[0:SUBLANES], SUBLANES - 1, 0))
    m2 = jnp.concatenate([s30, s31, u[0:lastrow - SUBLANES]], axis=0)
    m1 = jnp.concatenate([s31, u[0:lastrow]], axis=0)
    p1 = jnp.concatenate([u[SUBLANES:TS], n0], axis=0)
    return m2 * cw[0:1, :] + m1 * cw[1:2, :] + u * cw[2:3, :] + p1 * cw[3:4, :] + cb


def _lru_terms(uh, wg_ref, bg_ref, lam_ref):
    ucb = uh.astype(BF16)
    rs, is_ = [], []
    for nblk in range(LRU_BLOCKS):
        z = jnp.dot(ucb[:, nblk * LRU_BLOCK_W:(nblk + 1) * LRU_BLOCK_W], wg_ref[nblk],
                    preferred_element_type=F32)
        rs.append(z[:, 0:LRU_BLOCK_W])
        is_.append(z[:, LRU_BLOCK_W:2 * LRU_BLOCK_W])
    tr = jnp.tanh(jnp.concatenate(rs, axis=1) + bg_ref[0:1, :])
    ti = jnp.tanh(jnp.concatenate(is_, axis=1) + bg_ref[1:2, :])
    k2 = (-0.5 * LRU_C * LOG2_E) * jax.nn.softplus(-lam_ref[...])
    a = jnp.exp2(k2 + k2 * tr)
    w = 1.0 - a * a
    root = jnp.where(w > 0.0, w * lax.rsqrt(w), 0.0)
    v = root * ((1.0 + ti) * uh)
    return a, v


def _lru_scan(a, v, carry, reverse):
    order = range(LRU_CHUNK - 1, -1, -1) if reverse else range(LRU_CHUNK)
    h = jnp.zeros((SUBLANES, D_RNN), F32)
    p = jnp.ones((SUBLANES, D_RNN), F32)
    hs, ps = [None] * LRU_CHUNK, [None] * LRU_CHUNK
    for j in order:
        aj = a[j * SUBLANES:(j + 1) * SUBLANES]
        h = aj * h + v[j * SUBLANES:(j + 1) * SUBLANES]
        p = aj * p
        hs[j], ps[j] = h, p
    entry = [None] * SUBLANES
    for c in (range(SUBLANES - 1, -1, -1) if reverse else range(SUBLANES)):
        entry[c] = carry
        carry = h[c:c + 1] + p[c:c + 1] * carry
    hm = jnp.concatenate(entry, axis=0)
    return jnp.concatenate([hs[j] + ps[j] * hm for j in range(LRU_CHUNK)], axis=0), carry


LRU_SUB = 4
LRU_STEP = LRU_SUB * TS


def _lru_scans(terms, h0_ref, carry_ref, hs_ref, cfg, tile, direction):
    carry = carry_ref[...]
    ys = [None] * LRU_SUB
    for sb in (range(LRU_SUB - 1, -1, -1) if direction else range(LRU_SUB)):
        is_ctx, jseq, first, last = _block_position(cfg, tile * LRU_SUB + sb)
        start = last if direction else first
        carry = jnp.where(start, jnp.where(is_ctx, 0.0, h0_ref[jseq]), carry)
        ys[sb], carry = _lru_scan(*terms[sb], carry, bool(direction))
        hs_ref[sb] = carry
    carry_ref[...] = carry
    return ys


def _lru_tile_terms(uc_ref, unext_ref, cw_ref, wg_ref, bg_ref, lam_ref, cfg, tile):
    _, _, _, last = _block_position(cfg, tile * LRU_SUB + LRU_SUB - 1)
    missing = jnp.where(last, 0.0, cw_ref[CONV_W - 1:CONV_W, :] * unext_ref[...])
    sub = lax.broadcasted_iota(jnp.int32, (SUBLANES, D_RNN), 0)
    terms = []
    for sb in range(LRU_SUB):
        uc = uc_ref[sb * TS:(sb + 1) * TS, :]
        if sb == LRU_SUB - 1:
            tail = uc[TS - SUBLANES:TS] + jnp.where(sub == SUBLANES - 1, missing, 0.0)
            uc = jnp.concatenate([uc[0:TS - SUBLANES], tail], axis=0)
        terms.append(_lru_terms(uc, wg_ref, bg_ref, lam_ref))
    return terms


def _odd_fwd_kernel(uc_ref, unext_ref, cw_ref, wg_ref, bg_ref, lam_ref, h0_ref,
                    yf_ref, hs_ref, carry_ref, *, cfg):
    tile = pl.program_id(0)
    terms = _lru_tile_terms(uc_ref, unext_ref, cw_ref, wg_ref, bg_ref, lam_ref, cfg, tile)
    for sb, y in enumerate(_lru_scans(terms, h0_ref, carry_ref, hs_ref, cfg, tile, 0)):
        yf_ref[sb * TS:(sb + 1) * TS, :] = y.astype(BF16)


def _odd_bwd_kernel(g_ref, uc_ref, unext_ref, yf_ref, mod_ref, cw_ref, wg_ref, bg_ref, lam_ref, h0_ref, wout_ref,
                    o_ref, hs_ref, carry_ref, *, cfg):
    tile = pl.num_programs(0) - 1 - pl.program_id(0)
    terms = _lru_tile_terms(uc_ref, unext_ref, cw_ref, wg_ref, bg_ref, lam_ref, cfg, tile)
    y = jnp.concatenate(_lru_scans(terms, h0_ref, carry_ref, hs_ref, cfg, tile, 1), axis=0)
    mix = (g_ref[...].astype(F32) * (yf_ref[...].astype(F32) + y)).astype(BF16)
    grp = _group_of_tile(cfg, tile, LRU_STEP)
    gate = mod_ref[2, pl.ds(grp, 1), :]
    o_ref[...] = gate * jnp.dot(mix, wout_ref[...], preferred_element_type=F32)


def _odd_scans(cfg, l, g, uc, ufirst, mods, cw, wg, bg, lam, h0, w_out):
    n = cfg.n_tok
    assert LRU_STEP == TM_ODD_IN and n % LRU_STEP == 0 and cfg.n_ctx % LRU_STEP == 0 and cfg.tl % LRU_STEP == 0
    nblk = n // TS
    nstep = n // LRU_STEP

    def specs(rev):
        it = (lambda i: nstep - 1 - i) if rev else (lambda i: i)
        cur = pl.BlockSpec((LRU_STEP, D_RNN), lambda i: (it(i), 0))
        unext = pl.BlockSpec((None, 1, D_RNN), lambda i: (jnp.minimum(it(i) + 1, nstep - 1), 0, 0))
        hs = pl.BlockSpec((LRU_SUB, 1, D_RNN), lambda i: (it(i), 0, 0))
        return cur, unext, hs

    def weights(direction):
        return [pl.BlockSpec((None, LRU_BLOCKS, LRU_BLOCK_W, 2 * LRU_BLOCK_W), lambda i: (direction, 0, 0, 0)),
                pl.BlockSpec((None, 2, D_RNN), lambda i: (direction, 0, 0)),
                pl.BlockSpec((None, 1, D_RNN), lambda i: (direction, 0, 0)),
                pl.BlockSpec((None, cfg.nl, 1, D_RNN), lambda i: (direction, 0, 0, 0))]

    cur, unext, hs = specs(False)
    conv_taps = _const_spec((CONV_W, D_RNN))
    y_f, hs_f = pl.pallas_call(
        functools.partial(_odd_fwd_kernel, cfg=cfg),
        grid=(nstep,),
        in_specs=[cur, unext, conv_taps] + weights(0),
        out_specs=[cur, hs],
        out_shape=[jax.ShapeDtypeStruct((n, D_RNN), BF16), jax.ShapeDtypeStruct((nblk, 1, D_RNN), F32)],
        scratch_shapes=[pltpu.VMEM((1, D_RNN), F32)],
        compiler_params=_cparams(1),
        name=f"lru_fwd{l}",
    )(uc, ufirst, cw, wg, bg, lam, h0)
    cur, unext, hs = specs(True)
    upd, hs_b = pl.pallas_call(
        functools.partial(_odd_bwd_kernel, cfg=cfg),
        grid=(nstep,),
        in_specs=[cur, cur, unext, cur, _mod_spec(l), conv_taps] + weights(1) + [_const_spec((D_RNN, D_MODEL))],
        out_specs=[cur, hs],
        out_shape=[jax.ShapeDtypeStruct((n, D_RNN), F32), jax.ShapeDtypeStruct((nblk, 1, D_RNN), F32)],
        scratch_shapes=[pltpu.VMEM((1, D_RNN), F32)],
        compiler_params=_cparams(1),
        name=f"lru_bwd{l}",
    )(g, uc, ufirst, y_f, mods, cw, wg, bg, lam, h0, w_out)
    return upd, hs_f, hs_b


def _rope_table(tl):
    pos = np.arange(tl)
    nf = HEAD_DIM // 4
    inv = ROPE_BASE ** (-np.arange(nf, dtype=np.float32) / nf)
    ar = (pos // GRID_W).astype(np.float32)[:, None] * inv
    ac = (pos % GRID_W).astype(np.float32)[:, None] * inv
    ang = np.concatenate([ar, ar, ac, ac] * 2, axis=-1)
    lo = (np.arange(LANES) % (HEAD_DIM // 2)) < nf
    cos, sin = np.cos(ang), np.sin(ang)
    tab = np.concatenate([cos, np.where(lo, -sin, 0.0), np.where(lo, 0.0, sin)], axis=-1)
    return jnp.asarray(tab, F32)


def _dup_heads(w):
    lead = w.shape[:-1]
    w = w.reshape(lead + (A_KV_HEADS, 1, HEAD_DIM))
    return jnp.broadcast_to(w, lead + (A_KV_HEADS, 2, HEAD_DIM)).reshape(lead + (2 * LANES,))


def _low_rank_columns(w):
    low = lax.slice_in_dim(w, P_MAIN, w.shape[2], axis=2)
    return jnp.pad(low, ((0, 0), (0, 0), (0, LANES - 2 * GLA_RANK))).astype(BF16)


def _pack_alpha(w_alpha, b_alpha):
    w = jnp.zeros((LANES, 2 * B_QK), F32)
    w = w.at[0:GLA_RANK, 0:B_QK].set(w_alpha[0]).at[GLA_RANK:2 * GLA_RANK, B_QK:].set(w_alpha[1])
    return w.astype(BF16), b_alpha.reshape(1, 2 * B_QK)


def _forward(cfg, x_prompt, x_sample, c, cache_k, cache_v, state_gla, state_lru, c_ctx,
             w_ada, b_ada, norm_mix, norm_ffn, w_in_even, attn_sink, w_alpha, b_alpha, gla_gain,
             w_out_even, w_in_odd, conv_w, conv_b, w_gate_a, b_gate_a, w_gate_x, b_gate_x,
             lru_lambda, w_out_odd, w_ffn_in, w_ffn_out, norm_final):
    assert cfg.tc == TS and cfg.tl % TM == 0 and cfg.n_ctx % TM == 0 and cfg.nl + 1 <= SUBLANES
    x = (x_prompt.reshape(cfg.n_ctx, D_MODEL), x_sample.reshape(cfg.n_lat, D_MODEL))
    cvec = jnp.concatenate([c_ctx[None, :], c, jnp.zeros((SUBLANES - 1 - cfg.nl, D_MODEL), F32)], axis=0)
    mods = _ada(cvec, w_ada, b_ada)
    rope_tab = _rope_table(cfg.tl)
    nf = norm_final.reshape(1, D_MODEL)
    w_out_even_bf = w_out_even.astype(BF16)
    w_even_main, w_even_low = w_in_even, _low_rank_columns(w_in_even)
    new_k, new_v, new_gla, new_lru = [], [], [], []
    for l in range(DEPTH):
        nw = norm_mix[l].reshape(1, D_MODEL)
        if l % 2 == 0:
            e = l // 2
            za, zb, k_ctx, v_ctx = _even_in(cfg, l, x, mods, nw, w_even_main, w_even_low, rope_tab)
            new_k.append(k_ctx.reshape(cfg.nc, cfg.tc, A_KV_HEADS, HEAD_DIM))
            new_v.append(v_ctx.reshape(cfg.nc, cfg.tc, A_KV_HEADS, HEAD_DIM))
            ck2 = _dup_heads(cache_k[:, e].reshape(cfg.nl, PAST_LEN, A_KV_HEADS * HEAD_DIM)).astype(BF16)
            cv2 = _dup_heads(cache_v[:, e].reshape(cfg.nl, PAST_LEN, A_KV_HEADS * HEAD_DIM)).astype(BF16)
            oa_ctx = _attn_ctx(cfg, l, za, attn_sink[e])
            oa_lat = _attn_lat(cfg, l, za, ck2, cv2, attn_sink[e])
            wal, bal = _pack_alpha(w_alpha[e], b_alpha[e])
            gain = gla_gain[e].reshape(1, B_WIDTH)
            ob_ctx, s_new = _gla_ctx(cfg, l, zb, wal, bal, gain)
            ob_lat = _gla_lat(cfg, l, zb, wal, bal, gain, state_gla[:, e])
            new_gla.append(s_new)
            pre_args = (oa_ctx, oa_lat, ob_ctx, ob_lat, w_out_even_bf)
        else:
            o = l // 2
            cw_half, cb_half = 0.5 * conv_w[o], 0.5 * conv_b[o].reshape(1, D_RNN)
            g_act, uc, ufirst = _odd_in(cfg, l, x, mods, nw, w_in_odd, cw_half, cb_half)
            wg = jnp.concatenate([w_gate_a[o], w_gate_x[o]], axis=-1).astype(BF16)
            bg = 0.5 * jnp.stack([b_gate_a[o], b_gate_x[o]], axis=1)
            lam = lru_lambda[o].reshape(2, 1, D_RNN)
            h0 = jnp.transpose(state_lru[:, o], (1, 0, 2)).reshape(2, cfg.nl, 1, D_RNN)
            upd, hs_f, hs_b = _odd_scans(cfg, l, g_act, uc, ufirst, mods, cw_half,
                                         wg, bg, lam, h0, w_out_odd[o].astype(BF16))
            new_lru.append(jnp.stack([hs_f[:cfg.nc, 0], hs_b[:cfg.nc, 0]], axis=1))
            pre_args = (upd,)
        x = _ffn(cfg, l, x, pre_args, mods, norm_ffn[l].reshape(1, D_MODEL), w_ffn_in, w_ffn_out, nf,
                 final=(l == DEPTH - 1))
    y_prompt = x[0].reshape(cfg.nc, cfg.tc, D_MODEL)
    y_sample = x[1].reshape(cfg.nl, cfg.tl, D_MODEL)
    return (y_prompt, y_sample, jnp.stack(new_k, axis=1), jnp.stack(new_v, axis=1),
            jnp.stack(new_gla, axis=1), jnp.stack(new_lru, axis=1))


def kernel(x_prompt, x_sample, c, cache_k, cache_v, state_gla, state_lru, c_ctx, w_ada, b_ada, norm_mix, norm_ffn, w_in_even, attn_sink, w_alpha, b_alpha, gla_gain, w_out_even, w_in_odd, conv_w, conv_b, w_gate_a, b_gate_a, w_gate_x, b_gate_x, lru_lambda, w_out_odd, w_ffn_in, w_ffn_out, norm_final):
    cfg = Cfg(nc=x_prompt.shape[0], tc=x_prompt.shape[1], nl=x_sample.shape[0], tl=x_sample.shape[1])
    return _forward(cfg, x_prompt, x_sample, c, cache_k, cache_v, state_gla, state_lru, c_ctx,
                    w_ada, b_ada, norm_mix, norm_ffn, w_in_even, attn_sink, w_alpha, b_alpha, gla_gain,
                    w_out_even, w_in_odd, conv_w, conv_b, w_gate_a, b_gate_a, w_gate_x, b_gate_x,
                    lru_lambda, w_out_odd, w_ffn_in, w_ffn_out, norm_final)
```

```python
import functools
from typing import NamedTuple

import jax
import jax.numpy as jnp
import numpy as np
from jax import lax
from jax.experimental import pallas as pl
from jax.experimental.pallas import tpu as pltpu

F32 = jnp.float32
BF16 = jnp.bfloat16

D_MODEL = 1024
DEPTH = 4
EPS = 1e-6

HEAD_DIM = 64
A_Q_HEADS = 8
A_KV_HEADS = 2
A_GROUPS = A_Q_HEADS // A_KV_HEADS
A_WIDTH = A_Q_HEADS * HEAD_DIM
ATT_BLOCK = 128
ROPE_BASE = 10000.0
GRID_W = 64
NEG_INF = -1e30
PAST_LEN = 256

B_HEADS = 4
B_DK = 64
B_DV = 128
B_QK = B_HEADS * B_DK
B_WIDTH = B_HEADS * B_DV
GLA_RANK = 16
GLA_NORMALIZER = 16.0
GLA_CHUNK = 64

D_RNN = D_MODEL
LRU_BLOCK_W = 256
LRU_BLOCKS = D_RNN // LRU_BLOCK_W
CONV_W = 4
LRU_C = 8.0
LOG2_E = 1.4426950408889634

D_FF = 2816

LANES = 128
SUBLANES = 8

ZA_W = A_WIDTH + 2 * 2 * LANES
ZB_LR = B_QK + B_QK + B_WIDTH + B_WIDTH
ZB_W = ZB_LR + LANES
P_A = A_WIDTH + 2 * LANES
P_MAIN = P_A + ZB_LR

MXU_COLS_V7X = 256
TM = 512
TS = 256
FF_CHUNK = MXU_COLS_V7X

VMEM_BYTES_V7X = 64 * 1024 * 1024
VMEM_LIMIT = VMEM_BYTES_V7X - 8 * 1024 * 1024


class Cfg(NamedTuple):
    nc: int
    tc: int
    nl: int
    tl: int

    @property
    def n_ctx(self):
        return self.nc * self.tc

    @property
    def n_lat(self):
        return self.nl * self.tl

    @property
    def n_tok(self):
        return self.n_ctx + self.n_lat


def _cparams(n_axes):
    return pltpu.CompilerParams(dimension_semantics=("arbitrary",) * n_axes,
                                vmem_limit_bytes=VMEM_LIMIT)


def _const_spec(shape):
    nd = len(shape)
    return pl.BlockSpec(shape, lambda *_: (0,) * nd)


def _silu(x):
    hx = 0.5 * x
    return hx + hx * jnp.tanh(hx)


def _group_of_tile(cfg, i, tile):
    nct = cfg.n_ctx // tile
    per_lat = cfg.tl // tile
    return jnp.where(i < nct, 0, 1 + jnp.maximum(i - nct, 0) // per_lat)


def _modnorm(x, nw, shift, scale):
    ms = jnp.mean(x * x, axis=-1, keepdims=True)
    y = x * lax.rsqrt(ms + EPS) * nw
    return y * (1.0 + scale) + shift


def _mod_rows(mod_ref, g, first):
    return tuple(mod_ref[first + j, pl.ds(g, 1), :] for j in range(3))


ADA_CHUNKS = 3


def _ada_kernel(c_ref, w_ref, b_ref, o_ref):
    s = _silu(c_ref[...]).astype(BF16)
    for j in range(ADA_CHUNKS):
        o_ref[j] = jnp.dot(s, w_ref[:, j * D_MODEL:(j + 1) * D_MODEL].astype(BF16),
                           preferred_element_type=F32) + b_ref[j]


def _ada(cvec, w_ada, b_ada):
    b4 = b_ada.reshape(DEPTH, 6, 1, D_MODEL)
    return pl.pallas_call(
        _ada_kernel,
        grid=(DEPTH, 6 // ADA_CHUNKS),
        in_specs=[
            _const_spec((SUBLANES, D_MODEL)),
            pl.BlockSpec((None, D_MODEL, ADA_CHUNKS * D_MODEL), lambda l, j: (l, 0, j)),
            pl.BlockSpec((None, ADA_CHUNKS, 1, D_MODEL), lambda l, j: (l, j, 0, 0)),
        ],
        out_specs=pl.BlockSpec((None, ADA_CHUNKS, SUBLANES, D_MODEL), lambda l, j: (l, j, 0, 0)),
        out_shape=jax.ShapeDtypeStruct((DEPTH, 6, SUBLANES, D_MODEL), F32),
        compiler_params=_cparams(2),
        name="ada",
    )(cvec, w_ada, b4)


def _mod_spec(l):
    return pl.BlockSpec((None, 6, SUBLANES, D_MODEL), lambda *_: (l, 0, 0, 0))


def _ffn_tile(x, nw, shift, scale, gate, win_ref, wout_ref):
    h = _modnorm(x, nw, shift, scale).astype(BF16)
    acc = None
    for c in range(D_FF // FF_CHUNK):
        lo, hi = c * FF_CHUNK, (c + 1) * FF_CHUNK
        g = jnp.dot(h, win_ref[:, lo:hi].astype(BF16), preferred_element_type=F32)
        u = jnp.dot(h, win_ref[:, D_FF + lo:D_FF + hi].astype(BF16), preferred_element_type=F32)
        a = (_silu(g) * u).astype(BF16)
        part = jnp.dot(a, wout_ref[lo:hi, :].astype(BF16), preferred_element_type=F32)
        acc = part if acc is None else acc + part
    return x + gate * acc


def _x_args(cfg, x, tile=TM):
    if not isinstance(x, tuple):
        return [x], [pl.BlockSpec((tile, D_MODEL), lambda i: (i, 0))]
    nct = cfg.n_ctx // tile
    return list(x), [pl.BlockSpec((tile, D_MODEL), lambda i: (jnp.minimum(i, nct - 1), 0)),
                     pl.BlockSpec((tile, D_MODEL), lambda i: (jnp.maximum(i - nct, 0), 0))]


def _load_x(cfg, x_refs, i, tile=TM):
    if len(x_refs) == 1:
        return x_refs[0][...]
    return jnp.where(i < cfg.n_ctx // tile, x_refs[0][...], x_refs[1][...])


def _to_scan_order(x):
    nb, d = x.shape[0] // TS, x.shape[1]
    return jnp.swapaxes(x.reshape(nb, SUBLANES, TS // SUBLANES, d), 1, 2).reshape(nb * TS, d)


def _from_scan_order(x):
    nb, d = x.shape[0] // TS, x.shape[1]
    return jnp.swapaxes(x.reshape(nb, TS // SUBLANES, SUBLANES, d), 1, 2).reshape(nb * TS, d)


def _ffn_kernel(*refs, cfg, n_x, pre, final):
    x_refs, refs = refs[:n_x], refs[n_x:]
    i = pl.program_id(0)
    g = _group_of_tile(cfg, i, TM)
    x = _load_x(cfg, x_refs, i)
    if pre == "even":
        (oac_ref, oal_ref, obc_ref, obl_ref, wmix_ref), refs = refs[:5], refs[5:]
    else:
        d_ref, refs = refs[0], refs[1:]
    mod_ref, nw_ref, win_ref, wout_ref, nf_ref = refs[:5]
    outs = refs[5:]
    if pre == "even":
        is_ctx_tile = i < cfg.n_ctx // TM
        o = jnp.concatenate([jnp.where(is_ctx_tile, oac_ref[...], oal_ref[...]),
                             jnp.where(is_ctx_tile, obc_ref[...], obl_ref[...])], axis=1)
        x = x + mod_ref[2, pl.ds(g, 1), :] * jnp.dot(o, wmix_ref[...], preferred_element_type=F32)
    else:
        x = x + _from_scan_order(d_ref[...])
    shift, scale, gate = _mod_rows(mod_ref, g, 3)
    y = _ffn_tile(x, nw_ref[...], shift, scale, gate, win_ref, wout_ref)
    if not final:
        outs[0][...] = y
        return
    ms = jnp.mean(y * y, axis=-1, keepdims=True)
    y = y * lax.rsqrt(ms + EPS) * nf_ref[...]
    is_ctx = i < cfg.n_ctx // TM

    @pl.when(is_ctx)
    def _():
        outs[0][...] = y

    @pl.when(jnp.logical_not(is_ctx))
    def _():
        outs[1][...] = y


def _ffn(cfg, l, x, pre_args, mods, nw, win, wout, nf, final):
    n = cfg.n_tok
    nct = cfg.n_ctx // TM
    tok = lambda w: pl.BlockSpec((TM, w), lambda i: (i, 0))
    ctx = lambda w: pl.BlockSpec((TM, w), lambda i: (jnp.minimum(i, nct - 1), 0))
    lat = lambda w: pl.BlockSpec((TM, w), lambda i: (jnp.maximum(i - nct, 0), 0))
    x_ops, x_specs = _x_args(cfg, x)
    if l % 2 == 0:
        pre = "even"
        pre_specs = [ctx(A_WIDTH), lat(A_WIDTH), ctx(B_WIDTH), lat(B_WIDTH),
                     pl.BlockSpec((None, A_WIDTH + B_WIDTH, D_MODEL), lambda i: (l // 2, 0, 0))]
    else:
        pre, pre_specs = "odd", [tok(D_MODEL)]
    if final:
        out_specs = [pl.BlockSpec((TM, D_MODEL), lambda i: (jnp.minimum(i, nct - 1), 0)),
                     pl.BlockSpec((TM, D_MODEL), lambda i: (jnp.maximum(i - nct, 0), 0))]
        out_shape = [jax.ShapeDtypeStruct((cfg.n_ctx, D_MODEL), F32),
                     jax.ShapeDtypeStruct((cfg.n_lat, D_MODEL), F32)]
    else:
        out_specs, out_shape = tok(D_MODEL), jax.ShapeDtypeStruct((n, D_MODEL), F32)
    return pl.pallas_call(
        functools.partial(_ffn_kernel, cfg=cfg, n_x=len(x_ops), pre=pre, final=final),
        grid=(n // TM,),
        in_specs=x_specs + pre_specs + [
            _mod_spec(l), _const_spec((1, D_MODEL)),
            pl.BlockSpec((None, D_MODEL, 2 * D_FF), lambda i: (l, 0, 0), pipeline_mode=pl.Buffered(1)),
            pl.BlockSpec((None, D_FF, D_MODEL), lambda i: (l, 0, 0), pipeline_mode=pl.Buffered(1)),
            _const_spec((1, D_MODEL))],
        out_specs=out_specs,
        out_shape=out_shape,
        compiler_params=_cparams(1),
        name=f"ffn{l}",
    )(*x_ops, *pre_args, mods, nw, win, wout, nf)


TM_EVEN_IN = 1024


def _even_in_kernel(*refs, cfg, n_x):
    x_refs, (mod_ref, nw_ref, w_ref, wlr_ref, tab_ref, za_ref, zb_ref, k_ref, v_ref) = refs[:n_x], refs[n_x:]
    i = pl.program_id(0)
    g = _group_of_tile(cfg, i, TM_EVEN_IN)
    shift, scale, _ = _mod_rows(mod_ref, g, 0)
    h = _modnorm(_load_x(cfg, x_refs, i, TM_EVEN_IN), nw_ref[...], shift, scale).astype(BF16)
    def proj(lo_col, hi_col):
        return jnp.dot(h, w_ref[:, lo_col:hi_col].astype(BF16), preferred_element_type=F32)

    is_lat = g != 0
    cos = jnp.where(is_lat, tab_ref[:, 0:LANES], 1.0)
    sin_a = jnp.where(is_lat, tab_ref[:, LANES:2 * LANES], 0.0)
    sin_b = jnp.where(is_lat, tab_ref[:, 2 * LANES:3 * LANES], 0.0)

    def rope(a):
        parts = []
        for j in range(a.shape[1] // LANES):
            t = a[:, j * LANES:(j + 1) * LANES]
            parts.append(t * cos + pltpu.roll(t, LANES - 16, 1) * sin_a + pltpu.roll(t, 16, 1) * sin_b)
        return jnp.concatenate(parts, axis=1)

    lo = lax.broadcasted_iota(jnp.int32, (TM_EVEN_IN, LANES), 1) < HEAD_DIM

    def twice(a):
        swapped = pltpu.roll(a, HEAD_DIM, 1)
        return jnp.concatenate([jnp.where(lo, a, swapped), jnp.where(lo, swapped, a)], axis=1).astype(BF16)

    q = proj(0, A_WIDTH) * (LOG2_E * HEAD_DIM ** -0.5)
    za_ref[:, 0:A_WIDTH] = rope(q).astype(BF16)
    kv = proj(A_WIDTH, P_A)
    k, v = kv[:, 0:LANES], kv[:, LANES:2 * LANES]
    za_ref[:, A_WIDTH:A_WIDTH + 2 * LANES] = twice(rope(k))
    za_ref[:, A_WIDTH + 2 * LANES:ZA_W] = twice(v)
    zb_ref[:, 0:B_QK] = proj(P_A, P_A + B_QK) * (B_DK ** -0.5)
    for c0 in range(B_QK, ZB_LR, 2 * LANES):
        zb_ref[:, c0:c0 + 2 * LANES] = proj(P_A + c0, P_A + c0 + 2 * LANES)
    zb_ref[:, ZB_LR:ZB_W] = jnp.dot(h, wlr_ref[...], preferred_element_type=F32)

    @pl.when(g == 0)
    def _():
        k_ref[...] = k
        v_ref[...] = v


def _even_in(cfg, l, x, mods, nw, w_all, w_lr, rope_tab):
    n = cfg.n_tok
    tile = TM_EVEN_IN
    assert cfg.n_ctx % tile == 0 and cfg.tl % tile == 0
    nct = cfg.n_ctx // tile
    per_lat = cfg.tl // tile
    tok = lambda w: pl.BlockSpec((tile, w), lambda i: (i, 0))
    x_ops, x_specs = _x_args(cfg, x, tile)
    ctx_kv = pl.BlockSpec((tile, LANES), lambda i: (jnp.minimum(i, nct - 1), 0))
    return pl.pallas_call(
        functools.partial(_even_in_kernel, cfg=cfg, n_x=len(x_ops)),
        grid=(n // tile,),
        in_specs=x_specs + [_mod_spec(l), _const_spec((1, D_MODEL)),
                            pl.BlockSpec((None, D_MODEL, P_MAIN), lambda i: (l // 2, 0, 0),
                                         pipeline_mode=pl.Buffered(1)),
                            pl.BlockSpec((None, D_MODEL, LANES), lambda i: (l // 2, 0, 0)),
                            pl.BlockSpec((tile, 3 * LANES), lambda i: (jnp.maximum(i - nct, 0) % per_lat, 0))],
        out_specs=[tok(ZA_W), tok(ZB_W), ctx_kv, ctx_kv],
        out_shape=[jax.ShapeDtypeStruct((n, ZA_W), BF16),
                   jax.ShapeDtypeStruct((n, ZB_W), F32),
                   jax.ShapeDtypeStruct((cfg.n_ctx, LANES), F32),
                   jax.ShapeDtypeStruct((cfg.n_ctx, LANES), F32)],
        compiler_params=_cparams(1),
        name=f"even_in{l}",
    )(*x_ops, mods, nw, w_all, w_lr, rope_tab)


def _sink_attention(units, sink_ref):
    scores = []
    for q, k_all, _, mask, kvh in units:
        tq = q.shape[0]
        lo = lax.broadcasted_iota(jnp.int32, (tq, LANES), 1) < HEAD_DIM
        qs = []
        for gq in range(A_GROUPS):
            h = kvh * A_GROUPS + gq
            pair = q[:, (h // 2) * LANES:(h // 2 + 1) * LANES]
            keep = lo if h % 2 == 0 else jnp.logical_not(lo)
            qs.append(jnp.where(keep, pair, jnp.zeros_like(pair)))
        qst = jnp.concatenate(qs, axis=0)
        s = lax.dot_general(qst, k_all, (((1,), (1,)), ((), ())), preferred_element_type=F32)
        if mask is not None:
            nfree = s.shape[1] - mask.shape[1]
            s = jnp.concatenate([s[:, :nfree], jnp.where(mask, s[:, nfree:], NEG_INF)], axis=1)
        scores.append(s)
    probs = []
    for (q, _, _, _, kvh), s in zip(units, scores):
        tq = q.shape[0]
        rows = lax.broadcasted_iota(jnp.int32, (A_GROUPS * tq, 1), 0) // tq
        sink = jnp.zeros((A_GROUPS * tq, 1), F32)
        for gq in range(A_GROUPS):
            sink = jnp.where(rows == gq, sink_ref[kvh * A_GROUPS + gq] * LOG2_E, sink)
        m = jnp.maximum(jnp.max(s, axis=-1, keepdims=True), sink)
        p = jnp.exp2(s - m)
        den = jnp.sum(p, axis=-1, keepdims=True) + jnp.exp2(sink - m)
        probs.append((p.astype(BF16), den))
    outs = []
    for (q, _, v_all, _, _), (p, den) in zip(units, probs):
        tq = q.shape[0]
        lo = lax.broadcasted_iota(jnp.int32, (tq, LANES), 1) < HEAD_DIM
        o = jnp.dot(p, v_all, preferred_element_type=F32) / den
        pairs = [jnp.where(lo, o[(2 * pr) * tq:(2 * pr + 1) * tq], o[(2 * pr + 1) * tq:(2 * pr + 2) * tq])
                 for pr in range(2)]
        outs.append(jnp.concatenate(pairs, axis=1).astype(BF16))
    return outs


def _attn_ctx_kernel(sink_ref, q_ref, k_ref, v_ref, o_ref):
    q = q_ref[...]
    units = [(q, k_ref[:, kvh * LANES:(kvh + 1) * LANES], v_ref[:, kvh * LANES:(kvh + 1) * LANES], None, kvh)
             for kvh in range(A_KV_HEADS)]
    for kvh, o in enumerate(_sink_attention(units, sink_ref)):
        o_ref[:, 2 * kvh * LANES:2 * (kvh + 1) * LANES] = o


def _attn_ctx(cfg, l, za, sink):
    tc = cfg.tc
    return pl.pallas_call(
        _attn_ctx_kernel,
        grid=(cfg.nc,),
        in_specs=[pl.BlockSpec(memory_space=pltpu.SMEM),
                  pl.BlockSpec((tc, A_WIDTH), lambda b: (b, 0)),
                  pl.BlockSpec((tc, 2 * LANES), lambda b: (b, A_WIDTH // (2 * LANES))),
                  pl.BlockSpec((tc, 2 * LANES), lambda b: (b, A_WIDTH // (2 * LANES) + 1))],
        out_specs=pl.BlockSpec((tc, A_WIDTH), lambda b: (b, 0)),
        out_shape=jax.ShapeDtypeStruct((cfg.n_ctx, A_WIDTH), BF16),
        compiler_params=_cparams(1),
        name=f"attn_ctx{l}",
    )(sink, za, za, za)


ATT_STEP = 8 * ATT_BLOCK
ATT_WIN = 3 * ATT_BLOCK


def _attn_lat_kernel(sink_ref, q_ref, k_ref, v_ref, ck_ref, cv_ref, o_ref):
    n = pl.program_id(1)
    tl = k_ref.shape[0]
    shape = (A_GROUPS * ATT_BLOCK, ATT_WIN)
    rc = (lax.broadcasted_iota(jnp.int32, shape, 0) % ATT_BLOCK - lax.broadcasted_iota(jnp.int32, shape, 1)
          + ATT_BLOCK)
    units = []
    for j in range(ATT_STEP // ATT_BLOCK):
        q0 = (n * (ATT_STEP // ATT_BLOCK) + j) * ATT_BLOCK
        start = pl.multiple_of(jnp.clip(q0 - ATT_BLOCK, 0, tl - ATT_WIN), ATT_BLOCK)
        mask = lax.bitcast_convert_type(rc + (q0 - start), jnp.uint32) <= jnp.uint32(2 * ATT_BLOCK)
        q = q_ref[j * ATT_BLOCK:(j + 1) * ATT_BLOCK, :]
        for kvh in range(A_KV_HEADS):
            sl = slice(kvh * LANES, (kvh + 1) * LANES)
            k_all = jnp.concatenate([ck_ref[:, sl], k_ref[pl.ds(start, ATT_WIN), sl]], axis=0)
            v_all = jnp.concatenate([cv_ref[:, sl], v_ref[pl.ds(start, ATT_WIN), sl]], axis=0)
            units.append((q, k_all, v_all, mask, kvh))
    for u, o in enumerate(_sink_attention(units, sink_ref)):
        j, kvh = divmod(u, A_KV_HEADS)
        o_ref[j * ATT_BLOCK:(j + 1) * ATT_BLOCK, 2 * kvh * LANES:2 * (kvh + 1) * LANES] = o


def _attn_lat(cfg, l, za, cache_k2, cache_v2, sink):
    assert cfg.n_ctx % cfg.tl == 0 and cfg.tl % ATT_STEP == 0 and cfg.tl >= ATT_WIN
    steps = cfg.tl // ATT_STEP
    base = cfg.n_ctx // ATT_STEP
    kcol = A_WIDTH // (2 * LANES)
    seq = lambda col: pl.BlockSpec((cfg.tl, 2 * LANES), lambda b, n: (cfg.n_ctx // cfg.tl + b, col))
    cache = pl.BlockSpec((None, PAST_LEN, 2 * LANES), lambda b, n: (b, 0, 0))
    return pl.pallas_call(
        _attn_lat_kernel,
        grid=(cfg.nl, steps),
        in_specs=[pl.BlockSpec(memory_space=pltpu.SMEM),
                  pl.BlockSpec((ATT_STEP, A_WIDTH), lambda b, n: (base + b * steps + n, 0)),
                  seq(kcol), seq(kcol + 1), cache, cache],
        out_specs=pl.BlockSpec((ATT_STEP, A_WIDTH), lambda b, n: (b * steps + n, 0)),
        out_shape=jax.ShapeDtypeStruct((cfg.n_lat, A_WIDTH), BF16),
        compiler_params=_cparams(2),
        name=f"attn_lat{l}",
    )(sink, za, za, za, cache_k2, cache_v2)


def _gla_chunk_cumsum(la, reverse):
    cn = GLA_CHUNK
    row = lax.broadcasted_iota(jnp.int32, (cn, B_QK), 0)
    b = la
    s = 1
    while s < cn:
        if reverse:
            b = b + jnp.where(row < cn - s, pltpu.roll(b, cn - s, 0), 0.0)
        else:
            b = b + jnp.where(row >= s, pltpu.roll(b, s, 0), 0.0)
        s *= 2
    return b


def _gla_chunk_prepare(q, k, v, b, reverse):
    cn = GLA_CHUNK
    row = lax.broadcasted_iota(jnp.int32, (cn, B_QK), 0)
    b_last = b[0:1, :] if reverse else b[cn - 1:cn, :]
    q_e = q * jnp.exp(b)
    k_e = (k * jnp.exp(-b)).astype(BF16)
    k_s = (k * jnp.exp(b_last - b)).astype(BF16)
    vb = v.astype(BF16)
    row_t = row[:, 0:LANES]
    lane_t = lax.broadcasted_iota(jnp.int32, (cn, LANES), 1)
    lo = lane_t < B_DK
    col = lane_t % cn
    tri = (col >= row_t) if reverse else (col <= row_t)
    q_tiles, att_tiles = [], []
    for t in range(B_HEADS // 2):
        sl = slice(t * LANES, (t + 1) * LANES)
        k_t = k_e[:, sl]
        zero = jnp.zeros_like(k_t)
        kbd = jnp.concatenate([jnp.where(lo, k_t, zero), jnp.where(lo, zero, k_t)], axis=0)
        q_t = q_e[:, sl]
        att = lax.dot_general(q_t.astype(BF16), kbd, (((1,), (1,)), ((), ())), preferred_element_type=F32)
        q_tiles.append(q_t)
        att_tiles.append(jnp.where(tri, att, 0.0))
    blocks = []
    for t in range(B_HEADS // 2):
        upd = lax.dot_general(k_s[:, t * LANES:(t + 1) * LANES], vb[:, 2 * t * B_DV:2 * (t + 1) * B_DV],
                              (((0,), (0,)), ((), ())), preferred_element_type=F32)
        blocks += [upd[g * B_DK:(g + 1) * B_DK, g * B_DV:(g + 1) * B_DV] for g in range(2)]
    own = jnp.concatenate(blocks, axis=0)
    decay = jnp.broadcast_to(jnp.exp(b_last), (LANES, B_QK)).T
    return q_tiles, att_tiles, vb, own, decay


def _gla_chunk_output(q_tiles, att_tiles, vb, st):
    lo = lax.broadcasted_iota(jnp.int32, (GLA_CHUNK, LANES), 1) < B_DK
    st_b = st.astype(BF16)
    outs = []
    for h in range(B_HEADS):
        att, q_t = att_tiles[h // 2], q_tiles[h // 2]
        x = (jnp.where(lo, att, pltpu.roll(q_t, B_DK, 1)) if h % 2 == 0
             else jnp.where(lo, pltpu.roll(att, B_DK, 1), q_t))
        rhs = jnp.concatenate([vb[:, h * B_DV:(h + 1) * B_DV], st_b[h * B_DK:(h + 1) * B_DK, :]], axis=0)
        outs.append(jnp.dot(x.astype(BF16), rhs, preferred_element_type=F32))
    return jnp.concatenate(outs, axis=1)


def _gla_log_decay(lr_pad, wal_ref, bal_ref, direction):
    z = jnp.dot(lr_pad.astype(BF16), wal_ref[:, direction * B_QK:(direction + 1) * B_QK],
                preferred_element_type=F32) + bal_ref[:, direction * B_QK:(direction + 1) * B_QK]
    return jax.nn.log_sigmoid(z) / GLA_NORMALIZER


def _gla_blocks(jobs, wal_ref, bal_ref):
    prepared = []
    for qkv_ref, lr_ref, _, direction in jobs:
        nchunk = qkv_ref.shape[0] // GLA_CHUNK
        la_all = _gla_log_decay(lr_ref[...], wal_ref, bal_ref, direction)
        b_all = jnp.concatenate(
            [_gla_chunk_cumsum(la_all[r0:r0 + GLA_CHUNK, :], bool(direction))
             for r0 in range(0, qkv_ref.shape[0], GLA_CHUNK)], axis=0)
        chunks = {}
        for ci in range(nchunk):
            rs = slice(ci * GLA_CHUNK, (ci + 1) * GLA_CHUNK)
            chunks[ci] = _gla_chunk_prepare(qkv_ref[rs, 0:B_QK], qkv_ref[rs, B_QK:2 * B_QK],
                                            qkv_ref[rs, 2 * B_QK:2 * B_QK + B_WIDTH], b_all[rs, :],
                                            bool(direction))
        prepared.append(chunks)
    results = []
    for (qkv_ref, _, st, direction), chunks in zip(jobs, prepared):
        nchunk = len(chunks)
        states = {}
        for ci in (range(nchunk - 1, -1, -1) if direction else range(nchunk)):
            states[ci] = st
            _, _, _, own, decay = chunks[ci]
            st = st * decay + own
        outs = [_gla_chunk_output(*chunks[ci][:3], states[ci]) for ci in range(nchunk)]
        results.append((jnp.concatenate(outs, axis=0), st))
    return results


def _stack_state(s_ref):
    return s_ref[...].reshape(B_QK, B_DV)


def _unstack_state(st, s_ref):
    s_ref[...] = st.reshape(B_HEADS, B_DK, B_DV)


def _gla_finish(o, r, gain):
    parts = []
    for h in range(B_HEADS):
        oh = o[:, h * B_DV:(h + 1) * B_DV]
        parts.append(oh * lax.rsqrt(jnp.mean(oh * oh, axis=-1, keepdims=True) + EPS))
    return (jnp.concatenate(parts, axis=1) * gain * _silu(r)).astype(BF16)


def _gla_ctx_kernel(qkv_ref, r_ref, lr_ref, wal_ref, bal_ref, gain_ref, ob_ref, snew_ref):
    zero = jnp.zeros((B_QK, B_DV), F32)
    (o_f, st_f), (o_b, st_b) = _gla_blocks([(qkv_ref, lr_ref, zero, 0), (qkv_ref, lr_ref, zero, 1)],
                                           wal_ref, bal_ref)
    _unstack_state(st_f, snew_ref.at[0])
    _unstack_state(st_b, snew_ref.at[1])
    ob_ref[...] = _gla_finish(o_f + o_b, r_ref[...], gain_ref[...])


def _gla_ctx(cfg, l, zb, wal, bal, gain):
    tc = cfg.tc
    return pl.pallas_call(
        _gla_ctx_kernel,
        grid=(cfg.nc,),
        in_specs=[pl.BlockSpec((tc, 2 * B_QK + B_WIDTH), lambda b: (b, 0)),
                  pl.BlockSpec((tc, B_WIDTH), lambda b: (b, (2 * B_QK + B_WIDTH) // B_WIDTH)),
                  pl.BlockSpec((tc, LANES), lambda b: (b, ZB_LR // LANES)),
                  _const_spec((LANES, 2 * B_QK)), _const_spec((1, 2 * B_QK)), _const_spec((1, B_WIDTH))],
        out_specs=[pl.BlockSpec((tc, B_WIDTH), lambda b: (b, 0)),
                   pl.BlockSpec((None, 2, B_HEADS, B_DK, B_DV), lambda b: (b, 0, 0, 0, 0))],
        out_shape=[jax.ShapeDtypeStruct((cfg.n_ctx, B_WIDTH), BF16),
                   jax.ShapeDtypeStruct((cfg.nc, 2, B_HEADS, B_DK, B_DV), F32)],
        compiler_params=_cparams(1),
        name=f"gla_ctx{l}",
    )(zb, zb, zb, wal, bal, gain)


GLA_STEP = 512


def _gla_lat_kernel(qkvf_ref, lrf_ref, rf_ref, qkvb_ref, lrb_ref, rb_ref, wal_ref, bal_ref, gain_ref,
                    s0_ref, ob_ref, stf_ref, stb_ref, of_ref, obk_ref):
    n = pl.program_id(1)
    nb = pl.num_programs(1)

    @pl.when(n == 0)
    def _():
        stf_ref[...] = _stack_state(s0_ref.at[0])
        stb_ref[...] = _stack_state(s0_ref.at[1])
    (o_f, st_f), (o_b, st_b) = _gla_blocks([(qkvf_ref, lrf_ref, stf_ref[...], 0),
                                            (qkvb_ref, lrb_ref, stb_ref[...], 1)], wal_ref, bal_ref)
    stf_ref[...] = st_f
    stb_ref[...] = st_b
    rows_f = pl.ds(pl.multiple_of(n * GLA_STEP, GLA_STEP), GLA_STEP)
    rows_b = pl.ds(pl.multiple_of((nb - 1 - n) * GLA_STEP, GLA_STEP), GLA_STEP)
    of_ref[rows_f, :] = o_f
    obk_ref[rows_b, :] = o_b

    @pl.when(2 * n >= nb)
    def _():
        ob_ref[rows_f, :] = _gla_finish(o_f + obk_ref[rows_f, :], rf_ref[...], gain_ref[...])
        ob_ref[rows_b, :] = _gla_finish(of_ref[rows_b, :] + o_b, rb_ref[...], gain_ref[...])


def _gla_lat(cfg, l, zb, wal, bal, gain, s0):
    nbl = cfg.tl // GLA_STEP
    base = cfg.n_ctx // GLA_STEP
    assert nbl % 2 == 0 and cfg.tl % GLA_STEP == 0 and cfg.n_ctx % GLA_STEP == 0

    def blk(width, col, rev):
        def imap(b, n):
            return (base + b * nbl + (nbl - 1 - n if rev else n), col)
        return pl.BlockSpec((GLA_STEP, width), imap)

    qkv_w = 2 * B_QK + B_WIDTH
    per_dir = lambda rev: [blk(qkv_w, 0, rev), blk(LANES, ZB_LR // LANES, rev), blk(B_WIDTH, qkv_w // B_WIDTH, rev)]
    return pl.pallas_call(
        _gla_lat_kernel,
        grid=(cfg.nl, nbl),
        in_specs=per_dir(False) + per_dir(True) + [
            _const_spec((LANES, 2 * B_QK)), _const_spec((1, 2 * B_QK)), _const_spec((1, B_WIDTH)),
            pl.BlockSpec((None, 2, B_HEADS, B_DK, B_DV), lambda b, n: (b, 0, 0, 0, 0))],
        out_specs=pl.BlockSpec((cfg.tl, B_WIDTH), lambda b, n: (b, 0)),
        out_shape=jax.ShapeDtypeStruct((cfg.n_lat, B_WIDTH), BF16),
        scratch_shapes=[pltpu.VMEM((B_QK, B_DV), F32), pltpu.VMEM((B_QK, B_DV), F32),
                        pltpu.VMEM((cfg.tl, B_WIDTH), F32), pltpu.VMEM((cfg.tl, B_WIDTH), F32)],
        compiler_params=_cparams(2),
        name=f"gla_lat{l}",
    )(zb, zb, zb, zb, zb, zb, wal, bal, gain, s0)


LRU_CHUNK = TS // SUBLANES


TM_ODD_IN = 1024
ODD_IN_COLS = 256


def _odd_in_kernel(x_ref, mod_ref, nw_ref, w_ref, cw_ref, cb_ref, g_ref, uc_ref, ufirst_ref, tail_ref, *, cfg):
    tile = pl.program_id(0)

    @pl.when(tile == 0)
    def _():
        tail_ref[...] = jnp.zeros_like(tail_ref)
    g = _group_of_tile(cfg, tile, TM_ODD_IN)
    shift, scale, _ = _mod_rows(mod_ref, g, 0)
    h = _modnorm(_to_scan_order(x_ref[...]), nw_ref[...], shift, scale).astype(BF16)
    nsub = TM_ODD_IN // TS
    for c0 in range(0, D_RNN, ODD_IN_COLS):
        z = jnp.dot(h, w_ref[:, c0:c0 + ODD_IN_COLS].astype(BF16), preferred_element_type=F32)
        g_ref[:, c0:c0 + ODD_IN_COLS] = jax.nn.gelu(z).astype(BF16)
    for c0 in range(0, D_RNN, ODD_IN_COLS):
        cols = slice(c0, c0 + ODD_IN_COLS)
        u = jnp.dot(h, w_ref[:, D_RNN + c0:D_RNN + c0 + ODD_IN_COLS].astype(BF16), preferred_element_type=F32)
        cw, cb = cw_ref[:, cols], cb_ref[:, cols]
        for sb in range(nsub):
            _, _, first, last = _block_position(cfg, tile * nsub + sb)
            prev16 = tail_ref[:, cols] if sb == 0 else u[sb * TS - 2 * SUBLANES:sb * TS]
            if sb == nsub - 1:
                next8, last = jnp.zeros((SUBLANES, ODD_IN_COLS), F32), True
            else:
                next8 = u[(sb + 1) * TS:(sb + 1) * TS + SUBLANES]
            uc_ref[sb * TS:(sb + 1) * TS, cols] = _conv_centred(u[sb * TS:(sb + 1) * TS], prev16, next8,
                                                                first, last, cw, cb)
        ufirst_ref[:, cols] = u[0:1]
        tail_ref[:, cols] = u[TM_ODD_IN - 2 * SUBLANES:TM_ODD_IN]


def _odd_in(cfg, l, x, mods, nw, w, cw, cb):
    n = cfg.n_tok
    return pl.pallas_call(
        functools.partial(_odd_in_kernel, cfg=cfg),
        grid=(n // TM_ODD_IN,),
        in_specs=[pl.BlockSpec((TM_ODD_IN, D_MODEL), lambda i: (i, 0)), _mod_spec(l), _const_spec((1, D_MODEL)),
                  pl.BlockSpec((None, D_MODEL, 2 * D_RNN), lambda i: (l // 2, 0, 0),
                               pipeline_mode=pl.Buffered(1)),
                  _const_spec((CONV_W, D_RNN)), _const_spec((1, D_RNN))],
        out_specs=[pl.BlockSpec((TM_ODD_IN, D_RNN), lambda i: (i, 0))] * 2
                  + [pl.BlockSpec((None, 1, D_RNN), lambda i: (i, 0, 0))],
        out_shape=[jax.ShapeDtypeStruct((n, D_RNN), BF16), jax.ShapeDtypeStruct((n, D_RNN), F32),
                   jax.ShapeDtypeStruct((n // TM_ODD_IN, 1, D_RNN), F32)],
        scratch_shapes=[pltpu.VMEM((2 * SUBLANES, D_RNN), F32)],
        compiler_params=_cparams(1),
        name=f"odd_in{l}",
    )(x, mods, nw, w, cw, cb)


def _block_position(cfg, ib):
    ncb = cfg.n_ctx // TS
    bpc = cfg.tc // TS
    bpl = cfg.tl // TS
    is_ctx = ib < ncb
    jl = jnp.maximum(ib - ncb, 0)
    pos = jnp.where(is_ctx, ib % bpc, jl % bpl)
    per = jnp.where(is_ctx, bpc, bpl)
    return is_ctx, jl // bpl, pos == 0, pos == per - 1


def _conv_centred(u, prev16, next8, first, last, cw, cb):
    sub = lax.broadcasted_iota(jnp.int32, (SUBLANES, u.shape[1]), 0)
    lastrow = TS - SUBLANES

    def from_prev_chunk(own, other):
        return jnp.where(sub == 0, jnp.where(first, 0.0, pltpu.roll(other, 1, 0)), pltpu.roll(own, 1, 0))

    s30 = from_prev_chunk(u[lastrow - SUBLANES:lastrow], prev16[0:SUBLANES])
    s31 = from_prev_chunk(u[lastrow:TS], prev16[SUBLANES:2 * SUBLANES])
    n0 = jnp.where(sub == SUBLANES - 1, jnp.where(last, 0.0, pltpu.roll(next8, SUBLANES - 1, 0)),
                   pltpu.roll(u[0:SUBLANES], SUBLANES - 1, 0))
    m2 = jnp.concatenate([s30, s31, u[0:lastrow - SUBLANES]], axis=0)
    m1 = jnp.concatenate([s31, u[0:lastrow]], axis=0)
    p1 = jnp.concatenate([u[SUBLANES:TS], n0], axis=0)
    return m2 * cw[0:1, :] + m1 * cw[1:2, :] + u * cw[2:3, :] + p1 * cw[3:4, :] + cb


def _lru_terms(uh, wg_ref, bg_ref, lam_ref):
    ucb = uh.astype(BF16)
    rs, is_ = [], []
    for nblk in range(LRU_BLOCKS):
        z = jnp.dot(ucb[:, nblk * LRU_BLOCK_W:(nblk + 1) * LRU_BLOCK_W], wg_ref[nblk],
                    preferred_element_type=F32)
        rs.append(z[:, 0:LRU_BLOCK_W])
        is_.append(z[:, LRU_BLOCK_W:2 * LRU_BLOCK_W])
    tr = jnp.tanh(jnp.concatenate(rs, axis=1) + bg_ref[0:1, :])
    ti = jnp.tanh(jnp.concatenate(is_, axis=1) + bg_ref[1:2, :])
    k2 = (-0.5 * LRU_C * LOG2_E) * jax.nn.softplus(-lam_ref[...])
    a = jnp.exp2(k2 + k2 * tr)
    w = 1.0 - a * a
    root = jnp.where(w > 0.0, w * lax.rsqrt(w), 0.0)
    v = root * ((1.0 + ti) * uh)
    return a, v


def _lru_scan(a, v, carry, reverse):
    order = range(LRU_CHUNK - 1, -1, -1) if reverse else range(LRU_CHUNK)
    h = jnp.zeros((SUBLANES, D_RNN), F32)
    p = jnp.ones((SUBLANES, D_RNN), F32)
    hs, ps = [None] * LRU_CHUNK, [None] * LRU_CHUNK
    for j in order:
        aj = a[j * SUBLANES:(j + 1) * SUBLANES]
        h = aj * h + v[j * SUBLANES:(j + 1) * SUBLANES]
        p = aj * p
        hs[j], ps[j] = h, p
    entry = [None] * SUBLANES
    for c in (range(SUBLANES - 1, -1, -1) if reverse else range(SUBLANES)):
        entry[c] = carry
        carry = h[c:c + 1] + p[c:c + 1] * carry
    hm = jnp.concatenate(entry, axis=0)
    return jnp.concatenate([hs[j] + ps[j] * hm for j in range(LRU_CHUNK)], axis=0), carry


LRU_SUB = 4
LRU_STEP = LRU_SUB * TS


def _lru_scans(terms, h0_ref, carry_ref, hs_ref, cfg, tile, direction):
    carry = carry_ref[...]
    ys = [None] * LRU_SUB
    for sb in (range(LRU_SUB - 1, -1, -1) if direction else range(LRU_SUB)):
        is_ctx, jseq, first, last = _block_position(cfg, tile * LRU_SUB + sb)
        start = last if direction else first
        carry = jnp.where(start, jnp.where(is_ctx, 0.0, h0_ref[jseq]), carry)
        ys[sb], carry = _lru_scan(*terms[sb], carry, bool(direction))
        hs_ref[sb] = carry
    carry_ref[...] = carry
    return ys


def _lru_tile_terms(uc_ref, unext_ref, cw_ref, wg_ref, bg_ref, lam_ref, cfg, tile):
    _, _, _, last = _block_position(cfg, tile * LRU_SUB + LRU_SUB - 1)
    missing = jnp.where(last, 0.0, cw_ref[CONV_W - 1:CONV_W, :] * unext_ref[...])
    sub = lax.broadcasted_iota(jnp.int32, (SUBLANES, D_RNN), 0)
    terms = []
    for sb in range(LRU_SUB):
        uc = uc_ref[sb * TS:(sb + 1) * TS, :]
        if sb == LRU_SUB - 1:
            tail = uc[TS - SUBLANES:TS] + jnp.where(sub == SUBLANES - 1, missing, 0.0)
            uc = jnp.concatenate([uc[0:TS - SUBLANES], tail], axis=0)
        terms.append(_lru_terms(uc, wg_ref, bg_ref, lam_ref))
    return terms


def _odd_fwd_kernel(uc_ref, unext_ref, cw_ref, wg_ref, bg_ref, lam_ref, h0_ref,
                    yf_ref, hs_ref, carry_ref, *, cfg):
    tile = pl.program_id(0)
    terms = _lru_tile_terms(uc_ref, unext_ref, cw_ref, wg_ref, bg_ref, lam_ref, cfg, tile)
    for sb, y in enumerate(_lru_scans(terms, h0_ref, carry_ref, hs_ref, cfg, tile, 0)):
        yf_ref[sb * TS:(sb + 1) * TS, :] = y.astype(BF16)


def _odd_bwd_kernel(g_ref, uc_ref, unext_ref, yf_ref, mod_ref, cw_ref, wg_ref, bg_ref, lam_ref, h0_ref, wout_ref,
                    o_ref, hs_ref, carry_ref, *, cfg):
    tile = pl.num_programs(0) - 1 - pl.program_id(0)
    terms = _lru_tile_terms(uc_ref, unext_ref, cw_ref, wg_ref, bg_ref, lam_ref, cfg, tile)
    y = jnp.concatenate(_lru_scans(terms, h0_ref, carry_ref, hs_ref, cfg, tile, 1), axis=0)
    mix = (g_ref[...].astype(F32) * (yf_ref[...].astype(F32) + y)).astype(BF16)
    grp = _group_of_tile(cfg, tile, LRU_STEP)
    gate = mod_ref[2, pl.ds(grp, 1), :]
    o_ref[...] = gate * jnp.dot(mix, wout_ref[...], preferred_element_type=F32)


def _odd_scans(cfg, l, g, uc, ufirst, mods, cw, wg, bg, lam, h0, w_out):
    n = cfg.n_tok
    assert LRU_STEP == TM_ODD_IN and n % LRU_STEP == 0 and cfg.n_ctx % LRU_STEP == 0 and cfg.tl % LRU_STEP == 0
    nblk = n // TS
    nstep = n // LRU_STEP

    def specs(rev):
        it = (lambda i: nstep - 1 - i) if rev else (lambda i: i)
        cur = pl.BlockSpec((LRU_STEP, D_RNN), lambda i: (it(i), 0))
        unext = pl.BlockSpec((None, 1, D_RNN), lambda i: (jnp.minimum(it(i) + 1, nstep - 1), 0, 0))
        hs = pl.BlockSpec((LRU_SUB, 1, D_RNN), lambda i: (it(i), 0, 0))
        return cur, unext, hs

    def weights(direction):
        return [pl.BlockSpec((None, LRU_BLOCKS, LRU_BLOCK_W, 2 * LRU_BLOCK_W), lambda i: (direction, 0, 0, 0)),
                pl.BlockSpec((None, 2, D_RNN), lambda i: (direction, 0, 0)),
                pl.BlockSpec((None, 1, D_RNN), lambda i: (direction, 0, 0)),
                pl.BlockSpec((None, cfg.nl, 1, D_RNN), lambda i: (direction, 0, 0, 0))]

    cur, unext, hs = specs(False)
    conv_taps = _const_spec((CONV_W, D_RNN))
    y_f, hs_f = pl.pallas_call(
        functools.partial(_odd_fwd_kernel, cfg=cfg),
        grid=(nstep,),
        in_specs=[cur, unext, conv_taps] + weights(0),
        out_specs=[cur, hs],
        out_shape=[jax.ShapeDtypeStruct((n, D_RNN), BF16), jax.ShapeDtypeStruct((nblk, 1, D_RNN), F32)],
        scratch_shapes=[pltpu.VMEM((1, D_RNN), F32)],
        compiler_params=_cparams(1),
        name=f"lru_fwd{l}",
    )(uc, ufirst, cw, wg, bg, lam, h0)
    cur, unext, hs = specs(True)
    upd, hs_b = pl.pallas_call(
        functools.partial(_odd_bwd_kernel, cfg=cfg),
        grid=(nstep,),
        in_specs=[cur, cur, unext, cur, _mod_spec(l), conv_taps] + weights(1) + [_const_spec((D_RNN, D_MODEL))],
        out_specs=[cur, hs],
        out_shape=[jax.ShapeDtypeStruct((n, D_RNN), F32), jax.ShapeDtypeStruct((nblk, 1, D_RNN), F32)],
        scratch_shapes=[pltpu.VMEM((1, D_RNN), F32)],
        compiler_params=_cparams(1),
        name=f"lru_bwd{l}",
    )(g, uc, ufirst, y_f, mods, cw, wg, bg, lam, h0, w_out)
    return upd, hs_f, hs_b


def _rope_table(tl):
    pos = np.arange(tl)
    nf = HEAD_DIM // 4
    inv = ROPE_BASE ** (-np.arange(nf, dtype=np.float32) / nf)
    ar = (pos // GRID_W).astype(np.float32)[:, None] * inv
    ac = (pos % GRID_W).astype(np.float32)[:, None] * inv
    ang = np.concatenate([ar, ar, ac, ac] * 2, axis=-1)
    lo = (np.arange(LANES) % (HEAD_DIM // 2)) < nf
    cos, sin = np.cos(ang), np.sin(ang)
    tab = np.concatenate([cos, np.where(lo, -sin, 0.0), np.where(lo, 0.0, sin)], axis=-1)
    return jnp.asarray(tab, F32)


def _dup_heads(w):
    lead = w.shape[:-1]
    w = w.reshape(lead + (A_KV_HEADS, 1, HEAD_DIM))
    return jnp.broadcast_to(w, lead + (A_KV_HEADS, 2, HEAD_DIM)).reshape(lead + (2 * LANES,))


def _low_rank_columns(w):
    low = lax.slice_in_dim(w, P_MAIN, w.shape[2], axis=2)
    return jnp.pad(low, ((0, 0), (0, 0), (0, LANES - 2 * GLA_RANK))).astype(BF16)


def _pack_alpha(w_alpha, b_alpha):
    w = jnp.zeros((LANES, 2 * B_QK), F32)
    w = w.at[0:GLA_RANK, 0:B_QK].set(w_alpha[0]).at[GLA_RANK:2 * GLA_RANK, B_QK:].set(w_alpha[1])
    return w.astype(BF16), b_alpha.reshape(1, 2 * B_QK)


def _forward(cfg, x_prompt, x_sample, c, cache_k, cache_v, state_gla, state_lru, c_ctx,
             w_ada, b_ada, norm_mix, norm_ffn, w_in_even, attn_sink, w_alpha, b_alpha, gla_gain,
             w_out_even, w_in_odd, conv_w, conv_b, w_gate_a, b_gate_a, w_gate_x, b_gate_x,
             lru_lambda, w_out_odd, w_ffn_in, w_ffn_out, norm_final):
    assert cfg.tc == TS and cfg.tl % TM == 0 and cfg.n_ctx % TM == 0 and cfg.nl + 1 <= SUBLANES
    x = (x_prompt.reshape(cfg.n_ctx, D_MODEL), x_sample.reshape(cfg.n_lat, D_MODEL))
    cvec = jnp.concatenate([c_ctx[None, :], c, jnp.zeros((SUBLANES - 1 - cfg.nl, D_MODEL), F32)], axis=0)
    mods = _ada(cvec, w_ada, b_ada)
    rope_tab = _rope_table(cfg.tl)
    nf = norm_final.reshape(1, D_MODEL)
    w_out_even_bf = w_out_even.astype(BF16)
    w_even_main, w_even_low = w_in_even, _low_rank_columns(w_in_even)
    new_k, new_v, new_gla, new_lru = [], [], [], []
    for l in range(DEPTH):
        nw = norm_mix[l].reshape(1, D_MODEL)
        if l % 2 == 0:
            e = l // 2
            za, zb, k_ctx, v_ctx = _even_in(cfg, l, x, mods, nw, w_even_main, w_even_low, rope_tab)
            new_k.append(k_ctx.reshape(cfg.nc, cfg.tc, A_KV_HEADS, HEAD_DIM))
            new_v.append(v_ctx.reshape(cfg.nc, cfg.tc, A_KV_HEADS, HEAD_DIM))
            ck2 = _dup_heads(cache_k[:, e].reshape(cfg.nl, PAST_LEN, A_KV_HEADS * HEAD_DIM)).astype(BF16)
            cv2 = _dup_heads(cache_v[:, e].reshape(cfg.nl, PAST_LEN, A_KV_HEADS * HEAD_DIM)).astype(BF16)
            oa_ctx = _attn_ctx(cfg, l, za, attn_sink[e])
            oa_lat = _attn_lat(cfg, l, za, ck2, cv2, attn_sink[e])
            wal, bal = _pack_alpha(w_alpha[e], b_alpha[e])
            gain = gla_gain[e].reshape(1, B_WIDTH)
            ob_ctx, s_new = _gla_ctx(cfg, l, zb, wal, bal, gain)
            ob_lat = _gla_lat(cfg, l, zb, wal, bal, gain, state_gla[:, e])
            new_gla.append(s_new)
            pre_args = (oa_ctx, oa_lat, ob_ctx, ob_lat, w_out_even_bf)
        else:
            o = l // 2
            cw_half, cb_half = 0.5 * conv_w[o], 0.5 * conv_b[o].reshape(1, D_RNN)
            g_act, uc, ufirst = _odd_in(cfg, l, x, mods, nw, w_in_odd, cw_half, cb_half)
            wg = jnp.concatenate([w_gate_a[o], w_gate_x[o]], axis=-1).astype(BF16)
            bg = 0.5 * jnp.stack([b_gate_a[o], b_gate_x[o]], axis=1)
            lam = lru_lambda[o].reshape(2, 1, D_RNN)
            h0 = jnp.transpose(state_lru[:, o], (1, 0, 2)).reshape(2, cfg.nl, 1, D_RNN)
            upd, hs_f, hs_b = _odd_scans(cfg, l, g_act, uc, ufirst, mods, cw_half,
                                         wg, bg, lam, h0, w_out_odd[o].astype(BF16))
            new_lru.append(jnp.stack([hs_f[:cfg.nc, 0], hs_b[:cfg.nc, 0]], axis=1))
            pre_args = (upd,)
        x = _ffn(cfg, l, x, pre_args, mods, norm_ffn[l].reshape(1, D_MODEL), w_ffn_in, w_ffn_out, nf,
                 final=(l == DEPTH - 1))
    y_prompt = x[0].reshape(cfg.nc, cfg.tc, D_MODEL)
    y_sample = x[1].reshape(cfg.nl, cfg.tl, D_MODEL)
    return (y_prompt, y_sample, jnp.stack(new_k, axis=1), jnp.stack(new_v, axis=1),
            jnp.stack(new_gla, axis=1), jnp.stack(new_lru, axis=1))


def kernel(x_prompt, x_sample, c, cache_k, cache_v, state_gla, state_lru, c_ctx, w_ada, b_ada, norm_mix, norm_ffn, w_in_even, attn_sink, w_alpha, b_alpha, gla_gain, w_out_even, w_in_odd, conv_w, conv_b, w_gate_a, b_gate_a, w_gate_x, b_gate_x, lru_lambda, w_out_odd, w_ffn_in, w_ffn_out, norm_final):
    cfg = Cfg(nc=x_prompt.shape[0], tc=x_prompt.shape[1], nl=x_sample.shape[0], tl=x_sample.shape[1])
    return _forward(cfg, x_prompt, x_sample, c, cache_k, cache_v, state_gla, state_lru, c_ctx,
                    w_ada, b_ada, norm_mix, norm_ffn, w_in_even, attn_sink, w_alpha, b_alpha, gla_gain,
                    w_out_even, w_in_odd, conv_w, conv_b, w_gate_a, b_gate_a, w_gate_x, b_gate_x,
                    lru_lambda, w_out_odd, w_ffn_in, w_ffn_out, norm_final)
```
